```python
import functools
import jax, jax.numpy as jnp
from jax import lax
import numpy as np

D_MODEL = 1024
BATCH = 4
SEQ = 4096
DEPTH = 1

PLE_DIM = 256
D_FF = 2816
NORM_EPS = 1e-6
GM_WIDTH = 1024
GM_GROUPS = 8
GM_GROUP_DIM = GM_WIDTH // GM_GROUPS
GM_CHUNK = 128
N_HEADS = 16
N_KV_HEADS = 4
HEAD_DIM = 64
Q_PER_KV = N_HEADS // N_KV_HEADS
KV_WIDTH = N_KV_HEADS * HEAD_DIM
ROPE_DIM = HEAD_DIM // 4
ROPE_THETA = 500000.0
CMP_LEN = 32
CMP_STRIDE = 16
CMP_HIDDEN = 256
SEL_LEN = 64
SEL_TOP = 16
WINDOW = 512
Q_BLOCK = 64
N_NSA_BRANCH = 3
N_MERGE = 2
MASK_VALUE = -1e30
FORCE_SCORE = 1e9
IN_SPLITS = [GM_WIDTH, GM_WIDTH, N_HEADS * HEAD_DIM, 6 * KV_WIDTH, N_HEADS * N_NSA_BRANCH, N_MERGE * D_MODEL]

kernel_name = 'hybrid_gmlp_nsa_macaron'


def rms_norm(x, g):
    xf = x.astype(jnp.float32)
    y = xf * lax.rsqrt(jnp.mean(xf * xf, axis=-1, keepdims=True) + NORM_EPS)
    return (y * g.astype(jnp.float32)).astype(x.dtype)


def layer_norm(x, g, b):
    xf = x.astype(jnp.float32)
    mu = jnp.mean(xf, axis=-1, keepdims=True)
    var = jnp.mean(jnp.square(xf - mu), axis=-1, keepdims=True)
    y = (xf - mu) * lax.rsqrt(var + NORM_EPS)
    return (y * g.astype(jnp.float32) + b.astype(jnp.float32)).astype(x.dtype)


def swiglu(x, w_in, w_out):
    gate, up = jnp.split(x @ w_in, 2, axis=-1)
    return (jax.nn.silu(gate) * up) @ w_out


def rotary(x, pos):
    inv_freq = ROPE_THETA ** (-jnp.arange(0, ROPE_DIM, 2, dtype=jnp.float32) / ROPE_DIM)
    ang = pos.astype(jnp.float32)[:, None] * inv_freq[None, :]
    cos = jnp.cos(ang)[None, :, None, :]
    sin = jnp.sin(ang)[None, :, None, :]
    xr = x[..., :ROPE_DIM].astype(jnp.float32)
    x1, x2 = jnp.split(xr, 2, axis=-1)
    rot = jnp.concatenate([x1 * cos - x2 * sin, x2 * cos + x1 * sin], axis=-1).astype(x.dtype)
    return jnp.concatenate([rot, x[..., ROPE_DIM:]], axis=-1)


def masked_softmax(s, mask):
    s = jnp.where(mask, s.astype(jnp.float32), MASK_VALUE)
    return jax.nn.softmax(s, axis=-1) * mask


def gmlp_mixer(u, v, ln_g, ln_b, w_s, b_s):
    B, S, _ = u.shape
    vn = layer_norm(v, ln_g, ln_b).reshape(B, S // GM_CHUNK, GM_CHUNK, GM_GROUPS, GM_GROUP_DIM)
    causal = jnp.tril(jnp.ones((GM_CHUNK, GM_CHUNK), dtype=bool))
    w = jnp.where(causal[None], w_s, jnp.zeros_like(w_s))
    mix = jnp.einsum('gts,bcsgd->bctgd', w, vn) + b_s.T[None, None, :, :, None]
    return u * mix.reshape(B, S, GM_WIDTH)


def compress(x, pos_emb, w1, w2):
    B, S = x.shape[:2]
    n_sub = CMP_LEN // CMP_STRIDE
    n_chunks = S // CMP_STRIDE
    n_cmp = n_chunks - n_sub + 1
    xc = x.reshape(B, n_chunks, CMP_STRIDE, N_KV_HEADS, HEAD_DIM)
    blocks = jnp.concatenate([xc[:, r:r + n_cmp] for r in range(n_sub)], axis=2)
    blocks = blocks + pos_emb[None, None, :, None, :]
    flat = blocks.transpose(0, 1, 3, 2, 4).reshape(B, n_cmp, N_KV_HEADS, CMP_LEN * HEAD_DIM)
    return jax.nn.gelu(flat @ w1) @ w2


def selection_importance(p_cmp, n_sel):
    r_sel = SEL_LEN // CMP_STRIDE
    l_cmp = CMP_LEN // CMP_STRIDE
    pad = ((0, 0),) * (p_cmp.ndim - 1) + ((l_cmp - 1, r_sel + l_cmp),)
    padded = jnp.pad(p_cmp, pad)
    imp = jnp.zeros(p_cmp.shape[:-1] + (n_sel,), p_cmp.dtype)
    for r in range(1 - l_cmp, r_sel):
        start = r + l_cmp - 1
        imp = imp + padded[..., start:start + r_sel * n_sel:r_sel]
    return imp


def nsa_query_block(args, k_cmp, v_cmp, k_sel_blocks, v_sel_blocks, k_win_pad, v_win_pad):
    q_raw, q_rot, gate_logits, s0 = args
    B = q_raw.shape[0]
    n_cmp = k_cmp.shape[1]
    n_sel = k_sel_blocks.shape[2]
    n_top = min(SEL_TOP, n_sel)
    t = s0 + jnp.arange(Q_BLOCK)
    scale = HEAD_DIM ** -0.5
    qg = q_raw.reshape(B, Q_BLOCK, N_KV_HEADS, Q_PER_KV, HEAD_DIM)
    s_c = jnp.einsum('bqkgd,bnkd->bkgqn', qg, k_cmp) * scale
    cmp_end = jnp.arange(n_cmp) * CMP_STRIDE + CMP_LEN - 1
    p_c = masked_softmax(s_c, cmp_end[None, :] <= t[:, None])
    o_cmp = jnp.einsum('bkgqn,bnkd->bqkgd', p_c.astype(v_cmp.dtype), v_cmp)
    imp = selection_importance(p_c.sum(axis=2), n_sel)
    blk = jnp.arange(n_sel)[None, :]
    cur = (t // SEL_LEN)[:, None]
    forced = (blk == 0) | (blk == cur) | (blk == cur - 1)
    future = blk > cur
    score = jnp.where(forced, FORCE_SCORE, jnp.where(future, -FORCE_SCORE, imp))
    _, idx = lax.top_k(score, n_top)
    bi = jnp.arange(B)[:, None, None, None]
    ki = jnp.arange(N_KV_HEADS)[None, :, None, None]
    k_g = k_sel_blocks[bi, ki, idx]
    v_g = v_sel_blocks[bi, ki, idx]
    qs = q_rot.reshape(B, Q_BLOCK, N_KV_HEADS, Q_PER_KV, HEAD_DIM)
    s_s = jnp.einsum('bqkgd,bkqnld->bkgqnl', qs, k_g) * scale
    key_pos = idx[..., None] * SEL_LEN + jnp.arange(SEL_LEN)
    sel_mask = (key_pos <= t[None, None, :, None, None]).reshape(B, N_KV_HEADS, 1, Q_BLOCK, n_top * SEL_LEN)
    p_s = masked_softmax(s_s.reshape(B, N_KV_HEADS, Q_PER_KV, Q_BLOCK, n_top * SEL_LEN), sel_mask)
    p_s = p_s.reshape(s_s.shape).astype(v_g.dtype)
    o_sel = jnp.einsum('bkgqnl,bkqnld->bqkgd', p_s, v_g)
    k_w = lax.dynamic_slice_in_dim(k_win_pad, s0, WINDOW + Q_BLOCK, axis=1)
    v_w = lax.dynamic_slice_in_dim(v_win_pad, s0, WINDOW + Q_BLOCK, axis=1)
    pos = s0 - WINDOW + jnp.arange(WINDOW + Q_BLOCK)
    dist = t[:, None] - pos[None, :]
    win_mask = (dist >= 0) & (dist < WINDOW) & (pos[None, :] >= 0)
    s_w = jnp.einsum('bqkgd,blkd->bkgql', qs, k_w) * scale
    p_w = masked_softmax(s_w, win_mask).astype(v_w.dtype)
    o_win = jnp.einsum('bkgql,blkd->bqkgd', p_w, v_w)
    g = jax.nn.sigmoid(gate_logits.astype(jnp.float32)).reshape(B, Q_BLOCK, N_KV_HEADS, Q_PER_KV, N_NSA_BRANCH)
    o = g[..., 0:1] * o_cmp + g[..., 1:2] * o_sel + g[..., 2:3] * o_win
    return o.reshape(B, Q_BLOCK, N_HEADS * HEAD_DIM).astype(q_raw.dtype)


def nsa_mixer(q, k_c, v_c, k_s, v_s, k_w, v_w, gate_logits,
              cmp_pos_k, cmp_k_w1, cmp_k_w2, cmp_pos_v, cmp_v_w1, cmp_v_w2):
    B, S = q.shape[:2]
    pos = jnp.arange(S)
    q_rot = rotary(q, pos)
    k_cmp = compress(k_c, cmp_pos_k, cmp_k_w1, cmp_k_w2)
    v_cmp = compress(v_c, cmp_pos_v, cmp_v_w1, cmp_v_w2)
    n_sel = S // SEL_LEN

    def to_blocks(a):
        return a.reshape(B, n_sel, SEL_LEN, N_KV_HEADS, HEAD_DIM).transpose(0, 3, 1, 2, 4)

    k_sel_blocks = to_blocks(rotary(k_s, pos))
    v_sel_blocks = to_blocks(v_s)
    pad = ((0, 0), (WINDOW, 0), (0, 0), (0, 0))
    k_win_pad = jnp.pad(rotary(k_w, pos), pad)
    v_win_pad = jnp.pad(v_w, pad)
    n_qb = S // Q_BLOCK

    def q_blocks(a):
        return a.reshape((B, n_qb, Q_BLOCK) + a.shape[2:]).swapaxes(0, 1)

    starts = jnp.arange(n_qb, dtype=jnp.int32) * Q_BLOCK
    step = functools.partial(nsa_query_block, k_cmp=k_cmp, v_cmp=v_cmp, k_sel_blocks=k_sel_blocks,
                             v_sel_blocks=v_sel_blocks, k_win_pad=k_win_pad, v_win_pad=v_win_pad)
    out = lax.map(step, (q_blocks(q), q_blocks(q_rot), q_blocks(gate_logits), starts))
    return out.swapaxes(0, 1).reshape(B, S, N_HEADS * HEAD_DIM)


def setup_inputs(seed: int = 0) -> dict:
    key = jax.random.key(seed)
    ks = iter(jax.random.split(key, 40))

    def nrm(shape, scale):
        return scale * jax.random.normal(next(ks), shape, jnp.float32)

    def gain(n):
        return 1.0 + nrm((DEPTH, n), 0.02)

    in_width = sum(IN_SPLITS)
    return {
        'x': nrm((BATCH, SEQ, D_MODEL), 1.0),
        'p': nrm((DEPTH, BATCH, SEQ, PLE_DIM), 1.0),
        'ffn1_norm': gain(D_MODEL),
        'ffn1_w_in': nrm((DEPTH, D_MODEL, 2 * D_FF), D_MODEL ** -0.5),
        'ffn1_w_out': nrm((DEPTH, D_FF, D_MODEL), D_FF ** -0.5),
        'mix_norm': gain(D_MODEL),
        'w_in': nrm((DEPTH, D_MODEL, in_width), D_MODEL ** -0.5),
        'gm_ln_g': gain(GM_WIDTH),
        'gm_ln_b': nrm((DEPTH, GM_WIDTH), 0.02),
        'gm_w_s': nrm((DEPTH, GM_GROUPS, GM_CHUNK, GM_CHUNK), GM_CHUNK ** -0.5),
        'gm_b_s': 1.0 + nrm((DEPTH, GM_GROUPS, GM_CHUNK), 0.02),
        'w_branch_a': nrm((DEPTH, GM_WIDTH, D_MODEL), GM_WIDTH ** -0.5),
        'cmp_pos_k': nrm((DEPTH, CMP_LEN, HEAD_DIM), 0.02),
        'cmp_k_w1': nrm((DEPTH, CMP_LEN * HEAD_DIM, CMP_HIDDEN), (CMP_LEN * HEAD_DIM) ** -0.5),
        'cmp_k_w2': nrm((DEPTH, CMP_HIDDEN, HEAD_DIM), CMP_HIDDEN ** -0.5),
        'cmp_pos_v': nrm((DEPTH, CMP_LEN, HEAD_DIM), 0.02),
        'cmp_v_w1': nrm((DEPTH, CMP_LEN * HEAD_DIM, CMP_HIDDEN), (CMP_LEN * HEAD_DIM) ** -0.5),
        'cmp_v_w2': nrm((DEPTH, CMP_HIDDEN, HEAD_DIM), CMP_HIDDEN ** -0.5),
        'w_branch_b': nrm((DEPTH, N_HEADS * HEAD_DIM, D_MODEL), (N_HEADS * HEAD_DIM) ** -0.5),
        'w_out': nrm((DEPTH, D_MODEL, D_MODEL), D_MODEL ** -0.5),
        'ffn2_norm': gain(D_MODEL),
        'ffn2_w_in': nrm((DEPTH, D_MODEL, 2 * D_FF), D_MODEL ** -0.5),
        'ffn2_w_out': nrm((DEPTH, D_FF, D_MODEL), D_FF ** -0.5),
        'ple_norm': gain(D_MODEL),
        'ple_w_gate': nrm((DEPTH, D_MODEL, D_MODEL), D_MODEL ** -0.5),
        'ple_w_proj': nrm((DEPTH, PLE_DIM, D_MODEL), PLE_DIM ** -0.5),
        'final_norm': 1.0 + nrm((D_MODEL,), 0.02),
    }


def reference(x, p, ffn1_norm, ffn1_w_in, ffn1_w_out, mix_norm, w_in, gm_ln_g, gm_ln_b, gm_w_s, gm_b_s,
              w_branch_a, cmp_pos_k, cmp_k_w1, cmp_k_w2, cmp_pos_v, cmp_v_w1, cmp_v_w2, w_branch_b, w_out,
              ffn2_norm, ffn2_w_in, ffn2_w_out, ple_norm, ple_w_gate, ple_w_proj, final_norm):
    B, S, _ = x.shape
    offsets = np.cumsum(IN_SPLITS)[:-1].tolist()
    h = x
    for i in range(DEPTH):
        h = h + 0.5 * swiglu(rms_norm(h, ffn1_norm[i]), ffn1_w_in[i], ffn1_w_out[i])
        n = rms_norm(h, mix_norm[i])
        u, v, q, kv, nsa_gate, merge_gate = jnp.split(n @ w_in[i], offsets, axis=-1)
        y_a = gmlp_mixer(jax.nn.gelu(u), jax.nn.gelu(v), gm_ln_g[i], gm_ln_b[i], gm_w_s[i], gm_b_s[i]) @ w_branch_a[i]
        k_c, v_c, k_s, v_s, k_w, v_w = [a.reshape(B, S, N_KV_HEADS, HEAD_DIM) for a in jnp.split(kv, 6, axis=-1)]
        o_b = nsa_mixer(q.reshape(B, S, N_HEADS, HEAD_DIM), k_c, v_c, k_s, v_s, k_w, v_w,
                        nsa_gate.reshape(B, S, N_HEADS, N_NSA_BRANCH),
                        cmp_pos_k[i], cmp_k_w1[i], cmp_k_w2[i], cmp_pos_v[i], cmp_v_w1[i], cmp_v_w2[i])
        y_b = o_b @ w_branch_b[i]
        g_a, g_b = jnp.split(jax.nn.sigmoid(merge_gate), 2, axis=-1)
        h = h + (g_a * y_a + g_b * y_b) @ w_out[i]
        h = h + 0.5 * swiglu(rms_norm(h, ffn2_norm[i]), ffn2_w_in[i], ffn2_w_out[i])
        gate = jax.nn.sigmoid(rms_norm(h, ple_norm[i]) @ ple_w_gate[i])
        h = h + gate * (p[i] @ ple_w_proj[i])
    return rms_norm(h, final_norm)
```

```python
import functools

import jax
import jax.numpy as jnp
from jax import lax
from jax.experimental import pallas as pl
from jax.experimental.pallas import tpu as pltpu

D_MODEL = 1024
BATCH = 4
SEQ = 4096
PLE_DIM = 256
D_FF = 2816
NORM_EPS = 1e-6
GM_WIDTH = 1024
GM_GROUPS = 8
GM_CHUNK = 128
N_HEADS = 16
N_KV_HEADS = 4
HEAD_DIM = 64
Q_PER_KV = N_HEADS // N_KV_HEADS
KV_WIDTH = N_KV_HEADS * HEAD_DIM
ROPE_DIM = HEAD_DIM // 4
ROPE_HALF = ROPE_DIM // 2
ROPE_THETA = 500000.0
CMP_LEN = 32
CMP_STRIDE = 16
CMP_HIDDEN = 256
SEL_LEN = 64
SEL_TOP = 16
WINDOW = 512
N_NSA_BRANCH = 3
MASK_VALUE = -1e30
FORCE_SCORE = 1e9

TOKENS = BATCH * SEQ
N_CHUNKS = SEQ // CMP_STRIDE
N_SEL = SEQ // SEL_LEN
LANES = 128
VMEM_LIMIT = 56 * 1024 * 1024

OFF_U = 0
OFF_V = OFF_U + GM_WIDTH
OFF_Q = OFF_V + GM_WIDTH
OFF_KV = OFF_Q + N_HEADS * HEAD_DIM
OFF_MERGE = OFF_KV + 6 * KV_WIDTH
OFF_GATE = OFF_MERGE + 2 * D_MODEL
W_PACKED = OFF_GATE + N_KV_HEADS * LANES

F32 = jnp.float32
BF16 = jnp.bfloat16

NT_DIMS = (((1,), (1,)), ((), ()))


def _const_spec(shape):
    nd = len(shape)
    return pl.BlockSpec(shape, lambda *_: (0,) * nd, pipeline_mode=pl.Buffered(1))


def _rms(x, g):
    return x * lax.rsqrt(jnp.mean(x * x, axis=-1, keepdims=True) + NORM_EPS) * g


def _ffn_kernel(x_ref, g_ref, wg_ref, wu_ref, wo_ref, o_ref, xn_ref, acc_ref):
    j = pl.program_id(1)

    @pl.when(j == 0)
    def _():
        xn_ref[...] = _rms(x_ref[...], g_ref[...]).astype(BF16)
        acc_ref[...] = jnp.zeros_like(acc_ref)

    xn = xn_ref[...]
    gate = jnp.dot(xn, wg_ref[...], preferred_element_type=F32)
    up = jnp.dot(xn, wu_ref[...], preferred_element_type=F32)
    act = (gate * jax.nn.sigmoid(gate) * up).astype(BF16)
    acc_ref[...] += jnp.dot(act, wo_ref[...], preferred_element_type=F32)

    @pl.when(j == pl.num_programs(1) - 1)
    def _():
        o_ref[...] = x_ref[...] + 0.5 * acc_ref[...]


def _ffn(x, g, w_in, w_out, *, tm=512, tf=1408):
    nf = D_FF // tf
    return pl.pallas_call(
        _ffn_kernel,
        grid=(TOKENS // tm, nf),
        in_specs=[
            pl.BlockSpec((tm, D_MODEL), lambda i, j: (i, 0)),
            pl.BlockSpec((1, D_MODEL), lambda i, j: (0, 0)),
            pl.BlockSpec((D_MODEL, tf), lambda i, j: (0, j)),
            pl.BlockSpec((D_MODEL, tf), lambda i, j: (0, j + nf)),
            pl.BlockSpec((tf, D_MODEL), lambda i, j: (j, 0)),
        ],
        out_specs=pl.BlockSpec((tm, D_MODEL), lambda i, j: (i, 0)),
        out_shape=jax.ShapeDtypeStruct((TOKENS, D_MODEL), F32),
        scratch_shapes=[pltpu.VMEM((tm, D_MODEL), BF16), pltpu.VMEM((tm, D_MODEL), F32)],
        compiler_params=pltpu.CompilerParams(
            dimension_semantics=("parallel", "arbitrary"), vmem_limit_bytes=VMEM_LIMIT),
        name="ffn",
    )(x, g, w_in, w_in, w_out)


def _rope(x, c, sa, sb):
    w = x.shape[1]
    return x * c + pltpu.roll(x, ROPE_HALF, 1) * sa + pltpu.roll(x, w - ROPE_HALF, 1) * sb


def _inproj_kernel(h_ref, g_ref, w_ref, lng_ref, lnb_ref, c_ref, sa_ref, sb_ref,
                   gu_ref, vn_ref, qraw_ref, qrot_ref, kc_ref, vc_ref, ks_ref, vs_ref,
                   kw_ref, vw_ref, gate_ref, ga_ref, gb_ref, *, tm):
    n = _rms(h_ref[...], g_ref[...]).astype(BF16)

    def seg(lo, width):
        return jnp.dot(n, w_ref[:, lo:lo + width], preferred_element_type=F32)

    gu_ref[...] = jax.nn.gelu(seg(OFF_U, GM_WIDTH)).astype(BF16)
    v = jax.nn.gelu(seg(OFF_V, GM_WIDTH))
    mu = jnp.mean(v, axis=-1, keepdims=True)
    vc = v - mu
    var = jnp.mean(vc * vc, axis=-1, keepdims=True)
    vn_ref[...] = (vc * lax.rsqrt(var + NORM_EPS) * lng_ref[...] + lnb_ref[...]).astype(BF16)

    c, sa, sb = c_ref[...], sa_ref[...], sb_ref[...]

    def tile_lanes(t, reps):
        return jnp.concatenate([t] * reps, axis=1)

    q = seg(OFF_Q, N_HEADS * HEAD_DIM) * (HEAD_DIM ** -0.5)
    reps = N_HEADS * HEAD_DIM // LANES
    q_rot = _rope(q, tile_lanes(c, reps), tile_lanes(sa, reps), tile_lanes(sb, reps))
    zeros_h = jnp.zeros((tm, HEAD_DIM), F32)
    for h in range(N_HEADS):
        sl = slice(h * HEAD_DIM, (h + 1) * HEAD_DIM)
        qraw_ref[0, h] = q[:, sl].astype(BF16)
        qrot_ref[0, h] = jnp.concatenate([q_rot[:, sl], zeros_h], axis=1).astype(BF16)

    kv = seg(OFF_KV, 6 * KV_WIDTH)
    k_c, v_c, k_s, v_s, k_w, v_w = [kv[:, i * KV_WIDTH:(i + 1) * KV_WIDTH] for i in range(6)]
    reps = KV_WIDTH // LANES
    ck, sak, sbk = tile_lanes(c, reps), tile_lanes(sa, reps), tile_lanes(sb, reps)
    k_s = _rope(k_s, ck, sak, sbk)
    k_w = _rope(k_w, ck, sak, sbk)
    pos = (pl.program_id(0) % (SEQ // tm)) * tm + lax.broadcasted_iota(jnp.int32, (tm, N_SEL), 0)
    blk = lax.broadcasted_iota(jnp.int32, (tm, N_SEL), 1)
    onehot = jnp.where(lax.shift_right_logical(pos, 6) == blk, 1.0, 0.0).astype(F32)
    ones_h = jnp.ones((tm, HEAD_DIM), F32)
    for h in range(N_KV_HEADS):
        sl = slice(h * HEAD_DIM, (h + 1) * HEAD_DIM)
        kc_ref[0, h] = k_c[:, sl]
        vc_ref[0, h] = v_c[:, sl]
        ks_ref[0, h] = jnp.concatenate([k_s[:, sl], onehot], axis=1).astype(BF16)
        vs_ref[0, h] = jnp.concatenate([v_s[:, sl], ones_h], axis=1).astype(BF16)
        kw_ref[0, h] = k_w[:, sl].astype(BF16)
        vw_ref[0, h] = jnp.concatenate([v_w[:, sl], ones_h], axis=1).astype(BF16)

    merge = jax.nn.sigmoid(seg(OFF_MERGE, 2 * D_MODEL))
    ga_ref[...] = merge[:, :D_MODEL].astype(BF16)
    gb_ref[...] = merge[:, D_MODEL:].astype(BF16)
    for h in range(N_KV_HEADS):
        gate_ref[h] = jax.nn.sigmoid(seg(OFF_GATE + h * LANES, LANES))


def _inproj(h, g, w_packed, ln_g, ln_b, rope_c, rope_sa, rope_sb, *, tm=256):
    nq = SEQ // tm
    row = lambda i: (i, 0)
    head = lambda i: (i // nq, 0, i % nq, 0)
    tok_bf = jax.ShapeDtypeStruct((TOKENS, D_MODEL), BF16)

    def head_shape(nh, width, dtype):
        return jax.ShapeDtypeStruct((BATCH, nh, SEQ, width), dtype)

    def head_spec(nh, width):
        return pl.BlockSpec((1, nh, tm, width), head)

    return pl.pallas_call(
        functools.partial(_inproj_kernel, tm=tm),
        grid=(TOKENS // tm,),
        in_specs=[
            pl.BlockSpec((tm, D_MODEL), row),
            _const_spec((1, D_MODEL)),
            _const_spec((D_MODEL, W_PACKED)),
            _const_spec((1, GM_WIDTH)),
            _const_spec((1, GM_WIDTH)),
            pl.BlockSpec((tm, LANES), lambda i: (i % nq, 0)),
            pl.BlockSpec((tm, LANES), lambda i: (i % nq, 0)),
            pl.BlockSpec((tm, LANES), lambda i: (i % nq, 0)),
        ],
        out_specs=[
            pl.BlockSpec((tm, GM_WIDTH), row),
            pl.BlockSpec((tm, GM_WIDTH), row),
            head_spec(N_HEADS, HEAD_DIM),
            head_spec(N_HEADS, LANES),
            head_spec(N_KV_HEADS, HEAD_DIM),
            head_spec(N_KV_HEADS, HEAD_DIM),
            head_spec(N_KV_HEADS, LANES),
            head_spec(N_KV_HEADS, LANES),
            head_spec(N_KV_HEADS, HEAD_DIM),
            head_spec(N_KV_HEADS, LANES),
            pl.BlockSpec((N_KV_HEADS, tm, LANES), lambda i: (0, i, 0)),
            pl.BlockSpec((tm, D_MODEL), row),
            pl.BlockSpec((tm, D_MODEL), row),
        ],
        out_shape=[
            tok_bf, tok_bf,
            head_shape(N_HEADS, HEAD_DIM, BF16), head_shape(N_HEADS, LANES, BF16),
            head_shape(N_KV_HEADS, HEAD_DIM, F32), head_shape(N_KV_HEADS, HEAD_DIM, F32),
            head_shape(N_KV_HEADS, LANES, BF16), head_shape(N_KV_HEADS, LANES, BF16),
            head_shape(N_KV_HEADS, HEAD_DIM, BF16), head_shape(N_KV_HEADS, LANES, BF16),
            jax.ShapeDtypeStruct((N_KV_HEADS, TOKENS, LANES), F32),
            tok_bf, tok_bf,
        ],
        compiler_params=pltpu.CompilerParams(
            dimension_semantics=("parallel",), vmem_limit_bytes=VMEM_LIMIT),
        name="inproj",
    )(h, g, w_packed, ln_g, ln_b, rope_c, rope_sa, rope_sb)


def _compress_kernel(x_ref, ptop_ref, pbot_ref, w1t_ref, w1b_ref, w2_ref, o_ref):
    x = x_ref[0, 0]
    top = jnp.dot((x + ptop_ref[...]).astype(BF16), w1t_ref[...], preferred_element_type=F32)
    bot = jnp.dot((x + pbot_ref[...]).astype(BF16), w1b_ref[...], preferred_element_type=F32)
    hidden = top + pltpu.roll(bot, N_CHUNKS - 1, 0)
    o_ref[0, 0] = jnp.dot(jax.nn.gelu(hidden).astype(BF16), w2_ref[...],
                          preferred_element_type=F32).astype(BF16)


def _compress(x_chunks, pos_top, pos_bot, w1_top, w1_bot, w2):
    half = CMP_STRIDE * HEAD_DIM
    return pl.pallas_call(
        _compress_kernel,
        grid=(BATCH, N_KV_HEADS),
        in_specs=[
            pl.BlockSpec((1, 1, N_CHUNKS, half), lambda b, k: (b, k, 0, 0)),
            _const_spec((1, half)),
            _const_spec((1, half)),
            _const_spec((half, CMP_HIDDEN)),
            _const_spec((half, CMP_HIDDEN)),
            _const_spec((CMP_HIDDEN, HEAD_DIM)),
        ],
        out_specs=pl.BlockSpec((1, 1, N_CHUNKS, HEAD_DIM), lambda b, k: (b, k, 0, 0)),
        out_shape=jax.ShapeDtypeStruct((BATCH, N_KV_HEADS, N_CHUNKS, HEAD_DIM), BF16),
        compiler_params=pltpu.CompilerParams(dimension_semantics=("parallel", "parallel")),
        name="compress",
    )(x_chunks, pos_top, pos_bot, w1_top, w1_bot, w2)


NSA_Q = 256
NSA_ROWS = Q_PER_KV * NSA_Q


def _split_bf16(x):
    hi = x.astype(BF16)
    r1 = x - hi.astype(F32)
    mid = r1.astype(BF16)
    lo = (r1 - mid.astype(F32)).astype(BF16)
    return hi, mid, lo


def _nsa_kernel(qraw_ref, qrot_ref, kcmp_ref, vcmp_ref, ks_ref, vs_ref, kw_ref, vw_ref,
                gate_ref, mt_ref, o_ref, m_sc, acc_sc):
    qi = pl.program_id(2)
    s0 = qi * NSA_Q
    rows_i = lax.broadcasted_iota(jnp.int32, (NSA_ROWS, NSA_Q), 0) & (NSA_Q - 1)
    cols_i = lax.broadcasted_iota(jnp.int32, (NSA_ROWS, NSA_Q), 1)

    def softmax_step(s, v):
        m_prev = m_sc[...]
        m_new = jnp.maximum(m_prev, jnp.max(s, axis=1, keepdims=True))
        p = jnp.exp(s - m_new).astype(BF16)
        acc_sc[...] = jnp.exp(m_prev - m_new) * acc_sc[...] + jnp.dot(
            p, v, preferred_element_type=F32)
        m_sc[...] = m_new

    def softmax_reset():
        m_sc[...] = jnp.full_like(m_sc, MASK_VALUE)
        acc_sc[...] = jnp.zeros_like(acc_sc)

    def softmax_result():
        acc = acc_sc[...]
        return acc[:, :HEAD_DIM] / acc[:, HEAD_DIM:]

    q_raw = qraw_ref[0].reshape(NSA_ROWS, HEAD_DIM)
    s_c = lax.dot_general(q_raw, kcmp_ref[0, 0], NT_DIMS, preferred_element_type=F32)
    cmp_valid = cols_i * CMP_STRIDE + (CMP_LEN - 1) <= s0 + rows_i
    s_c = jnp.where(cmp_valid, s_c, MASK_VALUE)
    e_c = jnp.exp(s_c - jnp.max(s_c, axis=1, keepdims=True))
    p_c = jnp.where(cmp_valid, e_c / jnp.sum(e_c, axis=1, keepdims=True), 0.0)
    o_cmp = jnp.dot(p_c.astype(BF16), vcmp_ref[0, 0], preferred_element_type=F32)

    p_sum = p_c[0:NSA_Q]
    for g in range(1, Q_PER_KV):
        p_sum = p_sum + p_c[g * NSA_Q:(g + 1) * NSA_Q]
    mt = mt_ref[...]
    imp_t = sum(lax.dot_general(mt, part, NT_DIMS, preferred_element_type=F32)
                for part in _split_bf16(p_sum))
    blk = lax.broadcasted_iota(jnp.int32, (N_SEL, NSA_Q), 0)
    cur = lax.shift_right_logical(s0 + lax.broadcasted_iota(jnp.int32, (N_SEL, NSA_Q), 1), 6)
    forced = (blk == 0) | (blk == cur) | (blk == cur - 1)
    score = jnp.where(forced, FORCE_SCORE, jnp.where(blk > cur, -FORCE_SCORE, imp_t))
    sub = 8
    groups = [score[g * sub:(g + 1) * sub] for g in range(N_SEL // sub)]
    sub_iota = lax.broadcasted_iota(jnp.int32, (sub, NSA_Q), 0)
    ranks = [jnp.zeros((sub, NSA_Q), jnp.int32) for _ in groups]
    for jp in range(N_SEL):
        other = score[jp:jp + 1, :]
        for g, grp in enumerate(groups):
            if g * sub > jp:
                before = other >= grp
            elif g * sub + sub - 1 <= jp:
                before = other > grp
            else:
                before = (other > grp) | ((other == grp) & (sub_iota + g * sub > jp))
            ranks[g] = ranks[g] + jnp.where(before, 1, 0)
    rank = jnp.concatenate(ranks, axis=0)
    bias_t = jnp.where(rank < SEL_TOP, 0.0, MASK_VALUE)
    bias = jnp.concatenate([jnp.zeros_like(bias_t), bias_t], axis=0).T.astype(BF16)
    q_rot = qrot_ref[0].reshape(NSA_ROWS, LANES)
    q_sel = q_rot + jnp.concatenate([bias] * Q_PER_KV, axis=0)

    softmax_reset()

    def sel_tile(kt, carry):
        k0 = pl.multiple_of(kt * NSA_Q, NSA_Q)
        s = lax.dot_general(q_sel, ks_ref[0, 0, pl.ds(k0, NSA_Q), :], NT_DIMS,
                            preferred_element_type=F32)
        softmax_step(s, vs_ref[0, 0, pl.ds(k0, NSA_Q), :])
        return carry

    lax.fori_loop(0, qi, sel_tile, 0)
    d0 = pl.multiple_of(s0, NSA_Q)
    s = lax.dot_general(q_sel, ks_ref[0, 0, pl.ds(d0, NSA_Q), :], NT_DIMS,
                        preferred_element_type=F32)
    softmax_step(jnp.where(cols_i <= rows_i, s, MASK_VALUE), vs_ref[0, 0, pl.ds(d0, NSA_Q), :])
    o_sel = softmax_result()

    q_win = q_rot[:, :HEAD_DIM]
    softmax_reset()
    s = lax.dot_general(q_win, kw_ref[0, 0, pl.ds(d0, NSA_Q), :], NT_DIMS,
                        preferred_element_type=F32)
    softmax_step(jnp.where(cols_i <= rows_i, s, MASK_VALUE), vw_ref[0, 0, pl.ds(d0, NSA_Q), :])

    @pl.when(qi >= 1)
    def _():
        k0 = pl.multiple_of(s0 - NSA_Q, NSA_Q)
        s = lax.dot_general(q_win, kw_ref[0, 0, pl.ds(k0, NSA_Q), :], NT_DIMS,
                            preferred_element_type=F32)
        softmax_step(s, vw_ref[0, 0, pl.ds(k0, NSA_Q), :])

    @pl.when(qi >= 2)
    def _():
        k0 = pl.multiple_of(s0 - 2 * NSA_Q, NSA_Q)
        s = lax.dot_general(q_win, kw_ref[0, 0, pl.ds(k0, NSA_Q), :], NT_DIMS,
                            preferred_element_type=F32)
        softmax_step(jnp.where(cols_i > rows_i, s, MASK_VALUE), vw_ref[0, 0, pl.ds(k0, NSA_Q), :])

    o_win = softmax_result()

    gates = gate_ref[0]
    outs = []
    for h in range(Q_PER_KV):
        r = slice(h * NSA_Q, (h + 1) * NSA_Q)
        c = h * N_NSA_BRANCH
        outs.append(gates[:, c:c + 1] * o_cmp[r] + gates[:, c + 1:c + 2] * o_sel[r]
                    + gates[:, c + 2:c + 3] * o_win[r])
    o_ref[0] = jnp.concatenate(outs, axis=1).astype(BF16)


def _nsa(q_raw, q_rot, k_cmp, v_cmp, k_sel, v_sel, k_win, v_win, gates, overlap_t):
    nq = SEQ // NSA_Q
    qspec = lambda w: pl.BlockSpec((1, Q_PER_KV, NSA_Q, w), lambda b, k, i: (b, k, i, 0))
    whole = lambda n, w: pl.BlockSpec((1, 1, n, w), lambda b, k, i: (b, k, 0, 0))
    return pl.pallas_call(
        _nsa_kernel,
        grid=(BATCH, N_KV_HEADS, nq),
        in_specs=[
            qspec(HEAD_DIM), qspec(LANES),
            whole(N_CHUNKS, HEAD_DIM), whole(N_CHUNKS, HEAD_DIM),
            whole(SEQ, LANES), whole(SEQ, LANES), whole(SEQ, HEAD_DIM), whole(SEQ, LANES),
            pl.BlockSpec((1, NSA_Q, LANES), lambda b, k, i: (k, b * nq + i, 0)),
            _const_spec((N_SEL, N_CHUNKS)),
        ],
        out_specs=pl.BlockSpec((1, NSA_Q, Q_PER_KV * HEAD_DIM), lambda b, k, i: (b, i, k)),
        out_shape=jax.ShapeDtypeStruct((BATCH, SEQ, N_HEADS * HEAD_DIM), BF16),
        scratch_shapes=[pltpu.VMEM((NSA_ROWS, 1), F32), pltpu.VMEM((NSA_ROWS, LANES), F32)],
        compiler_params=pltpu.CompilerParams(
            dimension_semantics=("parallel", "parallel", "arbitrary"),
            vmem_limit_bytes=VMEM_LIMIT),
        name="nsa",
    )(q_raw, q_rot, k_cmp, v_cmp, k_sel, v_sel, k_win, v_win, gates, overlap_t)


def _merge_kernel(h_ref, gu_ref, vn_ref, ob_ref, ga_ref, gb_ref, ws_ref, bs_ref,
                  wa_ref, wb_ref, wo_ref, o_ref, *, tm):
    r = lax.broadcasted_iota(jnp.int32, (GM_CHUNK, GM_CHUNK), 0)
    c = lax.broadcasted_iota(jnp.int32, (GM_CHUNK, GM_CHUNK), 1)
    w_tril = [jnp.where(c <= r, ws_ref[g], 0.0).astype(BF16) for g in range(GM_GROUPS)]
    bias = bs_ref[...]
    rows = []
    for ch in range(tm // GM_CHUNK):
        rs = slice(ch * GM_CHUNK, (ch + 1) * GM_CHUNK)
        mix = jnp.concatenate(
            [jnp.dot(w_tril[g], vn_ref[rs, g * LANES:(g + 1) * LANES], preferred_element_type=F32)
             for g in range(GM_GROUPS)], axis=1)
        rows.append(gu_ref[rs, :].astype(F32) * (mix + bias))
    z = jnp.concatenate(rows, axis=0).astype(BF16)
    y_a = jnp.dot(z, wa_ref[...], preferred_element_type=F32)
    y_b = jnp.dot(ob_ref[...], wb_ref[...], preferred_element_type=F32)
    merged = (ga_ref[...].astype(F32) * y_a + gb_ref[...].astype(F32) * y_b).astype(BF16)
    o_ref[...] = h_ref[...] + jnp.dot(merged, wo_ref[...], preferred_element_type=F32)


def _merge(h, gu, vn, o_b, ga, gb, w_s, b_s_exp, w_a, w_b, w_o, *, tm=256):
    row = pl.BlockSpec((tm, D_MODEL), lambda i: (i, 0))
    return pl.pallas_call(
        functools.partial(_merge_kernel, tm=tm),
        grid=(TOKENS // tm,),
        in_specs=[row, row, row, row, row, row,
                  _const_spec((GM_GROUPS, GM_CHUNK, GM_CHUNK)),
                  _const_spec((GM_CHUNK, GM_WIDTH)),
                  _const_spec((GM_WIDTH, D_MODEL)),
                  _const_spec((N_HEADS * HEAD_DIM, D_MODEL)),
                  _const_spec((D_MODEL, D_MODEL))],
        out_specs=row,
        out_shape=jax.ShapeDtypeStruct((TOKENS, D_MODEL), F32),
        compiler_params=pltpu.CompilerParams(
            dimension_semantics=("parallel",), vmem_limit_bytes=VMEM_LIMIT),
        name="merge",
    )(h, gu, vn, o_b, ga, gb, w_s, b_s_exp, w_a, w_b, w_o)


def _ple_final_kernel(h_ref, p_ref, gp_ref, wg_ref, wp_ref, gf_ref, o_ref):
    h = h_ref[...]
    gate = jax.nn.sigmoid(jnp.dot(_rms(h, gp_ref[...]).astype(BF16), wg_ref[...],
                                  preferred_element_type=F32))
    proj = jnp.dot(p_ref[...].astype(BF16), wp_ref[...], preferred_element_type=F32)
    o_ref[...] = _rms(h + gate * proj, gf_ref[...])


def _ple_final(h, p, g_ple, w_gate, w_proj, g_final, *, tm=512):
    return pl.pallas_call(
        _ple_final_kernel,
        grid=(TOKENS // tm,),
        in_specs=[pl.BlockSpec((tm, D_MODEL), lambda i: (i, 0)),
                  pl.BlockSpec((tm, PLE_DIM), lambda i: (i, 0)),
                  _const_spec((1, D_MODEL)),
                  _const_spec((D_MODEL, D_MODEL)),
                  _const_spec((PLE_DIM, D_MODEL)),
                  _const_spec((1, D_MODEL))],
        out_specs=pl.BlockSpec((tm, D_MODEL), lambda i: (i, 0)),
        out_shape=jax.ShapeDtypeStruct((TOKENS, D_MODEL), F32),
        compiler_params=pltpu.CompilerParams(
            dimension_semantics=("parallel",), vmem_limit_bytes=VMEM_LIMIT),
        name="ple_final",
    )(h, p, g_ple, w_gate, w_proj, g_final)


def _rope_tables():
    inv_freq = ROPE_THETA ** (-jnp.arange(0, ROPE_DIM, 2, dtype=jnp.float32) / ROPE_DIM)
    ang = jnp.arange(SEQ).astype(jnp.float32)[:, None] * inv_freq[None, :]
    cos, sin = jnp.cos(ang), jnp.sin(ang)
    zero = jnp.zeros_like(cos)
    rest = HEAD_DIM - ROPE_DIM
    c = jnp.concatenate([cos, cos, jnp.ones((SEQ, rest), F32)], axis=1)
    sa = jnp.concatenate([zero, sin, jnp.zeros((SEQ, rest), F32)], axis=1)
    sb = jnp.concatenate([-sin, zero, jnp.zeros((SEQ, rest), F32)], axis=1)
    return [jnp.tile(t, (1, LANES // HEAD_DIM)) for t in (c, sa, sb)]


def _overlap_matrix_t():
    j = jnp.arange(N_SEL)[:, None]
    i = jnp.arange(N_CHUNKS)[None, :]
    r_sel = SEL_LEN // CMP_STRIDE
    l_cmp = CMP_LEN // CMP_STRIDE
    return ((i >= r_sel * j - (l_cmp - 1)) & (i <= r_sel * j + r_sel - 1)).astype(BF16)


def _pack_w_in(w_in):
    u_v_q_kv = w_in[:, :OFF_MERGE]
    gate0 = OFF_MERGE
    merge0 = gate0 + N_HEADS * N_NSA_BRANCH
    gate = w_in[:, gate0:merge0].reshape(D_MODEL, N_KV_HEADS, Q_PER_KV * N_NSA_BRANCH)
    gate = jnp.pad(gate, ((0, 0), (0, 0), (0, LANES - Q_PER_KV * N_NSA_BRANCH)))
    return jnp.concatenate(
        [u_v_q_kv, w_in[:, merge0:], gate.reshape(D_MODEL, N_KV_HEADS * LANES)], axis=1).astype(BF16)


def kernel(x, p, ffn1_norm, ffn1_w_in, ffn1_w_out, mix_norm, w_in, gm_ln_g, gm_ln_b, gm_w_s, gm_b_s,
           w_branch_a, cmp_pos_k, cmp_k_w1, cmp_k_w2, cmp_pos_v, cmp_v_w1, cmp_v_w2, w_branch_b, w_out,
           ffn2_norm, ffn2_w_in, ffn2_w_out, ple_norm, ple_w_gate, ple_w_proj, final_norm):
    assert x.shape == (BATCH, SEQ, D_MODEL) and p.shape == (1, BATCH, SEQ, PLE_DIM)
    row = lambda a: a.reshape(1, -1)
    h = x.reshape(TOKENS, D_MODEL)

    h = _ffn(h, row(ffn1_norm[0]), ffn1_w_in[0].astype(BF16), ffn1_w_out[0].astype(BF16))

    rope_c, rope_sa, rope_sb = _rope_tables()
    (gu, vn, q_raw, q_rot, k_c, v_c, k_sel, v_sel, k_win, v_win, gates, g_a, g_b) = _inproj(
        h, row(mix_norm[0]), _pack_w_in(w_in[0]), row(gm_ln_g[0]), row(gm_ln_b[0]),
        rope_c, rope_sa, rope_sb)

    half = CMP_STRIDE * HEAD_DIM

    def compress(x_heads, pos, w1, w2):
        chunks = x_heads.reshape(BATCH, N_KV_HEADS, N_CHUNKS, half)
        return _compress(chunks, pos[:CMP_STRIDE].reshape(1, half), pos[CMP_STRIDE:].reshape(1, half),
                         w1[:half].astype(BF16), w1[half:].astype(BF16), w2.astype(BF16))

    k_cmp = compress(k_c, cmp_pos_k[0], cmp_k_w1[0], cmp_k_w2[0])
    v_cmp = compress(v_c, cmp_pos_v[0], cmp_v_w1[0], cmp_v_w2[0])

    o_b = _nsa(q_raw, q_rot, k_cmp, v_cmp, k_sel, v_sel, k_win, v_win, gates, _overlap_matrix_t())

    b_s_exp = jnp.repeat(gm_b_s[0].T, GM_WIDTH // GM_GROUPS, axis=1)
    h = _merge(h, gu, vn, o_b.reshape(TOKENS, N_HEADS * HEAD_DIM), g_a, g_b, gm_w_s[0], b_s_exp,
               w_branch_a[0].astype(BF16), w_branch_b[0].astype(BF16), w_out[0].astype(BF16))

    h = _ffn(h, row(ffn2_norm[0]), ffn2_w_in[0].astype(BF16), ffn2_w_out[0].astype(BF16))

    out = _ple_final(h, p[0].reshape(TOKENS, PLE_DIM), row(ple_norm[0]), ple_w_gate[0].astype(BF16),
                     ple_w_proj[0].astype(BF16), row(final_norm))
    return out.reshape(BATCH, SEQ, D_MODEL)
```

```python
import functools

import jax
import jax.numpy as jnp
from jax import lax
from jax.experimental import pallas as pl
from jax.experimental.pallas import tpu as pltpu

D_MODEL = 1024
BATCH = 4
SEQ = 4096
PLE_DIM = 256
D_FF = 2816
NORM_EPS = 1e-6
GM_WIDTH = 1024
GM_GROUPS = 8
GM_CHUNK = 128
N_HEADS = 16
N_KV_HEADS = 4
HEAD_DIM = 64
Q_PER_KV = N_HEADS // N_KV_HEADS
KV_WIDTH = N_KV_HEADS * HEAD_DIM
ROPE_DIM = HEAD_DIM // 4
ROPE_HALF = ROPE_DIM // 2
ROPE_THETA = 500000.0
CMP_LEN = 32
CMP_STRIDE = 16
CMP_HIDDEN = 256
SEL_LEN = 64
SEL_TOP = 16
WINDOW = 512
N_NSA_BRANCH = 3
MASK_VALUE = -1e30
FORCE_SCORE = 1e9

TOKENS = BATCH * SEQ
N_CHUNKS = SEQ // CMP_STRIDE
N_SEL = SEQ // SEL_LEN
LANES = 128
VMEM_LIMIT = 56 * 1024 * 1024

OFF_U = 0
OFF_V = OFF_U + GM_WIDTH
OFF_Q = OFF_V + GM_WIDTH
OFF_KV = OFF_Q + N_HEADS * HEAD_DIM
OFF_MERGE = OFF_KV + 6 * KV_WIDTH
OFF_GATE = OFF_MERGE + 2 * D_MODEL
W_PACKED = OFF_GATE + N_KV_HEADS * LANES

F32 = jnp.float32
BF16 = jnp.bfloat16

NT_DIMS = (((1,), (1,)), ((), ()))


def _const_spec(shape):
    nd = len(shape)
    return pl.BlockSpec(shape, lambda *_: (0,) * nd, pipeline_mode=pl.Buffered(1))


def _rms(x, g):
    return x * lax.rsqrt(jnp.mean(x * x, axis=-1, keepdims=True) + NORM_EPS) * g


def _ffn_kernel(x_ref, g_ref, wg_ref, wu_ref, wo_ref, o_ref, xn_ref, acc_ref):
    j = pl.program_id(1)

    @pl.when(j == 0)
    def _():
        xn_ref[...] = _rms(x_ref[...], g_ref[...]).astype(BF16)
        acc_ref[...] = jnp.zeros_like(acc_ref)

    xn = xn_ref[...]
    gate = jnp.dot(xn, wg_ref[...], preferred_element_type=F32)
    up = jnp.dot(xn, wu_ref[...], preferred_element_type=F32)
    act = (gate * jax.nn.sigmoid(gate) * up).astype(BF16)
    acc_ref[...] += jnp.dot(act, wo_ref[...], preferred_element_type=F32)

    @pl.when(j == pl.num_programs(1) - 1)
    def _():
        o_ref[...] = x_ref[...] + 0.5 * acc_ref[...]


def _ffn(x, g, w_in, w_out, *, tm=512, tf=1408):
    nf = D_FF // tf
    return pl.pallas_call(
        _ffn_kernel,
        grid=(TOKENS // tm, nf),
        in_specs=[
            pl.BlockSpec((tm, D_MODEL), lambda i, j: (i, 0)),
            pl.BlockSpec((1, D_MODEL), lambda i, j: (0, 0)),
            pl.BlockSpec((D_MODEL, tf), lambda i, j: (0, j)),
            pl.BlockSpec((D_MODEL, tf), lambda i, j: (0, j + nf)),
            pl.BlockSpec((tf, D_MODEL), lambda i, j: (j, 0)),
        ],
        out_specs=pl.BlockSpec((tm, D_MODEL), lambda i, j: (i, 0)),
        out_shape=jax.ShapeDtypeStruct((TOKENS, D_MODEL), F32),
        scratch_shapes=[pltpu.VMEM((tm, D_MODEL), BF16), pltpu.VMEM((tm, D_MODEL), F32)],
        compiler_params=pltpu.CompilerParams(
            dimension_semantics=("parallel", "arbitrary"), vmem_limit_bytes=VMEM_LIMIT),
        name="ffn",
    )(x, g, w_in, w_in, w_out)


def _rope(x, c, sa, sb):
    w = x.shape[1]
    return x * c + pltpu.roll(x, ROPE_HALF, 1) * sa + pltpu.roll(x, w - ROPE_HALF, 1) * sb


def _inproj_kernel(h_ref, g_ref, w_ref, lng_ref, lnb_ref, c_ref, sa_ref, sb_ref,
                   gu_ref, vn_ref, qraw_ref, qrot_ref, kc_ref, vc_ref, ks_ref, vs_ref,
                   kw_ref, vw_ref, gate_ref, ga_ref, gb_ref, *, tm):
    n = _rms(h_ref[...], g_ref[...]).astype(BF16)

    def seg(lo, width):
        return jnp.dot(n, w_ref[:, lo:lo + width], preferred_element_type=F32)

    gu_ref[...] = jax.nn.gelu(seg(OFF_U, GM_WIDTH)).astype(BF16)
    v = jax.nn.gelu(seg(OFF_V, GM_WIDTH))
    mu = jnp.mean(v, axis=-1, keepdims=True)
    vc = v - mu
    var = jnp.mean(vc * vc, axis=-1, keepdims=True)
    vn_ref[...] = (vc * lax.rsqrt(var + NORM_EPS) * lng_ref[...] + lnb_ref[...]).astype(BF16)

    c, sa, sb = c_ref[...], sa_ref[...], sb_ref[...]

    def tile_lanes(t, reps):
        return jnp.concatenate([t] * reps, axis=1)

    q = seg(OFF_Q, N_HEADS * HEAD_DIM) * (HEAD_DIM ** -0.5)
    reps = N_HEADS * HEAD_DIM // LANES
    q_rot = _rope(q, tile_lanes(c, reps), tile_lanes(sa, reps), tile_lanes(sb, reps))
    qraw_ref[0] = q.T.astype(BF16).reshape(N_HEADS, HEAD_DIM, tm)
    qrot_ref[0] = q_rot.T.astype(BF16).reshape(N_HEADS, HEAD_DIM, tm)

    kv = seg(OFF_KV, 6 * KV_WIDTH)
    k_c, v_c, k_s, v_s, k_w, v_w = [kv[:, i * KV_WIDTH:(i + 1) * KV_WIDTH] for i in range(6)]
    reps = KV_WIDTH // LANES
    ck, sak, sbk = tile_lanes(c, reps), tile_lanes(sa, reps), tile_lanes(sb, reps)
    k_s = _rope(k_s, ck, sak, sbk)
    k_w = _rope(k_w, ck, sak, sbk)
    pos = (pl.program_id(0) % (SEQ // tm)) * tm + lax.broadcasted_iota(jnp.int32, (tm, N_SEL), 0)
    blk = lax.broadcasted_iota(jnp.int32, (tm, N_SEL), 1)
    onehot = jnp.where(lax.shift_right_logical(pos, 6) == blk, 1.0, 0.0).astype(F32)
    ones_t = jnp.ones((HEAD_DIM, tm), F32)
    v_s_t, v_w_t = v_s.T, v_w.T
    for h in range(N_KV_HEADS):
        sl = slice(h * HEAD_DIM, (h + 1) * HEAD_DIM)
        kc_ref[0, h] = k_c[:, sl]
        vc_ref[0, h] = v_c[:, sl]
        ks_ref[0, h] = jnp.concatenate([k_s[:, sl], onehot], axis=1).astype(BF16)
        vs_ref[0, h, 0] = jnp.concatenate([v_s_t[sl], ones_t], axis=0).astype(BF16)
        kw_ref[0, h] = k_w[:, sl].astype(BF16)
        vw_ref[0, h, 0] = jnp.concatenate([v_w_t[sl], ones_t], axis=0).astype(BF16)

    merge = jax.nn.sigmoid(seg(OFF_MERGE, 2 * D_MODEL))
    ga_ref[...] = merge[:, :D_MODEL].astype(BF16)
    gb_ref[...] = merge[:, D_MODEL:].astype(BF16)
    for h in range(N_KV_HEADS):
        gate_ref[h] = jax.nn.sigmoid(seg(OFF_GATE + h * LANES, LANES))


def _inproj(h, g, w_packed, ln_g, ln_b, rope_c, rope_sa, rope_sb, *, tm=256):
    nq = SEQ // tm
    row = lambda i: (i, 0)
    head = lambda i: (i // nq, 0, i % nq, 0)
    tok_bf = jax.ShapeDtypeStruct((TOKENS, D_MODEL), BF16)

    def head_shape(nh, width, dtype):
        return jax.ShapeDtypeStruct((BATCH, nh, SEQ, width), dtype)

    def head_spec(nh, width):
        return pl.BlockSpec((1, nh, tm, width), head)

    qt_spec = pl.BlockSpec((1, N_HEADS, HEAD_DIM, tm), lambda i: (i // nq, 0, 0, i % nq))
    qt_shape = jax.ShapeDtypeStruct((BATCH, N_HEADS, HEAD_DIM, SEQ), BF16)
    vt_spec = pl.BlockSpec((1, N_KV_HEADS, 1, 2 * HEAD_DIM, tm), lambda i: (i // nq, 0, i % nq, 0, 0))
    vt_shape = jax.ShapeDtypeStruct((BATCH, N_KV_HEADS, nq, 2 * HEAD_DIM, tm), BF16)

    return pl.pallas_call(
        functools.partial(_inproj_kernel, tm=tm),
        grid=(TOKENS // tm,),
        in_specs=[
            pl.BlockSpec((tm, D_MODEL), row),
            _const_spec((1, D_MODEL)),
            _const_spec((D_MODEL, W_PACKED)),
            _const_spec((1, GM_WIDTH)),
            _const_spec((1, GM_WIDTH)),
            pl.BlockSpec((tm, LANES), lambda i: (i % nq, 0)),
            pl.BlockSpec((tm, LANES), lambda i: (i % nq, 0)),
            pl.BlockSpec((tm, LANES), lambda i: (i % nq, 0)),
        ],
        out_specs=[
            pl.BlockSpec((tm, GM_WIDTH), row),
            pl.BlockSpec((tm, GM_WIDTH), row),
            qt_spec,
            qt_spec,
            head_spec(N_KV_HEADS, HEAD_DIM),
            head_spec(N_KV_HEADS, HEAD_DIM),
            head_spec(N_KV_HEADS, LANES),
            vt_spec,
            head_spec(N_KV_HEADS, HEAD_DIM),
            vt_spec,
            pl.BlockSpec((N_KV_HEADS, tm, LANES), lambda i: (0, i, 0)),
            pl.BlockSpec((tm, D_MODEL), row),
            pl.BlockSpec((tm, D_MODEL), row),
        ],
        out_shape=[
            tok_bf, tok_bf,
            qt_shape, qt_shape,
            head_shape(N_KV_HEADS, HEAD_DIM, F32), head_shape(N_KV_HEADS, HEAD_DIM, F32),
            head_shape(N_KV_HEADS, LANES, BF16), vt_shape,
            head_shape(N_KV_HEADS, HEAD_DIM, BF16), vt_shape,
            jax.ShapeDtypeStruct((N_KV_HEADS, TOKENS, LANES), F32),
            tok_bf, tok_bf,
        ],
        compiler_params=pltpu.CompilerParams(
            dimension_semantics=("parallel",), vmem_limit_bytes=VMEM_LIMIT),
        name="inproj",
    )(h, g, w_packed, ln_g, ln_b, rope_c, rope_sa, rope_sb)


def _compress_kernel(x_ref, ptop_ref, pbot_ref, w1t_ref, w1b_ref, w2_ref, o_ref, *, feature_major):
    x = x_ref[0, 0]
    top = jnp.dot((x + ptop_ref[...]).astype(BF16), w1t_ref[...], preferred_element_type=F32)
    bot = jnp.dot((x + pbot_ref[...]).astype(BF16), w1b_ref[...], preferred_element_type=F32)
    hidden = jax.nn.gelu(top + pltpu.roll(bot, N_CHUNKS - 1, 0)).astype(BF16)
    if feature_major:
        out = lax.dot_general(w2_ref[...], hidden, NT_DIMS, preferred_element_type=F32)
    else:
        out = jnp.dot(hidden, w2_ref[...], preferred_element_type=F32)
    o_ref[0, 0] = out.astype(BF16)


def _compress(x_chunks, pos_top, pos_bot, w1_top, w1_bot, w2, *, feature_major):
    half = CMP_STRIDE * HEAD_DIM
    out_tile = (HEAD_DIM, N_CHUNKS) if feature_major else (N_CHUNKS, HEAD_DIM)
    return pl.pallas_call(
        functools.partial(_compress_kernel, feature_major=feature_major),
        grid=(BATCH, N_KV_HEADS),
        in_specs=[
            pl.BlockSpec((1, 1, N_CHUNKS, half), lambda b, k: (b, k, 0, 0)),
            _const_spec((1, half)),
            _const_spec((1, half)),
            _const_spec((half, CMP_HIDDEN)),
            _const_spec((half, CMP_HIDDEN)),
            _const_spec(w2.shape),
        ],
        out_specs=pl.BlockSpec((1, 1) + out_tile, lambda b, k: (b, k, 0, 0)),
        out_shape=jax.ShapeDtypeStruct((BATCH, N_KV_HEADS) + out_tile, BF16),
        compiler_params=pltpu.CompilerParams(dimension_semantics=("parallel", "parallel")),
        name="compress",
    )(x_chunks, pos_top, pos_bot, w1_top, w1_bot, w2)


NSA_Q = 256


def _split_bf16(x):
    hi = x.astype(BF16)
    r1 = x - hi.astype(F32)
    mid = r1.astype(BF16)
    lo = (r1 - mid.astype(F32)).astype(BF16)
    return hi, mid, lo


def _nsa_kernel(qraw_ref, qrot_ref, kcmp_ref, vcmp_ref, ks_ref, vs_ref, kw_ref, vw_ref,
                gate_ref, mt_ref, o_ref, m_sc, acc_sc):
    qi = pl.program_id(2)
    s0 = qi * NSA_Q
    key_i = lax.broadcasted_iota(jnp.int32, (NSA_Q, NSA_Q), 0)
    qry_i = lax.broadcasted_iota(jnp.int32, (NSA_Q, NSA_Q), 1)
    heads = range(Q_PER_KV)

    def softmax_step(h, s, v_t):
        m_prev = m_sc[h]
        m_new = jnp.maximum(m_prev, jnp.max(s, axis=0, keepdims=True))
        p = jnp.exp(s - m_new).astype(BF16)
        acc_sc[h] = jnp.exp(m_prev - m_new) * acc_sc[h] + jnp.dot(v_t, p, preferred_element_type=F32)
        m_sc[h] = m_new

    def softmax_reset():
        m_sc[...] = jnp.full_like(m_sc, MASK_VALUE)
        acc_sc[...] = jnp.zeros_like(acc_sc)

    def softmax_result(h):
        acc = acc_sc[h]
        return acc[:HEAD_DIM] / acc[HEAD_DIM:HEAD_DIM + 1]

    def scores(k, q_t):
        return jnp.dot(k, q_t, preferred_element_type=F32)

    k_cmp = kcmp_ref[0, 0]
    s_cmp = [scores(k_cmp, qraw_ref[0, h]) for h in heads]
    win_tiles = [qi, jnp.maximum(qi - 1, 0), jnp.maximum(qi - 2, 0)]
    win_valid = [key_i <= qry_i,
                 key_i >= jnp.where(qi >= 1, 0, NSA_Q),
                 key_i > qry_i + jnp.where(qi >= 2, 0, NSA_Q)]
    s_win = []
    for kt in win_tiles:
        k = kw_ref[0, 0, pl.ds(pl.multiple_of(kt * NSA_Q, NSA_Q), NSA_Q), :]
        s_win.append([scores(k, qrot_ref[0, h]) for h in heads])

    v_cmp_t = vcmp_ref[0, 0]
    cmp_valid = key_i * CMP_STRIDE + (CMP_LEN - 1) <= s0 + qry_i
    p_cmp = []
    for h in heads:
        s_c = jnp.where(cmp_valid, s_cmp[h], MASK_VALUE)
        e_c = jnp.exp(s_c - jnp.max(s_c, axis=0, keepdims=True))
        p_cmp.append(jnp.where(cmp_valid, e_c / jnp.sum(e_c, axis=0, keepdims=True), 0.0))
    o_cmp = [jnp.dot(v_cmp_t, p.astype(BF16), preferred_element_type=F32) for p in p_cmp]
    p_sum = (p_cmp[0] + p_cmp[1]) + (p_cmp[2] + p_cmp[3])

    o_win = []
    for h in heads:
        s_w = [jnp.where(win_valid[t], s_win[t][h], MASK_VALUE) for t in range(3)]
        m_w = jnp.max(jnp.maximum(jnp.maximum(s_w[0], s_w[1]), s_w[2]), axis=0, keepdims=True)
        acc = sum(jnp.dot(vw_ref[0, 0, win_tiles[t]], jnp.exp(s_w[t] - m_w).astype(BF16),
                          preferred_element_type=F32) for t in range(3))
        o_win.append(acc[:HEAD_DIM] / acc[HEAD_DIM:HEAD_DIM + 1])

    mt = mt_ref[...]
    imp_t = sum(jnp.dot(mt, part, preferred_element_type=F32)
                for part in _split_bf16(p_sum))
    blk = lax.broadcasted_iota(jnp.int32, (N_SEL, NSA_Q), 0)
    cur = lax.shift_right_logical(s0 + lax.broadcasted_iota(jnp.int32, (N_SEL, NSA_Q), 1), 6)
    forced = (blk == 0) | (blk == cur) | (blk == cur - 1)
    score = jnp.where(forced, FORCE_SCORE, jnp.where(blk > cur, -FORCE_SCORE, imp_t))
    sub = 8
    groups = [score[g * sub:(g + 1) * sub] for g in range(N_SEL // sub)]
    sub_iota = lax.broadcasted_iota(jnp.int32, (sub, NSA_Q), 0)
    ranks = [jnp.zeros((sub, NSA_Q), jnp.int32) for _ in groups]
    for jp in range(N_SEL):
        other = score[jp:jp + 1, :]
        for g, grp in enumerate(groups):
            if g * sub > jp:
                before = other >= grp
            elif g * sub + sub - 1 <= jp:
                before = other > grp
            else:
                before = (other > grp) | ((other == grp) & (sub_iota + g * sub > jp))
            ranks[g] = ranks[g] + jnp.where(before, 1, 0)
    rank = jnp.concatenate(ranks, axis=0)
    bias_t = jnp.where(rank < SEL_TOP, 0.0, MASK_VALUE).astype(BF16)

    q_sel = [jnp.concatenate([qrot_ref[0, h], bias_t], axis=0) for h in heads]
    softmax_reset()

    def sel_scores(kt):
        k = ks_ref[0, 0, pl.ds(pl.multiple_of(kt * NSA_Q, NSA_Q), NSA_Q), :]
        return tuple(scores(k, q_sel[h]) for h in heads)

    def sel_tile(kt, s_cur):
        s_next = sel_scores(kt + 1)
        v_t = vs_ref[0, 0, kt]
        for h in heads:
            softmax_step(h, s_cur[h], v_t)
        return s_next

    s_diag = lax.fori_loop(0, qi, sel_tile, sel_scores(0))
    for h in heads:
        softmax_step(h, jnp.where(key_i <= qry_i, s_diag[h], MASK_VALUE), vs_ref[0, 0, qi])

    gates_t = gate_ref[0].T
    outs = []
    for h in heads:
        c = h * N_NSA_BRANCH
        outs.append(gates_t[c:c + 1] * o_cmp[h] + gates_t[c + 1:c + 2] * softmax_result(h)
                    + gates_t[c + 2:c + 3] * o_win[h])
    o_ref[0] = jnp.concatenate(outs, axis=0).T.astype(BF16)


def _nsa(q_raw_t, q_rot_t, k_cmp, v_cmp_t, k_sel, v_sel_t, k_win, v_win_t, gates, overlap_t):
    nq = SEQ // NSA_Q
    qspec = pl.BlockSpec((1, Q_PER_KV, HEAD_DIM, NSA_Q), lambda b, k, i: (b, k, 0, i))
    whole = lambda *tile: pl.BlockSpec((1, 1) + tile, lambda b, k, i: (b, k) + (0,) * len(tile))
    return pl.pallas_call(
        _nsa_kernel,
        grid=(BATCH, N_KV_HEADS, nq),
        in_specs=[
            qspec, qspec,
            whole(N_CHUNKS, HEAD_DIM), whole(HEAD_DIM, N_CHUNKS),
            whole(SEQ, LANES), whole(nq, 2 * HEAD_DIM, NSA_Q),
            whole(SEQ, HEAD_DIM), whole(nq, 2 * HEAD_DIM, NSA_Q),
            pl.BlockSpec((1, NSA_Q, LANES), lambda b, k, i: (k, b * nq + i, 0)),
            _const_spec((N_SEL, N_CHUNKS)),
        ],
        out_specs=pl.BlockSpec((1, NSA_Q, Q_PER_KV * HEAD_DIM), lambda b, k, i: (b, i, k)),
        out_shape=jax.ShapeDtypeStruct((BATCH, SEQ, N_HEADS * HEAD_DIM), BF16),
        scratch_shapes=[pltpu.VMEM((Q_PER_KV, 1, NSA_Q), F32),
                        pltpu.VMEM((Q_PER_KV, 2 * HEAD_DIM, NSA_Q), F32)],
        compiler_params=pltpu.CompilerParams(
            dimension_semantics=("parallel", "parallel", "arbitrary"),
            vmem_limit_bytes=VMEM_LIMIT),
        name="nsa",
    )(q_raw_t, q_rot_t, k_cmp, v_cmp_t, k_sel, v_sel_t, k_win, v_win_t, gates, overlap_t)


def _merge_kernel(h_ref, gu_ref, vn_ref, ob_ref, ga_ref, gb_ref, ws_ref, bs_ref,
                  wa_ref, wb_ref, wo_ref, o_ref, *, tm):
    r = lax.broadcasted_iota(jnp.int32, (GM_CHUNK, GM_CHUNK), 0)
    c = lax.broadcasted_iota(jnp.int32, (GM_CHUNK, GM_CHUNK), 1)
    w_tril = [jnp.where(c <= r, ws_ref[g], 0.0).astype(BF16) for g in range(GM_GROUPS)]
    bias = bs_ref[...]
    rows = []
    for ch in range(tm // GM_CHUNK):
        rs = slice(ch * GM_CHUNK, (ch + 1) * GM_CHUNK)
        mix = jnp.concatenate(
            [jnp.dot(w_tril[g], vn_ref[rs, g * LANES:(g + 1) * LANES], preferred_element_type=F32)
             for g in range(GM_GROUPS)], axis=1)
        rows.append(gu_ref[rs, :].astype(F32) * (mix + bias))
    z = jnp.concatenate(rows, axis=0).astype(BF16)
    y_a = jnp.dot(z, wa_ref[...], preferred_element_type=F32)
    y_b = jnp.dot(ob_ref[...], wb_ref[...], preferred_element_type=F32)
    merged = (ga_ref[...].astype(F32) * y_a + gb_ref[...].astype(F32) * y_b).astype(BF16)
    o_ref[...] = h_ref[...] + jnp.dot(merged, wo_ref[...], preferred_element_type=F32)


def _merge(h, gu, vn, o_b, ga, gb, w_s, b_s_exp, w_a, w_b, w_o, *, tm=256):
    row = pl.BlockSpec((tm, D_MODEL), lambda i: (i, 0))
    return pl.pallas_call(
        functools.partial(_merge_kernel, tm=tm),
        grid=(TOKENS // tm,),
        in_specs=[row, row, row, row, row, row,
                  _const_spec((GM_GROUPS, GM_CHUNK, GM_CHUNK)),
                  _const_spec((GM_CHUNK, GM_WIDTH)),
                  _const_spec((GM_WIDTH, D_MODEL)),
                  _const_spec((N_HEADS * HEAD_DIM, D_MODEL)),
                  _const_spec((D_MODEL, D_MODEL))],
        out_specs=row,
        out_shape=jax.ShapeDtypeStruct((TOKENS, D_MODEL), F32),
        compiler_params=pltpu.CompilerParams(
            dimension_semantics=("parallel",), vmem_limit_bytes=VMEM_LIMIT),
        name="merge",
    )(h, gu, vn, o_b, ga, gb, w_s, b_s_exp, w_a, w_b, w_o)


def _ple_final_kernel(h_ref, p_ref, gp_ref, wg_ref, wp_ref, gf_ref, o_ref):
    h = h_ref[...]
    gate = jax.nn.sigmoid(jnp.dot(_rms(h, gp_ref[...]).astype(BF16), wg_ref[...],
                                  preferred_element_type=F32))
    proj = jnp.dot(p_ref[...].astype(BF16), wp_ref[...], preferred_element_type=F32)
    o_ref[...] = _rms(h + gate * proj, gf_ref[...])


def _ple_final(h, p, g_ple, w_gate, w_proj, g_final, *, tm=512):
    return pl.pallas_call(
        _ple_final_kernel,
        grid=(TOKENS // tm,),
        in_specs=[pl.BlockSpec((tm, D_MODEL), lambda i: (i, 0)),
                  pl.BlockSpec((tm, PLE_DIM), lambda i: (i, 0)),
                  _const_spec((1, D_MODEL)),
                  _const_spec((D_MODEL, D_MODEL)),
                  _const_spec((PLE_DIM, D_MODEL)),
                  _const_spec((1, D_MODEL))],
        out_specs=pl.BlockSpec((tm, D_MODEL), lambda i: (i, 0)),
        out_shape=jax.ShapeDtypeStruct((TOKENS, D_MODEL), F32),
        compiler_params=pltpu.CompilerParams(
            dimension_semantics=("parallel",), vmem_limit_bytes=VMEM_LIMIT),
        name="ple_final",
    )(h, p, g_ple, w_gate, w_proj, g_final)


def _rope_tables():
    inv_freq = ROPE_THETA ** (-jnp.arange(0, ROPE_DIM, 2, dtype=jnp.float32) / ROPE_DIM)
    ang = jnp.arange(SEQ).astype(jnp.float32)[:, None] * inv_freq[None, :]
    cos, sin = jnp.cos(ang), jnp.sin(ang)
    zero = jnp.zeros_like(cos)
    rest = HEAD_DIM - ROPE_DIM
    c = jnp.concatenate([cos, cos, jnp.ones((SEQ, rest), F32)], axis=1)
    sa = jnp.concatenate([zero, sin, jnp.zeros((SEQ, rest), F32)], axis=1)
    sb = jnp.concatenate([-sin, zero, jnp.zeros((SEQ, rest), F32)], axis=1)
    return [jnp.tile(t, (1, LANES // HEAD_DIM)) for t in (c, sa, sb)]


def _overlap_matrix_t():
    j = jnp.arange(N_SEL)[:, None]
    i = jnp.arange(N_CHUNKS)[None, :]
    r_sel = SEL_LEN // CMP_STRIDE
    l_cmp = CMP_LEN // CMP_STRIDE
    return ((i >= r_sel * j - (l_cmp - 1)) & (i <= r_sel * j + r_sel - 1)).astype(BF16)


def _pack_w_in(w_in):
    u_v_q_kv = w_in[:, :OFF_MERGE]
    gate0 = OFF_MERGE
    merge0 = gate0 + N_HEADS * N_NSA_BRANCH
    gate = w_in[:, gate0:merge0].reshape(D_MODEL, N_KV_HEADS, Q_PER_KV * N_NSA_BRANCH)
    gate = jnp.pad(gate, ((0, 0), (0, 0), (0, LANES - Q_PER_KV * N_NSA_BRANCH)))
    return jnp.concatenate(
        [u_v_q_kv, w_in[:, merge0:], gate.reshape(D_MODEL, N_KV_HEADS * LANES)], axis=1).astype(BF16)


def kernel(x, p, ffn1_norm, ffn1_w_in, ffn1_w_out, mix_norm, w_in, gm_ln_g, gm_ln_b, gm_w_s, gm_b_s,
           w_branch_a, cmp_pos_k, cmp_k_w1, cmp_k_w2, cmp_pos_v, cmp_v_w1, cmp_v_w2, w_branch_b, w_out,
           ffn2_norm, ffn2_w_in, ffn2_w_out, ple_norm, ple_w_gate, ple_w_proj, final_norm):
    assert x.shape == (BATCH, SEQ, D_MODEL) and p.shape == (1, BATCH, SEQ, PLE_DIM)
    row = lambda a: a.reshape(1, -1)
    h = x.reshape(TOKENS, D_MODEL)

    h = _ffn(h, row(ffn1_norm[0]), ffn1_w_in[0].astype(BF16), ffn1_w_out[0].astype(BF16))

    rope_c, rope_sa, rope_sb = _rope_tables()
    (gu, vn, q_raw, q_rot, k_c, v_c, k_sel, v_sel, k_win, v_win, gates, g_a, g_b) = _inproj(
        h, row(mix_norm[0]), _pack_w_in(w_in[0]), row(gm_ln_g[0]), row(gm_ln_b[0]),
        rope_c, rope_sa, rope_sb)

    half = CMP_STRIDE * HEAD_DIM

    def compress(x_heads, pos, w1, w2, feature_major):
        chunks = x_heads.reshape(BATCH, N_KV_HEADS, N_CHUNKS, half)
        w2 = (w2.T if feature_major else w2).astype(BF16)
        return _compress(chunks, pos[:CMP_STRIDE].reshape(1, half), pos[CMP_STRIDE:].reshape(1, half),
                         w1[:half].astype(BF16), w1[half:].astype(BF16), w2, feature_major=feature_major)

    k_cmp = compress(k_c, cmp_pos_k[0], cmp_k_w1[0], cmp_k_w2[0], False)
    v_cmp = compress(v_c, cmp_pos_v[0], cmp_v_w1[0], cmp_v_w2[0], True)

    o_b = _nsa(q_raw, q_rot, k_cmp, v_cmp, k_sel, v_sel, k_win, v_win, gates, _overlap_matrix_t())

    b_s_exp = jnp.repeat(gm_b_s[0].T, GM_WIDTH // GM_GROUPS, axis=1)
    h = _merge(h, gu, vn, o_b.reshape(TOKENS, N_HEADS * HEAD_DIM), g_a, g_b, gm_w_s[0], b_s_exp,
               w_branch_a[0].astype(BF16), w_branch_b[0].astype(BF16), w_out[0].astype(BF16))

    h = _ffn(h, row(ffn2_norm[0]), ffn2_w_in[0].astype(BF16), ffn2_w_out[0].astype(BF16))

    out = _ple_final(h, p[0].reshape(TOKENS, PLE_DIM), row(ple_norm[0]), ple_w_gate[0].astype(BF16),
                     ple_w_proj[0].astype(BF16), row(final_norm))
    return out.reshape(BATCH, SEQ, D_MODEL)
```

```python
import functools

import jax
import jax.numpy as jnp
from jax import lax
from jax.experimental import pallas as pl
from jax.experimental.pallas import tpu as pltpu

D_MODEL = 1024
BATCH = 4
SEQ = 4096
PLE_DIM = 256
D_FF = 2816
NORM_EPS = 1e-6
GM_WIDTH = 1024
GM_GROUPS = 8
GM_CHUNK = 128
N_HEADS = 16
N_KV_HEADS = 4
HEAD_DIM = 64
Q_PER_KV = N_HEADS // N_KV_HEADS
KV_WIDTH = N_KV_HEADS * HEAD_DIM
ROPE_DIM = HEAD_DIM // 4
ROPE_HALF = ROPE_DIM // 2
ROPE_THETA = 500000.0
CMP_LEN = 32
CMP_STRIDE = 16
CMP_HIDDEN = 256
SEL_LEN = 64
SEL_TOP = 16
WINDOW = 512
N_NSA_BRANCH = 3
MASK_VALUE = -1e30
FORCE_SCORE = 1e9
LOG2_E = 1.4426950408889634

TOKENS = BATCH * SEQ
N_CHUNKS = SEQ // CMP_STRIDE
N_SEL = SEQ // SEL_LEN
LANES = 128
VMEM_LIMIT = 56 * 1024 * 1024

OFF_U = 0
OFF_V = OFF_U + GM_WIDTH
OFF_Q = OFF_V + GM_WIDTH
OFF_KV = OFF_Q + N_HEADS * HEAD_DIM
OFF_MERGE = OFF_KV + 6 * KV_WIDTH
OFF_GATE = OFF_MERGE + 2 * D_MODEL
W_PACKED = OFF_GATE + N_KV_HEADS * LANES

F32 = jnp.float32
BF16 = jnp.bfloat16

NT_DIMS = (((1,), (1,)), ((), ()))


def _const_spec(shape):
    nd = len(shape)
    return pl.BlockSpec(shape, lambda *_: (0,) * nd, pipeline_mode=pl.Buffered(1))


def _rms(x, g):
    return x * lax.rsqrt(jnp.mean(x * x, axis=-1, keepdims=True) + NORM_EPS) * g


def _ffn_kernel(x_ref, g_ref, wg_ref, wu_ref, wo_ref, o_ref, xn_ref, acc_ref):
    j = pl.program_id(1)

    @pl.when(j == 0)
    def _():
        xn_ref[...] = _rms(x_ref[...], g_ref[...]).astype(BF16)
        acc_ref[...] = jnp.zeros_like(acc_ref)

    xn = xn_ref[...]
    gate = jnp.dot(xn, wg_ref[...], preferred_element_type=F32)
    up = jnp.dot(xn, wu_ref[...], preferred_element_type=F32)
    act = (gate * jax.nn.sigmoid(gate) * up).astype(BF16)
    acc_ref[...] += jnp.dot(act, wo_ref[...], preferred_element_type=F32)

    @pl.when(j == pl.num_programs(1) - 1)
    def _():
        o_ref[...] = x_ref[...] + 0.5 * acc_ref[...]


def _ffn(x, g, w_in, w_out, *, tm=512, tf=1408):
    nf = D_FF // tf
    return pl.pallas_call(
        _ffn_kernel,
        grid=(TOKENS // tm, nf),
        in_specs=[
            pl.BlockSpec((tm, D_MODEL), lambda i, j: (i, 0)),
            pl.BlockSpec((1, D_MODEL), lambda i, j: (0, 0)),
            pl.BlockSpec((D_MODEL, tf), lambda i, j: (0, j)),
            pl.BlockSpec((D_MODEL, tf), lambda i, j: (0, j + nf)),
            pl.BlockSpec((tf, D_MODEL), lambda i, j: (j, 0)),
        ],
        out_specs=pl.BlockSpec((tm, D_MODEL), lambda i, j: (i, 0)),
        out_shape=jax.ShapeDtypeStruct((TOKENS, D_MODEL), F32),
        scratch_shapes=[pltpu.VMEM((tm, D_MODEL), BF16), pltpu.VMEM((tm, D_MODEL), F32)],
        compiler_params=pltpu.CompilerParams(
            dimension_semantics=("parallel", "arbitrary"), vmem_limit_bytes=VMEM_LIMIT),
        name="ffn",
    )(x, g, w_in, w_in, w_out)


def _rope(x, c, sa, sb):
    w = x.shape[1]
    return x * c + pltpu.roll(x, ROPE_HALF, 1) * sa + pltpu.roll(x, w - ROPE_HALF, 1) * sb


def _inproj_kernel(h_ref, g_ref, w_ref, lng_ref, lnb_ref, c_ref, sa_ref, sb_ref,
                   gu_ref, vn_ref, qraw_ref, qrot_ref, kc_ref, vc_ref, ks_ref, vs_ref,
                   kw_ref, vw_ref, gate_ref, ga_ref, gb_ref, *, tm):
    n = _rms(h_ref[...], g_ref[...]).astype(BF16)

    def seg(lo, width):
        return jnp.dot(n, w_ref[:, lo:lo + width], preferred_element_type=F32)

    gu_ref[...] = jax.nn.gelu(seg(OFF_U, GM_WIDTH)).astype(BF16)
    v = jax.nn.gelu(seg(OFF_V, GM_WIDTH))
    mu = jnp.mean(v, axis=-1, keepdims=True)
    vc = v - mu
    var = jnp.mean(vc * vc, axis=-1, keepdims=True)
    vn_ref[...] = (vc * lax.rsqrt(var + NORM_EPS) * lng_ref[...] + lnb_ref[...]).astype(BF16)

    c, sa, sb = c_ref[...], sa_ref[...], sb_ref[...]

    def tile_lanes(t, reps):
        return jnp.concatenate([t] * reps, axis=1)

    q = seg(OFF_Q, N_HEADS * HEAD_DIM) * (HEAD_DIM ** -0.5 * LOG2_E)
    reps = N_HEADS * HEAD_DIM // LANES
    q_rot = _rope(q, tile_lanes(c, reps), tile_lanes(sa, reps), tile_lanes(sb, reps))
    qraw_ref[0] = q.T.astype(BF16).reshape(N_HEADS, HEAD_DIM, tm)
    qrot_ref[0] = q_rot.T.astype(BF16).reshape(N_HEADS, HEAD_DIM, tm)

    kv = seg(OFF_KV, 6 * KV_WIDTH)
    k_c, v_c, k_s, v_s, k_w, v_w = [kv[:, i * KV_WIDTH:(i + 1) * KV_WIDTH] for i in range(6)]
    reps = KV_WIDTH // LANES
    ck, sak, sbk = tile_lanes(c, reps), tile_lanes(sa, reps), tile_lanes(sb, reps)
    k_s = _rope(k_s, ck, sak, sbk)
    k_w = _rope(k_w, ck, sak, sbk)
    pos = (pl.program_id(0) % (SEQ // tm)) * tm + lax.broadcasted_iota(jnp.int32, (tm, N_SEL), 0)
    blk = lax.broadcasted_iota(jnp.int32, (tm, N_SEL), 1)
    onehot = jnp.where(lax.shift_right_logical(pos, 6) == blk, 1.0, 0.0).astype(F32)
    ones_t = jnp.ones((HEAD_DIM, tm), F32)
    v_s_t, v_w_t = v_s.T, v_w.T
    for h in range(N_KV_HEADS):
        sl = slice(h * HEAD_DIM, (h + 1) * HEAD_DIM)
        kc_ref[0, h] = k_c[:, sl]
        vc_ref[0, h] = v_c[:, sl]
        ks_ref[0, h] = jnp.concatenate([k_s[:, sl], onehot], axis=1).astype(BF16)
        vs_ref[0, h, 0] = jnp.concatenate([v_s_t[sl], ones_t], axis=0).astype(BF16)
        kw_ref[0, h] = k_w[:, sl].astype(BF16)
        vw_ref[0, h, 0] = jnp.concatenate([v_w_t[sl], ones_t], axis=0).astype(BF16)

    merge = jax.nn.sigmoid(seg(OFF_MERGE, 2 * D_MODEL))
    ga_ref[...] = merge[:, :D_MODEL].astype(BF16)
    gb_ref[...] = merge[:, D_MODEL:].astype(BF16)
    for h in range(N_KV_HEADS):
        gate_ref[h] = jax.nn.sigmoid(seg(OFF_GATE + h * LANES, LANES))


def _inproj(h, g, w_packed, ln_g, ln_b, rope_c, rope_sa, rope_sb, *, tm=256):
    nq = SEQ // tm
    row = lambda i: (i, 0)
    head = lambda i: (i // nq, 0, i % nq, 0)
    tok_bf = jax.ShapeDtypeStruct((TOKENS, D_MODEL), BF16)

    def head_shape(nh, width, dtype):
        return jax.ShapeDtypeStruct((BATCH, nh, SEQ, width), dtype)

    def head_spec(nh, width):
        return pl.BlockSpec((1, nh, tm, width), head)

    qt_spec = pl.BlockSpec((1, N_HEADS, HEAD_DIM, tm), lambda i: (i // nq, 0, 0, i % nq))
    qt_shape = jax.ShapeDtypeStruct((BATCH, N_HEADS, HEAD_DIM, SEQ), BF16)
    vt_spec = pl.BlockSpec((1, N_KV_HEADS, 1, 2 * HEAD_DIM, tm), lambda i: (i // nq, 0, i % nq, 0, 0))
    vt_shape = jax.ShapeDtypeStruct((BATCH, N_KV_HEADS, nq, 2 * HEAD_DIM, tm), BF16)

    return pl.pallas_call(
        functools.partial(_inproj_kernel, tm=tm),
        grid=(TOKENS // tm,),
        in_specs=[
            pl.BlockSpec((tm, D_MODEL), row),
            _const_spec((1, D_MODEL)),
            _const_spec((D_MODEL, W_PACKED)),
            _const_spec((1, GM_WIDTH)),
            _const_spec((1, GM_WIDTH)),
            pl.BlockSpec((tm, LANES), lambda i: (i % nq, 0)),
            pl.BlockSpec((tm, LANES), lambda i: (i % nq, 0)),
            pl.BlockSpec((tm, LANES), lambda i: (i % nq, 0)),
        ],
        out_specs=[
            pl.BlockSpec((tm, GM_WIDTH), row),
            pl.BlockSpec((tm, GM_WIDTH), row),
            qt_spec,
            qt_spec,
            head_spec(N_KV_HEADS, HEAD_DIM),
            head_spec(N_KV_HEADS, HEAD_DIM),
            head_spec(N_KV_HEADS, LANES),
            vt_spec,
            head_spec(N_KV_HEADS, HEAD_DIM),
            vt_spec,
            pl.BlockSpec((N_KV_HEADS, tm, LANES), lambda i: (0, i, 0)),
            pl.BlockSpec((tm, D_MODEL), row),
            pl.BlockSpec((tm, D_MODEL), row),
        ],
        out_shape=[
            tok_bf, tok_bf,
            qt_shape, qt_shape,
            head_shape(N_KV_HEADS, HEAD_DIM, F32), head_shape(N_KV_HEADS, HEAD_DIM, F32),
            head_shape(N_KV_HEADS, LANES, BF16), vt_shape,
            head_shape(N_KV_HEADS, HEAD_DIM, BF16), vt_shape,
            jax.ShapeDtypeStruct((N_KV_HEADS, TOKENS, LANES), F32),
            tok_bf, tok_bf,
        ],
        compiler_params=pltpu.CompilerParams(
            dimension_semantics=("parallel",), vmem_limit_bytes=VMEM_LIMIT),
        name="inproj",
    )(h, g, w_packed, ln_g, ln_b, rope_c, rope_sa, rope_sb)


def _compress_kernel(x_ref, ptop_ref, pbot_ref, w1t_ref, w1b_ref, w2_ref, o_ref, *, feature_major):
    x = x_ref[0, 0]
    top = jnp.dot((x + ptop_ref[...]).astype(BF16), w1t_ref[...], preferred_element_type=F32)
    bot = jnp.dot((x + pbot_ref[...]).astype(BF16), w1b_ref[...], preferred_element_type=F32)
    hidden = jax.nn.gelu(top + pltpu.roll(bot, N_CHUNKS - 1, 0)).astype(BF16)
    if feature_major:
        out = lax.dot_general(w2_ref[...], hidden, NT_DIMS, preferred_element_type=F32)
    else:
        out = jnp.dot(hidden, w2_ref[...], preferred_element_type=F32)
    o_ref[0, 0] = out.astype(BF16)


def _compress(x_chunks, pos_top, pos_bot, w1_top, w1_bot, w2, *, feature_major):
    half = CMP_STRIDE * HEAD_DIM
    out_tile = (HEAD_DIM, N_CHUNKS) if feature_major else (N_CHUNKS, HEAD_DIM)
    return pl.pallas_call(
        functools.partial(_compress_kernel, feature_major=feature_major),
        grid=(BATCH, N_KV_HEADS),
        in_specs=[
            pl.BlockSpec((1, 1, N_CHUNKS, half), lambda b, k: (b, k, 0, 0)),
            _const_spec((1, half)),
            _const_spec((1, half)),
            _const_spec((half, CMP_HIDDEN)),
            _const_spec((half, CMP_HIDDEN)),
            _const_spec(w2.shape),
        ],
        out_specs=pl.BlockSpec((1, 1) + out_tile, lambda b, k: (b, k, 0, 0)),
        out_shape=jax.ShapeDtypeStruct((BATCH, N_KV_HEADS) + out_tile, BF16),
        compiler_params=pltpu.CompilerParams(dimension_semantics=("parallel", "parallel")),
        name="compress",
    )(x_chunks, pos_top, pos_bot, w1_top, w1_bot, w2)


NSA_Q = 256


def _split_bf16(x):
    hi = x.astype(BF16)
    r1 = x - hi.astype(F32)
    mid = r1.astype(BF16)
    lo = (r1 - mid.astype(F32)).astype(BF16)
    return hi, mid, lo


def _nsa_kernel(qraw_ref, qrot_ref, kcmp_ref, vcmp_ref, ks_ref, vs_ref, kw_ref, vw_ref,
                gate_ref, mt_ref, o_ref, m_sc, acc_sc, sa_sc, sb_sc):
    qi = pl.program_id(2)
    s0 = qi * NSA_Q
    key_i = lax.broadcasted_iota(jnp.int32, (NSA_Q, NSA_Q), 0)
    qry_i = lax.broadcasted_iota(jnp.int32, (NSA_Q, NSA_Q), 1)
    heads = range(Q_PER_KV)

    def softmax_step(h, s, v_t):
        m_prev = m_sc[h]
        m_new = jnp.maximum(m_prev, jnp.max(s, axis=0, keepdims=True))
        p = jnp.exp2(s - m_new).astype(BF16)
        acc_sc[h] = jnp.exp2(m_prev - m_new) * acc_sc[h] + jnp.dot(v_t, p, preferred_element_type=F32)
        m_sc[h] = m_new

    def softmax_reset():
        m_sc[...] = jnp.full_like(m_sc, MASK_VALUE)
        acc_sc[...] = jnp.zeros_like(acc_sc)

    def softmax_result(h):
        acc = acc_sc[h]
        return acc[:HEAD_DIM] / acc[HEAD_DIM:HEAD_DIM + 1]

    def scores(k, q_t):
        return jnp.dot(k, q_t, preferred_element_type=F32)

    k_cmp = kcmp_ref[0, 0]
    s_cmp = [scores(k_cmp, qraw_ref[0, h]) for h in heads]
    win_tiles = [qi, jnp.maximum(qi - 1, 0), jnp.maximum(qi - 2, 0)]
    win_valid = [key_i <= qry_i,
                 key_i >= jnp.where(qi >= 1, 0, NSA_Q),
                 key_i > qry_i + jnp.where(qi >= 2, 0, NSA_Q)]
    s_win = []
    for kt in win_tiles:
        k = kw_ref[0, 0, pl.ds(pl.multiple_of(kt * NSA_Q, NSA_Q), NSA_Q), :]
        s_win.append([scores(k, qrot_ref[0, h]) for h in heads])

    v_cmp_t = vcmp_ref[0, 0]
    cmp_valid = key_i * CMP_STRIDE + (CMP_LEN - 1) <= s0 + qry_i
    p_cmp = []
    for h in heads:
        s_c = jnp.where(cmp_valid, s_cmp[h], MASK_VALUE)
        e_c = jnp.exp2(s_c - jnp.max(s_c, axis=0, keepdims=True))
        p_cmp.append(jnp.where(cmp_valid, e_c / jnp.sum(e_c, axis=0, keepdims=True), 0.0))
    o_cmp = [jnp.dot(v_cmp_t, p.astype(BF16), preferred_element_type=F32) for p in p_cmp]
    p_sum = (p_cmp[0] + p_cmp[1]) + (p_cmp[2] + p_cmp[3])

    o_win = []
    for h in heads:
        s_w = [jnp.where(win_valid[t], s_win[t][h], MASK_VALUE) for t in range(3)]
        m_w = jnp.max(jnp.maximum(jnp.maximum(s_w[0], s_w[1]), s_w[2]), axis=0, keepdims=True)
        acc = sum(jnp.dot(vw_ref[0, 0, win_tiles[t]], jnp.exp2(s_w[t] - m_w).astype(BF16),
                          preferred_element_type=F32) for t in range(3))
        o_win.append(acc[:HEAD_DIM] / acc[HEAD_DIM:HEAD_DIM + 1])

    mt = mt_ref[...]
    imp_t = sum(jnp.dot(mt, part, preferred_element_type=F32)
                for part in _split_bf16(p_sum))
    blk = lax.broadcasted_iota(jnp.int32, (N_SEL, NSA_Q), 0)
    cur = lax.shift_right_logical(s0 + lax.broadcasted_iota(jnp.int32, (N_SEL, NSA_Q), 1), 6)
    forced = (blk == 0) | (blk == cur) | (blk == cur - 1)
    score = jnp.where(forced, FORCE_SCORE, jnp.where(blk > cur, -FORCE_SCORE, imp_t))
    sub = 8
    groups = [score[g * sub:(g + 1) * sub] for g in range(N_SEL // sub)]
    sub_iota = lax.broadcasted_iota(jnp.int32, (sub, NSA_Q), 0)
    ranks = [jnp.zeros((sub, NSA_Q), jnp.int32) for _ in groups]
    for jp in range(N_SEL):
        other = score[jp:jp + 1, :]
        for g, grp in enumerate(groups):
            if g * sub > jp:
                before = other >= grp
            elif g * sub + sub - 1 <= jp:
                before = other > grp
            else:
                before = (other > grp) | ((other == grp) & (sub_iota + g * sub > jp))
            ranks[g] = ranks[g] + jnp.where(before, 1, 0)
    rank = jnp.concatenate(ranks, axis=0)
    bias_t = jnp.where(rank < SEL_TOP, 0.0, MASK_VALUE).astype(BF16)

    q_sel = [jnp.concatenate([qrot_ref[0, h], bias_t], axis=0) for h in heads]
    softmax_reset()

    def sel_scores(kt, dst):
        k = ks_ref[0, 0, pl.ds(pl.multiple_of(kt * NSA_Q, NSA_Q), NSA_Q), :]
        for h in heads:
            dst[h] = scores(k, q_sel[h])

    def sel_softmax(kt, src, diagonal=False):
        v_t = vs_ref[0, 0, kt]
        for h in heads:
            s = src[h]
            softmax_step(h, jnp.where(key_i <= qry_i, s, MASK_VALUE) if diagonal else s, v_t)

    sel_scores(0, sa_sc)

    def sel_pair(t, carry):
        sel_scores(2 * t + 1, sb_sc)
        sel_softmax(2 * t, sa_sc)
        sel_scores(2 * t + 2, sa_sc)
        sel_softmax(2 * t + 1, sb_sc)
        return carry

    lax.fori_loop(0, lax.shift_right_logical(qi, 1), sel_pair, 0)

    @pl.when((qi & 1) == 1)
    def _():
        sel_scores(qi, sb_sc)
        sel_softmax(qi - 1, sa_sc)
        sel_softmax(qi, sb_sc, diagonal=True)

    @pl.when((qi & 1) == 0)
    def _():
        sel_softmax(qi, sa_sc, diagonal=True)

    gates_t = gate_ref[0].T
    outs = []
    for h in heads:
        c = h * N_NSA_BRANCH
        outs.append(gates_t[c:c + 1] * o_cmp[h] + gates_t[c + 1:c + 2] * softmax_result(h)
                    + gates_t[c + 2:c + 3] * o_win[h])
    o_ref[0] = jnp.concatenate(outs, axis=0).T.astype(BF16)


def _nsa(q_raw_t, q_rot_t, k_cmp, v_cmp_t, k_sel, v_sel_t, k_win, v_win_t, gates, overlap_t):
    nq = SEQ // NSA_Q
    qspec = pl.BlockSpec((1, Q_PER_KV, HEAD_DIM, NSA_Q), lambda b, k, i: (b, k, 0, i))
    whole = lambda *tile: pl.BlockSpec((1, 1) + tile, lambda b, k, i: (b, k) + (0,) * len(tile))
    return pl.pallas_call(
        _nsa_kernel,
        grid=(BATCH, N_KV_HEADS, nq),
        in_specs=[
            qspec, qspec,
            whole(N_CHUNKS, HEAD_DIM), whole(HEAD_DIM, N_CHUNKS),
            whole(SEQ, LANES), whole(nq, 2 * HEAD_DIM, NSA_Q),
            whole(SEQ, HEAD_DIM), whole(nq, 2 * HEAD_DIM, NSA_Q),
            pl.BlockSpec((1, NSA_Q, LANES), lambda b, k, i: (k, b * nq + i, 0)),
            _const_spec((N_SEL, N_CHUNKS)),
        ],
        out_specs=pl.BlockSpec((1, NSA_Q, Q_PER_KV * HEAD_DIM), lambda b, k, i: (b, i, k)),
        out_shape=jax.ShapeDtypeStruct((BATCH, SEQ, N_HEADS * HEAD_DIM), BF16),
        scratch_shapes=[pltpu.VMEM((Q_PER_KV, 1, NSA_Q), F32),
                        pltpu.VMEM((Q_PER_KV, 2 * HEAD_DIM, NSA_Q), F32),
                        pltpu.VMEM((Q_PER_KV, NSA_Q, NSA_Q), F32),
                        pltpu.VMEM((Q_PER_KV, NSA_Q, NSA_Q), F32)],
        compiler_params=pltpu.CompilerParams(
            dimension_semantics=("parallel", "parallel", "arbitrary"),
            vmem_limit_bytes=VMEM_LIMIT),
        name="nsa",
    )(q_raw_t, q_rot_t, k_cmp, v_cmp_t, k_sel, v_sel_t, k_win, v_win_t, gates, overlap_t)


def _merge_kernel(h_ref, gu_ref, vn_ref, ob_ref, ga_ref, gb_ref, ws_ref, bs_ref,
                  wa_ref, wb_ref, wo_ref, o_ref, *, tm):
    r = lax.broadcasted_iota(jnp.int32, (GM_CHUNK, GM_CHUNK), 0)
    c = lax.broadcasted_iota(jnp.int32, (GM_CHUNK, GM_CHUNK), 1)
    w_tril = [jnp.where(c <= r, ws_ref[g], 0.0).astype(BF16) for g in range(GM_GROUPS)]
    bias = bs_ref[...]
    rows = []
    for ch in range(tm // GM_CHUNK):
        rs = slice(ch * GM_CHUNK, (ch + 1) * GM_CHUNK)
        mix = jnp.concatenate(
            [jnp.dot(w_tril[g], vn_ref[rs, g * LANES:(g + 1) * LANES], preferred_element_type=F32)
             for g in range(GM_GROUPS)], axis=1)
        rows.append(gu_ref[rs, :].astype(F32) * (mix + bias))
    z = jnp.concatenate(rows, axis=0).astype(BF16)
    y_a = jnp.dot(z, wa_ref[...], preferred_element_type=F32)
    y_b = jnp.dot(ob_ref[...], wb_ref[...], preferred_element_type=F32)
    merged = (ga_ref[...].astype(F32) * y_a + gb_ref[...].astype(F32) * y_b).astype(BF16)
    o_ref[...] = h_ref[...] + jnp.dot(merged, wo_ref[...], preferred_element_type=F32)


def _merge(h, gu, vn, o_b, ga, gb, w_s, b_s_exp, w_a, w_b, w_o, *, tm=256):
    row = pl.BlockSpec((tm, D_MODEL), lambda i: (i, 0))
    return pl.pallas_call(
        functools.partial(_merge_kernel, tm=tm),
        grid=(TOKENS // tm,),
        in_specs=[row, row, row, row, row, row,
                  _const_spec((GM_GROUPS, GM_CHUNK, GM_CHUNK)),
                  _const_spec((GM_CHUNK, GM_WIDTH)),
                  _const_spec((GM_WIDTH, D_MODEL)),
                  _const_spec((N_HEADS * HEAD_DIM, D_MODEL)),
                  _const_spec((D_MODEL, D_MODEL))],
        out_specs=row,
        out_shape=jax.ShapeDtypeStruct((TOKENS, D_MODEL), F32),
        compiler_params=pltpu.CompilerParams(
            dimension_semantics=("parallel",), vmem_limit_bytes=VMEM_LIMIT),
        name="merge",
    )(h, gu, vn, o_b, ga, gb, w_s, b_s_exp, w_a, w_b, w_o)


def _ple_final_kernel(h_ref, p_ref, gp_ref, wg_ref, wp_ref, gf_ref, o_ref):
    h = h_ref[...]
    gate = jax.nn.sigmoid(jnp.dot(_rms(h, gp_ref[...]).astype(BF16), wg_ref[...],
                                  preferred_element_type=F32))
    proj = jnp.dot(p_ref[...].astype(BF16), wp_ref[...], preferred_element_type=F32)
    o_ref[...] = _rms(h + gate * proj, gf_ref[...])


def _ple_final(h, p, g_ple, w_gate, w_proj, g_final, *, tm=512):
    return pl.pallas_call(
        _ple_final_kernel,
        grid=(TOKENS // tm,),
        in_specs=[pl.BlockSpec((tm, D_MODEL), lambda i: (i, 0)),
                  pl.BlockSpec((tm, PLE_DIM), lambda i: (i, 0)),
                  _const_spec((1, D_MODEL)),
                  _const_spec((D_MODEL, D_MODEL)),
                  _const_spec((PLE_DIM, D_MODEL)),
                  _const_spec((1, D_MODEL))],
        out_specs=pl.BlockSpec((tm, D_MODEL), lambda i: (i, 0)),
        out_shape=jax.ShapeDtypeStruct((TOKENS, D_MODEL), F32),
        compiler_params=pltpu.CompilerParams(
            dimension_semantics=("parallel",), vmem_limit_bytes=VMEM_LIMIT),
        name="ple_final",
    )(h, p, g_ple, w_gate, w_proj, g_final)


def _rope_tables():
    inv_freq = ROPE_THETA ** (-jnp.arange(0, ROPE_DIM, 2, dtype=jnp.float32) / ROPE_DIM)
    ang = jnp.arange(SEQ).astype(jnp.float32)[:, None] * inv_freq[None, :]
    cos, sin = jnp.cos(ang), jnp.sin(ang)
    zero = jnp.zeros_like(cos)
    rest = HEAD_DIM - ROPE_DIM
    c = jnp.concatenate([cos, cos, jnp.ones((SEQ, rest), F32)], axis=1)
    sa = jnp.concatenate([zero, sin, jnp.zeros((SEQ, rest), F32)], axis=1)
    sb = jnp.concatenate([-sin, zero, jnp.zeros((SEQ, rest), F32)], axis=1)
    return [jnp.tile(t, (1, LANES // HEAD_DIM)) for t in (c, sa, sb)]


def _overlap_matrix_t():
    j = jnp.arange(N_SEL)[:, None]
    i = jnp.arange(N_CHUNKS)[None, :]
    r_sel = SEL_LEN // CMP_STRIDE
    l_cmp = CMP_LEN // CMP_STRIDE
    return ((i >= r_sel * j - (l_cmp - 1)) & (i <= r_sel * j + r_sel - 1)).astype(BF16)


def _pack_w_in(w_in):
    u_v_q_kv = w_in[:, :OFF_MERGE]
    gate0 = OFF_MERGE
    merge0 = gate0 + N_HEADS * N_NSA_BRANCH
    gate = w_in[:, gate0:merge0].reshape(D_MODEL, N_KV_HEADS, Q_PER_KV * N_NSA_BRANCH)
    gate = jnp.pad(gate, ((0, 0), (0, 0), (0, LANES - Q_PER_KV * N_NSA_BRANCH)))
    return jnp.concatenate(
        [u_v_q_kv, w_in[:, merge0:], gate.reshape(D_MODEL, N_KV_HEADS * LANES)], axis=1).astype(BF16)


def kernel(x, p, ffn1_norm, ffn1_w_in, ffn1_w_out, mix_norm, w_in, gm_ln_g, gm_ln_b, gm_w_s, gm_b_s,
           w_branch_a, cmp_pos_k, cmp_k_w1, cmp_k_w2, cmp_pos_v, cmp_v_w1, cmp_v_w2, w_branch_b, w_out,
           ffn2_norm, ffn2_w_in, ffn2_w_out, ple_norm, ple_w_gate, ple_w_proj, final_norm):
    assert x.shape == (BATCH, SEQ, D_MODEL) and p.shape == (1, BATCH, SEQ, PLE_DIM)
    row = lambda a: a.reshape(1, -1)
    h = x.reshape(TOKENS, D_MODEL)

    h = _ffn(h, row(ffn1_norm[0]), ffn1_w_in[0].astype(BF16), ffn1_w_out[0].astype(BF16))

    rope_c, rope_sa, rope_sb = _rope_tables()
    (gu, vn, q_raw, q_rot, k_c, v_c, k_sel, v_sel, k_win, v_win, gates, g_a, g_b) = _inproj(
        h, row(mix_norm[0]), _pack_w_in(w_in[0]), row(gm_ln_g[0]), row(gm_ln_b[0]),
        rope_c, rope_sa, rope_sb)

    half = CMP_STRIDE * HEAD_DIM

    def compress(x_heads, pos, w1, w2, feature_major):
        chunks = x_heads.reshape(BATCH, N_KV_HEADS, N_CHUNKS, half)
        w2 = (w2.T if feature_major else w2).astype(BF16)
        return _compress(chunks, pos[:CMP_STRIDE].reshape(1, half), pos[CMP_STRIDE:].reshape(1, half),
                         w1[:half].astype(BF16), w1[half:].astype(BF16), w2, feature_major=feature_major)

    k_cmp = compress(k_c, cmp_pos_k[0], cmp_k_w1[0], cmp_k_w2[0], False)
    v_cmp = compress(v_c, cmp_pos_v[0], cmp_v_w1[0], cmp_v_w2[0], True)

    o_b = _nsa(q_raw, q_rot, k_cmp, v_cmp, k_sel, v_sel, k_win, v_win, gates, _overlap_matrix_t())

    b_s_exp = jnp.repeat(gm_b_s[0].T, GM_WIDTH // GM_GROUPS, axis=1)
    h = _merge(h, gu, vn, o_b.reshape(TOKENS, N_HEADS * HEAD_DIM), g_a, g_b, gm_w_s[0], b_s_exp,
               w_branch_a[0].astype(BF16), w_branch_b[0].astype(BF16), w_out[0].astype(BF16))

    h = _ffn(h, row(ffn2_norm[0]), ffn2_w_in[0].astype(BF16), ffn2_w_out[0].astype(BF16))

    out = _ple_final(h, p[0].reshape(TOKENS, PLE_DIM), row(ple_norm[0]), ple_w_gate[0].astype(BF16),
                     ple_w_proj[0].astype(BF16), row(final_norm))
    return out.reshape(BATCH, SEQ, D_MODEL)
```

```python
import functools

import jax
import jax.numpy as jnp
from jax import lax
from jax.experimental import pallas as pl
from jax.experimental.pallas import tpu as pltpu

D_MODEL = 1024
BATCH = 4
SEQ = 4096
PLE_DIM = 256
D_FF = 2816
NORM_EPS = 1e-6
GM_WIDTH = 1024
GM_GROUPS = 8
GM_CHUNK = 128
N_HEADS = 16
N_KV_HEADS = 4
HEAD_DIM = 64
Q_PER_KV = N_HEADS // N_KV_HEADS
KV_WIDTH = N_KV_HEADS * HEAD_DIM
ROPE_DIM = HEAD_DIM // 4
ROPE_HALF = ROPE_DIM // 2
ROPE_THETA = 500000.0
CMP_LEN = 32
CMP_STRIDE = 16
CMP_HIDDEN = 256
SEL_LEN = 64
SEL_TOP = 16
WINDOW = 512
N_NSA_BRANCH = 3
MASK_VALUE = -1e30
FORCE_SCORE = 1e9
LOG2_E = 1.4426950408889634

TOKENS = BATCH * SEQ
N_CHUNKS = SEQ // CMP_STRIDE
N_SEL = SEQ // SEL_LEN
LANES = 128
SUBLANES = 8
VMEM_LIMIT = 56 * 1024 * 1024

OFF_U = 0
OFF_V = OFF_U + GM_WIDTH
OFF_Q = OFF_V + GM_WIDTH
OFF_KV = OFF_Q + N_HEADS * HEAD_DIM
OFF_MERGE = OFF_KV + 6 * KV_WIDTH
OFF_GATE = OFF_MERGE + 2 * D_MODEL
W_PACKED = OFF_GATE + N_KV_HEADS * LANES

F32 = jnp.float32
BF16 = jnp.bfloat16

NT_DIMS = (((1,), (1,)), ((), ()))


def _const_spec(shape):
    nd = len(shape)
    return pl.BlockSpec(shape, lambda *_: (0,) * nd, pipeline_mode=pl.Buffered(1))


def _rms(x, g):
    return x * lax.rsqrt(jnp.mean(x * x, axis=-1, keepdims=True) + NORM_EPS) * g


def _ffn_kernel(x_ref, g_ref, wg_ref, wu_ref, wo_ref, o_ref, xn_ref, acc_ref):
    j = pl.program_id(1)

    @pl.when(j == 0)
    def _():
        xn_ref[...] = _rms(x_ref[...], g_ref[...]).astype(BF16)
        acc_ref[...] = jnp.zeros_like(acc_ref)

    xn = xn_ref[...]
    gate = jnp.dot(xn, wg_ref[...], preferred_element_type=F32)
    up = jnp.dot(xn, wu_ref[...], preferred_element_type=F32)
    act = (gate * jax.nn.sigmoid(gate) * up).astype(BF16)
    acc_ref[...] += jnp.dot(act, wo_ref[...], preferred_element_type=F32)

    @pl.when(j == pl.num_programs(1) - 1)
    def _():
        o_ref[...] = x_ref[...] + 0.5 * acc_ref[...]


def _ffn(x, g, w_in, w_out, *, tm=512, tf=1408):
    nf = D_FF // tf
    return pl.pallas_call(
        _ffn_kernel,
        grid=(TOKENS // tm, nf),
        in_specs=[
            pl.BlockSpec((tm, D_MODEL), lambda i, j: (i, 0)),
            pl.BlockSpec((1, D_MODEL), lambda i, j: (0, 0)),
            pl.BlockSpec((D_MODEL, tf), lambda i, j: (0, j)),
            pl.BlockSpec((D_MODEL, tf), lambda i, j: (0, j + nf)),
            pl.BlockSpec((tf, D_MODEL), lambda i, j: (j, 0)),
        ],
        out_specs=pl.BlockSpec((tm, D_MODEL), lambda i, j: (i, 0)),
        out_shape=jax.ShapeDtypeStruct((TOKENS, D_MODEL), F32),
        scratch_shapes=[pltpu.VMEM((tm, D_MODEL), BF16), pltpu.VMEM((tm, D_MODEL), F32)],
        compiler_params=pltpu.CompilerParams(
            dimension_semantics=("parallel", "arbitrary"), vmem_limit_bytes=VMEM_LIMIT),
        name="ffn",
    )(x, g, w_in, w_in, w_out)


NSA_Q = 256
BF16_ROWS = 16
V_ROWS = HEAD_DIM + BF16_ROWS


def _rope(x, c, sa, sb):
    w = x.shape[1]
    return x * c + pltpu.roll(x, ROPE_HALF, 1) * sa + pltpu.roll(x, w - ROPE_HALF, 1) * sb


def _inproj_kernel(h_ref, g_ref, w_ref, lng_ref, lnb_ref, c_ref, sa_ref, sb_ref,
                   gu_ref, vn_ref, qraw_ref, qrot_ref, kc_ref, vc_ref, ks_ref, vs_ref,
                   kw_ref, vw_ref, gate_ref, ga_ref, gb_ref, *, tm):
    n = _rms(h_ref[...], g_ref[...]).astype(BF16)

    def seg(lo, width):
        return jnp.dot(n, w_ref[:, lo:lo + width], preferred_element_type=F32)

    gu_ref[...] = jax.nn.gelu(seg(OFF_U, GM_WIDTH)).astype(BF16)
    v = jax.nn.gelu(seg(OFF_V, GM_WIDTH))
    mu = jnp.mean(v, axis=-1, keepdims=True)
    vc = v - mu
    var = jnp.mean(vc * vc, axis=-1, keepdims=True)
    vn_ref[...] = (vc * lax.rsqrt(var + NORM_EPS) * lng_ref[...] + lnb_ref[...]).astype(BF16)

    c, sa, sb = c_ref[...], sa_ref[...], sb_ref[...]

    def tile_lanes(t, reps):
        return jnp.concatenate([t] * reps, axis=1)

    q = seg(OFF_Q, N_HEADS * HEAD_DIM) * (HEAD_DIM ** -0.5 * LOG2_E)
    reps = N_HEADS * HEAD_DIM // LANES
    q_rot = _rope(q, tile_lanes(c, reps), tile_lanes(sa, reps), tile_lanes(sb, reps))
    qraw_ref[0] = q.T.astype(BF16).reshape(N_HEADS, HEAD_DIM, tm)
    qrot_ref[0] = q_rot.T.astype(BF16).reshape(N_HEADS, HEAD_DIM, tm)

    kv = seg(OFF_KV, 6 * KV_WIDTH)
    k_c, v_c, k_s, v_s, k_w, v_w = [kv[:, i * KV_WIDTH:(i + 1) * KV_WIDTH] for i in range(6)]
    reps = KV_WIDTH // LANES
    ck, sak, sbk = tile_lanes(c, reps), tile_lanes(sa, reps), tile_lanes(sb, reps)
    k_s = _rope(k_s, ck, sak, sbk)
    k_w = _rope(k_w, ck, sak, sbk)
    pos = (pl.program_id(0) % (SEQ // tm)) * tm + lax.broadcasted_iota(jnp.int32, (tm, N_SEL), 0)
    blk = lax.broadcasted_iota(jnp.int32, (tm, N_SEL), 1)
    onehot = jnp.where(lax.shift_right_logical(pos, 6) == blk, 1.0, 0.0).astype(F32)
    ones_t = jnp.ones((V_ROWS - HEAD_DIM, tm), F32)
    ones_col = jnp.where(lax.broadcasted_iota(jnp.int32, (tm, HEAD_DIM), 1) == 0, 1.0, 0.0)
    v_s_t, v_w_t = v_s.T, v_w.T
    for h in range(N_KV_HEADS):
        sl = slice(h * HEAD_DIM, (h + 1) * HEAD_DIM)
        kc_ref[0, h] = k_c[:, sl]
        vc_ref[0, h] = v_c[:, sl]
        ks_ref[0, h] = jnp.concatenate([k_s[:, sl], onehot], axis=1).astype(BF16)
        vs_ref[0, h, 0] = jnp.concatenate([v_s_t[sl], ones_t], axis=0).astype(BF16)
        kw_ref[0, h] = jnp.concatenate([k_w[:, sl], ones_col], axis=1).astype(BF16)
        vw_ref[0, h, 0] = jnp.concatenate([v_w_t[sl], ones_t], axis=0).astype(BF16)

    merge = jax.nn.sigmoid(seg(OFF_MERGE, 2 * D_MODEL))
    ga_ref[...] = merge[:, :D_MODEL].astype(BF16)
    gb_ref[...] = merge[:, D_MODEL:].astype(BF16)
    for h in range(N_KV_HEADS):
        gate_ref[h] = jax.nn.sigmoid(seg(OFF_GATE + h * LANES, LANES))


def _inproj(h, g, w_packed, ln_g, ln_b, rope_c, rope_sa, rope_sb):
    tm = NSA_Q
    nq = SEQ // tm
    row = lambda i: (i, 0)
    head = lambda i: (i // nq, 0, i % nq, 0)
    tok_bf = jax.ShapeDtypeStruct((TOKENS, D_MODEL), BF16)

    def head_shape(nh, width, dtype):
        return jax.ShapeDtypeStruct((BATCH, nh, SEQ, width), dtype)

    def head_spec(nh, width):
        return pl.BlockSpec((1, nh, tm, width), head)

    qt_spec = pl.BlockSpec((1, N_HEADS, HEAD_DIM, tm), lambda i: (i // nq, 0, 0, i % nq))
    qt_shape = jax.ShapeDtypeStruct((BATCH, N_HEADS, HEAD_DIM, SEQ), BF16)
    vt_spec = pl.BlockSpec((1, N_KV_HEADS, 1, V_ROWS, tm), lambda i: (i // nq, 0, i % nq, 0, 0))
    vt_shape = jax.ShapeDtypeStruct((BATCH, N_KV_HEADS, nq, V_ROWS, tm), BF16)

    return pl.pallas_call(
        functools.partial(_inproj_kernel, tm=tm),
        grid=(TOKENS // tm,),
        in_specs=[
            pl.BlockSpec((tm, D_MODEL), row),
            _const_spec((1, D_MODEL)),
            _const_spec((D_MODEL, W_PACKED)),
            _const_spec((1, GM_WIDTH)),
            _const_spec((1, GM_WIDTH)),
            pl.BlockSpec((tm, LANES), lambda i: (i % nq, 0)),
            pl.BlockSpec((tm, LANES), lambda i: (i % nq, 0)),
            pl.BlockSpec((tm, LANES), lambda i: (i % nq, 0)),
        ],
        out_specs=[
            pl.BlockSpec((tm, GM_WIDTH), row),
            pl.BlockSpec((tm, GM_WIDTH), row),
            qt_spec,
            qt_spec,
            head_spec(N_KV_HEADS, HEAD_DIM),
            head_spec(N_KV_HEADS, HEAD_DIM),
            head_spec(N_KV_HEADS, LANES),
            vt_spec,
            head_spec(N_KV_HEADS, LANES),
            vt_spec,
            pl.BlockSpec((N_KV_HEADS, tm, LANES), lambda i: (0, i, 0)),
            pl.BlockSpec((tm, D_MODEL), row),
            pl.BlockSpec((tm, D_MODEL), row),
        ],
        out_shape=[
            tok_bf, tok_bf,
            qt_shape, qt_shape,
            head_shape(N_KV_HEADS, HEAD_DIM, F32), head_shape(N_KV_HEADS, HEAD_DIM, F32),
            head_shape(N_KV_HEADS, LANES, BF16), vt_shape,
            head_shape(N_KV_HEADS, LANES, BF16), vt_shape,
            jax.ShapeDtypeStruct((N_KV_HEADS, TOKENS, LANES), F32),
            tok_bf, tok_bf,
        ],
        compiler_params=pltpu.CompilerParams(
            dimension_semantics=("parallel",), vmem_limit_bytes=VMEM_LIMIT),
        name="inproj",
    )(h, g, w_packed, ln_g, ln_b, rope_c, rope_sa, rope_sb)


def _compress_kernel(x_ref, ptop_ref, pbot_ref, w1t_ref, w1b_ref, w2_ref, o_ref, *, feature_major):
    x = x_ref[0, 0]
    top = jnp.dot((x + ptop_ref[...]).astype(BF16), w1t_ref[...], preferred_element_type=F32)
    bot = jnp.dot((x + pbot_ref[...]).astype(BF16), w1b_ref[...], preferred_element_type=F32)
    hidden = jax.nn.gelu(top + pltpu.roll(bot, N_CHUNKS - 1, 0)).astype(BF16)
    if feature_major:
        out = lax.dot_general(w2_ref[...], hidden, NT_DIMS, preferred_element_type=F32)
    else:
        out = jnp.dot(hidden, w2_ref[...], preferred_element_type=F32)
    o_ref[0, 0] = out.astype(BF16)


def _compress(x_chunks, pos_top, pos_bot, w1_top, w1_bot, w2, *, feature_major):
    half = CMP_STRIDE * HEAD_DIM
    out_tile = (HEAD_DIM, N_CHUNKS) if feature_major else (N_CHUNKS, HEAD_DIM)
    return pl.pallas_call(
        functools.partial(_compress_kernel, feature_major=feature_major),
        grid=(BATCH, N_KV_HEADS),
        in_specs=[
            pl.BlockSpec((1, 1, N_CHUNKS, half), lambda b, k: (b, k, 0, 0)),
            _const_spec((1, half)),
            _const_spec((1, half)),
            _const_spec((half, CMP_HIDDEN)),
            _const_spec((half, CMP_HIDDEN)),
            _const_spec(w2.shape),
        ],
        out_specs=pl.BlockSpec((1, 1) + out_tile, lambda b, k: (b, k, 0, 0)),
        out_shape=jax.ShapeDtypeStruct((BATCH, N_KV_HEADS) + out_tile, BF16),
        compiler_params=pltpu.CompilerParams(dimension_semantics=("parallel", "parallel")),
        name="compress",
    )(x_chunks, pos_top, pos_bot, w1_top, w1_bot, w2)


RANK_STRIDE = 16
RANK_ACCUMULATORS = 4
SELECT_Q = 512


def _split_bf16(x):
    hi = x.astype(BF16)
    r1 = x - hi.astype(F32)
    mid = r1.astype(BF16)
    lo = (r1 - mid.astype(F32)).astype(BF16)
    return hi, mid, lo


def _selection_bias(score, n_blocks):
    n_q = score.shape[1]
    sub_iota = lax.broadcasted_iota(jnp.int32, (SUBLANES, n_q), 0)
    groups = [score[g * SUBLANES:(g + 1) * SUBLANES] for g in range(n_blocks // SUBLANES)]
    counts = [[None] * RANK_ACCUMULATORS for _ in groups]
    for jp in range(n_blocks):
        other = score[jp:jp + 1, :]
        for g, grp in enumerate(groups):
            lo = g * SUBLANES
            if lo > jp:
                before = other >= grp
            elif lo + SUBLANES - 1 <= jp:
                before = other > grp
            else:
                before = (other > grp) | ((other == grp) & (sub_iota + lo > jp))
            inc = jnp.where(before, 1, 0)
            a = jp % RANK_ACCUMULATORS
            counts[g][a] = inc if counts[g][a] is None else counts[g][a] + inc
    rank = jnp.concatenate([(c[0] + c[1]) + (c[2] + c[3]) for c in counts], axis=0)
    bias = jnp.where(rank < SEL_TOP, 0.0, MASK_VALUE)
    if n_blocks < N_SEL:
        bias = jnp.concatenate([bias, jnp.zeros((N_SEL - n_blocks, n_q), F32)], axis=0)
    return bias


def _select_kernel(qraw_ref, kcmp_ref, vcmp_ref, mt_ref, bias_ref, ocmp_ref):
    qi = pl.program_id(2)
    s0 = qi * SELECT_Q
    key_i = lax.broadcasted_iota(jnp.int32, (N_CHUNKS, SELECT_Q), 0)
    qry_i = lax.broadcasted_iota(jnp.int32, (N_CHUNKS, SELECT_Q), 1)
    heads = range(Q_PER_KV)

    k_cmp = kcmp_ref[0, 0]
    v_cmp_t = vcmp_ref[0, 0]
    s_cmp = [jnp.dot(k_cmp, qraw_ref[0, h], preferred_element_type=F32) for h in heads]
    cmp_valid = key_i * CMP_STRIDE + (CMP_LEN - 1) <= s0 + qry_i
    any_valid = s0 + lax.broadcasted_iota(jnp.int32, (1, SELECT_Q), 1) >= CMP_LEN - 1
    p_cmp = []
    for h in heads:
        s_c = jnp.where(cmp_valid, s_cmp[h], MASK_VALUE)
        e_c = jnp.exp2(s_c - jnp.max(s_c, axis=0, keepdims=True))
        p_cmp.append(e_c * jnp.where(any_valid, 1.0 / jnp.sum(e_c, axis=0, keepdims=True), 0.0))
    for h in heads:
        ocmp_ref[0, 0, h * HEAD_DIM:(h + 1) * HEAD_DIM, :] = jnp.dot(
            v_cmp_t, p_cmp[h].astype(BF16), preferred_element_type=F32)

    p_sum = (p_cmp[0] + p_cmp[1]) + (p_cmp[2] + p_cmp[3])
    mt = mt_ref[...]
    imp_t = sum(jnp.dot(mt, part, preferred_element_type=F32)
                for part in _split_bf16(p_sum))
    blk = lax.broadcasted_iota(jnp.int32, (N_SEL, SELECT_Q), 0)
    cur = lax.shift_right_logical(s0 + lax.broadcasted_iota(jnp.int32, (N_SEL, SELECT_Q), 1), 6)
    forced = (blk == 0) | (blk == cur) | (blk == cur - 1)
    score = jnp.where(forced, FORCE_SCORE, jnp.where(blk > cur, -FORCE_SCORE, imp_t))

    steps_per_variant = RANK_STRIDE * SEL_LEN // SELECT_Q
    for variant in range(N_SEL // RANK_STRIDE):
        @pl.when(qi // steps_per_variant == variant)
        def _():
            bias_ref[0, 0] = _selection_bias(score, (variant + 1) * RANK_STRIDE).astype(BF16)


def _select(q_raw_t, k_cmp, v_cmp_t, overlap_t):
    nq = SEQ // SELECT_Q
    step = lambda rows: pl.BlockSpec((1, 1, rows, SELECT_Q), lambda b, k, i: (b, k, 0, i))
    whole = lambda *tile: pl.BlockSpec((1, 1) + tile, lambda b, k, i: (b, k) + (0,) * len(tile))
    return pl.pallas_call(
        _select_kernel,
        grid=(BATCH, N_KV_HEADS, nq),
        in_specs=[
            pl.BlockSpec((1, Q_PER_KV, HEAD_DIM, SELECT_Q), lambda b, k, i: (b, k, 0, i)),
            whole(N_CHUNKS, HEAD_DIM), whole(HEAD_DIM, N_CHUNKS),
            _const_spec((N_SEL, N_CHUNKS)),
        ],
        out_specs=[step(N_SEL), step(Q_PER_KV * HEAD_DIM)],
        out_shape=[jax.ShapeDtypeStruct((BATCH, N_KV_HEADS, N_SEL, SEQ), BF16),
                   jax.ShapeDtypeStruct((BATCH, N_KV_HEADS, Q_PER_KV * HEAD_DIM, SEQ), F32)],
        compiler_params=pltpu.CompilerParams(
            dimension_semantics=("parallel", "parallel", "parallel"), vmem_limit_bytes=VMEM_LIMIT),
        name="select",
    )(q_raw_t, k_cmp, v_cmp_t, overlap_t)


def _attend_kernel(qrot_ref, bias_ref, ocmp_ref, ks_ref, vs_ref, kw_ref, vw_ref, gate_ref,
                   o_ref, m_sc, acc_sc, sa_sc, sb_sc, owin_sc):
    qi = pl.program_id(2)
    s0 = qi * NSA_Q
    key_i = lax.broadcasted_iota(jnp.int32, (NSA_Q, NSA_Q), 0)
    qry_i = lax.broadcasted_iota(jnp.int32, (NSA_Q, NSA_Q), 1)
    causal = key_i <= qry_i
    heads = range(Q_PER_KV)

    def scores(k, q_t):
        return jnp.dot(k, q_t, preferred_element_type=F32)

    def softmax_step(h, s, v_t):
        m_prev = m_sc[h]
        m_new = jnp.maximum(m_prev, jnp.max(s, axis=0, keepdims=True))
        p = jnp.exp2(s - m_new).astype(BF16)
        acc_sc[h] = jnp.exp2(m_prev - m_new) * acc_sc[h] + jnp.dot(v_t, p, preferred_element_type=F32)
        m_sc[h] = m_new

    bias_t = bias_ref[0, 0]
    q_sel = [jnp.concatenate([qrot_ref[0, h], bias_t], axis=0) for h in heads]
    m_sc[...] = jnp.full_like(m_sc, MASK_VALUE)
    acc_sc[...] = jnp.zeros_like(acc_sc)

    def sel_scores(kt, dst):
        k = ks_ref[0, 0, pl.ds(pl.multiple_of(kt * NSA_Q, NSA_Q), NSA_Q), :]
        for h in heads:
            dst[h] = scores(k, q_sel[h])

    def sel_softmax(kt, src, diagonal=False):
        v_t = vs_ref[0, 0, kt]
        for h in heads:
            s = src[h]
            softmax_step(h, jnp.where(causal, s, MASK_VALUE) if diagonal else s, v_t)

    sel_scores(0, sa_sc)

    def win_scores(kt, penalty):
        k = kw_ref[0, 0, pl.ds(pl.multiple_of(kt * NSA_Q, NSA_Q), NSA_Q), :]
        pen_rows = jnp.where(lax.broadcasted_iota(jnp.int32, (HEAD_DIM, NSA_Q), 0) == 0, penalty, 0.0)
        pen_rows = pen_rows.astype(BF16)
        return [scores(k, jnp.concatenate([qrot_ref[0, h], pen_rows], axis=0)) for h in heads]

    k_diag = kw_ref[0, 0, pl.ds(pl.multiple_of(s0, NSA_Q), NSA_Q), 0:HEAD_DIM]
    s_win_diag = [scores(k_diag, qrot_ref[0, h]) for h in heads]
    win_mid, win_far = jnp.maximum(qi - 1, 0), jnp.maximum(qi - 2, 0)
    s_win_mid = win_scores(win_mid, jnp.where(qi >= 1, 0.0, MASK_VALUE))
    s_win_far = win_scores(win_far, jnp.where(qi >= 2, 0.0, MASK_VALUE))
    for h in heads:
        s_edge = jnp.where(causal, s_win_diag[h], s_win_far[h])
        s_mid = s_win_mid[h]
        m_w = jnp.max(jnp.maximum(s_edge, s_mid), axis=0, keepdims=True)
        p_edge = jnp.exp2(s_edge - m_w).astype(BF16)
        p_mid = jnp.exp2(s_mid - m_w).astype(BF16)
        zero = jnp.zeros_like(p_edge)
        acc = (jnp.dot(vw_ref[0, 0, qi], jnp.where(causal, p_edge, zero), preferred_element_type=F32)
               + jnp.dot(vw_ref[0, 0, win_far], jnp.where(causal, zero, p_edge), preferred_element_type=F32)
               + jnp.dot(vw_ref[0, 0, win_mid], p_mid, preferred_element_type=F32))
        owin_sc[h] = acc[:HEAD_DIM] / acc[HEAD_DIM:HEAD_DIM + 1]

    def sel_pair(t, carry):
        sel_scores(2 * t + 1, sb_sc)
        sel_softmax(2 * t, sa_sc)
        sel_scores(2 * t + 2, sa_sc)
        sel_softmax(2 * t + 1, sb_sc)
        return carry

    lax.fori_loop(0, lax.shift_right_logical(qi, 1), sel_pair, 0)

    @pl.when((qi & 1) == 1)
    def _():
        sel_scores(qi, sb_sc)
        sel_softmax(qi - 1, sa_sc)
        sel_softmax(qi, sb_sc, diagonal=True)

    @pl.when((qi & 1) == 0)
    def _():
        sel_softmax(qi, sa_sc, diagonal=True)

    gates_t = gate_ref[0].T
    outs = []
    for h in heads:
        c = h * N_NSA_BRANCH
        acc = acc_sc[h]
        o_sel = acc[:HEAD_DIM] / acc[HEAD_DIM:HEAD_DIM + 1]
        o_cmp = ocmp_ref[0, 0, h * HEAD_DIM:(h + 1) * HEAD_DIM, :]
        outs.append(gates_t[c:c + 1] * o_cmp + gates_t[c + 1:c + 2] * o_sel
                    + gates_t[c + 2:c + 3] * owin_sc[h])
    o_ref[0] = jnp.concatenate(outs, axis=0).T.astype(BF16)


def _attend(q_rot_t, bias, o_cmp_t, k_sel, v_sel_t, k_win, v_win_t, gates):
    nq = SEQ // NSA_Q
    step = lambda rows: pl.BlockSpec((1, 1, rows, NSA_Q), lambda b, k, i: (b, k, 0, i))
    whole = lambda *tile: pl.BlockSpec((1, 1) + tile, lambda b, k, i: (b, k) + (0,) * len(tile))
    return pl.pallas_call(
        _attend_kernel,
        grid=(BATCH, N_KV_HEADS, nq),
        in_specs=[
            pl.BlockSpec((1, Q_PER_KV, HEAD_DIM, NSA_Q), lambda b, k, i: (b, k, 0, i)),
            step(N_SEL), step(Q_PER_KV * HEAD_DIM),
            whole(SEQ, LANES), whole(nq, V_ROWS, NSA_Q),
            whole(SEQ, LANES), whole(nq, V_ROWS, NSA_Q),
            pl.BlockSpec((1, NSA_Q, LANES), lambda b, k, i: (k, b * nq + i, 0)),
        ],
        out_specs=pl.BlockSpec((1, NSA_Q, Q_PER_KV * HEAD_DIM), lambda b, k, i: (b, i, k)),
        out_shape=jax.ShapeDtypeStruct((BATCH, SEQ, N_HEADS * HEAD_DIM), BF16),
        scratch_shapes=[pltpu.VMEM((Q_PER_KV, 1, NSA_Q), F32),
                        pltpu.VMEM((Q_PER_KV, V_ROWS, NSA_Q), F32),
                        pltpu.VMEM((Q_PER_KV, NSA_Q, NSA_Q), F32),
                        pltpu.VMEM((Q_PER_KV, NSA_Q, NSA_Q), F32),
                        pltpu.VMEM((Q_PER_KV, HEAD_DIM, NSA_Q), F32)],
        compiler_params=pltpu.CompilerParams(
            dimension_semantics=("parallel", "parallel", "arbitrary"),
            vmem_limit_bytes=VMEM_LIMIT),
        name="attend",
    )(q_rot_t, bias, o_cmp_t, k_sel, v_sel_t, k_win, v_win_t, gates)


def _merge_kernel(h_ref, gu_ref, vn_ref, ob_ref, ga_ref, gb_ref, ws_ref, bs_ref,
                  wa_ref, wb_ref, wo_ref, o_ref, *, tm):
    r = lax.broadcasted_iota(jnp.int32, (GM_CHUNK, GM_CHUNK), 0)
    c = lax.broadcasted_iota(jnp.int32, (GM_CHUNK, GM_CHUNK), 1)
    w_tril = [jnp.where(c <= r, ws_ref[g], 0.0).astype(BF16) for g in range(GM_GROUPS)]
    bias = bs_ref[...]
    rows = []
    for ch in range(tm // GM_CHUNK):
        rs = slice(ch * GM_CHUNK, (ch + 1) * GM_CHUNK)
        mix = jnp.concatenate(
            [jnp.dot(w_tril[g], vn_ref[rs, g * LANES:(g + 1) * LANES], preferred_element_type=F32)
             for g in range(GM_GROUPS)], axis=1)
        rows.append(gu_ref[rs, :].astype(F32) * (mix + bias))
    z = jnp.concatenate(rows, axis=0).astype(BF16)
    y_a = jnp.dot(z, wa_ref[...], preferred_element_type=F32)
    y_b = jnp.dot(ob_ref[...], wb_ref[...], preferred_element_type=F32)
    merged = (ga_ref[...].astype(F32) * y_a + gb_ref[...].astype(F32) * y_b).astype(BF16)
    o_ref[...] = h_ref[...] + jnp.dot(merged, wo_ref[...], preferred_element_type=F32)


def _merge(h, gu, vn, o_b, ga, gb, w_s, b_s_exp, w_a, w_b, w_o, *, tm=256):
    row = pl.BlockSpec((tm, D_MODEL), lambda i: (i, 0))
    return pl.pallas_call(
        functools.partial(_merge_kernel, tm=tm),
        grid=(TOKENS // tm,),
        in_specs=[row, row, row, row, row, row,
                  _const_spec((GM_GROUPS, GM_CHUNK, GM_CHUNK)),
                  _const_spec((GM_CHUNK, GM_WIDTH)),
                  _const_spec((GM_WIDTH, D_MODEL)),
                  _const_spec((N_HEADS * HEAD_DIM, D_MODEL)),
                  _const_spec((D_MODEL, D_MODEL))],
        out_specs=row,
        out_shape=jax.ShapeDtypeStruct((TOKENS, D_MODEL), F32),
        compiler_params=pltpu.CompilerParams(
            dimension_semantics=("parallel",), vmem_limit_bytes=VMEM_LIMIT),
        name="merge",
    )(h, gu, vn, o_b, ga, gb, w_s, b_s_exp, w_a, w_b, w_o)


def _ple_final_kernel(h_ref, p_ref, gp_ref, wg_ref, wp_ref, gf_ref, o_ref):
    h = h_ref[...]
    gate = jax.nn.sigmoid(jnp.dot(_rms(h, gp_ref[...]).astype(BF16), wg_ref[...],
                                  preferred_element_type=F32))
    proj = jnp.dot(p_ref[...].astype(BF16), wp_ref[...], preferred_element_type=F32)
    o_ref[...] = _rms(h + gate * proj, gf_ref[...])


def _ple_final(h, p, g_ple, w_gate, w_proj, g_final, *, tm=512):
    return pl.pallas_call(
        _ple_final_kernel,
        grid=(TOKENS // tm,),
        in_specs=[pl.BlockSpec((tm, D_MODEL), lambda i: (i, 0)),
                  pl.BlockSpec((tm, PLE_DIM), lambda i: (i, 0)),
                  _const_spec((1, D_MODEL)),
                  _const_spec((D_MODEL, D_MODEL)),
                  _const_spec((PLE_DIM, D_MODEL)),
                  _const_spec((1, D_MODEL))],
        out_specs=pl.BlockSpec((tm, D_MODEL), lambda i: (i, 0)),
        out_shape=jax.ShapeDtypeStruct((TOKENS, D_MODEL), F32),
        compiler_params=pltpu.CompilerParams(
            dimension_semantics=("parallel",), vmem_limit_bytes=VMEM_LIMIT),
        name="ple_final",
    )(h, p, g_ple, w_gate, w_proj, g_final)


def _rope_tables():
    inv_freq = ROPE_THETA ** (-jnp.arange(0, ROPE_DIM, 2, dtype=jnp.float32) / ROPE_DIM)
    ang = jnp.arange(SEQ).astype(jnp.float32)[:, None] * inv_freq[None, :]
    cos, sin = jnp.cos(ang), jnp.sin(ang)
    zero = jnp.zeros_like(cos)
    rest = HEAD_DIM - ROPE_DIM
    c = jnp.concatenate([cos, cos, jnp.ones((SEQ, rest), F32)], axis=1)
    sa = jnp.concatenate([zero, sin, jnp.zeros((SEQ, rest), F32)], axis=1)
    sb = jnp.concatenate([-sin, zero, jnp.zeros((SEQ, rest), F32)], axis=1)
    return [jnp.tile(t, (1, LANES // HEAD_DIM)) for t in (c, sa, sb)]


def _overlap_matrix_t():
    j = jnp.arange(N_SEL)[:, None]
    i = jnp.arange(N_CHUNKS)[None, :]
    r_sel = SEL_LEN // CMP_STRIDE
    l_cmp = CMP_LEN // CMP_STRIDE
    return ((i >= r_sel * j - (l_cmp - 1)) & (i <= r_sel * j + r_sel - 1)).astype(BF16)


def _pack_w_in(w_in):
    u_v_q_kv = w_in[:, :OFF_MERGE]
    gate0 = OFF_MERGE
    merge0 = gate0 + N_HEADS * N_NSA_BRANCH
    gate = w_in[:, gate0:merge0].reshape(D_MODEL, N_KV_HEADS, Q_PER_KV * N_NSA_BRANCH)
    gate = jnp.pad(gate, ((0, 0), (0, 0), (0, LANES - Q_PER_KV * N_NSA_BRANCH)))
    return jnp.concatenate(
        [u_v_q_kv, w_in[:, merge0:], gate.reshape(D_MODEL, N_KV_HEADS * LANES)], axis=1).astype(BF16)


def kernel(x, p, ffn1_norm, ffn1_w_in, ffn1_w_out, mix_norm, w_in, gm_ln_g, gm_ln_b, gm_w_s, gm_b_s,
           w_branch_a, cmp_pos_k, cmp_k_w1, cmp_k_w2, cmp_pos_v, cmp_v_w1, cmp_v_w2, w_branch_b, w_out,
           ffn2_norm, ffn2_w_in, ffn2_w_out, ple_norm, ple_w_gate, ple_w_proj, final_norm):
    assert x.shape == (BATCH, SEQ, D_MODEL) and p.shape == (1, BATCH, SEQ, PLE_DIM)
    row = lambda a: a.reshape(1, -1)
    h = x.reshape(TOKENS, D_MODEL)

    h = _ffn(h, row(ffn1_norm[0]), ffn1_w_in[0].astype(BF16), ffn1_w_out[0].astype(BF16))

    rope_c, rope_sa, rope_sb = _rope_tables()
    (gu, vn, q_raw, q_rot, k_c, v_c, k_sel, v_sel, k_win, v_win, gates, g_a, g_b) = _inproj(
        h, row(mix_norm[0]), _pack_w_in(w_in[0]), row(gm_ln_g[0]), row(gm_ln_b[0]),
        rope_c, rope_sa, rope_sb)

    half = CMP_STRIDE * HEAD_DIM

    def compress(x_heads, pos, w1, w2, feature_major):
        chunks = x_heads.reshape(BATCH, N_KV_HEADS, N_CHUNKS, half)
        w2 = (w2.T if feature_major else w2).astype(BF16)
        return _compress(chunks, pos[:CMP_STRIDE].reshape(1, half), pos[CMP_STRIDE:].reshape(1, half),
                         w1[:half].astype(BF16), w1[half:].astype(BF16), w2, feature_major=feature_major)

    k_cmp = compress(k_c, cmp_pos_k[0], cmp_k_w1[0], cmp_k_w2[0], False)
    v_cmp = compress(v_c, cmp_pos_v[0], cmp_v_w1[0], cmp_v_w2[0], True)

    sel_bias, o_cmp = _select(q_raw, k_cmp, v_cmp, _overlap_matrix_t())
    o_b = _attend(q_rot, sel_bias, o_cmp, k_sel, v_sel, k_win, v_win, gates)

    b_s_exp = jnp.repeat(gm_b_s[0].T, GM_WIDTH // GM_GROUPS, axis=1)
    h = _merge(h, gu, vn, o_b.reshape(TOKENS, N_HEADS * HEAD_DIM), g_a, g_b, gm_w_s[0], b_s_exp,
               w_branch_a[0].astype(BF16), w_branch_b[0].astype(BF16), w_out[0].astype(BF16))

    h = _ffn(h, row(ffn2_norm[0]), ffn2_w_in[0].astype(BF16), ffn2_w_out[0].astype(BF16))

    out = _ple_final(h, p[0].reshape(TOKENS, PLE_DIM), row(ple_norm[0]), ple_w_gate[0].astype(BF16),
                     ple_w_proj[0].astype(BF16), row(final_norm))
    return out.reshape(BATCH, SEQ, D_MODEL)
```

```python
import functools

import jax
import jax.numpy as jnp
from jax import lax
from jax.experimental import pallas as pl
from jax.experimental.pallas import tpu as pltpu

D_MODEL = 1024
BATCH = 4
SEQ = 4096
PLE_DIM = 256
D_FF = 2816
NORM_EPS = 1e-6
GM_WIDTH = 1024
GM_GROUPS = 8
GM_CHUNK = 128
N_HEADS = 16
N_KV_HEADS = 4
HEAD_DIM = 64
Q_PER_KV = N_HEADS // N_KV_HEADS
KV_WIDTH = N_KV_HEADS * HEAD_DIM
ROPE_DIM = HEAD_DIM // 4
ROPE_HALF = ROPE_DIM // 2
ROPE_THETA = 500000.0
CMP_LEN = 32
CMP_STRIDE = 16
CMP_HIDDEN = 256
SEL_LEN = 64
SEL_TOP = 16
WINDOW = 512
N_NSA_BRANCH = 3
MASK_VALUE = -1e30
FORCE_SCORE = 1e9
LOG2_E = 1.4426950408889634

TOKENS = BATCH * SEQ
N_CHUNKS = SEQ // CMP_STRIDE
N_SEL = SEQ // SEL_LEN
LANES = 128
SUBLANES = 8
VMEM_LIMIT = 56 * 1024 * 1024

OFF_U = 0
OFF_V = OFF_U + GM_WIDTH
OFF_Q = OFF_V + GM_WIDTH
OFF_KV = OFF_Q + N_HEADS * HEAD_DIM
OFF_MERGE = OFF_KV + 6 * KV_WIDTH
OFF_GATE = OFF_MERGE + 2 * D_MODEL
W_PACKED = OFF_GATE + N_KV_HEADS * LANES

F32 = jnp.float32
BF16 = jnp.bfloat16

NT_DIMS = (((1,), (1,)), ((), ()))


def _const_spec(shape):
    nd = len(shape)
    return pl.BlockSpec(shape, lambda *_: (0,) * nd, pipeline_mode=pl.Buffered(1))


def _rms(x, g):
    return x * lax.rsqrt(jnp.mean(x * x, axis=-1, keepdims=True) + NORM_EPS) * g


MXU_WIDTH = 256


def _half_step_ffn(x, g, wi_ref, wo_ref):
    xn = _rms(x, g).astype(BF16)
    acc = None
    for c in range(D_FF // MXU_WIDTH):
        lo = c * MXU_WIDTH
        gate = jnp.dot(xn, wi_ref[:, lo:lo + MXU_WIDTH], preferred_element_type=F32)
        up = jnp.dot(xn, wi_ref[:, D_FF + lo:D_FF + lo + MXU_WIDTH], preferred_element_type=F32)
        act = (gate * jax.nn.sigmoid(gate) * up).astype(BF16)
        part = jnp.dot(act, wo_ref[lo:lo + MXU_WIDTH, :], preferred_element_type=F32)
        acc = part if acc is None else acc + part
    return x + 0.5 * acc


def _ffn_kernel(x_ref, g_ref, wi_ref, wo_ref, o_ref):
    o_ref[...] = _half_step_ffn(x_ref[...], g_ref[...], wi_ref, wo_ref)


def _ffn_ple_final_kernel(x_ref, g_ref, wi_ref, wo_ref, p_ref, gp_ref, wg_ref, wp_ref, gf_ref, o_ref):
    h = _half_step_ffn(x_ref[...], g_ref[...], wi_ref, wo_ref)
    gate = jax.nn.sigmoid(jnp.dot(_rms(h, gp_ref[...]).astype(BF16), wg_ref[...],
                                  preferred_element_type=F32))
    proj = jnp.dot(p_ref[...].astype(BF16), wp_ref[...], preferred_element_type=F32)
    o_ref[...] = _rms(h + gate * proj, gf_ref[...])


def _ffn(x, g, w_in, w_out, ple=None, *, tm=512):
    row = pl.BlockSpec((tm, D_MODEL), lambda i: (i, 0))
    in_specs = [row, _const_spec((1, D_MODEL)), _const_spec((D_MODEL, 2 * D_FF)),
                _const_spec((D_FF, D_MODEL))]
    operands = [x, g, w_in, w_out]
    if ple is not None:
        in_specs += [pl.BlockSpec((tm, PLE_DIM), lambda i: (i, 0)), _const_spec((1, D_MODEL)),
                     _const_spec((D_MODEL, D_MODEL)), _const_spec((PLE_DIM, D_MODEL)),
                     _const_spec((1, D_MODEL))]
        operands += list(ple)
    return pl.pallas_call(
        _ffn_kernel if ple is None else _ffn_ple_final_kernel,
        grid=(TOKENS // tm,),
        in_specs=in_specs,
        out_specs=row,
        out_shape=jax.ShapeDtypeStruct((TOKENS, D_MODEL), F32),
        compiler_params=pltpu.CompilerParams(
            dimension_semantics=("parallel",), vmem_limit_bytes=VMEM_LIMIT),
        name="ffn" if ple is None else "ffn_ple_final",
    )(*operands)


NSA_Q = 256
BF16_ROWS = 16
V_ROWS = HEAD_DIM + BF16_ROWS


def _rope(x, c, sa, sb):
    w = x.shape[1]
    return x * c + pltpu.roll(x, ROPE_HALF, 1) * sa + pltpu.roll(x, w - ROPE_HALF, 1) * sb


def _inproj_kernel(h_ref, g_ref, w_ref, lng_ref, lnb_ref, c_ref, sa_ref, sb_ref,
                   gu_ref, vn_ref, qraw_ref, qrot_ref, kc_ref, vc_ref, ks_ref, vs_ref,
                   kw_ref, vw_ref, gate_ref, ga_ref, gb_ref, *, tm):
    n = _rms(h_ref[...], g_ref[...]).astype(BF16)

    def seg(lo, width):
        return jnp.dot(n, w_ref[:, lo:lo + width], preferred_element_type=F32)

    gu_ref[...] = jax.nn.gelu(seg(OFF_U, GM_WIDTH)).astype(BF16)
    v = jax.nn.gelu(seg(OFF_V, GM_WIDTH))
    mu = jnp.mean(v, axis=-1, keepdims=True)
    vc = v - mu
    var = jnp.mean(vc * vc, axis=-1, keepdims=True)
    vn_ref[...] = (vc * lax.rsqrt(var + NORM_EPS) * lng_ref[...] + lnb_ref[...]).astype(BF16)

    c, sa, sb = c_ref[...], sa_ref[...], sb_ref[...]

    def tile_lanes(t, reps):
        return jnp.concatenate([t] * reps, axis=1)

    q = seg(OFF_Q, N_HEADS * HEAD_DIM) * (HEAD_DIM ** -0.5 * LOG2_E)
    reps = N_HEADS * HEAD_DIM // LANES
    q_rot = _rope(q, tile_lanes(c, reps), tile_lanes(sa, reps), tile_lanes(sb, reps))
    qraw_ref[0] = q.T.astype(BF16).reshape(N_HEADS, HEAD_DIM, tm)
    qrot_ref[0] = q_rot.T.astype(BF16).reshape(N_HEADS, HEAD_DIM, tm)

    kv = seg(OFF_KV, 6 * KV_WIDTH)
    k_c, v_c, k_s, v_s, k_w, v_w = [kv[:, i * KV_WIDTH:(i + 1) * KV_WIDTH] for i in range(6)]
    reps = KV_WIDTH // LANES
    ck, sak, sbk = tile_lanes(c, reps), tile_lanes(sa, reps), tile_lanes(sb, reps)
    k_s = _rope(k_s, ck, sak, sbk)
    k_w = _rope(k_w, ck, sak, sbk)
    pos = (pl.program_id(0) % (SEQ // tm)) * tm + lax.broadcasted_iota(jnp.int32, (tm, N_SEL), 0)
    blk = lax.broadcasted_iota(jnp.int32, (tm, N_SEL), 1)
    onehot = jnp.where(lax.shift_right_logical(pos, 6) == blk, 1.0, 0.0).astype(F32)
    ones_t = jnp.ones((V_ROWS - HEAD_DIM, tm), F32)
    ones_col = jnp.where(lax.broadcasted_iota(jnp.int32, (tm, HEAD_DIM), 1) == 0, 1.0, 0.0)
    v_s_t, v_w_t = v_s.T, v_w.T
    for h in range(N_KV_HEADS):
        sl = slice(h * HEAD_DIM, (h + 1) * HEAD_DIM)
        kc_ref[0, h] = k_c[:, sl]
        vc_ref[0, h] = v_c[:, sl]
        ks_ref[0, h] = jnp.concatenate([k_s[:, sl], onehot], axis=1).astype(BF16)
        vs_ref[0, h, 0] = jnp.concatenate([v_s_t[sl], ones_t], axis=0).astype(BF16)
        kw_ref[0, h] = jnp.concatenate([k_w[:, sl], ones_col], axis=1).astype(BF16)
        vw_ref[0, h, 0] = jnp.concatenate([v_w_t[sl], ones_t], axis=0).astype(BF16)

    merge = jax.nn.sigmoid(seg(OFF_MERGE, 2 * D_MODEL))
    ga_ref[...] = merge[:, :D_MODEL].astype(BF16)
    gb_ref[...] = merge[:, D_MODEL:].astype(BF16)
    for h in range(N_KV_HEADS):
        gate_ref[h] = jax.nn.sigmoid(seg(OFF_GATE + h * LANES, LANES))


def _inproj(h, g, w_packed, ln_g, ln_b, rope_c, rope_sa, rope_sb):
    tm = NSA_Q
    nq = SEQ // tm
    row = lambda i: (i, 0)
    head = lambda i: (i // nq, 0, i % nq, 0)
    tok_bf = jax.ShapeDtypeStruct((TOKENS, D_MODEL), BF16)

    def head_shape(nh, width, dtype):
        return jax.ShapeDtypeStruct((BATCH, nh, SEQ, width), dtype)

    def head_spec(nh, width):
        return pl.BlockSpec((1, nh, tm, width), head)

    qt_spec = pl.BlockSpec((1, N_HEADS, HEAD_DIM, tm), lambda i: (i // nq, 0, 0, i % nq))
    qt_shape = jax.ShapeDtypeStruct((BATCH, N_HEADS, HEAD_DIM, SEQ), BF16)
    vt_spec = pl.BlockSpec((1, N_KV_HEADS, 1, V_ROWS, tm), lambda i: (i // nq, 0, i % nq, 0, 0))
    vt_shape = jax.ShapeDtypeStruct((BATCH, N_KV_HEADS, nq, V_ROWS, tm), BF16)

    return pl.pallas_call(
        functools.partial(_inproj_kernel, tm=tm),
        grid=(TOKENS // tm,),
        in_specs=[
            pl.BlockSpec((tm, D_MODEL), row),
            _const_spec((1, D_MODEL)),
            _const_spec((D_MODEL, W_PACKED)),
            _const_spec((1, GM_WIDTH)),
            _const_spec((1, GM_WIDTH)),
            pl.BlockSpec((tm, LANES), lambda i: (i % nq, 0)),
            pl.BlockSpec((tm, LANES), lambda i: (i % nq, 0)),
            pl.BlockSpec((tm, LANES), lambda i: (i % nq, 0)),
        ],
        out_specs=[
            pl.BlockSpec((tm, GM_WIDTH), row),
            pl.BlockSpec((tm, GM_WIDTH), row),
            qt_spec,
            qt_spec,
            head_spec(N_KV_HEADS, HEAD_DIM),
            head_spec(N_KV_HEADS, HEAD_DIM),
            head_spec(N_KV_HEADS, LANES),
            vt_spec,
            head_spec(N_KV_HEADS, LANES),
            vt_spec,
            pl.BlockSpec((N_KV_HEADS, tm, LANES), lambda i: (0, i, 0)),
            pl.BlockSpec((tm, D_MODEL), row),
            pl.BlockSpec((tm, D_MODEL), row),
        ],
        out_shape=[
            tok_bf, tok_bf,
            qt_shape, qt_shape,
            head_shape(N_KV_HEADS, HEAD_DIM, F32), head_shape(N_KV_HEADS, HEAD_DIM, F32),
            head_shape(N_KV_HEADS, LANES, BF16), vt_shape,
            head_shape(N_KV_HEADS, LANES, BF16), vt_shape,
            jax.ShapeDtypeStruct((N_KV_HEADS, TOKENS, LANES), F32),
            tok_bf, tok_bf,
        ],
        compiler_params=pltpu.CompilerParams(
            dimension_semantics=("parallel",), vmem_limit_bytes=VMEM_LIMIT),
        name="inproj",
    )(h, g, w_packed, ln_g, ln_b, rope_c, rope_sa, rope_sb)


def _compress_kernel(x_ref, ptop_ref, pbot_ref, w1t_ref, w1b_ref, w2_ref, o_ref, *, feature_major):
    x = x_ref[0, 0]
    top = jnp.dot((x + ptop_ref[...]).astype(BF16), w1t_ref[...], preferred_element_type=F32)
    bot = jnp.dot((x + pbot_ref[...]).astype(BF16), w1b_ref[...], preferred_element_type=F32)
    hidden = jax.nn.gelu(top + pltpu.roll(bot, N_CHUNKS - 1, 0)).astype(BF16)
    if feature_major:
        out = lax.dot_general(w2_ref[...], hidden, NT_DIMS, preferred_element_type=F32)
    else:
        out = jnp.dot(hidden, w2_ref[...], preferred_element_type=F32)
    o_ref[0, 0] = out.astype(BF16)


def _compress(x_chunks, pos_top, pos_bot, w1_top, w1_bot, w2, *, feature_major):
    half = CMP_STRIDE * HEAD_DIM
    out_tile = (HEAD_DIM, N_CHUNKS) if feature_major else (N_CHUNKS, HEAD_DIM)
    return pl.pallas_call(
        functools.partial(_compress_kernel, feature_major=feature_major),
        grid=(BATCH, N_KV_HEADS),
        in_specs=[
            pl.BlockSpec((1, 1, N_CHUNKS, half), lambda b, k: (b, k, 0, 0)),
            _const_spec((1, half)),
            _const_spec((1, half)),
            _const_spec((half, CMP_HIDDEN)),
            _const_spec((half, CMP_HIDDEN)),
            _const_spec(w2.shape),
        ],
        out_specs=pl.BlockSpec((1, 1) + out_tile, lambda b, k: (b, k, 0, 0)),
        out_shape=jax.ShapeDtypeStruct((BATCH, N_KV_HEADS) + out_tile, BF16),
        compiler_params=pltpu.CompilerParams(dimension_semantics=("parallel", "parallel")),
        name="compress",
    )(x_chunks, pos_top, pos_bot, w1_top, w1_bot, w2)


RANK_STRIDE = 16
RANK_ACCUMULATORS = 4
SELECT_Q = 512


def _split_bf16(x):
    hi = x.astype(BF16)
    r1 = x - hi.astype(F32)
    mid = r1.astype(BF16)
    lo = (r1 - mid.astype(F32)).astype(BF16)
    return hi, mid, lo


def _selection_bias(score, n_blocks):
    n_q = score.shape[1]
    sub_iota = lax.broadcasted_iota(jnp.int32, (SUBLANES, n_q), 0)
    groups = [score[g * SUBLANES:(g + 1) * SUBLANES] for g in range(n_blocks // SUBLANES)]
    counts = [[None] * RANK_ACCUMULATORS for _ in groups]
    for jp in range(n_blocks):
        other = score[jp:jp + 1, :]
        for g, grp in enumerate(groups):
            lo = g * SUBLANES
            if lo > jp:
                before = other >= grp
            elif lo + SUBLANES - 1 <= jp:
                before = other > grp
            else:
                before = (other > grp) | ((other == grp) & (sub_iota + lo > jp))
            inc = jnp.where(before, 1, 0)
            a = jp % RANK_ACCUMULATORS
            counts[g][a] = inc if counts[g][a] is None else counts[g][a] + inc
    rank = jnp.concatenate([(c[0] + c[1]) + (c[2] + c[3]) for c in counts], axis=0)
    bias = jnp.where(rank < SEL_TOP, 0.0, MASK_VALUE)
    if n_blocks < N_SEL:
        bias = jnp.concatenate([bias, jnp.zeros((N_SEL - n_blocks, n_q), F32)], axis=0)
    return bias


def _select_kernel(qraw_ref, kcmp_ref, vcmp_ref, mt_ref, bias_ref, ocmp_ref):
    qi = pl.program_id(2)
    s0 = qi * SELECT_Q
    key_i = lax.broadcasted_iota(jnp.int32, (N_CHUNKS, SELECT_Q), 0)
    qry_i = lax.broadcasted_iota(jnp.int32, (N_CHUNKS, SELECT_Q), 1)
    heads = range(Q_PER_KV)

    k_cmp = kcmp_ref[0, 0]
    v_cmp_t = vcmp_ref[0, 0]
    s_cmp = [jnp.dot(k_cmp, qraw_ref[0, h], preferred_element_type=F32) for h in heads]
    cmp_valid = key_i * CMP_STRIDE + (CMP_LEN - 1) <= s0 + qry_i
    any_valid = s0 + lax.broadcasted_iota(jnp.int32, (1, SELECT_Q), 1) >= CMP_LEN - 1
    p_cmp = []
    for h in heads:
        s_c = jnp.where(cmp_valid, s_cmp[h], MASK_VALUE)
        e_c = jnp.exp2(s_c - jnp.max(s_c, axis=0, keepdims=True))
        p_cmp.append(e_c * jnp.where(any_valid, 1.0 / jnp.sum(e_c, axis=0, keepdims=True), 0.0))
    for h in heads:
        ocmp_ref[0, 0, h * HEAD_DIM:(h + 1) * HEAD_DIM, :] = jnp.dot(
            v_cmp_t, p_cmp[h].astype(BF16), preferred_element_type=F32)

    p_sum = (p_cmp[0] + p_cmp[1]) + (p_cmp[2] + p_cmp[3])
    mt = mt_ref[...]
    imp_t = sum(jnp.dot(mt, part, preferred_element_type=F32)
                for part in _split_bf16(p_sum))
    blk = lax.broadcasted_iota(jnp.int32, (N_SEL, SELECT_Q), 0)
    cur = lax.shift_right_logical(s0 + lax.broadcasted_iota(jnp.int32, (N_SEL, SELECT_Q), 1), 6)
    forced = (blk == 0) | (blk == cur) | (blk == cur - 1)
    score = jnp.where(forced, FORCE_SCORE, jnp.where(blk > cur, -FORCE_SCORE, imp_t))

    steps_per_variant = RANK_STRIDE * SEL_LEN // SELECT_Q
    for variant in range(N_SEL // RANK_STRIDE):
        @pl.when(qi // steps_per_variant == variant)
        def _():
            bias_ref[0, 0] = _selection_bias(score, (variant + 1) * RANK_STRIDE).astype(BF16)


def _select(q_raw_t, k_cmp, v_cmp_t, overlap_t):
    nq = SEQ // SELECT_Q
    step = lambda rows: pl.BlockSpec((1, 1, rows, SELECT_Q), lambda b, k, i: (b, k, 0, i))
    whole = lambda *tile: pl.BlockSpec((1, 1) + tile, lambda b, k, i: (b, k) + (0,) * len(tile))
    return pl.pallas_call(
        _select_kernel,
        grid=(BATCH, N_KV_HEADS, nq),
        in_specs=[
            pl.BlockSpec((1, Q_PER_KV, HEAD_DIM, SELECT_Q), lambda b, k, i: (b, k, 0, i)),
            whole(N_CHUNKS, HEAD_DIM), whole(HEAD_DIM, N_CHUNKS),
            _const_spec((N_SEL, N_CHUNKS)),
        ],
        out_specs=[step(N_SEL), step(Q_PER_KV * HEAD_DIM)],
        out_shape=[jax.ShapeDtypeStruct((BATCH, N_KV_HEADS, N_SEL, SEQ), BF16),
                   jax.ShapeDtypeStruct((BATCH, N_KV_HEADS, Q_PER_KV * HEAD_DIM, SEQ), F32)],
        compiler_params=pltpu.CompilerParams(
            dimension_semantics=("parallel", "parallel", "parallel"), vmem_limit_bytes=VMEM_LIMIT),
        name="select",
    )(q_raw_t, k_cmp, v_cmp_t, overlap_t)


def _attend_kernel(qrot_ref, bias_ref, ocmp_ref, ks_ref, vs_ref, kw_ref, vw_ref, gate_ref,
                   o_ref, m_sc, acc_sc, sa_sc, sb_sc, owin_sc):
    qi = pl.program_id(2)
    s0 = qi * NSA_Q
    key_i = lax.broadcasted_iota(jnp.int32, (NSA_Q, NSA_Q), 0)
    qry_i = lax.broadcasted_iota(jnp.int32, (NSA_Q, NSA_Q), 1)
    causal = key_i <= qry_i
    heads = range(Q_PER_KV)

    def scores(k, q_t):
        return jnp.dot(k, q_t, preferred_element_type=F32)

    def softmax_step(h, s, v_t):
        m_prev = m_sc[h]
        m_new = jnp.maximum(m_prev, jnp.max(s, axis=0, keepdims=True))
        p = jnp.exp2(s - m_new).astype(BF16)
        acc_sc[h] = jnp.exp2(m_prev - m_new) * acc_sc[h] + jnp.dot(v_t, p, preferred_element_type=F32)
        m_sc[h] = m_new

    bias_t = bias_ref[0, 0]
    q_sel = [jnp.concatenate([qrot_ref[0, h], bias_t], axis=0) for h in heads]
    m_sc[...] = jnp.full_like(m_sc, MASK_VALUE)
    acc_sc[...] = jnp.zeros_like(acc_sc)

    def sel_scores(kt, dst):
        k = ks_ref[0, 0, pl.ds(pl.multiple_of(kt * NSA_Q, NSA_Q), NSA_Q), :]
        for h in heads:
            dst[h] = scores(k, q_sel[h])

    def sel_softmax(kt, src, diagonal=False):
        v_t = vs_ref[0, 0, kt]
        for h in heads:
            s = src[h]
            softmax_step(h, jnp.where(causal, s, MASK_VALUE) if diagonal else s, v_t)

    sel_scores(0, sa_sc)

    def win_scores(kt, penalty):
        k = kw_ref[0, 0, pl.ds(pl.multiple_of(kt * NSA_Q, NSA_Q), NSA_Q), :]
        pen_rows = jnp.where(lax.broadcasted_iota(jnp.int32, (HEAD_DIM, NSA_Q), 0) == 0, penalty, 0.0)
        pen_rows = pen_rows.astype(BF16)
        return [scores(k, jnp.concatenate([qrot_ref[0, h], pen_rows], axis=0)) for h in heads]

    k_diag = kw_ref[0, 0, pl.ds(pl.multiple_of(s0, NSA_Q), NSA_Q), 0:HEAD_DIM]
    s_win_diag = [scores(k_diag, qrot_ref[0, h]) for h in heads]
    win_mid, win_far = jnp.maximum(qi - 1, 0), jnp.maximum(qi - 2, 0)
    s_win_mid = win_scores(win_mid, jnp.where(qi >= 1, 0.0, MASK_VALUE))
    s_win_far = win_scores(win_far, jnp.where(qi >= 2, 0.0, MASK_VALUE))
    for h in heads:
        s_edge = jnp.where(causal, s_win_diag[h], s_win_far[h])
        s_mid = s_win_mid[h]
        m_w = jnp.max(jnp.maximum(s_edge, s_mid), axis=0, keepdims=True)
        p_edge = jnp.exp2(s_edge - m_w).astype(BF16)
        p_mid = jnp.exp2(s_mid - m_w).astype(BF16)
        zero = jnp.zeros_like(p_edge)
        acc = (jnp.dot(vw_ref[0, 0, qi], jnp.where(causal, p_edge, zero), preferred_element_type=F32)
               + jnp.dot(vw_ref[0, 0, win_far], jnp.where(causal, zero, p_edge), preferred_element_type=F32)
               + jnp.dot(vw_ref[0, 0, win_mid], p_mid, preferred_element_type=F32))
        owin_sc[h] = acc[:HEAD_DIM] / acc[HEAD_DIM:HEAD_DIM + 1]

    def sel_pair(t, carry):
        sel_scores(2 * t + 1, sb_sc)
        sel_softmax(2 * t, sa_sc)
        sel_scores(2 * t + 2, sa_sc)
        sel_softmax(2 * t + 1, sb_sc)
        return carry

    lax.fori_loop(0, lax.shift_right_logical(qi, 1), sel_pair, 0)

    @pl.when((qi & 1) == 1)
    def _():
        sel_scores(qi, sb_sc)
        sel_softmax(qi - 1, sa_sc)
        sel_softmax(qi, sb_sc, diagonal=True)

    @pl.when((qi & 1) == 0)
    def _():
        sel_softmax(qi, sa_sc, diagonal=True)

    gates_t = gate_ref[0].T
    outs = []
    for h in heads:
        c = h * N_NSA_BRANCH
        acc = acc_sc[h]
        o_sel = acc[:HEAD_DIM] / acc[HEAD_DIM:HEAD_DIM + 1]
        o_cmp = ocmp_ref[0, 0, h * HEAD_DIM:(h + 1) * HEAD_DIM, :]
        outs.append(gates_t[c:c + 1] * o_cmp + gates_t[c + 1:c + 2] * o_sel
                    + gates_t[c + 2:c + 3] * owin_sc[h])
    o_ref[0] = jnp.concatenate(outs, axis=0).T.astype(BF16)


def _attend(q_rot_t, bias, o_cmp_t, k_sel, v_sel_t, k_win, v_win_t, gates):
    nq = SEQ // NSA_Q
    step = lambda rows: pl.BlockSpec((1, 1, rows, NSA_Q), lambda b, k, i: (b, k, 0, i))
    whole = lambda *tile: pl.BlockSpec((1, 1) + tile, lambda b, k, i: (b, k) + (0,) * len(tile))
    return pl.pallas_call(
        _attend_kernel,
        grid=(BATCH, N_KV_HEADS, nq),
        in_specs=[
            pl.BlockSpec((1, Q_PER_KV, HEAD_DIM, NSA_Q), lambda b, k, i: (b, k, 0, i)),
            step(N_SEL), step(Q_PER_KV * HEAD_DIM),
            whole(SEQ, LANES), whole(nq, V_ROWS, NSA_Q),
            whole(SEQ, LANES), whole(nq, V_ROWS, NSA_Q),
            pl.BlockSpec((1, NSA_Q, LANES), lambda b, k, i: (k, b * nq + i, 0)),
        ],
        out_specs=pl.BlockSpec((1, NSA_Q, Q_PER_KV * HEAD_DIM), lambda b, k, i: (b, i, k)),
        out_shape=jax.ShapeDtypeStruct((BATCH, SEQ, N_HEADS * HEAD_DIM), BF16),
        scratch_shapes=[pltpu.VMEM((Q_PER_KV, 1, NSA_Q), F32),
                        pltpu.VMEM((Q_PER_KV, V_ROWS, NSA_Q), F32),
                        pltpu.VMEM((Q_PER_KV, NSA_Q, NSA_Q), F32),
                        pltpu.VMEM((Q_PER_KV, NSA_Q, NSA_Q), F32),
                        pltpu.VMEM((Q_PER_KV, HEAD_DIM, NSA_Q), F32)],
        compiler_params=pltpu.CompilerParams(
            dimension_semantics=("parallel", "parallel", "arbitrary"),
            vmem_limit_bytes=VMEM_LIMIT),
        name="attend",
    )(q_rot_t, bias, o_cmp_t, k_sel, v_sel_t, k_win, v_win_t, gates)


def _merge_kernel(h_ref, gu_ref, vn_ref, ob_ref, ga_ref, gb_ref, ws_ref, bs_ref,
                  wa_ref, wb_ref, wo_ref, o_ref, *, tm):
    r = lax.broadcasted_iota(jnp.int32, (GM_CHUNK, GM_CHUNK), 0)
    c = lax.broadcasted_iota(jnp.int32, (GM_CHUNK, GM_CHUNK), 1)
    w_tril = [jnp.where(c <= r, ws_ref[g], 0.0).astype(BF16) for g in range(GM_GROUPS)]
    bias = bs_ref[...]
    rows = []
    for ch in range(tm // GM_CHUNK):
        rs = slice(ch * GM_CHUNK, (ch + 1) * GM_CHUNK)
        mix = jnp.concatenate(
            [jnp.dot(w_tril[g], vn_ref[rs, g * LANES:(g + 1) * LANES], preferred_element_type=F32)
             for g in range(GM_GROUPS)], axis=1)
        rows.append(gu_ref[rs, :].astype(F32) * (mix + bias))
    z = jnp.concatenate(rows, axis=0).astype(BF16)
    y_a = jnp.dot(z, wa_ref[...], preferred_element_type=F32)
    y_b = jnp.dot(ob_ref[...], wb_ref[...], preferred_element_type=F32)
    merged = (ga_ref[...].astype(F32) * y_a + gb_ref[...].astype(F32) * y_b).astype(BF16)
    o_ref[...] = h_ref[...] + jnp.dot(merged, wo_ref[...], preferred_element_type=F32)


def _merge(h, gu, vn, o_b, ga, gb, w_s, b_s_exp, w_a, w_b, w_o, *, tm=256):
    row = pl.BlockSpec((tm, D_MODEL), lambda i: (i, 0))
    return pl.pallas_call(
        functools.partial(_merge_kernel, tm=tm),
        grid=(TOKENS // tm,),
        in_specs=[row, row, row, row, row, row,
                  _const_spec((GM_GROUPS, GM_CHUNK, GM_CHUNK)),
                  _const_spec((GM_CHUNK, GM_WIDTH)),
                  _const_spec((GM_WIDTH, D_MODEL)),
                  _const_spec((N_HEADS * HEAD_DIM, D_MODEL)),
                  _const_spec((D_MODEL, D_MODEL))],
        out_specs=row,
        out_shape=jax.ShapeDtypeStruct((TOKENS, D_MODEL), F32),
        compiler_params=pltpu.CompilerParams(
            dimension_semantics=("parallel",), vmem_limit_bytes=VMEM_LIMIT),
        name="merge",
    )(h, gu, vn, o_b, ga, gb, w_s, b_s_exp, w_a, w_b, w_o)


def _rope_tables():
    inv_freq = ROPE_THETA ** (-jnp.arange(0, ROPE_DIM, 2, dtype=jnp.float32) / ROPE_DIM)
    ang = jnp.arange(SEQ).astype(jnp.float32)[:, None] * inv_freq[None, :]
    cos, sin = jnp.cos(ang), jnp.sin(ang)
    zero = jnp.zeros_like(cos)
    rest = HEAD_DIM - ROPE_DIM
    c = jnp.concatenate([cos, cos, jnp.ones((SEQ, rest), F32)], axis=1)
    sa = jnp.concatenate([zero, sin, jnp.zeros((SEQ, rest), F32)], axis=1)
    sb = jnp.concatenate([-sin, zero, jnp.zeros((SEQ, rest), F32)], axis=1)
    return [jnp.tile(t, (1, LANES // HEAD_DIM)) for t in (c, sa, sb)]


def _overlap_matrix_t():
    j = jnp.arange(N_SEL)[:, None]
    i = jnp.arange(N_CHUNKS)[None, :]
    r_sel = SEL_LEN // CMP_STRIDE
    l_cmp = CMP_LEN // CMP_STRIDE
    return ((i >= r_sel * j - (l_cmp - 1)) & (i <= r_sel * j + r_sel - 1)).astype(BF16)


def _pack_w_in(w_in):
    u_v_q_kv = w_in[:, :OFF_MERGE]
    gate0 = OFF_MERGE
    merge0 = gate0 + N_HEADS * N_NSA_BRANCH
    gate = w_in[:, gate0:merge0].reshape(D_MODEL, N_KV_HEADS, Q_PER_KV * N_NSA_BRANCH)
    gate = jnp.pad(gate, ((0, 0), (0, 0), (0, LANES - Q_PER_KV * N_NSA_BRANCH)))
    return jnp.concatenate(
        [u_v_q_kv, w_in[:, merge0:], gate.reshape(D_MODEL, N_KV_HEADS * LANES)], axis=1).astype(BF16)


def kernel(x, p, ffn1_norm, ffn1_w_in, ffn1_w_out, mix_norm, w_in, gm_ln_g, gm_ln_b, gm_w_s, gm_b_s,
           w_branch_a, cmp_pos_k, cmp_k_w1, cmp_k_w2, cmp_pos_v, cmp_v_w1, cmp_v_w2, w_branch_b, w_out,
           ffn2_norm, ffn2_w_in, ffn2_w_out, ple_norm, ple_w_gate, ple_w_proj, final_norm):
    assert x.shape == (BATCH, SEQ, D_MODEL) and p.shape == (1, BATCH, SEQ, PLE_DIM)
    row = lambda a: a.reshape(1, -1)
    h = x.reshape(TOKENS, D_MODEL)

    h = _ffn(h, row(ffn1_norm[0]), ffn1_w_in[0].astype(BF16), ffn1_w_out[0].astype(BF16))

    rope_c, rope_sa, rope_sb = _rope_tables()
    (gu, vn, q_raw, q_rot, k_c, v_c, k_sel, v_sel, k_win, v_win, gates, g_a, g_b) = _inproj(
        h, row(mix_norm[0]), _pack_w_in(w_in[0]), row(gm_ln_g[0]), row(gm_ln_b[0]),
        rope_c, rope_sa, rope_sb)

    half = CMP_STRIDE * HEAD_DIM

    def compress(x_heads, pos, w1, w2, feature_major):
        chunks = x_heads.reshape(BATCH, N_KV_HEADS, N_CHUNKS, half)
        w2 = (w2.T if feature_major else w2).astype(BF16)
        return _compress(chunks, pos[:CMP_STRIDE].reshape(1, half), pos[CMP_STRIDE:].reshape(1, half),
                         w1[:half].astype(BF16), w1[half:].astype(BF16), w2, feature_major=feature_major)

    k_cmp = compress(k_c, cmp_pos_k[0], cmp_k_w1[0], cmp_k_w2[0], False)
    v_cmp = compress(v_c, cmp_pos_v[0], cmp_v_w1[0], cmp_v_w2[0], True)

    sel_bias, o_cmp = _select(q_raw, k_cmp, v_cmp, _overlap_matrix_t())
    o_b = _attend(q_rot, sel_bias, o_cmp, k_sel, v_sel, k_win, v_win, gates)

    b_s_exp = jnp.repeat(gm_b_s[0].T, GM_WIDTH // GM_GROUPS, axis=1)
    h = _merge(h, gu, vn, o_b.reshape(TOKENS, N_HEADS * HEAD_DIM), g_a, g_b, gm_w_s[0], b_s_exp,
               w_branch_a[0].astype(BF16), w_branch_b[0].astype(BF16), w_out[0].astype(BF16))

    ple = (p[0].reshape(TOKENS, PLE_DIM), row(ple_norm[0]), ple_w_gate[0].astype(BF16),
           ple_w_proj[0].astype(BF16), row(final_norm))
    out = _ffn(h, row(ffn2_norm[0]), ffn2_w_in[0].astype(BF16), ffn2_w_out[0].astype(BF16), ple)
    return out.reshape(BATCH, SEQ, D_MODEL)
```

```python
import functools

import jax
import jax.numpy as jnp
from jax import lax
from jax.experimental import pallas as pl
from jax.experimental.pallas import tpu as pltpu

D_MODEL = 1024
BATCH = 4
SEQ = 4096
PLE_DIM = 256
D_FF = 2816
NORM_EPS = 1e-6
GM_WIDTH = 1024
GM_GROUPS = 8
GM_CHUNK = 128
N_HEADS = 16
N_KV_HEADS = 4
HEAD_DIM = 64
Q_PER_KV = N_HEADS // N_KV_HEADS
KV_WIDTH = N_KV_HEADS * HEAD_DIM
ROPE_DIM = HEAD_DIM // 4
ROPE_HALF = ROPE_DIM // 2
ROPE_THETA = 500000.0
CMP_LEN = 32
CMP_STRIDE = 16
CMP_HIDDEN = 256
SEL_LEN = 64
SEL_TOP = 16
WINDOW = 512
N_NSA_BRANCH = 3
MASK_VALUE = -1e30
FORCE_SCORE = 1e9
LOG2_E = 1.4426950408889634

TOKENS = BATCH * SEQ
N_CHUNKS = SEQ // CMP_STRIDE
N_SEL = SEQ // SEL_LEN
LANES = 128
SUBLANES = 8
VMEM_LIMIT = 56 * 1024 * 1024

OFF_U = 0
OFF_V = OFF_U + GM_WIDTH
OFF_Q = OFF_V + GM_WIDTH
OFF_KV = OFF_Q + N_HEADS * HEAD_DIM
OFF_MERGE = OFF_KV + 6 * KV_WIDTH
OFF_GATE = OFF_MERGE + 2 * D_MODEL
W_PACKED = OFF_GATE + N_KV_HEADS * LANES

F32 = jnp.float32
BF16 = jnp.bfloat16

NT_DIMS = (((1,), (1,)), ((), ()))


def _const_spec(shape):
    nd = len(shape)
    return pl.BlockSpec(shape, lambda *_: (0,) * nd, pipeline_mode=pl.Buffered(1))


def _rms(x, g):
    return x * lax.rsqrt(jnp.mean(x * x, axis=-1, keepdims=True) + NORM_EPS) * g


MXU_WIDTH = 256


def _half_step_ffn(x, g, wi_ref, wo_ref):
    xn = _rms(x, g).astype(BF16)
    acc = None
    for c in range(D_FF // MXU_WIDTH):
        lo = c * MXU_WIDTH
        gate = jnp.dot(xn, wi_ref[:, lo:lo + MXU_WIDTH], preferred_element_type=F32)
        up = jnp.dot(xn, wi_ref[:, D_FF + lo:D_FF + lo + MXU_WIDTH], preferred_element_type=F32)
        act = (gate * jax.nn.sigmoid(gate) * up).astype(BF16)
        part = jnp.dot(act, wo_ref[lo:lo + MXU_WIDTH, :], preferred_element_type=F32)
        acc = part if acc is None else acc + part
    return x + 0.5 * acc


def _ffn_kernel(x_ref, g_ref, wi_ref, wo_ref, o_ref):
    o_ref[...] = _half_step_ffn(x_ref[...], g_ref[...], wi_ref, wo_ref)


def _ffn_ple_final_kernel(x_ref, g_ref, wi_ref, wo_ref, p_ref, gp_ref, wg_ref, wp_ref, gf_ref, o_ref):
    h = _half_step_ffn(x_ref[...], g_ref[...], wi_ref, wo_ref)
    gate = jax.nn.sigmoid(jnp.dot(_rms(h, gp_ref[...]).astype(BF16), wg_ref[...],
                                  preferred_element_type=F32))
    proj = jnp.dot(p_ref[...].astype(BF16), wp_ref[...], preferred_element_type=F32)
    o_ref[...] = _rms(h + gate * proj, gf_ref[...])


def _ffn(x, g, w_in, w_out, ple=None, *, tm=512):
    row = pl.BlockSpec((tm, D_MODEL), lambda i: (i, 0))
    in_specs = [row, _const_spec((1, D_MODEL)), _const_spec((D_MODEL, 2 * D_FF)),
                _const_spec((D_FF, D_MODEL))]
    operands = [x, g, w_in, w_out]
    if ple is not None:
        in_specs += [pl.BlockSpec((tm, PLE_DIM), lambda i: (i, 0)), _const_spec((1, D_MODEL)),
                     _const_spec((D_MODEL, D_MODEL)), _const_spec((PLE_DIM, D_MODEL)),
                     _const_spec((1, D_MODEL))]
        operands += list(ple)
    return pl.pallas_call(
        _ffn_kernel if ple is None else _ffn_ple_final_kernel,
        grid=(TOKENS // tm,),
        in_specs=in_specs,
        out_specs=row,
        out_shape=jax.ShapeDtypeStruct((TOKENS, D_MODEL), F32),
        compiler_params=pltpu.CompilerParams(
            dimension_semantics=("parallel",), vmem_limit_bytes=VMEM_LIMIT),
        name="ffn" if ple is None else "ffn_ple_final",
    )(*operands)


NSA_Q = 256
BF16_ROWS = 16
V_ROWS = HEAD_DIM + BF16_ROWS


def _rope(x, c, sa, sb):
    w = x.shape[1]
    return x * c + pltpu.roll(x, ROPE_HALF, 1) * sa + pltpu.roll(x, w - ROPE_HALF, 1) * sb


def _inproj_kernel(h_ref, g_ref, w_ref, lng_ref, lnb_ref, c_ref, sa_ref, sb_ref,
                   gu_ref, vn_ref, qraw_ref, qrot_ref, kc_ref, vc_ref, ks_ref, vs_ref,
                   kw_ref, vw_ref, gate_ref, ga_ref, gb_ref, *, tm):
    n = _rms(h_ref[...], g_ref[...]).astype(BF16)

    def seg(lo, width):
        return jnp.dot(n, w_ref[:, lo:lo + width], preferred_element_type=F32)

    gu_ref[...] = jax.nn.gelu(seg(OFF_U, GM_WIDTH)).astype(BF16)
    v = jax.nn.gelu(seg(OFF_V, GM_WIDTH))
    mu = jnp.mean(v, axis=-1, keepdims=True)
    vc = v - mu
    var = jnp.mean(vc * vc, axis=-1, keepdims=True)
    vn_ref[...] = (vc * lax.rsqrt(var + NORM_EPS) * lng_ref[...] + lnb_ref[...]).astype(BF16)

    c, sa, sb = c_ref[...], sa_ref[...], sb_ref[...]

    def tile_lanes(t, reps):
        return jnp.concatenate([t] * reps, axis=1)

    q = seg(OFF_Q, N_HEADS * HEAD_DIM) * (HEAD_DIM ** -0.5 * LOG2_E)
    reps = N_HEADS * HEAD_DIM // LANES
    q_rot = _rope(q, tile_lanes(c, reps), tile_lanes(sa, reps), tile_lanes(sb, reps))
    qraw_ref[0] = q.T.astype(BF16).reshape(N_HEADS, HEAD_DIM, tm)
    qrot_ref[0] = q_rot.T.astype(BF16).reshape(N_HEADS, HEAD_DIM, tm)

    kv = seg(OFF_KV, 6 * KV_WIDTH)
    k_c, v_c, k_s, v_s, k_w, v_w = [kv[:, i * KV_WIDTH:(i + 1) * KV_WIDTH] for i in range(6)]
    reps = KV_WIDTH // LANES
    ck, sak, sbk = tile_lanes(c, reps), tile_lanes(sa, reps), tile_lanes(sb, reps)
    k_s = _rope(k_s, ck, sak, sbk)
    k_w = _rope(k_w, ck, sak, sbk)
    pos = (pl.program_id(0) % (SEQ // tm)) * tm + lax.broadcasted_iota(jnp.int32, (tm, N_SEL), 0)
    blk = lax.broadcasted_iota(jnp.int32, (tm, N_SEL), 1)
    onehot = jnp.where(lax.shift_right_logical(pos, 6) == blk, 1.0, 0.0).astype(F32)
    ones_t = jnp.ones((V_ROWS - HEAD_DIM, tm), F32)
    ones_col = jnp.where(lax.broadcasted_iota(jnp.int32, (tm, HEAD_DIM), 1) == 0, 1.0, 0.0)
    v_s_t, v_w_t = v_s.T, v_w.T
    for h in range(N_KV_HEADS):
        sl = slice(h * HEAD_DIM, (h + 1) * HEAD_DIM)
        kc_ref[0, h] = k_c[:, sl]
        vc_ref[0, h] = v_c[:, sl]
        ks_ref[0, h] = jnp.concatenate([k_s[:, sl], onehot], axis=1).astype(BF16)
        vs_ref[0, h, 0] = jnp.concatenate([v_s_t[sl], ones_t], axis=0).astype(BF16)
        kw_ref[0, h] = jnp.concatenate([k_w[:, sl], ones_col], axis=1).astype(BF16)
        vw_ref[0, h, 0] = jnp.concatenate([v_w_t[sl], ones_t], axis=0).astype(BF16)

    merge = jax.nn.sigmoid(seg(OFF_MERGE, 2 * D_MODEL))
    ga_ref[...] = merge[:, :D_MODEL].astype(BF16)
    gb_ref[...] = merge[:, D_MODEL:].astype(BF16)
    for h in range(N_KV_HEADS):
        gate_ref[h] = jax.nn.sigmoid(seg(OFF_GATE + h * LANES, LANES))


def _inproj(h, g, w_packed, ln_g, ln_b, rope_c, rope_sa, rope_sb):
    tm = NSA_Q
    nq = SEQ // tm
    row = lambda i: (i, 0)
    head = lambda i: (i // nq, 0, i % nq, 0)
    tok_bf = jax.ShapeDtypeStruct((TOKENS, D_MODEL), BF16)

    def head_shape(nh, width, dtype):
        return jax.ShapeDtypeStruct((BATCH, nh, SEQ, width), dtype)

    def head_spec(nh, width):
        return pl.BlockSpec((1, nh, tm, width), head)

    qt_spec = pl.BlockSpec((1, N_HEADS, HEAD_DIM, tm), lambda i: (i // nq, 0, 0, i % nq))
    qt_shape = jax.ShapeDtypeStruct((BATCH, N_HEADS, HEAD_DIM, SEQ), BF16)
    vt_spec = pl.BlockSpec((1, N_KV_HEADS, 1, V_ROWS, tm), lambda i: (i // nq, 0, i % nq, 0, 0))
    vt_shape = jax.ShapeDtypeStruct((BATCH, N_KV_HEADS, nq, V_ROWS, tm), BF16)

    return pl.pallas_call(
        functools.partial(_inproj_kernel, tm=tm),
        grid=(TOKENS // tm,),
        in_specs=[
            pl.BlockSpec((tm, D_MODEL), row),
            _const_spec((1, D_MODEL)),
            _const_spec((D_MODEL, W_PACKED)),
            _const_spec((1, GM_WIDTH)),
            _const_spec((1, GM_WIDTH)),
            pl.BlockSpec((tm, LANES), lambda i: (i % nq, 0)),
            pl.BlockSpec((tm, LANES), lambda i: (i % nq, 0)),
            pl.BlockSpec((tm, LANES), lambda i: (i % nq, 0)),
        ],
        out_specs=[
            pl.BlockSpec((tm, GM_WIDTH), row),
            pl.BlockSpec((tm, GM_WIDTH), row),
            qt_spec,
            qt_spec,
            head_spec(N_KV_HEADS, HEAD_DIM),
            head_spec(N_KV_HEADS, HEAD_DIM),
            head_spec(N_KV_HEADS, LANES),
            vt_spec,
            head_spec(N_KV_HEADS, LANES),
            vt_spec,
            pl.BlockSpec((N_KV_HEADS, tm, LANES), lambda i: (0, i, 0)),
            pl.BlockSpec((tm, D_MODEL), row),
            pl.BlockSpec((tm, D_MODEL), row),
        ],
        out_shape=[
            tok_bf, tok_bf,
            qt_shape, qt_shape,
            head_shape(N_KV_HEADS, HEAD_DIM, F32), head_shape(N_KV_HEADS, HEAD_DIM, F32),
            head_shape(N_KV_HEADS, LANES, BF16), vt_shape,
            head_shape(N_KV_HEADS, LANES, BF16), vt_shape,
            jax.ShapeDtypeStruct((N_KV_HEADS, TOKENS, LANES), F32),
            tok_bf, tok_bf,
        ],
        compiler_params=pltpu.CompilerParams(
            dimension_semantics=("parallel",), vmem_limit_bytes=VMEM_LIMIT),
        name="inproj",
    )(h, g, w_packed, ln_g, ln_b, rope_c, rope_sa, rope_sb)


def _compress_kernel(x_ref, ptop_ref, pbot_ref, w1t_ref, w1b_ref, w2_ref, o_ref, *, feature_major):
    x = x_ref[0, 0]
    top = jnp.dot((x + ptop_ref[...]).astype(BF16), w1t_ref[...], preferred_element_type=F32)
    bot = jnp.dot((x + pbot_ref[...]).astype(BF16), w1b_ref[...], preferred_element_type=F32)
    hidden = jax.nn.gelu(top + pltpu.roll(bot, N_CHUNKS - 1, 0)).astype(BF16)
    if feature_major:
        out = lax.dot_general(w2_ref[...], hidden, NT_DIMS, preferred_element_type=F32)
    else:
        out = jnp.dot(hidden, w2_ref[...], preferred_element_type=F32)
    o_ref[0, 0] = out.astype(BF16)


def _compress(x_chunks, pos_top, pos_bot, w1_top, w1_bot, w2, *, feature_major):
    half = CMP_STRIDE * HEAD_DIM
    out_tile = (HEAD_DIM, N_CHUNKS) if feature_major else (N_CHUNKS, HEAD_DIM)
    return pl.pallas_call(
        functools.partial(_compress_kernel, feature_major=feature_major),
        grid=(BATCH, N_KV_HEADS),
        in_specs=[
            pl.BlockSpec((1, 1, N_CHUNKS, half), lambda b, k: (b, k, 0, 0)),
            _const_spec((1, half)),
            _const_spec((1, half)),
            _const_spec((half, CMP_HIDDEN)),
            _const_spec((half, CMP_HIDDEN)),
            _const_spec(w2.shape),
        ],
        out_specs=pl.BlockSpec((1, 1) + out_tile, lambda b, k: (b, k, 0, 0)),
        out_shape=jax.ShapeDtypeStruct((BATCH, N_KV_HEADS) + out_tile, BF16),
        compiler_params=pltpu.CompilerParams(dimension_semantics=("parallel", "parallel")),
        name="compress",
    )(x_chunks, pos_top, pos_bot, w1_top, w1_bot, w2)


RANK_STRIDE = 16
RANK_ACCUMULATORS = 4
SELECT_Q = 512


def _split_bf16(x):
    hi = x.astype(BF16)
    r1 = x - hi.astype(F32)
    mid = r1.astype(BF16)
    lo = (r1 - mid.astype(F32)).astype(BF16)
    return hi, mid, lo


def _selection_bias(score, n_blocks):
    n_q = score.shape[1]
    sub_iota = lax.broadcasted_iota(jnp.int32, (SUBLANES, n_q), 0)
    groups = [score[g * SUBLANES:(g + 1) * SUBLANES] for g in range(n_blocks // SUBLANES)]
    counts = [[None] * RANK_ACCUMULATORS for _ in groups]
    for jp in range(n_blocks):
        other = score[jp:jp + 1, :]
        for g, grp in enumerate(groups):
            lo = g * SUBLANES
            if lo > jp:
                before = other >= grp
            elif lo + SUBLANES - 1 <= jp:
                before = other > grp
            else:
                before = (other > grp) | ((other == grp) & (sub_iota + lo > jp))
            inc = jnp.where(before, 1, 0)
            a = jp % RANK_ACCUMULATORS
            counts[g][a] = inc if counts[g][a] is None else counts[g][a] + inc
    rank = jnp.concatenate([(c[0] + c[1]) + (c[2] + c[3]) for c in counts], axis=0)
    bias = jnp.where(rank < SEL_TOP, 0.0, MASK_VALUE)
    if n_blocks < N_SEL:
        bias = jnp.concatenate([bias, jnp.zeros((N_SEL - n_blocks, n_q), F32)], axis=0)
    return bias


def _select_kernel(qraw_ref, kcmp_ref, vcmp_ref, mt_ref, bias_ref, ocmp_ref):
    qi = pl.program_id(2)
    s0 = qi * SELECT_Q
    key_i = lax.broadcasted_iota(jnp.int32, (N_CHUNKS, SELECT_Q), 0)
    qry_i = lax.broadcasted_iota(jnp.int32, (N_CHUNKS, SELECT_Q), 1)
    heads = range(Q_PER_KV)

    k_cmp = kcmp_ref[0, 0]
    v_cmp_t = vcmp_ref[0, 0]
    s_cmp = [jnp.dot(k_cmp, qraw_ref[0, h], preferred_element_type=F32) for h in heads]
    cmp_valid = key_i * CMP_STRIDE + (CMP_LEN - 1) <= s0 + qry_i
    any_valid = s0 + lax.broadcasted_iota(jnp.int32, (1, SELECT_Q), 1) >= CMP_LEN - 1
    p_cmp = []
    for h in heads:
        s_c = jnp.where(cmp_valid, s_cmp[h], MASK_VALUE)
        e_c = jnp.exp2(s_c - jnp.max(s_c, axis=0, keepdims=True))
        p_cmp.append(e_c * jnp.where(any_valid, 1.0 / jnp.sum(e_c, axis=0, keepdims=True), 0.0))
    for h in heads:
        ocmp_ref[0, 0, h * HEAD_DIM:(h + 1) * HEAD_DIM, :] = jnp.dot(
            v_cmp_t, p_cmp[h].astype(BF16), preferred_element_type=F32)

    p_sum = (p_cmp[0] + p_cmp[1]) + (p_cmp[2] + p_cmp[3])
    mt = mt_ref[...]
    imp_t = sum(jnp.dot(mt, part, preferred_element_type=F32)
                for part in _split_bf16(p_sum))
    blk = lax.broadcasted_iota(jnp.int32, (N_SEL, SELECT_Q), 0)
    cur = lax.shift_right_logical(s0 + lax.broadcasted_iota(jnp.int32, (N_SEL, SELECT_Q), 1), 6)
    forced = (blk == 0) | (blk == cur) | (blk == cur - 1)
    score = jnp.where(forced, FORCE_SCORE, jnp.where(blk > cur, -FORCE_SCORE, imp_t))

    steps_per_variant = RANK_STRIDE * SEL_LEN // SELECT_Q
    for variant in range(N_SEL // RANK_STRIDE):
        @pl.when(qi // steps_per_variant == variant)
        def _():
            bias_ref[0, 0] = _selection_bias(score, (variant + 1) * RANK_STRIDE).astype(BF16)


def _select(q_raw_t, k_cmp, v_cmp_t, overlap_t):
    nq = SEQ // SELECT_Q
    step = lambda rows: pl.BlockSpec((1, 1, rows, SELECT_Q), lambda b, k, i: (b, k, 0, i))
    whole = lambda *tile: pl.BlockSpec((1, 1) + tile, lambda b, k, i: (b, k) + (0,) * len(tile))
    return pl.pallas_call(
        _select_kernel,
        grid=(BATCH, N_KV_HEADS, nq),
        in_specs=[
            pl.BlockSpec((1, Q_PER_KV, HEAD_DIM, SELECT_Q), lambda b, k, i: (b, k, 0, i)),
            whole(N_CHUNKS, HEAD_DIM), whole(HEAD_DIM, N_CHUNKS),
            _const_spec((N_SEL, N_CHUNKS)),
        ],
        out_specs=[step(N_SEL), step(Q_PER_KV * HEAD_DIM)],
        out_shape=[jax.ShapeDtypeStruct((BATCH, N_KV_HEADS, N_SEL, SEQ), BF16),
                   jax.ShapeDtypeStruct((BATCH, N_KV_HEADS, Q_PER_KV * HEAD_DIM, SEQ), F32)],
        compiler_params=pltpu.CompilerParams(
            dimension_semantics=("parallel", "parallel", "parallel"), vmem_limit_bytes=VMEM_LIMIT),
        name="select",
    )(q_raw_t, k_cmp, v_cmp_t, overlap_t)


KV_PER_STEP = 2


def _attend_kernel(qrot_ref, bias_ref, ocmp_ref, ks_ref, vs_ref, kw_ref, vw_ref, gate_ref,
                   o_ref, m_sc, acc_sc, sa_sc, sb_sc, owin_sc):
    qi = pl.program_id(2)
    s0 = qi * NSA_Q
    key_i = lax.broadcasted_iota(jnp.int32, (NSA_Q, NSA_Q), 0)
    qry_i = lax.broadcasted_iota(jnp.int32, (NSA_Q, NSA_Q), 1)
    causal = key_i <= qry_i
    kvs = range(KV_PER_STEP)
    heads = range(KV_PER_STEP * Q_PER_KV)

    def scores(k, q_t):
        return jnp.dot(k, q_t, preferred_element_type=F32)

    def softmax_step(h, s, v_t):
        m_prev = m_sc[h]
        m_new = jnp.maximum(m_prev, jnp.max(s, axis=0, keepdims=True))
        p = jnp.exp2(s - m_new).astype(BF16)
        acc_sc[h] = jnp.exp2(m_prev - m_new) * acc_sc[h] + jnp.dot(v_t, p, preferred_element_type=F32)
        m_sc[h] = m_new

    q_sel = [jnp.concatenate([qrot_ref[0, h], bias_ref[0, h // Q_PER_KV]], axis=0) for h in heads]
    m_sc[...] = jnp.full_like(m_sc, MASK_VALUE)
    acc_sc[...] = jnp.zeros_like(acc_sc)

    def sel_scores(kt, dst):
        rows = pl.ds(pl.multiple_of(kt * NSA_Q, NSA_Q), NSA_Q)
        k = [ks_ref[0, kv, rows, :] for kv in kvs]
        for h in heads:
            dst[h] = scores(k[h // Q_PER_KV], q_sel[h])

    def sel_softmax(kt, src, diagonal=False):
        v_t = [vs_ref[0, kv, kt] for kv in kvs]
        for h in heads:
            s = src[h]
            softmax_step(h, jnp.where(causal, s, MASK_VALUE) if diagonal else s, v_t[h // Q_PER_KV])

    def window_keys(kv, kt):
        return kw_ref[0, kv, pl.ds(pl.multiple_of(kt * NSA_Q, NSA_Q), NSA_Q), :]

    def penalty_rows(penalty):
        rows = jnp.where(lax.broadcasted_iota(jnp.int32, (HEAD_DIM, NSA_Q), 0) == 0, penalty, 0.0)
        return rows.astype(BF16)

    win_mid, win_far = jnp.maximum(qi - 1, 0), jnp.maximum(qi - 2, 0)
    k_diag = [window_keys(kv, qi)[:, 0:HEAD_DIM] for kv in kvs]
    k_mid = [window_keys(kv, win_mid) for kv in kvs]
    k_far = [window_keys(kv, win_far) for kv in kvs]
    pen_mid = penalty_rows(jnp.where(qi >= 1, 0.0, MASK_VALUE))
    pen_far = penalty_rows(jnp.where(qi >= 2, 0.0, MASK_VALUE))
    s_win_diag, s_win_mid, s_win_far = [], [], []
    for h in heads:
        q_t = qrot_ref[0, h]
        kv = h // Q_PER_KV
        s_win_diag.append(scores(k_diag[kv], q_t))
        s_win_mid.append(scores(k_mid[kv], jnp.concatenate([q_t, pen_mid], axis=0)))
        s_win_far.append(scores(k_far[kv], jnp.concatenate([q_t, pen_far], axis=0)))

    sel_scores(0, sa_sc)

    for h in heads:
        s_edge = jnp.where(causal, s_win_diag[h], s_win_far[h])
        s_mid = s_win_mid[h]
        m_w = jnp.max(jnp.maximum(s_edge, s_mid), axis=0, keepdims=True)
        p_edge = jnp.exp2(s_edge - m_w).astype(BF16)
        p_mid = jnp.exp2(s_mid - m_w).astype(BF16)
        zero = jnp.zeros_like(p_edge)
        kv = h // Q_PER_KV
        acc = (jnp.dot(vw_ref[0, kv, qi], jnp.where(causal, p_edge, zero), preferred_element_type=F32)
               + jnp.dot(vw_ref[0, kv, win_far], jnp.where(causal, zero, p_edge), preferred_element_type=F32)
               + jnp.dot(vw_ref[0, kv, win_mid], p_mid, preferred_element_type=F32))
        owin_sc[h] = acc[:HEAD_DIM] / acc[HEAD_DIM:HEAD_DIM + 1]

    def sel_pair(t, carry):
        sel_scores(2 * t + 1, sb_sc)
        sel_softmax(2 * t, sa_sc)
        sel_scores(2 * t + 2, sa_sc)
        sel_softmax(2 * t + 1, sb_sc)
        return carry

    lax.fori_loop(0, lax.shift_right_logical(qi, 1), sel_pair, 0)

    @pl.when((qi & 1) == 1)
    def _():
        sel_scores(qi, sb_sc)
        sel_softmax(qi - 1, sa_sc)
        sel_softmax(qi, sb_sc, diagonal=True)

    @pl.when((qi & 1) == 0)
    def _():
        sel_softmax(qi, sa_sc, diagonal=True)

    gates_t = [gate_ref[kv].T for kv in kvs]
    outs = []
    for h in heads:
        kv, hh = divmod(h, Q_PER_KV)
        g = gates_t[kv]
        c = hh * N_NSA_BRANCH
        acc = acc_sc[h]
        o_sel = acc[:HEAD_DIM] / acc[HEAD_DIM:HEAD_DIM + 1]
        o_cmp = ocmp_ref[0, kv, hh * HEAD_DIM:(hh + 1) * HEAD_DIM, :]
        outs.append(g[c:c + 1] * o_cmp + g[c + 1:c + 2] * o_sel + g[c + 2:c + 3] * owin_sc[h])
    o_ref[0] = jnp.concatenate(outs, axis=0).T.astype(BF16)


def _attend(q_rot_t, bias, o_cmp_t, k_sel, v_sel_t, k_win, v_win_t, gates):
    nq = SEQ // NSA_Q
    n_heads = KV_PER_STEP * Q_PER_KV
    step = lambda rows: pl.BlockSpec((1, KV_PER_STEP, rows, NSA_Q), lambda b, k, i: (b, k, 0, i))
    whole = lambda *tile: pl.BlockSpec((1, KV_PER_STEP) + tile, lambda b, k, i: (b, k) + (0,) * len(tile))
    return pl.pallas_call(
        _attend_kernel,
        grid=(BATCH, N_KV_HEADS // KV_PER_STEP, nq),
        in_specs=[
            pl.BlockSpec((1, n_heads, HEAD_DIM, NSA_Q), lambda b, k, i: (b, k, 0, i)),
            step(N_SEL), step(Q_PER_KV * HEAD_DIM),
            whole(SEQ, LANES), whole(nq, V_ROWS, NSA_Q),
            whole(SEQ, LANES), whole(nq, V_ROWS, NSA_Q),
            pl.BlockSpec((KV_PER_STEP, NSA_Q, LANES), lambda b, k, i: (k, b * nq + i, 0)),
        ],
        out_specs=pl.BlockSpec((1, NSA_Q, n_heads * HEAD_DIM), lambda b, k, i: (b, i, k)),
        out_shape=jax.ShapeDtypeStruct((BATCH, SEQ, N_HEADS * HEAD_DIM), BF16),
        scratch_shapes=[pltpu.VMEM((n_heads, 1, NSA_Q), F32),
                        pltpu.VMEM((n_heads, V_ROWS, NSA_Q), F32),
                        pltpu.VMEM((n_heads, NSA_Q, NSA_Q), F32),
                        pltpu.VMEM((n_heads, NSA_Q, NSA_Q), F32),
                        pltpu.VMEM((n_heads, HEAD_DIM, NSA_Q), F32)],
        compiler_params=pltpu.CompilerParams(
            dimension_semantics=("parallel", "parallel", "arbitrary"),
            vmem_limit_bytes=VMEM_LIMIT),
        name="attend",
    )(q_rot_t, bias, o_cmp_t, k_sel, v_sel_t, k_win, v_win_t, gates)


def _merge_kernel(h_ref, gu_ref, vn_ref, ob_ref, ga_ref, gb_ref, ws_ref, bs_ref,
                  wa_ref, wb_ref, wo_ref, o_ref, *, tm):
    r = lax.broadcasted_iota(jnp.int32, (GM_CHUNK, GM_CHUNK), 0)
    c = lax.broadcasted_iota(jnp.int32, (GM_CHUNK, GM_CHUNK), 1)
    w_tril = [jnp.where(c <= r, ws_ref[g], 0.0).astype(BF16) for g in range(GM_GROUPS)]
    bias = bs_ref[...]
    rows = []
    for ch in range(tm // GM_CHUNK):
        rs = slice(ch * GM_CHUNK, (ch + 1) * GM_CHUNK)
        mix = jnp.concatenate(
            [jnp.dot(w_tril[g], vn_ref[rs, g * LANES:(g + 1) * LANES], preferred_element_type=F32)
             for g in range(GM_GROUPS)], axis=1)
        rows.append(gu_ref[rs, :].astype(F32) * (mix + bias))
    z = jnp.concatenate(rows, axis=0).astype(BF16)
    y_a = jnp.dot(z, wa_ref[...], preferred_element_type=F32)
    y_b = jnp.dot(ob_ref[...], wb_ref[...], preferred_element_type=F32)
    merged = (ga_ref[...].astype(F32) * y_a + gb_ref[...].astype(F32) * y_b).astype(BF16)
    o_ref[...] = h_ref[...] + jnp.dot(merged, wo_ref[...], preferred_element_type=F32)


def _merge(h, gu, vn, o_b, ga, gb, w_s, b_s_exp, w_a, w_b, w_o, *, tm=256):
    row = pl.BlockSpec((tm, D_MODEL), lambda i: (i, 0))
    return pl.pallas_call(
        functools.partial(_merge_kernel, tm=tm),
        grid=(TOKENS // tm,),
        in_specs=[row, row, row, row, row, row,
                  _const_spec((GM_GROUPS, GM_CHUNK, GM_CHUNK)),
                  _const_spec((GM_CHUNK, GM_WIDTH)),
                  _const_spec((GM_WIDTH, D_MODEL)),
                  _const_spec((N_HEADS * HEAD_DIM, D_MODEL)),
                  _const_spec((D_MODEL, D_MODEL))],
        out_specs=row,
        out_shape=jax.ShapeDtypeStruct((TOKENS, D_MODEL), F32),
        compiler_params=pltpu.CompilerParams(
            dimension_semantics=("parallel",), vmem_limit_bytes=VMEM_LIMIT),
        name="merge",
    )(h, gu, vn, o_b, ga, gb, w_s, b_s_exp, w_a, w_b, w_o)


def _rope_tables():
    inv_freq = ROPE_THETA ** (-jnp.arange(0, ROPE_DIM, 2, dtype=jnp.float32) / ROPE_DIM)
    ang = jnp.arange(SEQ).astype(jnp.float32)[:, None] * inv_freq[None, :]
    cos, sin = jnp.cos(ang), jnp.sin(ang)
    zero = jnp.zeros_like(cos)
    rest = HEAD_DIM - ROPE_DIM
    c = jnp.concatenate([cos, cos, jnp.ones((SEQ, rest), F32)], axis=1)
    sa = jnp.concatenate([zero, sin, jnp.zeros((SEQ, rest), F32)], axis=1)
    sb = jnp.concatenate([-sin, zero, jnp.zeros((SEQ, rest), F32)], axis=1)
    return [jnp.tile(t, (1, LANES // HEAD_DIM)) for t in (c, sa, sb)]


def _overlap_matrix_t():
    j = jnp.arange(N_SEL)[:, None]
    i = jnp.arange(N_CHUNKS)[None, :]
    r_sel = SEL_LEN // CMP_STRIDE
    l_cmp = CMP_LEN // CMP_STRIDE
    return ((i >= r_sel * j - (l_cmp - 1)) & (i <= r_sel * j + r_sel - 1)).astype(BF16)


def _pack_w_in(w_in):
    u_v_q_kv = w_in[:, :OFF_MERGE]
    gate0 = OFF_MERGE
    merge0 = gate0 + N_HEADS * N_NSA_BRANCH
    gate = w_in[:, gate0:merge0].reshape(D_MODEL, N_KV_HEADS, Q_PER_KV * N_NSA_BRANCH)
    gate = jnp.pad(gate, ((0, 0), (0, 0), (0, LANES - Q_PER_KV * N_NSA_BRANCH)))
    return jnp.concatenate(
        [u_v_q_kv, w_in[:, merge0:], gate.reshape(D_MODEL, N_KV_HEADS * LANES)], axis=1).astype(BF16)


def kernel(x, p, ffn1_norm, ffn1_w_in, ffn1_w_out, mix_norm, w_in, gm_ln_g, gm_ln_b, gm_w_s, gm_b_s,
           w_branch_a, cmp_pos_k, cmp_k_w1, cmp_k_w2, cmp_pos_v, cmp_v_w1, cmp_v_w2, w_branch_b, w_out,
           ffn2_norm, ffn2_w_in, ffn2_w_out, ple_norm, ple_w_gate, ple_w_proj, final_norm):
    assert x.shape == (BATCH, SEQ, D_MODEL) and p.shape == (1, BATCH, SEQ, PLE_DIM)
    row = lambda a: a.reshape(1, -1)
    h = x.reshape(TOKENS, D_MODEL)

    h = _ffn(h, row(ffn1_norm[0]), ffn1_w_in[0].astype(BF16), ffn1_w_out[0].astype(BF16))

    rope_c, rope_sa, rope_sb = _rope_tables()
    (gu, vn, q_raw, q_rot, k_c, v_c, k_sel, v_sel, k_win, v_win, gates, g_a, g_b) = _inproj(
        h, row(mix_norm[0]), _pack_w_in(w_in[0]), row(gm_ln_g[0]), row(gm_ln_b[0]),
        rope_c, rope_sa, rope_sb)

    half = CMP_STRIDE * HEAD_DIM

    def compress(x_heads, pos, w1, w2, feature_major):
        chunks = x_heads.reshape(BATCH, N_KV_HEADS, N_CHUNKS, half)
        w2 = (w2.T if feature_major else w2).astype(BF16)
        return _compress(chunks, pos[:CMP_STRIDE].reshape(1, half), pos[CMP_STRIDE:].reshape(1, half),
                         w1[:half].astype(BF16), w1[half:].astype(BF16), w2, feature_major=feature_major)

    k_cmp = compress(k_c, cmp_pos_k[0], cmp_k_w1[0], cmp_k_w2[0], False)
    v_cmp = compress(v_c, cmp_pos_v[0], cmp_v_w1[0], cmp_v_w2[0], True)

    sel_bias, o_cmp = _select(q_raw, k_cmp, v_cmp, _overlap_matrix_t())
    o_b = _attend(q_rot, sel_bias, o_cmp, k_sel, v_sel, k_win, v_win, gates)

    b_s_exp = jnp.repeat(gm_b_s[0].T, GM_WIDTH // GM_GROUPS, axis=1)
    h = _merge(h, gu, vn, o_b.reshape(TOKENS, N_HEADS * HEAD_DIM), g_a, g_b, gm_w_s[0], b_s_exp,
               w_branch_a[0].astype(BF16), w_branch_b[0].astype(BF16), w_out[0].astype(BF16))

    ple = (p[0].reshape(TOKENS, PLE_DIM), row(ple_norm[0]), ple_w_gate[0].astype(BF16),
           ple_w_proj[0].astype(BF16), row(final_norm))
    out = _ffn(h, row(ffn2_norm[0]), ffn2_w_in[0].astype(BF16), ffn2_w_out[0].astype(BF16), ple)
    return out.reshape(BATCH, SEQ, D_MODEL)
```

```python
import functools

import jax
import jax.numpy as jnp
from jax import lax
from jax.experimental import pallas as pl
from jax.experimental.pallas import tpu as pltpu

D_MODEL = 1024
BATCH = 4
SEQ = 4096
PLE_DIM = 256
D_FF = 2816
NORM_EPS = 1e-6
GM_WIDTH = 1024
GM_GROUPS = 8
GM_CHUNK = 128
N_HEADS = 16
N_KV_HEADS = 4
HEAD_DIM = 64
Q_PER_KV = N_HEADS // N_KV_HEADS
KV_WIDTH = N_KV_HEADS * HEAD_DIM
ROPE_DIM = HEAD_DIM // 4
ROPE_HALF = ROPE_DIM // 2
ROPE_THETA = 500000.0
CMP_LEN = 32
CMP_STRIDE = 16
CMP_HIDDEN = 256
SEL_LEN = 64
SEL_TOP = 16
WINDOW = 512
N_NSA_BRANCH = 3
MASK_VALUE = -1e30
FORCE_SCORE = 1e9
LOG2_E = 1.4426950408889634

TOKENS = BATCH * SEQ
N_CHUNKS = SEQ // CMP_STRIDE
N_SEL = SEQ // SEL_LEN
LANES = 128
SUBLANES = 8
VMEM_LIMIT = 56 * 1024 * 1024

OFF_U = 0
OFF_V = OFF_U + GM_WIDTH
OFF_Q = OFF_V + GM_WIDTH
OFF_KV = OFF_Q + N_HEADS * HEAD_DIM
OFF_MERGE = OFF_KV + 6 * KV_WIDTH
OFF_GATE = OFF_MERGE + 2 * D_MODEL
W_PACKED = OFF_GATE + N_KV_HEADS * LANES

F32 = jnp.float32
BF16 = jnp.bfloat16

NT_DIMS = (((1,), (1,)), ((), ()))


def _const_spec(shape):
    nd = len(shape)
    return pl.BlockSpec(shape, lambda *_: (0,) * nd, pipeline_mode=pl.Buffered(1))


def _rms(x, g):
    return x * lax.rsqrt(jnp.mean(x * x, axis=-1, keepdims=True) + NORM_EPS) * g


MXU_WIDTH = 256


def _half_step_ffn(x, g, wi_ref, wo_ref):
    xn = _rms(x, g).astype(BF16)
    acc = None
    for c in range(D_FF // MXU_WIDTH):
        lo = c * MXU_WIDTH
        gate = jnp.dot(xn, wi_ref[:, lo:lo + MXU_WIDTH], preferred_element_type=F32)
        up = jnp.dot(xn, wi_ref[:, D_FF + lo:D_FF + lo + MXU_WIDTH], preferred_element_type=F32)
        act = (gate * jax.nn.sigmoid(gate) * up).astype(BF16)
        part = jnp.dot(act, wo_ref[lo:lo + MXU_WIDTH, :], preferred_element_type=F32)
        acc = part if acc is None else acc + part
    return x + 0.5 * acc


def _ffn_kernel(x_ref, g_ref, wi_ref, wo_ref, o_ref):
    o_ref[...] = _half_step_ffn(x_ref[...], g_ref[...], wi_ref, wo_ref)


def _ffn(x, g, w_in, w_out, *, tm=512):
    row = pl.BlockSpec((tm, D_MODEL), lambda i: (i, 0))
    return pl.pallas_call(
        _ffn_kernel,
        grid=(TOKENS // tm,),
        in_specs=[row, _const_spec((1, D_MODEL)), _const_spec((D_MODEL, 2 * D_FF)),
                  _const_spec((D_FF, D_MODEL))],
        out_specs=row,
        out_shape=jax.ShapeDtypeStruct((TOKENS, D_MODEL), F32),
        compiler_params=pltpu.CompilerParams(
            dimension_semantics=("parallel",), vmem_limit_bytes=VMEM_LIMIT),
        name="ffn",
    )(x, g, w_in, w_out)


NSA_Q = 256
BF16_ROWS = 16
V_ROWS = HEAD_DIM + BF16_ROWS


def _rope(x, c, sa, sb):
    w = x.shape[1]
    return x * c + pltpu.roll(x, ROPE_HALF, 1) * sa + pltpu.roll(x, w - ROPE_HALF, 1) * sb


def _inproj_kernel(h_ref, g_ref, w_ref, lng_ref, lnb_ref, c_ref, sa_ref, sb_ref,
                   gu_ref, vn_ref, qraw_ref, qrot_ref, kc_ref, vc_ref, ks_ref, vs_ref,
                   kw_ref, vw_ref, gate_ref, ga_ref, gb_ref, *, tm):
    n = _rms(h_ref[...], g_ref[...]).astype(BF16)

    def seg(lo, width):
        return jnp.dot(n, w_ref[:, lo:lo + width], preferred_element_type=F32)

    gu_ref[...] = jax.nn.gelu(seg(OFF_U, GM_WIDTH)).astype(BF16)
    v = jax.nn.gelu(seg(OFF_V, GM_WIDTH))
    mu = jnp.mean(v, axis=-1, keepdims=True)
    vc = v - mu
    var = jnp.mean(vc * vc, axis=-1, keepdims=True)
    vn_ref[...] = (vc * lax.rsqrt(var + NORM_EPS) * lng_ref[...] + lnb_ref[...]).astype(BF16)

    c, sa, sb = c_ref[...], sa_ref[...], sb_ref[...]

    def tile_lanes(t, reps):
        return jnp.concatenate([t] * reps, axis=1)

    q = seg(OFF_Q, N_HEADS * HEAD_DIM) * (HEAD_DIM ** -0.5 * LOG2_E)
    reps = N_HEADS * HEAD_DIM // LANES
    q_rot = _rope(q, tile_lanes(c, reps), tile_lanes(sa, reps), tile_lanes(sb, reps))
    qraw_ref[0] = q.T.astype(BF16).reshape(N_HEADS, HEAD_DIM, tm)
    qrot_ref[0] = q_rot.T.astype(BF16).reshape(N_HEADS, HEAD_DIM, tm)

    kv = seg(OFF_KV, 6 * KV_WIDTH)
    k_c, v_c, k_s, v_s, k_w, v_w = [kv[:, i * KV_WIDTH:(i + 1) * KV_WIDTH] for i in range(6)]
    reps = KV_WIDTH // LANES
    ck, sak, sbk = tile_lanes(c, reps), tile_lanes(sa, reps), tile_lanes(sb, reps)
    k_s = _rope(k_s, ck, sak, sbk)
    k_w = _rope(k_w, ck, sak, sbk)
    pos = (pl.program_id(0) % (SEQ // tm)) * tm + lax.broadcasted_iota(jnp.int32, (tm, N_SEL), 0)
    blk = lax.broadcasted_iota(jnp.int32, (tm, N_SEL), 1)
    onehot = jnp.where(lax.shift_right_logical(pos, 6) == blk, 1.0, 0.0).astype(F32)
    ones_t = jnp.ones((V_ROWS - HEAD_DIM, tm), F32)
    ones_col = jnp.where(lax.broadcasted_iota(jnp.int32, (tm, HEAD_DIM), 1) == 0, 1.0, 0.0)
    v_s_t, v_w_t = v_s.T, v_w.T
    for h in range(N_KV_HEADS):
        sl = slice(h * HEAD_DIM, (h + 1) * HEAD_DIM)
        kc_ref[0, h] = k_c[:, sl]
        vc_ref[0, h] = v_c[:, sl]
        ks_ref[0, h] = jnp.concatenate([k_s[:, sl], onehot], axis=1).astype(BF16)
        vs_ref[0, h, 0] = jnp.concatenate([v_s_t[sl], ones_t], axis=0).astype(BF16)
        kw_ref[0, h] = jnp.concatenate([k_w[:, sl], ones_col], axis=1).astype(BF16)
        vw_ref[0, h, 0] = jnp.concatenate([v_w_t[sl], ones_t], axis=0).astype(BF16)

    merge = jax.nn.sigmoid(seg(OFF_MERGE, 2 * D_MODEL))
    ga_ref[...] = merge[:, :D_MODEL].astype(BF16)
    gb_ref[...] = merge[:, D_MODEL:].astype(BF16)
    for h in range(N_KV_HEADS):
        gate_ref[h] = jax.nn.sigmoid(seg(OFF_GATE + h * LANES, LANES))


def _inproj(h, g, w_packed, ln_g, ln_b, rope_c, rope_sa, rope_sb):
    tm = NSA_Q
    nq = SEQ // tm
    row = lambda i: (i, 0)
    head = lambda i: (i // nq, 0, i % nq, 0)
    tok_bf = jax.ShapeDtypeStruct((TOKENS, D_MODEL), BF16)

    def head_shape(nh, width, dtype):
        return jax.ShapeDtypeStruct((BATCH, nh, SEQ, width), dtype)

    def head_spec(nh, width):
        return pl.BlockSpec((1, nh, tm, width), head)

    qt_spec = pl.BlockSpec((1, N_HEADS, HEAD_DIM, tm), lambda i: (i // nq, 0, 0, i % nq))
    qt_shape = jax.ShapeDtypeStruct((BATCH, N_HEADS, HEAD_DIM, SEQ), BF16)
    vt_spec = pl.BlockSpec((1, N_KV_HEADS, 1, V_ROWS, tm), lambda i: (i // nq, 0, i % nq, 0, 0))
    vt_shape = jax.ShapeDtypeStruct((BATCH, N_KV_HEADS, nq, V_ROWS, tm), BF16)

    return pl.pallas_call(
        functools.partial(_inproj_kernel, tm=tm),
        grid=(TOKENS // tm,),
        in_specs=[
            pl.BlockSpec((tm, D_MODEL), row),
            _const_spec((1, D_MODEL)),
            _const_spec((D_MODEL, W_PACKED)),
            _const_spec((1, GM_WIDTH)),
            _const_spec((1, GM_WIDTH)),
            pl.BlockSpec((tm, LANES), lambda i: (i % nq, 0)),
            pl.BlockSpec((tm, LANES), lambda i: (i % nq, 0)),
            pl.BlockSpec((tm, LANES), lambda i: (i % nq, 0)),
        ],
        out_specs=[
            pl.BlockSpec((tm, GM_WIDTH), row),
            pl.BlockSpec((tm, GM_WIDTH), row),
            qt_spec,
            qt_spec,
            head_spec(N_KV_HEADS, HEAD_DIM),
            head_spec(N_KV_HEADS, HEAD_DIM),
            head_spec(N_KV_HEADS, LANES),
            vt_spec,
            head_spec(N_KV_HEADS, LANES),
            vt_spec,
            pl.BlockSpec((N_KV_HEADS, tm, LANES), lambda i: (0, i, 0)),
            pl.BlockSpec((tm, D_MODEL), row),
            pl.BlockSpec((tm, D_MODEL), row),
        ],
        out_shape=[
            tok_bf, tok_bf,
            qt_shape, qt_shape,
            head_shape(N_KV_HEADS, HEAD_DIM, F32), head_shape(N_KV_HEADS, HEAD_DIM, F32),
            head_shape(N_KV_HEADS, LANES, BF16), vt_shape,
            head_shape(N_KV_HEADS, LANES, BF16), vt_shape,
            jax.ShapeDtypeStruct((N_KV_HEADS, TOKENS, LANES), F32),
            tok_bf, tok_bf,
        ],
        compiler_params=pltpu.CompilerParams(
            dimension_semantics=("parallel",), vmem_limit_bytes=VMEM_LIMIT),
        name="inproj",
    )(h, g, w_packed, ln_g, ln_b, rope_c, rope_sa, rope_sb)


def _compress_kernel(x_ref, ptop_ref, pbot_ref, w1t_ref, w1b_ref, w2_ref, o_ref, *, feature_major):
    x = x_ref[0, 0]
    top = jnp.dot((x + ptop_ref[...]).astype(BF16), w1t_ref[...], preferred_element_type=F32)
    bot = jnp.dot((x + pbot_ref[...]).astype(BF16), w1b_ref[...], preferred_element_type=F32)
    hidden = jax.nn.gelu(top + pltpu.roll(bot, N_CHUNKS - 1, 0)).astype(BF16)
    if feature_major:
        out = lax.dot_general(w2_ref[...], hidden, NT_DIMS, preferred_element_type=F32)
    else:
        out = jnp.dot(hidden, w2_ref[...], preferred_element_type=F32)
    o_ref[0, 0] = out.astype(BF16)


def _compress(x_chunks, pos_top, pos_bot, w1_top, w1_bot, w2, *, feature_major):
    half = CMP_STRIDE * HEAD_DIM
    out_tile = (HEAD_DIM, N_CHUNKS) if feature_major else (N_CHUNKS, HEAD_DIM)
    return pl.pallas_call(
        functools.partial(_compress_kernel, feature_major=feature_major),
        grid=(BATCH, N_KV_HEADS),
        in_specs=[
            pl.BlockSpec((1, 1, N_CHUNKS, half), lambda b, k: (b, k, 0, 0)),
            _const_spec((1, half)),
            _const_spec((1, half)),
            _const_spec((half, CMP_HIDDEN)),
            _const_spec((half, CMP_HIDDEN)),
            _const_spec(w2.shape),
        ],
        out_specs=pl.BlockSpec((1, 1) + out_tile, lambda b, k: (b, k, 0, 0)),
        out_shape=jax.ShapeDtypeStruct((BATCH, N_KV_HEADS) + out_tile, BF16),
        compiler_params=pltpu.CompilerParams(dimension_semantics=("parallel", "parallel")),
        name="compress",
    )(x_chunks, pos_top, pos_bot, w1_top, w1_bot, w2)


RANK_STRIDE = 16
RANK_ACCUMULATORS = 4
SELECT_Q = 1024


def _split_bf16(x):
    hi = x.astype(BF16)
    r1 = x - hi.astype(F32)
    mid = r1.astype(BF16)
    lo = (r1 - mid.astype(F32)).astype(BF16)
    return hi, mid, lo


def _selection_bias(score, n_blocks):
    n_q = score.shape[1]
    sub_iota = lax.broadcasted_iota(jnp.int32, (SUBLANES, n_q), 0)
    groups = [score[g * SUBLANES:(g + 1) * SUBLANES] for g in range(n_blocks // SUBLANES)]
    counts = [[None] * RANK_ACCUMULATORS for _ in groups]
    for jp in range(n_blocks):
        other = score[jp:jp + 1, :]
        for g, grp in enumerate(groups):
            lo = g * SUBLANES
            if lo > jp:
                before = other >= grp
            elif lo + SUBLANES - 1 <= jp:
                before = other > grp
            else:
                before = (other > grp) | ((other == grp) & (sub_iota + lo > jp))
            inc = jnp.where(before, 1, 0)
            a = jp % RANK_ACCUMULATORS
            counts[g][a] = inc if counts[g][a] is None else counts[g][a] + inc
    rank = jnp.concatenate([(c[0] + c[1]) + (c[2] + c[3]) for c in counts], axis=0)
    bias = jnp.where(rank < SEL_TOP, 0.0, MASK_VALUE)
    if n_blocks < N_SEL:
        bias = jnp.concatenate([bias, jnp.zeros((N_SEL - n_blocks, n_q), F32)], axis=0)
    return bias


def _select_kernel(qraw_ref, kcmp_ref, vcmp_ref, mt_ref, bias_ref, ocmp_ref):
    qi = pl.program_id(2)
    s0 = qi * SELECT_Q
    key_i = lax.broadcasted_iota(jnp.int32, (N_CHUNKS, SELECT_Q), 0)
    qry_i = lax.broadcasted_iota(jnp.int32, (N_CHUNKS, SELECT_Q), 1)
    heads = range(Q_PER_KV)

    k_cmp = kcmp_ref[0, 0]
    v_cmp_t = vcmp_ref[0, 0]
    s_cmp = [jnp.dot(k_cmp, qraw_ref[0, h], preferred_element_type=F32) for h in heads]
    cmp_valid = key_i * CMP_STRIDE + (CMP_LEN - 1) <= s0 + qry_i
    any_valid = s0 + lax.broadcasted_iota(jnp.int32, (1, SELECT_Q), 1) >= CMP_LEN - 1
    p_cmp = []
    for h in heads:
        s_c = jnp.where(cmp_valid, s_cmp[h], MASK_VALUE)
        e_c = jnp.exp2(s_c - jnp.max(s_c, axis=0, keepdims=True))
        p_cmp.append(e_c * jnp.where(any_valid, 1.0 / jnp.sum(e_c, axis=0, keepdims=True), 0.0))
    for h in heads:
        ocmp_ref[0, 0, h * HEAD_DIM:(h + 1) * HEAD_DIM, :] = jnp.dot(
            v_cmp_t, p_cmp[h].astype(BF16), preferred_element_type=F32)

    p_sum = (p_cmp[0] + p_cmp[1]) + (p_cmp[2] + p_cmp[3])
    mt = mt_ref[...]
    imp_t = sum(jnp.dot(mt, part, preferred_element_type=F32)
                for part in _split_bf16(p_sum))
    blk = lax.broadcasted_iota(jnp.int32, (N_SEL, SELECT_Q), 0)
    cur = lax.shift_right_logical(s0 + lax.broadcasted_iota(jnp.int32, (N_SEL, SELECT_Q), 1), 6)
    forced = (blk == 0) | (blk == cur) | (blk == cur - 1)
    score = jnp.where(forced, FORCE_SCORE, jnp.where(blk > cur, -FORCE_SCORE, imp_t))

    steps_per_variant = RANK_STRIDE * SEL_LEN // SELECT_Q
    for variant in range(N_SEL // RANK_STRIDE):
        @pl.when(qi // steps_per_variant == variant)
        def _():
            bias_ref[0, 0] = _selection_bias(score, (variant + 1) * RANK_STRIDE).astype(BF16)


def _select(q_raw_t, k_cmp, v_cmp_t, overlap_t):
    nq = SEQ // SELECT_Q
    step = lambda rows: pl.BlockSpec((1, 1, rows, SELECT_Q), lambda b, k, i: (b, k, 0, i))
    whole = lambda *tile: pl.BlockSpec((1, 1) + tile, lambda b, k, i: (b, k) + (0,) * len(tile))
    return pl.pallas_call(
        _select_kernel,
        grid=(BATCH, N_KV_HEADS, nq),
        in_specs=[
            pl.BlockSpec((1, Q_PER_KV, HEAD_DIM, SELECT_Q), lambda b, k, i: (b, k, 0, i)),
            whole(N_CHUNKS, HEAD_DIM), whole(HEAD_DIM, N_CHUNKS),
            _const_spec((N_SEL, N_CHUNKS)),
        ],
        out_specs=[step(N_SEL), step(Q_PER_KV * HEAD_DIM)],
        out_shape=[jax.ShapeDtypeStruct((BATCH, N_KV_HEADS, N_SEL, SEQ), BF16),
                   jax.ShapeDtypeStruct((BATCH, N_KV_HEADS, Q_PER_KV * HEAD_DIM, SEQ), F32)],
        compiler_params=pltpu.CompilerParams(
            dimension_semantics=("parallel", "parallel", "parallel"), vmem_limit_bytes=VMEM_LIMIT),
        name="select",
    )(q_raw_t, k_cmp, v_cmp_t, overlap_t)


KV_PER_STEP = 2


def _attend_kernel(qrot_ref, bias_ref, ocmp_ref, ks_ref, vs_ref, kw_ref, vw_ref, gate_ref,
                   o_ref, m_sc, acc_sc, sa_sc, sb_sc, owin_sc):
    qi = pl.program_id(2)
    s0 = qi * NSA_Q
    key_i = lax.broadcasted_iota(jnp.int32, (NSA_Q, NSA_Q), 0)
    qry_i = lax.broadcasted_iota(jnp.int32, (NSA_Q, NSA_Q), 1)
    causal = key_i <= qry_i
    kvs = range(KV_PER_STEP)
    heads = range(KV_PER_STEP * Q_PER_KV)

    def scores(k, q_t):
        return jnp.dot(k, q_t, preferred_element_type=F32)

    def softmax_step(h, s, v_t):
        m_prev = m_sc[h]
        m_new = jnp.maximum(m_prev, jnp.max(s, axis=0, keepdims=True))
        p = jnp.exp2(s - m_new).astype(BF16)
        acc_sc[h] = jnp.exp2(m_prev - m_new) * acc_sc[h] + jnp.dot(v_t, p, preferred_element_type=F32)
        m_sc[h] = m_new

    q_sel = [jnp.concatenate([qrot_ref[0, h], bias_ref[0, h // Q_PER_KV]], axis=0) for h in heads]
    m_sc[...] = jnp.full_like(m_sc, MASK_VALUE)
    acc_sc[...] = jnp.zeros_like(acc_sc)

    def sel_scores(kt, dst):
        rows = pl.ds(pl.multiple_of(kt * NSA_Q, NSA_Q), NSA_Q)
        k = [ks_ref[0, kv, rows, :] for kv in kvs]
        for h in heads:
            dst[h] = scores(k[h // Q_PER_KV], q_sel[h])

    def sel_softmax(kt, src, diagonal=False):
        v_t = [vs_ref[0, kv, kt] for kv in kvs]
        for h in heads:
            s = src[h]
            softmax_step(h, jnp.where(causal, s, MASK_VALUE) if diagonal else s, v_t[h // Q_PER_KV])

    def window_keys(kv, kt):
        return kw_ref[0, kv, pl.ds(pl.multiple_of(kt * NSA_Q, NSA_Q), NSA_Q), :]

    def penalty_rows(penalty):
        rows = jnp.where(lax.broadcasted_iota(jnp.int32, (HEAD_DIM, NSA_Q), 0) == 0, penalty, 0.0)
        return rows.astype(BF16)

    win_mid, win_far = jnp.maximum(qi - 1, 0), jnp.maximum(qi - 2, 0)
    k_diag = [window_keys(kv, qi)[:, 0:HEAD_DIM] for kv in kvs]
    k_mid = [window_keys(kv, win_mid) for kv in kvs]
    k_far = [window_keys(kv, win_far) for kv in kvs]
    pen_mid = penalty_rows(jnp.where(qi >= 1, 0.0, MASK_VALUE))
    pen_far = penalty_rows(jnp.where(qi >= 2, 0.0, MASK_VALUE))
    s_win_diag, s_win_mid, s_win_far = [], [], []
    for h in heads:
        q_t = qrot_ref[0, h]
        kv = h // Q_PER_KV
        s_win_diag.append(scores(k_diag[kv], q_t))
        s_win_mid.append(scores(k_mid[kv], jnp.concatenate([q_t, pen_mid], axis=0)))
        s_win_far.append(scores(k_far[kv], jnp.concatenate([q_t, pen_far], axis=0)))

    sel_scores(0, sa_sc)

    for h in heads:
        s_edge = jnp.where(causal, s_win_diag[h], s_win_far[h])
        s_mid = s_win_mid[h]
        m_w = jnp.max(jnp.maximum(s_edge, s_mid), axis=0, keepdims=True)
        p_edge = jnp.exp2(s_edge - m_w).astype(BF16)
        p_mid = jnp.exp2(s_mid - m_w).astype(BF16)
        zero = jnp.zeros_like(p_edge)
        kv = h // Q_PER_KV
        acc = (jnp.dot(vw_ref[0, kv, qi], jnp.where(causal, p_edge, zero), preferred_element_type=F32)
               + jnp.dot(vw_ref[0, kv, win_far], jnp.where(causal, zero, p_edge), preferred_element_type=F32)
               + jnp.dot(vw_ref[0, kv, win_mid], p_mid, preferred_element_type=F32))
        owin_sc[h] = acc[:HEAD_DIM] / acc[HEAD_DIM:HEAD_DIM + 1]

    def sel_pair(t, carry):
        sel_scores(2 * t + 1, sb_sc)
        sel_softmax(2 * t, sa_sc)
        sel_scores(2 * t + 2, sa_sc)
        sel_softmax(2 * t + 1, sb_sc)
        return carry

    lax.fori_loop(0, lax.shift_right_logical(qi, 1), sel_pair, 0)

    @pl.when((qi & 1) == 1)
    def _():
        sel_scores(qi, sb_sc)
        sel_softmax(qi - 1, sa_sc)
        sel_softmax(qi, sb_sc, diagonal=True)

    @pl.when((qi & 1) == 0)
    def _():
        sel_softmax(qi, sa_sc, diagonal=True)

    gates_t = [gate_ref[kv].T for kv in kvs]
    outs = []
    for h in heads:
        kv, hh = divmod(h, Q_PER_KV)
        g = gates_t[kv]
        c = hh * N_NSA_BRANCH
        acc = acc_sc[h]
        o_sel = acc[:HEAD_DIM] / acc[HEAD_DIM:HEAD_DIM + 1]
        o_cmp = ocmp_ref[0, kv, hh * HEAD_DIM:(hh + 1) * HEAD_DIM, :]
        outs.append(g[c:c + 1] * o_cmp + g[c + 1:c + 2] * o_sel + g[c + 2:c + 3] * owin_sc[h])
    o_ref[0] = jnp.concatenate(outs, axis=0).T.astype(BF16)


def _attend(q_rot_t, bias, o_cmp_t, k_sel, v_sel_t, k_win, v_win_t, gates):
    nq = SEQ // NSA_Q
    n_heads = KV_PER_STEP * Q_PER_KV
    step = lambda rows: pl.BlockSpec((1, KV_PER_STEP, rows, NSA_Q), lambda b, k, i: (b, k, 0, i))
    whole = lambda *tile: pl.BlockSpec((1, KV_PER_STEP) + tile, lambda b, k, i: (b, k) + (0,) * len(tile))
    return pl.pallas_call(
        _attend_kernel,
        grid=(BATCH, N_KV_HEADS // KV_PER_STEP, nq),
        in_specs=[
            pl.BlockSpec((1, n_heads, HEAD_DIM, NSA_Q), lambda b, k, i: (b, k, 0, i)),
            step(N_SEL), step(Q_PER_KV * HEAD_DIM),
            whole(SEQ, LANES), whole(nq, V_ROWS, NSA_Q),
            whole(SEQ, LANES), whole(nq, V_ROWS, NSA_Q),
            pl.BlockSpec((KV_PER_STEP, NSA_Q, LANES), lambda b, k, i: (k, b * nq + i, 0)),
        ],
        out_specs=pl.BlockSpec((1, NSA_Q, n_heads * HEAD_DIM), lambda b, k, i: (b, i, k)),
        out_shape=jax.ShapeDtypeStruct((BATCH, SEQ, N_HEADS * HEAD_DIM), BF16),
        scratch_shapes=[pltpu.VMEM((n_heads, 1, NSA_Q), F32),
                        pltpu.VMEM((n_heads, V_ROWS, NSA_Q), F32),
                        pltpu.VMEM((n_heads, NSA_Q, NSA_Q), F32),
                        pltpu.VMEM((n_heads, NSA_Q, NSA_Q), F32),
                        pltpu.VMEM((n_heads, HEAD_DIM, NSA_Q), F32)],
        compiler_params=pltpu.CompilerParams(
            dimension_semantics=("parallel", "parallel", "arbitrary"),
            vmem_limit_bytes=VMEM_LIMIT),
        name="attend",
    )(q_rot_t, bias, o_cmp_t, k_sel, v_sel_t, k_win, v_win_t, gates)


def _tail_kernel(h_ref, gu_ref, vn_ref, ob_ref, ga_ref, gb_ref, ws_ref, bs_ref,
                 wa_ref, wb_ref, wo_ref, gn_ref, wi_ref, wf_ref, p_ref, gp_ref, wg_ref, wp_ref,
                 gf_ref, o_ref, *, tm):
    r = lax.broadcasted_iota(jnp.int32, (GM_CHUNK, GM_CHUNK), 0)
    c = lax.broadcasted_iota(jnp.int32, (GM_CHUNK, GM_CHUNK), 1)
    w_tril = [jnp.where(c <= r, ws_ref[g], 0.0).astype(BF16) for g in range(GM_GROUPS)]
    bias = bs_ref[...]
    rows = []
    for ch in range(tm // GM_CHUNK):
        rs = slice(ch * GM_CHUNK, (ch + 1) * GM_CHUNK)
        mix = jnp.concatenate(
            [jnp.dot(w_tril[g], vn_ref[rs, g * LANES:(g + 1) * LANES], preferred_element_type=F32)
             for g in range(GM_GROUPS)], axis=1)
        rows.append(gu_ref[rs, :].astype(F32) * (mix + bias))
    z = jnp.concatenate(rows, axis=0).astype(BF16)
    y_a = jnp.dot(z, wa_ref[...], preferred_element_type=F32)
    y_b = jnp.dot(ob_ref[...], wb_ref[...], preferred_element_type=F32)
    merged = (ga_ref[...].astype(F32) * y_a + gb_ref[...].astype(F32) * y_b).astype(BF16)
    h = h_ref[...] + jnp.dot(merged, wo_ref[...], preferred_element_type=F32)

    h = _half_step_ffn(h, gn_ref[...], wi_ref, wf_ref)

    gate = jax.nn.sigmoid(jnp.dot(_rms(h, gp_ref[...]).astype(BF16), wg_ref[...],
                                  preferred_element_type=F32))
    proj = jnp.dot(p_ref[...].astype(BF16), wp_ref[...], preferred_element_type=F32)
    o_ref[...] = _rms(h + gate * proj, gf_ref[...])


def _tail(h, gu, vn, o_b, ga, gb, w_s, b_s_exp, w_a, w_b, w_o, ffn_norm, ffn_w_in, ffn_w_out,
          p, ple_norm, ple_w_gate, ple_w_proj, final_norm, *, tm=512):
    row = pl.BlockSpec((tm, D_MODEL), lambda i: (i, 0))
    square = _const_spec((D_MODEL, D_MODEL))
    gain = _const_spec((1, D_MODEL))
    return pl.pallas_call(
        functools.partial(_tail_kernel, tm=tm),
        grid=(TOKENS // tm,),
        in_specs=[row, row, row, row, row, row,
                  _const_spec((GM_GROUPS, GM_CHUNK, GM_CHUNK)),
                  _const_spec((GM_CHUNK, GM_WIDTH)),
                  square, square, square,
                  gain, _const_spec((D_MODEL, 2 * D_FF)), _const_spec((D_FF, D_MODEL)),
                  pl.BlockSpec((tm, PLE_DIM), lambda i: (i, 0)), gain, square,
                  _const_spec((PLE_DIM, D_MODEL)), gain],
        out_specs=row,
        out_shape=jax.ShapeDtypeStruct((TOKENS, D_MODEL), F32),
        compiler_params=pltpu.CompilerParams(
            dimension_semantics=("parallel",), vmem_limit_bytes=VMEM_LIMIT),
        name="tail",
    )(h, gu, vn, o_b, ga, gb, w_s, b_s_exp, w_a, w_b, w_o, ffn_norm, ffn_w_in, ffn_w_out,
      p, ple_norm, ple_w_gate, ple_w_proj, final_norm)


def _rope_tables():
    inv_freq = ROPE_THETA ** (-jnp.arange(0, ROPE_DIM, 2, dtype=jnp.float32) / ROPE_DIM)
    ang = jnp.arange(SEQ).astype(jnp.float32)[:, None] * inv_freq[None, :]
    cos, sin = jnp.cos(ang), jnp.sin(ang)
    zero = jnp.zeros_like(cos)
    rest = HEAD_DIM - ROPE_DIM
    c = jnp.concatenate([cos, cos, jnp.ones((SEQ, rest), F32)], axis=1)
    sa = jnp.concatenate([zero, sin, jnp.zeros((SEQ, rest), F32)], axis=1)
    sb = jnp.concatenate([-sin, zero, jnp.zeros((SEQ, rest), F32)], axis=1)
    return [jnp.tile(t, (1, LANES // HEAD_DIM)) for t in (c, sa, sb)]


def _overlap_matrix_t():
    j = jnp.arange(N_SEL)[:, None]
    i = jnp.arange(N_CHUNKS)[None, :]
    r_sel = SEL_LEN // CMP_STRIDE
    l_cmp = CMP_LEN // CMP_STRIDE
    return ((i >= r_sel * j - (l_cmp - 1)) & (i <= r_sel * j + r_sel - 1)).astype(BF16)


def _pack_w_in(w_in):
    u_v_q_kv = w_in[:, :OFF_MERGE]
    gate0 = OFF_MERGE
    merge0 = gate0 + N_HEADS * N_NSA_BRANCH
    gate = w_in[:, gate0:merge0].reshape(D_MODEL, N_KV_HEADS, Q_PER_KV * N_NSA_BRANCH)
    gate = jnp.pad(gate, ((0, 0), (0, 0), (0, LANES - Q_PER_KV * N_NSA_BRANCH)))
    return jnp.concatenate(
        [u_v_q_kv, w_in[:, merge0:], gate.reshape(D_MODEL, N_KV_HEADS * LANES)], axis=1).astype(BF16)


def kernel(x, p, ffn1_norm, ffn1_w_in, ffn1_w_out, mix_norm, w_in, gm_ln_g, gm_ln_b, gm_w_s, gm_b_s,
           w_branch_a, cmp_pos_k, cmp_k_w1, cmp_k_w2, cmp_pos_v, cmp_v_w1, cmp_v_w2, w_branch_b, w_out,
           ffn2_norm, ffn2_w_in, ffn2_w_out, ple_norm, ple_w_gate, ple_w_proj, final_norm):
    assert x.shape == (BATCH, SEQ, D_MODEL) and p.shape == (1, BATCH, SEQ, PLE_DIM)
    row = lambda a: a.reshape(1, -1)
    h = x.reshape(TOKENS, D_MODEL)

    h = _ffn(h, row(ffn1_norm[0]), ffn1_w_in[0].astype(BF16), ffn1_w_out[0].astype(BF16))

    rope_c, rope_sa, rope_sb = _rope_tables()
    (gu, vn, q_raw, q_rot, k_c, v_c, k_sel, v_sel, k_win, v_win, gates, g_a, g_b) = _inproj(
        h, row(mix_norm[0]), _pack_w_in(w_in[0]), row(gm_ln_g[0]), row(gm_ln_b[0]),
        rope_c, rope_sa, rope_sb)

    half = CMP_STRIDE * HEAD_DIM

    def compress(x_heads, pos, w1, w2, feature_major):
        chunks = x_heads.reshape(BATCH, N_KV_HEADS, N_CHUNKS, half)
        w2 = (w2.T if feature_major else w2).astype(BF16)
        return _compress(chunks, pos[:CMP_STRIDE].reshape(1, half), pos[CMP_STRIDE:].reshape(1, half),
                         w1[:half].astype(BF16), w1[half:].astype(BF16), w2, feature_major=feature_major)

    k_cmp = compress(k_c, cmp_pos_k[0], cmp_k_w1[0], cmp_k_w2[0], False)
    v_cmp = compress(v_c, cmp_pos_v[0], cmp_v_w1[0], cmp_v_w2[0], True)

    sel_bias, o_cmp = _select(q_raw, k_cmp, v_cmp, _overlap_matrix_t())
    o_b = _attend(q_rot, sel_bias, o_cmp, k_sel, v_sel, k_win, v_win, gates)

    b_s_exp = jnp.repeat(gm_b_s[0].T, GM_WIDTH // GM_GROUPS, axis=1)
    out = _tail(h, gu, vn, o_b.reshape(TOKENS, N_HEADS * HEAD_DIM), g_a, g_b, gm_w_s[0], b_s_exp,
                w_branch_a[0].astype(BF16), w_branch_b[0].astype(BF16), w_out[0].astype(BF16),
                row(ffn2_norm[0]), ffn2_w_in[0].astype(BF16), ffn2_w_out[0].astype(BF16),
                p[0].reshape(TOKENS, PLE_DIM), row(ple_norm[0]), ple_w_gate[0].astype(BF16),
                ple_w_proj[0].astype(BF16), row(final_norm))
    return out.reshape(BATCH, SEQ, D_MODEL)
```

```python
import functools

import jax
import jax.numpy as jnp
from jax import lax
from jax.experimental import pallas as pl
from jax.experimental.pallas import tpu as pltpu

D_MODEL = 1024
BATCH = 4
SEQ = 4096
PLE_DIM = 256
D_FF = 2816
NORM_EPS = 1e-6
GM_WIDTH = 1024
GM_GROUPS = 8
GM_CHUNK = 128
N_HEADS = 16
N_KV_HEADS = 4
HEAD_DIM = 64
Q_PER_KV = N_HEADS // N_KV_HEADS
KV_WIDTH = N_KV_HEADS * HEAD_DIM
ROPE_DIM = HEAD_DIM // 4
ROPE_HALF = ROPE_DIM // 2
ROPE_THETA = 500000.0
CMP_LEN = 32
CMP_STRIDE = 16
CMP_HIDDEN = 256
SEL_LEN = 64
SEL_TOP = 16
WINDOW = 512
N_NSA_BRANCH = 3
MASK_VALUE = -1e30
FORCE_SCORE = 1e9
LOG2_E = 1.4426950408889634

TOKENS = BATCH * SEQ
N_CHUNKS = SEQ // CMP_STRIDE
N_SEL = SEQ // SEL_LEN
LANES = 128
SUBLANES = 8
VMEM_LIMIT = 56 * 1024 * 1024

OFF_U = 0
OFF_V = OFF_U + GM_WIDTH
OFF_Q = OFF_V + GM_WIDTH
OFF_KV = OFF_Q + N_HEADS * HEAD_DIM
OFF_NSA_GATE = OFF_KV + 6 * KV_WIDTH
OFF_MERGE = OFF_NSA_GATE + N_HEADS * N_NSA_BRANCH
SEL_SHIFT = SEL_LEN.bit_length() - 1

F32 = jnp.float32
BF16 = jnp.bfloat16

NT_DIMS = (((1,), (1,)), ((), ()))


def _const_spec(shape):
    nd = len(shape)
    return pl.BlockSpec(shape, lambda *_: (0,) * nd, pipeline_mode=pl.Buffered(1))


def _rms(x, g):
    return x * lax.rsqrt(jnp.mean(x * x, axis=-1, keepdims=True) + NORM_EPS) * g


MXU_WIDTH = 256


def _half_step_ffn(x, g, wi_ref, wo_ref):
    xn = _rms(x, g).astype(BF16)
    acc = None
    for c in range(D_FF // MXU_WIDTH):
        lo = c * MXU_WIDTH
        gate = jnp.dot(xn, wi_ref[:, lo:lo + MXU_WIDTH], preferred_element_type=F32)
        up = jnp.dot(xn, wi_ref[:, D_FF + lo:D_FF + lo + MXU_WIDTH], preferred_element_type=F32)
        act = (gate * jax.nn.sigmoid(gate) * up).astype(BF16)
        part = jnp.dot(act, wo_ref[lo:lo + MXU_WIDTH, :], preferred_element_type=F32)
        acc = part if acc is None else acc + part
    return x + 0.5 * acc


def _ffn_kernel(x_ref, g_ref, wi_ref, wo_ref, o_ref):
    o_ref[...] = _half_step_ffn(x_ref[...], g_ref[...], wi_ref, wo_ref)


def _ffn(x, g, w_in, w_out, *, tm=512):
    row = pl.BlockSpec((tm, D_MODEL), lambda i: (i, 0))
    return pl.pallas_call(
        _ffn_kernel,
        grid=(TOKENS // tm,),
        in_specs=[row, _const_spec((1, D_MODEL)), _const_spec((D_MODEL, 2 * D_FF)),
                  _const_spec((D_FF, D_MODEL))],
        out_specs=row,
        out_shape=jax.ShapeDtypeStruct((TOKENS, D_MODEL), F32),
        compiler_params=pltpu.CompilerParams(
            dimension_semantics=("parallel",), vmem_limit_bytes=VMEM_LIMIT),
        name="ffn",
    )(x, g, w_in, w_out)


NSA_Q = 256
BF16_ROWS = 16
V_ROWS = HEAD_DIM + BF16_ROWS


def _rope(x, c, sa, sb):
    w = x.shape[1]
    return x * c + pltpu.roll(x, ROPE_HALF, 1) * sa + pltpu.roll(x, w - ROPE_HALF, 1) * sb


def _inproj_kernel(h_ref, g_ref, w_ref, wm_ref, wg_ref, lng_ref, lnb_ref, c_ref, sa_ref, sb_ref,
                   gu_ref, vn_ref, qraw_ref, qrot_ref, kc_ref, vc_ref, ks_ref, vs_ref,
                   kw_ref, vw_ref, gate_ref, ga_ref, gb_ref, *, tm):
    n = _rms(h_ref[...], g_ref[...]).astype(BF16)

    def seg(lo, width):
        return jnp.dot(n, w_ref[:, lo:lo + width], preferred_element_type=F32)

    gu_ref[...] = jax.nn.gelu(seg(OFF_U, GM_WIDTH)).astype(BF16)
    v = jax.nn.gelu(seg(OFF_V, GM_WIDTH))
    mu = jnp.mean(v, axis=-1, keepdims=True)
    vc = v - mu
    var = jnp.mean(vc * vc, axis=-1, keepdims=True)
    vn_ref[...] = (vc * lax.rsqrt(var + NORM_EPS) * lng_ref[...] + lnb_ref[...]).astype(BF16)

    c, sa, sb = c_ref[...], sa_ref[...], sb_ref[...]

    def tile_lanes(t, reps):
        return jnp.concatenate([t] * reps, axis=1)

    q = seg(OFF_Q, N_HEADS * HEAD_DIM) * (HEAD_DIM ** -0.5 * LOG2_E)
    reps = N_HEADS * HEAD_DIM // LANES
    q_rot = _rope(q, tile_lanes(c, reps), tile_lanes(sa, reps), tile_lanes(sb, reps))
    qraw_ref[0] = q.T.astype(BF16).reshape(N_HEADS, HEAD_DIM, tm)
    qrot_ref[0] = q_rot.T.astype(BF16).reshape(N_HEADS, HEAD_DIM, tm)

    kv = seg(OFF_KV, 6 * KV_WIDTH)
    k_c, v_c, k_s, v_s, k_w, v_w = [kv[:, i * KV_WIDTH:(i + 1) * KV_WIDTH] for i in range(6)]
    reps = KV_WIDTH // LANES
    ck, sak, sbk = tile_lanes(c, reps), tile_lanes(sa, reps), tile_lanes(sb, reps)
    k_s = _rope(k_s, ck, sak, sbk)
    k_w = _rope(k_w, ck, sak, sbk)
    pos = (pl.program_id(0) % (SEQ // tm)) * tm + lax.broadcasted_iota(jnp.int32, (tm, N_SEL), 0)
    blk = lax.broadcasted_iota(jnp.int32, (tm, N_SEL), 1)
    onehot = jnp.where(lax.shift_right_logical(pos, SEL_SHIFT) == blk, 1.0, 0.0).astype(F32)
    ones_t = jnp.ones((V_ROWS - HEAD_DIM, tm), F32)
    ones_col = jnp.where(lax.broadcasted_iota(jnp.int32, (tm, HEAD_DIM), 1) == 0, 1.0, 0.0)
    v_s_t, v_w_t = v_s.T, v_w.T
    for h in range(N_KV_HEADS):
        sl = slice(h * HEAD_DIM, (h + 1) * HEAD_DIM)
        kc_ref[0, h] = k_c[:, sl]
        vc_ref[0, h] = v_c[:, sl]
        ks_ref[0, h] = jnp.concatenate([k_s[:, sl], onehot], axis=1).astype(BF16)
        vs_ref[0, h, 0] = jnp.concatenate([v_s_t[sl], ones_t], axis=0).astype(BF16)
        kw_ref[0, h] = jnp.concatenate([k_w[:, sl], ones_col], axis=1).astype(BF16)
        vw_ref[0, h, 0] = jnp.concatenate([v_w_t[sl], ones_t], axis=0).astype(BF16)

    merge = jax.nn.sigmoid(jnp.dot(n, wm_ref[...], preferred_element_type=F32))
    ga_ref[...] = merge[:, :D_MODEL].astype(BF16)
    gb_ref[...] = merge[:, D_MODEL:].astype(BF16)
    for h in range(N_KV_HEADS):
        gate_ref[h] = jax.nn.sigmoid(
            jnp.dot(n, wg_ref[:, h * LANES:(h + 1) * LANES], preferred_element_type=F32))


def _inproj(h, g, w_main, w_merge, w_gate, ln_g, ln_b, rope_c, rope_sa, rope_sb):
    tm = NSA_Q
    nq = SEQ // tm
    row = lambda i: (i, 0)
    head = lambda i: (i // nq, 0, i % nq, 0)
    tok_bf = jax.ShapeDtypeStruct((TOKENS, D_MODEL), BF16)

    def head_shape(nh, width, dtype):
        return jax.ShapeDtypeStruct((BATCH, nh, SEQ, width), dtype)

    def head_spec(nh, width):
        return pl.BlockSpec((1, nh, tm, width), head)

    qt_spec = pl.BlockSpec((1, N_HEADS, HEAD_DIM, tm), lambda i: (i // nq, 0, 0, i % nq))
    qt_shape = jax.ShapeDtypeStruct((BATCH, N_HEADS, HEAD_DIM, SEQ), BF16)
    vt_spec = pl.BlockSpec((1, N_KV_HEADS, 1, V_ROWS, tm), lambda i: (i // nq, 0, i % nq, 0, 0))
    vt_shape = jax.ShapeDtypeStruct((BATCH, N_KV_HEADS, nq, V_ROWS, tm), BF16)

    return pl.pallas_call(
        functools.partial(_inproj_kernel, tm=tm),
        grid=(TOKENS // tm,),
        in_specs=[
            pl.BlockSpec((tm, D_MODEL), row),
            _const_spec((1, D_MODEL)),
            _const_spec((D_MODEL, OFF_NSA_GATE)),
            _const_spec((D_MODEL, 2 * D_MODEL)),
            _const_spec((D_MODEL, N_KV_HEADS * LANES)),
            _const_spec((1, GM_WIDTH)),
            _const_spec((1, GM_WIDTH)),
            pl.BlockSpec((tm, LANES), lambda i: (i % nq, 0)),
            pl.BlockSpec((tm, LANES), lambda i: (i % nq, 0)),
            pl.BlockSpec((tm, LANES), lambda i: (i % nq, 0)),
        ],
        out_specs=[
            pl.BlockSpec((tm, GM_WIDTH), row),
            pl.BlockSpec((tm, GM_WIDTH), row),
            qt_spec,
            qt_spec,
            head_spec(N_KV_HEADS, HEAD_DIM),
            head_spec(N_KV_HEADS, HEAD_DIM),
            head_spec(N_KV_HEADS, LANES),
            vt_spec,
            head_spec(N_KV_HEADS, LANES),
            vt_spec,
            pl.BlockSpec((N_KV_HEADS, tm, LANES), lambda i: (0, i, 0)),
            pl.BlockSpec((tm, D_MODEL), row),
            pl.BlockSpec((tm, D_MODEL), row),
        ],
        out_shape=[
            tok_bf, tok_bf,
            qt_shape, qt_shape,
            head_shape(N_KV_HEADS, HEAD_DIM, F32), head_shape(N_KV_HEADS, HEAD_DIM, F32),
            head_shape(N_KV_HEADS, LANES, BF16), vt_shape,
            head_shape(N_KV_HEADS, LANES, BF16), vt_shape,
            jax.ShapeDtypeStruct((N_KV_HEADS, TOKENS, LANES), F32),
            tok_bf, tok_bf,
        ],
        compiler_params=pltpu.CompilerParams(
            dimension_semantics=("parallel",), vmem_limit_bytes=VMEM_LIMIT),
        name="inproj",
    )(h, g, w_main, w_merge, w_gate, ln_g, ln_b, rope_c, rope_sa, rope_sb)


def _compress_kernel(x_ref, ptop_ref, pbot_ref, w1t_ref, w1b_ref, w2_ref, o_ref, *, feature_major):
    x = jnp.concatenate([x_ref[0, 0, pl.ds(j, N_CHUNKS, stride=CMP_STRIDE), :]
                         for j in range(CMP_STRIDE)], axis=1)
    top = jnp.dot((x + ptop_ref[...]).astype(BF16), w1t_ref[...], preferred_element_type=F32)
    bot = jnp.dot((x + pbot_ref[...]).astype(BF16), w1b_ref[...], preferred_element_type=F32)
    hidden = jax.nn.gelu(top + pltpu.roll(bot, N_CHUNKS - 1, 0)).astype(BF16)
    if feature_major:
        out = lax.dot_general(w2_ref[...], hidden, NT_DIMS, preferred_element_type=F32)
    else:
        out = jnp.dot(hidden, w2_ref[...], preferred_element_type=F32)
    o_ref[0, 0] = out.astype(BF16)


def _compress(x_heads, pos_top, pos_bot, w1_top, w1_bot, w2, *, feature_major):
    half = CMP_STRIDE * HEAD_DIM
    out_tile = (HEAD_DIM, N_CHUNKS) if feature_major else (N_CHUNKS, HEAD_DIM)
    return pl.pallas_call(
        functools.partial(_compress_kernel, feature_major=feature_major),
        grid=(BATCH, N_KV_HEADS),
        in_specs=[
            pl.BlockSpec((1, 1, SEQ, HEAD_DIM), lambda b, k: (b, k, 0, 0)),
            _const_spec((1, half)),
            _const_spec((1, half)),
            _const_spec((half, CMP_HIDDEN)),
            _const_spec((half, CMP_HIDDEN)),
            _const_spec(w2.shape),
        ],
        out_specs=pl.BlockSpec((1, 1) + out_tile, lambda b, k: (b, k, 0, 0)),
        out_shape=jax.ShapeDtypeStruct((BATCH, N_KV_HEADS) + out_tile, BF16),
        compiler_params=pltpu.CompilerParams(dimension_semantics=("parallel", "parallel")),
        name="compress",
    )(x_heads, pos_top, pos_bot, w1_top, w1_bot, w2)


RANK_STRIDE = 16
RANK_ACCUMULATORS = 4
SELECT_Q = 1024


def _split_bf16(x):
    hi = x.astype(BF16)
    r1 = x - hi.astype(F32)
    mid = r1.astype(BF16)
    lo = (r1 - mid.astype(F32)).astype(BF16)
    return hi, mid, lo


def _selection_bias(score, n_blocks):
    n_q = score.shape[1]
    sub_iota = lax.broadcasted_iota(jnp.int32, (SUBLANES, n_q), 0)
    groups = [score[g * SUBLANES:(g + 1) * SUBLANES] for g in range(n_blocks // SUBLANES)]
    counts = [[None] * RANK_ACCUMULATORS for _ in groups]
    for jp in range(n_blocks):
        other = score[jp:jp + 1, :]
        for g, grp in enumerate(groups):
            lo = g * SUBLANES
            if lo > jp:
                before = other >= grp
            elif lo + SUBLANES - 1 <= jp:
                before = other > grp
            else:
                before = (other > grp) | ((other == grp) & (sub_iota + lo > jp))
            inc = jnp.where(before, 1, 0)
            a = jp % RANK_ACCUMULATORS
            counts[g][a] = inc if counts[g][a] is None else counts[g][a] + inc
    rank = jnp.concatenate([(c[0] + c[1]) + (c[2] + c[3]) for c in counts], axis=0)
    bias = jnp.where(rank < SEL_TOP, 0.0, MASK_VALUE)
    if n_blocks < N_SEL:
        bias = jnp.concatenate([bias, jnp.zeros((N_SEL - n_blocks, n_q), F32)], axis=0)
    return bias


def _select_kernel(qraw_ref, kcmp_ref, vcmp_ref, mt_ref, bias_ref, ocmp_ref):
    qi = pl.program_id(2)
    s0 = qi * SELECT_Q
    key_i = lax.broadcasted_iota(jnp.int32, (N_CHUNKS, SELECT_Q), 0)
    qry_i = lax.broadcasted_iota(jnp.int32, (N_CHUNKS, SELECT_Q), 1)
    heads = range(Q_PER_KV)

    k_cmp = kcmp_ref[0, 0]
    v_cmp_t = vcmp_ref[0, 0]
    s_cmp = [jnp.dot(k_cmp, qraw_ref[0, h], preferred_element_type=F32) for h in heads]
    cmp_valid = key_i * CMP_STRIDE + (CMP_LEN - 1) <= s0 + qry_i
    any_valid = s0 + lax.broadcasted_iota(jnp.int32, (1, SELECT_Q), 1) >= CMP_LEN - 1
    p_cmp = []
    for h in heads:
        s_c = jnp.where(cmp_valid, s_cmp[h], MASK_VALUE)
        e_c = jnp.exp2(s_c - jnp.max(s_c, axis=0, keepdims=True))
        p_cmp.append(e_c * jnp.where(any_valid, 1.0 / jnp.sum(e_c, axis=0, keepdims=True), 0.0))
    for h in heads:
        ocmp_ref[0, 0, h * HEAD_DIM:(h + 1) * HEAD_DIM, :] = jnp.dot(
            v_cmp_t, p_cmp[h].astype(BF16), preferred_element_type=F32)

    p_sum = (p_cmp[0] + p_cmp[1]) + (p_cmp[2] + p_cmp[3])
    mt = mt_ref[...]
    imp_t = sum(jnp.dot(mt, part, preferred_element_type=F32)
                for part in _split_bf16(p_sum))
    blk = lax.broadcasted_iota(jnp.int32, (N_SEL, SELECT_Q), 0)
    cur = lax.shift_right_logical(s0 + lax.broadcasted_iota(jnp.int32, (N_SEL, SELECT_Q), 1), SEL_SHIFT)
    forced = (blk == 0) | (blk == cur) | (blk == cur - 1)
    score = jnp.where(forced, FORCE_SCORE, jnp.where(blk > cur, -FORCE_SCORE, imp_t))

    steps_per_variant = RANK_STRIDE * SEL_LEN // SELECT_Q
    for variant in range(N_SEL // RANK_STRIDE):
        @pl.when(qi // steps_per_variant == variant)
        def _():
            bias_ref[0, 0] = _selection_bias(score, (variant + 1) * RANK_STRIDE).astype(BF16)


def _select(q_raw_t, k_cmp, v_cmp_t, overlap_t):
    nq = SEQ // SELECT_Q
    step = lambda rows: pl.BlockSpec((1, 1, rows, SELECT_Q), lambda b, k, i: (b, k, 0, i))
    whole = lambda *tile: pl.BlockSpec((1, 1) + tile, lambda b, k, i: (b, k) + (0,) * len(tile))
    return pl.pallas_call(
        _select_kernel,
        grid=(BATCH, N_KV_HEADS, nq),
        in_specs=[
            pl.BlockSpec((1, Q_PER_KV, HEAD_DIM, SELECT_Q), lambda b, k, i: (b, k, 0, i)),
            whole(N_CHUNKS, HEAD_DIM), whole(HEAD_DIM, N_CHUNKS),
            _const_spec((N_SEL, N_CHUNKS)),
        ],
        out_specs=[step(N_SEL), step(Q_PER_KV * HEAD_DIM)],
        out_shape=[jax.ShapeDtypeStruct((BATCH, N_KV_HEADS, N_SEL, SEQ), BF16),
                   jax.ShapeDtypeStruct((BATCH, N_KV_HEADS, Q_PER_KV * HEAD_DIM, SEQ), F32)],
        compiler_params=pltpu.CompilerParams(
            dimension_semantics=("parallel", "parallel", "parallel"), vmem_limit_bytes=VMEM_LIMIT),
        name="select",
    )(q_raw_t, k_cmp, v_cmp_t, overlap_t)


KV_PER_STEP = 2


def _attend_kernel(qrot_ref, bias_ref, ocmp_ref, ks_ref, vs_ref, kw_ref, vw_ref, gate_ref,
                   o_ref, m_sc, acc_sc, sa_sc, sb_sc, owin_sc):
    qi = pl.program_id(2)
    s0 = qi * NSA_Q
    key_i = lax.broadcasted_iota(jnp.int32, (NSA_Q, NSA_Q), 0)
    qry_i = lax.broadcasted_iota(jnp.int32, (NSA_Q, NSA_Q), 1)
    causal = key_i <= qry_i
    kvs = range(KV_PER_STEP)
    heads = range(KV_PER_STEP * Q_PER_KV)

    def scores(k, q_t):
        return jnp.dot(k, q_t, preferred_element_type=F32)

    def softmax_step(h, s, v_t):
        m_prev = m_sc[h]
        m_new = jnp.maximum(m_prev, jnp.max(s, axis=0, keepdims=True))
        p = jnp.exp2(s - m_new).astype(BF16)
        acc_sc[h] = jnp.exp2(m_prev - m_new) * acc_sc[h] + jnp.dot(v_t, p, preferred_element_type=F32)
        m_sc[h] = m_new

    q_sel = [jnp.concatenate([qrot_ref[0, h], bias_ref[0, h // Q_PER_KV]], axis=0) for h in heads]
    m_sc[...] = jnp.full_like(m_sc, MASK_VALUE)
    acc_sc[...] = jnp.zeros_like(acc_sc)

    def sel_scores(kt, dst):
        rows = pl.ds(pl.multiple_of(kt * NSA_Q, NSA_Q), NSA_Q)
        k = [ks_ref[0, kv, rows, :] for kv in kvs]
        for h in heads:
            dst[h] = scores(k[h // Q_PER_KV], q_sel[h])

    def sel_softmax(kt, src, diagonal=False):
        v_t = [vs_ref[0, kv, kt] for kv in kvs]
        for h in heads:
            s = src[h]
            softmax_step(h, jnp.where(causal, s, MASK_VALUE) if diagonal else s, v_t[h // Q_PER_KV])

    def window_keys(kv, kt):
        return kw_ref[0, kv, pl.ds(pl.multiple_of(kt * NSA_Q, NSA_Q), NSA_Q), :]

    def penalty_rows(penalty):
        rows = jnp.where(lax.broadcasted_iota(jnp.int32, (HEAD_DIM, NSA_Q), 0) == 0, penalty, 0.0)
        return rows.astype(BF16)

    win_mid, win_far = jnp.maximum(qi - 1, 0), jnp.maximum(qi - 2, 0)
    k_diag = [window_keys(kv, qi)[:, 0:HEAD_DIM] for kv in kvs]
    k_mid = [window_keys(kv, win_mid) for kv in kvs]
    k_far = [window_keys(kv, win_far) for kv in kvs]
    pen_mid = penalty_rows(jnp.where(qi >= 1, 0.0, MASK_VALUE))
    pen_far = penalty_rows(jnp.where(qi >= 2, 0.0, MASK_VALUE))
    s_win_diag, s_win_mid, s_win_far = [], [], []
    for h in heads:
        q_t = qrot_ref[0, h]
        kv = h // Q_PER_KV
        s_win_diag.append(scores(k_diag[kv], q_t))
        s_win_mid.append(scores(k_mid[kv], jnp.concatenate([q_t, pen_mid], axis=0)))
        s_win_far.append(scores(k_far[kv], jnp.concatenate([q_t, pen_far], axis=0)))

    sel_scores(0, sa_sc)

    for h in heads:
        s_edge = jnp.where(causal, s_win_diag[h], s_win_far[h])
        s_mid = s_win_mid[h]
        m_w = jnp.max(jnp.maximum(s_edge, s_mid), axis=0, keepdims=True)
        p_edge = jnp.exp2(s_edge - m_w).astype(BF16)
        p_mid = jnp.exp2(s_mid - m_w).astype(BF16)
        zero = jnp.zeros_like(p_edge)
        kv = h // Q_PER_KV
        acc = (jnp.dot(vw_ref[0, kv, qi], jnp.where(causal, p_edge, zero), preferred_element_type=F32)
               + jnp.dot(vw_ref[0, kv, win_far], jnp.where(causal, zero, p_edge), preferred_element_type=F32)
               + jnp.dot(vw_ref[0, kv, win_mid], p_mid, preferred_element_type=F32))
        owin_sc[h] = acc[:HEAD_DIM] / acc[HEAD_DIM:HEAD_DIM + 1]

    def sel_pair(t, carry):
        sel_scores(2 * t + 1, sb_sc)
        sel_softmax(2 * t, sa_sc)
        sel_scores(2 * t + 2, sa_sc)
        sel_softmax(2 * t + 1, sb_sc)
        return carry

    lax.fori_loop(0, lax.shift_right_logical(qi, 1), sel_pair, 0)

    @pl.when((qi & 1) == 1)
    def _():
        sel_scores(qi, sb_sc)
        sel_softmax(qi - 1, sa_sc)
        sel_softmax(qi, sb_sc, diagonal=True)

    @pl.when((qi & 1) == 0)
    def _():
        sel_softmax(qi, sa_sc, diagonal=True)

    gates_t = [gate_ref[kv].T for kv in kvs]
    outs = []
    for h in heads:
        kv, hh = divmod(h, Q_PER_KV)
        g = gates_t[kv]
        c = hh * N_NSA_BRANCH
        acc = acc_sc[h]
        o_sel = acc[:HEAD_DIM] / acc[HEAD_DIM:HEAD_DIM + 1]
        o_cmp = ocmp_ref[0, kv, hh * HEAD_DIM:(hh + 1) * HEAD_DIM, :]
        outs.append(g[c:c + 1] * o_cmp + g[c + 1:c + 2] * o_sel + g[c + 2:c + 3] * owin_sc[h])
    o_ref[0] = jnp.concatenate(outs, axis=0).T.astype(BF16)


def _attend(q_rot_t, bias, o_cmp_t, k_sel, v_sel_t, k_win, v_win_t, gates):
    nq = SEQ // NSA_Q
    n_heads = KV_PER_STEP * Q_PER_KV
    step = lambda rows: pl.BlockSpec((1, KV_PER_STEP, rows, NSA_Q), lambda b, k, i: (b, k, 0, i))
    whole = lambda *tile: pl.BlockSpec((1, KV_PER_STEP) + tile, lambda b, k, i: (b, k) + (0,) * len(tile))
    return pl.pallas_call(
        _attend_kernel,
        grid=(BATCH, N_KV_HEADS // KV_PER_STEP, nq),
        in_specs=[
            pl.BlockSpec((1, n_heads, HEAD_DIM, NSA_Q), lambda b, k, i: (b, k, 0, i)),
            step(N_SEL), step(Q_PER_KV * HEAD_DIM),
            whole(SEQ, LANES), whole(nq, V_ROWS, NSA_Q),
            whole(SEQ, LANES), whole(nq, V_ROWS, NSA_Q),
            pl.BlockSpec((KV_PER_STEP, NSA_Q, LANES), lambda b, k, i: (k, b * nq + i, 0)),
        ],
        out_specs=pl.BlockSpec((1, NSA_Q, n_heads * HEAD_DIM), lambda b, k, i: (b, i, k)),
        out_shape=jax.ShapeDtypeStruct((BATCH, SEQ, N_HEADS * HEAD_DIM), BF16),
        scratch_shapes=[pltpu.VMEM((n_heads, 1, NSA_Q), F32),
                        pltpu.VMEM((n_heads, V_ROWS, NSA_Q), F32),
                        pltpu.VMEM((n_heads, NSA_Q, NSA_Q), F32),
                        pltpu.VMEM((n_heads, NSA_Q, NSA_Q), F32),
                        pltpu.VMEM((n_heads, HEAD_DIM, NSA_Q), F32)],
        compiler_params=pltpu.CompilerParams(
            dimension_semantics=("parallel", "parallel", "arbitrary"),
            vmem_limit_bytes=VMEM_LIMIT),
        name="attend",
    )(q_rot_t, bias, o_cmp_t, k_sel, v_sel_t, k_win, v_win_t, gates)


def _tail_kernel(h_ref, gu_ref, vn_ref, ob_ref, ga_ref, gb_ref, ws_ref, bs_ref,
                 wa_ref, wb_ref, wo_ref, gn_ref, wi_ref, wf_ref, p_ref, gp_ref, wg_ref, wp_ref,
                 gf_ref, o_ref, *, tm):
    r = lax.broadcasted_iota(jnp.int32, (GM_CHUNK, GM_CHUNK), 0)
    c = lax.broadcasted_iota(jnp.int32, (GM_CHUNK, GM_CHUNK), 1)
    w_tril = [jnp.where(c <= r, ws_ref[g], 0.0).astype(BF16) for g in range(GM_GROUPS)]
    bias = bs_ref[...]
    rows = []
    for ch in range(tm // GM_CHUNK):
        rs = slice(ch * GM_CHUNK, (ch + 1) * GM_CHUNK)
        mix = jnp.concatenate(
            [jnp.dot(w_tril[g], vn_ref[rs, g * LANES:(g + 1) * LANES], preferred_element_type=F32)
             for g in range(GM_GROUPS)], axis=1)
        rows.append(gu_ref[rs, :].astype(F32) * (mix + bias))
    z = jnp.concatenate(rows, axis=0).astype(BF16)
    y_a = jnp.dot(z, wa_ref[...], preferred_element_type=F32)
    y_b = jnp.dot(ob_ref[...], wb_ref[...], preferred_element_type=F32)
    merged = (ga_ref[...].astype(F32) * y_a + gb_ref[...].astype(F32) * y_b).astype(BF16)
    h = h_ref[...] + jnp.dot(merged, wo_ref[...], preferred_element_type=F32)

    h = _half_step_ffn(h, gn_ref[...], wi_ref, wf_ref)

    gate = jax.nn.sigmoid(jnp.dot(_rms(h, gp_ref[...]).astype(BF16), wg_ref[...],
                                  preferred_element_type=F32))
    proj = jnp.dot(p_ref[...].astype(BF16), wp_ref[...], preferred_element_type=F32)
    o_ref[...] = _rms(h + gate * proj, gf_ref[...])


def _tail(h, gu, vn, o_b, ga, gb, w_s, b_s_exp, w_a, w_b, w_o, ffn_norm, ffn_w_in, ffn_w_out,
          p, ple_norm, ple_w_gate, ple_w_proj, final_norm, *, tm=512):
    row = pl.BlockSpec((tm, D_MODEL), lambda i: (i, 0))
    square = _const_spec((D_MODEL, D_MODEL))
    gain = _const_spec((1, D_MODEL))
    return pl.pallas_call(
        functools.partial(_tail_kernel, tm=tm),
        grid=(TOKENS // tm,),
        in_specs=[row, row, row, row, row, row,
                  _const_spec((GM_GROUPS, GM_CHUNK, GM_CHUNK)),
                  _const_spec((GM_CHUNK, GM_WIDTH)),
                  square, square, square,
                  gain, _const_spec((D_MODEL, 2 * D_FF)), _const_spec((D_FF, D_MODEL)),
                  pl.BlockSpec((tm, PLE_DIM), lambda i: (i, 0)), gain, square,
                  _const_spec((PLE_DIM, D_MODEL)), gain],
        out_specs=row,
        out_shape=jax.ShapeDtypeStruct((TOKENS, D_MODEL), F32),
        compiler_params=pltpu.CompilerParams(
            dimension_semantics=("parallel",), vmem_limit_bytes=VMEM_LIMIT),
        name="tail",
    )(h, gu, vn, o_b, ga, gb, w_s, b_s_exp, w_a, w_b, w_o, ffn_norm, ffn_w_in, ffn_w_out,
      p, ple_norm, ple_w_gate, ple_w_proj, final_norm)


def _rope_tables():
    inv_freq = ROPE_THETA ** (-jnp.arange(0, ROPE_DIM, 2, dtype=jnp.float32) / ROPE_DIM)
    ang = jnp.arange(SEQ).astype(jnp.float32)[:, None] * inv_freq[None, :]
    cos, sin = jnp.cos(ang), jnp.sin(ang)
    zero = jnp.zeros_like(cos)
    rest = HEAD_DIM - ROPE_DIM
    c = jnp.concatenate([cos, cos, jnp.ones((SEQ, rest), F32)], axis=1)
    sa = jnp.concatenate([zero, sin, jnp.zeros((SEQ, rest), F32)], axis=1)
    sb = jnp.concatenate([-sin, zero, jnp.zeros((SEQ, rest), F32)], axis=1)
    return [jnp.tile(t, (1, LANES // HEAD_DIM)) for t in (c, sa, sb)]


def _overlap_matrix_t():
    j = jnp.arange(N_SEL)[:, None]
    i = jnp.arange(N_CHUNKS)[None, :]
    r_sel = SEL_LEN // CMP_STRIDE
    l_cmp = CMP_LEN // CMP_STRIDE
    return ((i >= r_sel * j - (l_cmp - 1)) & (i <= r_sel * j + r_sel - 1)).astype(BF16)


def _split_w_in(w_in):
    gate = w_in[:, OFF_NSA_GATE:OFF_MERGE].reshape(D_MODEL, N_KV_HEADS, Q_PER_KV * N_NSA_BRANCH)
    gate = jnp.pad(gate, ((0, 0), (0, 0), (0, LANES - Q_PER_KV * N_NSA_BRANCH)))
    return (w_in[:, :OFF_NSA_GATE].astype(BF16), w_in[:, OFF_MERGE:].astype(BF16),
            gate.reshape(D_MODEL, N_KV_HEADS * LANES).astype(BF16))


def kernel(x, p, ffn1_norm, ffn1_w_in, ffn1_w_out, mix_norm, w_in, gm_ln_g, gm_ln_b, gm_w_s, gm_b_s,
           w_branch_a, cmp_pos_k, cmp_k_w1, cmp_k_w2, cmp_pos_v, cmp_v_w1, cmp_v_w2, w_branch_b, w_out,
           ffn2_norm, ffn2_w_in, ffn2_w_out, ple_norm, ple_w_gate, ple_w_proj, final_norm):
    assert x.shape == (BATCH, SEQ, D_MODEL) and p.shape == (1, BATCH, SEQ, PLE_DIM)
    row = lambda a: a.reshape(1, -1)
    h = x.reshape(TOKENS, D_MODEL)

    h = _ffn(h, row(ffn1_norm[0]), ffn1_w_in[0].astype(BF16), ffn1_w_out[0].astype(BF16))

    rope_c, rope_sa, rope_sb = _rope_tables()
    (gu, vn, q_raw, q_rot, k_c, v_c, k_sel, v_sel, k_win, v_win, gates, g_a, g_b) = _inproj(
        h, row(mix_norm[0]), *_split_w_in(w_in[0]), row(gm_ln_g[0]), row(gm_ln_b[0]),
        rope_c, rope_sa, rope_sb)

    half = CMP_STRIDE * HEAD_DIM

    def compress(x_heads, pos, w1, w2, feature_major):
        w2 = (w2.T if feature_major else w2).astype(BF16)
        return _compress(x_heads, pos[:CMP_STRIDE].reshape(1, half), pos[CMP_STRIDE:].reshape(1, half),
                         w1[:half].astype(BF16), w1[half:].astype(BF16), w2, feature_major=feature_major)

    k_cmp = compress(k_c, cmp_pos_k[0], cmp_k_w1[0], cmp_k_w2[0], False)
    v_cmp = compress(v_c, cmp_pos_v[0], cmp_v_w1[0], cmp_v_w2[0], True)

    sel_bias, o_cmp = _select(q_raw, k_cmp, v_cmp, _overlap_matrix_t())
    o_b = _attend(q_rot, sel_bias, o_cmp, k_sel, v_sel, k_win, v_win, gates)

    b_s_exp = jnp.repeat(gm_b_s[0].T, GM_WIDTH // GM_GROUPS, axis=1)
    out = _tail(h, gu, vn, o_b.reshape(TOKENS, N_HEADS * HEAD_DIM), g_a, g_b, gm_w_s[0], b_s_exp,
                w_branch_a[0].astype(BF16), w_branch_b[0].astype(BF16), w_out[0].astype(BF16),
                row(ffn2_norm[0]), ffn2_w_in[0].astype(BF16), ffn2_w_out[0].astype(BF16),
                p[0].reshape(TOKENS, PLE_DIM), row(ple_norm[0]), ple_w_gate[0].astype(BF16),
                ple_w_proj[0].astype(BF16), row(final_norm))
    return out.reshape(BATCH, SEQ, D_MODEL)
```

```python
import functools

import jax
import jax.numpy as jnp
from jax import lax
from jax.experimental import pallas as pl
from jax.experimental.pallas import tpu as pltpu

D_MODEL = 1024
BATCH = 4
SEQ = 4096
PLE_DIM = 256
D_FF = 2816
NORM_EPS = 1e-6
GM_WIDTH = 1024
GM_GROUPS = 8
GM_CHUNK = 128
N_HEADS = 16
N_KV_HEADS = 4
HEAD_DIM = 64
Q_PER_KV = N_HEADS // N_KV_HEADS
KV_WIDTH = N_KV_HEADS * HEAD_DIM
ROPE_DIM = HEAD_DIM // 4
ROPE_HALF = ROPE_DIM // 2
ROPE_THETA = 500000.0
CMP_LEN = 32
CMP_STRIDE = 16
CMP_HIDDEN = 256
SEL_LEN = 64
SEL_TOP = 16
WINDOW = 512
N_NSA_BRANCH = 3
MASK_VALUE = -1e30
FORCE_SCORE = 1e9
LOG2_E = 1.4426950408889634

TOKENS = BATCH * SEQ
N_CHUNKS = SEQ // CMP_STRIDE
N_SEL = SEQ // SEL_LEN
LANES = 128
SUBLANES = 8
VMEM_LIMIT = 56 * 1024 * 1024

OFF_U = 0
OFF_V = OFF_U + GM_WIDTH
OFF_Q = OFF_V + GM_WIDTH
OFF_KV = OFF_Q + N_HEADS * HEAD_DIM
OFF_NSA_GATE = OFF_KV + 6 * KV_WIDTH
OFF_MERGE = OFF_NSA_GATE + N_HEADS * N_NSA_BRANCH
SEL_SHIFT = SEL_LEN.bit_length() - 1

F32 = jnp.float32
BF16 = jnp.bfloat16

NT_DIMS = (((1,), (1,)), ((), ()))


def _const_spec(shape):
    nd = len(shape)
    return pl.BlockSpec(shape, lambda *_: (0,) * nd, pipeline_mode=pl.Buffered(1))


def _rms(x, g):
    return x * lax.rsqrt(jnp.mean(x * x, axis=-1, keepdims=True) + NORM_EPS) * g


MXU_WIDTH = 256


def _half_step_ffn(x, g, wi_ref, wo_ref):
    xn = _rms(x, g).astype(BF16)
    acc = None
    for c in range(D_FF // MXU_WIDTH):
        lo = c * MXU_WIDTH
        gate = jnp.dot(xn, wi_ref[:, lo:lo + MXU_WIDTH], preferred_element_type=F32)
        up = jnp.dot(xn, wi_ref[:, D_FF + lo:D_FF + lo + MXU_WIDTH], preferred_element_type=F32)
        act = (gate * jax.nn.sigmoid(gate) * up).astype(BF16)
        part = jnp.dot(act, wo_ref[lo:lo + MXU_WIDTH, :], preferred_element_type=F32)
        acc = part if acc is None else acc + part
    return x + 0.5 * acc


def _ffn_kernel(x_ref, g_ref, wi_ref, wo_ref, o_ref):
    o_ref[...] = _half_step_ffn(x_ref[...], g_ref[...], wi_ref, wo_ref)


def _ffn(x, g, w_in, w_out, *, tm=512):
    row = pl.BlockSpec((tm, D_MODEL), lambda i: (i, 0))
    return pl.pallas_call(
        _ffn_kernel,
        grid=(TOKENS // tm,),
        in_specs=[row, _const_spec((1, D_MODEL)), _const_spec((D_MODEL, 2 * D_FF)),
                  _const_spec((D_FF, D_MODEL))],
        out_specs=row,
        out_shape=jax.ShapeDtypeStruct((TOKENS, D_MODEL), F32),
        compiler_params=pltpu.CompilerParams(
            dimension_semantics=("parallel",), vmem_limit_bytes=VMEM_LIMIT),
        name="ffn",
    )(x, g, w_in, w_out)


NSA_Q = 256
BF16_ROWS = 16
V_ROWS = HEAD_DIM + BF16_ROWS


def _rope(x, c, sa, sb):
    w = x.shape[1]
    return x * c + pltpu.roll(x, ROPE_HALF, 1) * sa + pltpu.roll(x, w - ROPE_HALF, 1) * sb


def _inproj_kernel(h_ref, g_ref, w_ref, wm_ref, wg_ref, lng_ref, lnb_ref, c_ref, sa_ref, sb_ref,
                   gu_ref, vn_ref, qraw_ref, qrot_ref, kc_ref, vc_ref, ks_ref, vs_ref,
                   kw_ref, vw_ref, gate_ref, ga_ref, gb_ref, *, tm):
    n = _rms(h_ref[...], g_ref[...]).astype(BF16)

    def seg(lo, width):
        return jnp.dot(n, w_ref[:, lo:lo + width], preferred_element_type=F32)

    gu_ref[...] = jax.nn.gelu(seg(OFF_U, GM_WIDTH)).astype(BF16)
    v = jax.nn.gelu(seg(OFF_V, GM_WIDTH))
    mu = jnp.mean(v, axis=-1, keepdims=True)
    vc = v - mu
    var = jnp.mean(vc * vc, axis=-1, keepdims=True)
    vn_ref[...] = (vc * lax.rsqrt(var + NORM_EPS) * lng_ref[...] + lnb_ref[...]).astype(BF16)

    c, sa, sb = c_ref[...], sa_ref[...], sb_ref[...]

    def tile_lanes(t, reps):
        return jnp.concatenate([t] * reps, axis=1)

    q = seg(OFF_Q, N_HEADS * HEAD_DIM) * (HEAD_DIM ** -0.5 * LOG2_E)
    reps = N_HEADS * HEAD_DIM // LANES
    q_rot = _rope(q, tile_lanes(c, reps), tile_lanes(sa, reps), tile_lanes(sb, reps))
    qraw_ref[0] = q.T.astype(BF16).reshape(N_HEADS, HEAD_DIM, tm)
    qrot_ref[0] = q_rot.T.astype(BF16).reshape(N_HEADS, HEAD_DIM, tm)

    kv = seg(OFF_KV, 6 * KV_WIDTH)
    k_c, v_c, k_s, v_s, k_w, v_w = [kv[:, i * KV_WIDTH:(i + 1) * KV_WIDTH] for i in range(6)]
    reps = KV_WIDTH // LANES
    ck, sak, sbk = tile_lanes(c, reps), tile_lanes(sa, reps), tile_lanes(sb, reps)
    k_s = _rope(k_s, ck, sak, sbk)
    k_w = _rope(k_w, ck, sak, sbk)
    pos = (pl.program_id(0) % (SEQ // tm)) * tm + lax.broadcasted_iota(jnp.int32, (tm, N_SEL), 0)
    blk = lax.broadcasted_iota(jnp.int32, (tm, N_SEL), 1)
    onehot = jnp.where(lax.shift_right_logical(pos, SEL_SHIFT) == blk, 1.0, 0.0).astype(F32)
    ones_t = jnp.ones((V_ROWS - HEAD_DIM, tm), F32)
    ones_col = jnp.where(lax.broadcasted_iota(jnp.int32, (tm, HEAD_DIM), 1) == 0, 1.0, 0.0)
    v_s_t, v_w_t = v_s.T, v_w.T
    for h in range(N_KV_HEADS):
        sl = slice(h * HEAD_DIM, (h + 1) * HEAD_DIM)
        kc_ref[0, h] = k_c[:, sl]
        vc_ref[0, h] = v_c[:, sl]
        ks_ref[0, h] = jnp.concatenate([k_s[:, sl], onehot], axis=1).astype(BF16)
        vs_ref[0, h, 0] = jnp.concatenate([v_s_t[sl], ones_t], axis=0).astype(BF16)
        kw_ref[0, h] = jnp.concatenate([k_w[:, sl], ones_col], axis=1).astype(BF16)
        vw_ref[0, h, 0] = jnp.concatenate([v_w_t[sl], ones_t], axis=0).astype(BF16)

    merge = jax.nn.sigmoid(jnp.dot(n, wm_ref[...], preferred_element_type=F32))
    ga_ref[...] = merge[:, :D_MODEL].astype(BF16)
    gb_ref[...] = merge[:, D_MODEL:].astype(BF16)
    for h in range(N_KV_HEADS):
        gate_ref[h] = jax.nn.sigmoid(
            jnp.dot(n, wg_ref[:, h * LANES:(h + 1) * LANES], preferred_element_type=F32))


def _inproj(h, g, w_main, w_merge, w_gate, ln_g, ln_b, rope_c, rope_sa, rope_sb):
    tm = NSA_Q
    nq = SEQ // tm
    row = lambda i: (i, 0)
    head = lambda i: (i // nq, 0, i % nq, 0)
    tok_bf = jax.ShapeDtypeStruct((TOKENS, D_MODEL), BF16)

    def head_shape(nh, width, dtype):
        return jax.ShapeDtypeStruct((BATCH, nh, SEQ, width), dtype)

    def head_spec(nh, width):
        return pl.BlockSpec((1, nh, tm, width), head)

    qt_spec = pl.BlockSpec((1, N_HEADS, HEAD_DIM, tm), lambda i: (i // nq, 0, 0, i % nq))
    qt_shape = jax.ShapeDtypeStruct((BATCH, N_HEADS, HEAD_DIM, SEQ), BF16)
    vt_spec = pl.BlockSpec((1, N_KV_HEADS, 1, V_ROWS, tm), lambda i: (i // nq, 0, i % nq, 0, 0))
    vt_shape = jax.ShapeDtypeStruct((BATCH, N_KV_HEADS, nq, V_ROWS, tm), BF16)

    return pl.pallas_call(
        functools.partial(_inproj_kernel, tm=tm),
        grid=(TOKENS // tm,),
        in_specs=[
            pl.BlockSpec((tm, D_MODEL), row),
            _const_spec((1, D_MODEL)),
            _const_spec((D_MODEL, OFF_NSA_GATE)),
            _const_spec((D_MODEL, 2 * D_MODEL)),
            _const_spec((D_MODEL, N_KV_HEADS * LANES)),
            _const_spec((1, GM_WIDTH)),
            _const_spec((1, GM_WIDTH)),
            pl.BlockSpec((tm, LANES), lambda i: (i % nq, 0)),
            pl.BlockSpec((tm, LANES), lambda i: (i % nq, 0)),
            pl.BlockSpec((tm, LANES), lambda i: (i % nq, 0)),
        ],
        out_specs=[
            pl.BlockSpec((tm, GM_WIDTH), row),
            pl.BlockSpec((tm, GM_WIDTH), row),
            qt_spec,
            qt_spec,
            head_spec(N_KV_HEADS, HEAD_DIM),
            head_spec(N_KV_HEADS, HEAD_DIM),
            head_spec(N_KV_HEADS, LANES),
            vt_spec,
            head_spec(N_KV_HEADS, LANES),
            vt_spec,
            pl.BlockSpec((N_KV_HEADS, tm, LANES), lambda i: (0, i, 0)),
            pl.BlockSpec((tm, D_MODEL), row),
            pl.BlockSpec((tm, D_MODEL), row),
        ],
        out_shape=[
            tok_bf, tok_bf,
            qt_shape, qt_shape,
            head_shape(N_KV_HEADS, HEAD_DIM, F32), head_shape(N_KV_HEADS, HEAD_DIM, F32),
            head_shape(N_KV_HEADS, LANES, BF16), vt_shape,
            head_shape(N_KV_HEADS, LANES, BF16), vt_shape,
            jax.ShapeDtypeStruct((N_KV_HEADS, TOKENS, LANES), F32),
            tok_bf, tok_bf,
        ],
        compiler_params=pltpu.CompilerParams(
            dimension_semantics=("parallel",), vmem_limit_bytes=VMEM_LIMIT),
        name="inproj",
    )(h, g, w_main, w_merge, w_gate, ln_g, ln_b, rope_c, rope_sa, rope_sb)


def _compress_kernel(x_ref, ptop_ref, pbot_ref, w1t_ref, w1b_ref, w2_ref, o_ref, *, feature_major):
    x = jnp.concatenate([x_ref[0, 0, pl.ds(j, N_CHUNKS, stride=CMP_STRIDE), :]
                         for j in range(CMP_STRIDE)], axis=1)
    top = jnp.dot((x + ptop_ref[...]).astype(BF16), w1t_ref[...], preferred_element_type=F32)
    bot = jnp.dot((x + pbot_ref[...]).astype(BF16), w1b_ref[...], preferred_element_type=F32)
    hidden = jax.nn.gelu(top + pltpu.roll(bot, N_CHUNKS - 1, 0)).astype(BF16)
    if feature_major:
        out = lax.dot_general(w2_ref[...], hidden, NT_DIMS, preferred_element_type=F32)
    else:
        out = jnp.dot(hidden, w2_ref[...], preferred_element_type=F32)
    o_ref[0, 0] = out.astype(BF16)


def _compress(x_heads, pos_top, pos_bot, w1_top, w1_bot, w2, *, feature_major):
    half = CMP_STRIDE * HEAD_DIM
    out_tile = (HEAD_DIM, N_CHUNKS) if feature_major else (N_CHUNKS, HEAD_DIM)
    return pl.pallas_call(
        functools.partial(_compress_kernel, feature_major=feature_major),
        grid=(BATCH, N_KV_HEADS),
        in_specs=[
            pl.BlockSpec((1, 1, SEQ, HEAD_DIM), lambda b, k: (b, k, 0, 0)),
            _const_spec((1, half)),
            _const_spec((1, half)),
            _const_spec((half, CMP_HIDDEN)),
            _const_spec((half, CMP_HIDDEN)),
            _const_spec(w2.shape),
        ],
        out_specs=pl.BlockSpec((1, 1) + out_tile, lambda b, k: (b, k, 0, 0)),
        out_shape=jax.ShapeDtypeStruct((BATCH, N_KV_HEADS) + out_tile, BF16),
        compiler_params=pltpu.CompilerParams(dimension_semantics=("parallel", "parallel")),
        name="compress",
    )(x_heads, pos_top, pos_bot, w1_top, w1_bot, w2)


RANK_STRIDE = 16
RANK_ACCUMULATORS = 4
SELECT_Q = 1024


def _split_bf16(x):
    hi = x.astype(BF16)
    r1 = x - hi.astype(F32)
    mid = r1.astype(BF16)
    lo = (r1 - mid.astype(F32)).astype(BF16)
    return hi, mid, lo


def _selection_bias(score, n_blocks):
    n_q = score.shape[1]
    sub_iota = lax.broadcasted_iota(jnp.int32, (SUBLANES, n_q), 0)
    groups = [score[g * SUBLANES:(g + 1) * SUBLANES] for g in range(n_blocks // SUBLANES)]
    counts = [[None] * RANK_ACCUMULATORS for _ in groups]
    for jp in range(n_blocks):
        other = score[jp:jp + 1, :]
        for g, grp in enumerate(groups):
            lo = g * SUBLANES
            if lo > jp:
                before = other >= grp
            elif lo + SUBLANES - 1 <= jp:
                before = other > grp
            else:
                before = (other > grp) | ((other == grp) & (sub_iota + lo > jp))
            inc = jnp.where(before, 1, 0)
            a = jp % RANK_ACCUMULATORS
            counts[g][a] = inc if counts[g][a] is None else counts[g][a] + inc
    rank = jnp.concatenate([(c[0] + c[1]) + (c[2] + c[3]) for c in counts], axis=0)
    bias = jnp.where(rank < SEL_TOP, 0.0, MASK_VALUE)
    if n_blocks < N_SEL:
        bias = jnp.concatenate([bias, jnp.zeros((N_SEL - n_blocks, n_q), F32)], axis=0)
    return bias


def _select_kernel(qraw_ref, kcmp_ref, vcmp_ref, mt_ref, bias_ref, ocmp_ref):
    qi = pl.program_id(2)
    s0 = qi * SELECT_Q
    key_i = lax.broadcasted_iota(jnp.int32, (N_CHUNKS, SELECT_Q), 0)
    qry_i = lax.broadcasted_iota(jnp.int32, (N_CHUNKS, SELECT_Q), 1)
    heads = range(Q_PER_KV)

    k_cmp = kcmp_ref[0, 0]
    v_cmp_t = vcmp_ref[0, 0]
    s_cmp = [jnp.dot(k_cmp, qraw_ref[0, h], preferred_element_type=F32) for h in heads]
    cmp_valid = key_i * CMP_STRIDE + (CMP_LEN - 1) <= s0 + qry_i
    any_valid = s0 + lax.broadcasted_iota(jnp.int32, (1, SELECT_Q), 1) >= CMP_LEN - 1
    p_cmp = []
    for h in heads:
        s_c = jnp.where(cmp_valid, s_cmp[h], MASK_VALUE)
        e_c = jnp.exp2(s_c - jnp.max(s_c, axis=0, keepdims=True))
        p_cmp.append(e_c * jnp.where(any_valid, 1.0 / jnp.sum(e_c, axis=0, keepdims=True), 0.0))
    for h in heads:
        ocmp_ref[0, 0, h * HEAD_DIM:(h + 1) * HEAD_DIM, :] = jnp.dot(
            v_cmp_t, p_cmp[h].astype(BF16), preferred_element_type=F32)

    p_sum = (p_cmp[0] + p_cmp[1]) + (p_cmp[2] + p_cmp[3])
    mt = mt_ref[...]
    imp_t = sum(jnp.dot(mt, part, preferred_element_type=F32)
                for part in _split_bf16(p_sum))
    blk = lax.broadcasted_iota(jnp.int32, (N_SEL, SELECT_Q), 0)
    cur = lax.shift_right_logical(s0 + lax.broadcasted_iota(jnp.int32, (N_SEL, SELECT_Q), 1), SEL_SHIFT)
    forced = (blk == 0) | (blk == cur) | (blk == cur - 1)
    score = jnp.where(forced, FORCE_SCORE, jnp.where(blk > cur, -FORCE_SCORE, imp_t))

    steps_per_variant = RANK_STRIDE * SEL_LEN // SELECT_Q
    for variant in range(N_SEL // RANK_STRIDE):
        @pl.when(qi // steps_per_variant == variant)
        def _():
            bias_ref[0, 0] = _selection_bias(score, (variant + 1) * RANK_STRIDE).astype(BF16)


def _select(q_raw_t, k_cmp, v_cmp_t, overlap_t):
    nq = SEQ // SELECT_Q
    step = lambda rows: pl.BlockSpec((1, 1, rows, SELECT_Q), lambda b, k, i: (b, k, 0, i))
    whole = lambda *tile: pl.BlockSpec((1, 1) + tile, lambda b, k, i: (b, k) + (0,) * len(tile))
    return pl.pallas_call(
        _select_kernel,
        grid=(BATCH, N_KV_HEADS, nq),
        in_specs=[
            pl.BlockSpec((1, Q_PER_KV, HEAD_DIM, SELECT_Q), lambda b, k, i: (b, k, 0, i)),
            whole(N_CHUNKS, HEAD_DIM), whole(HEAD_DIM, N_CHUNKS),
            _const_spec((N_SEL, N_CHUNKS)),
        ],
        out_specs=[step(N_SEL), step(Q_PER_KV * HEAD_DIM)],
        out_shape=[jax.ShapeDtypeStruct((BATCH, N_KV_HEADS, N_SEL, SEQ), BF16),
                   jax.ShapeDtypeStruct((BATCH, N_KV_HEADS, Q_PER_KV * HEAD_DIM, SEQ), F32)],
        compiler_params=pltpu.CompilerParams(
            dimension_semantics=("parallel", "parallel", "parallel"), vmem_limit_bytes=VMEM_LIMIT),
        name="select",
    )(q_raw_t, k_cmp, v_cmp_t, overlap_t)


KV_PER_STEP = 2


def _attend_kernel(qrot_ref, bias_ref, ocmp_ref, ks_ref, vs_ref, kw_ref, vw_ref, gate_ref,
                   o_ref, m_sc, acc_sc, sa_sc, sb_sc, owin_sc):
    qi = pl.program_id(2)
    s0 = qi * NSA_Q
    key_i = lax.broadcasted_iota(jnp.int32, (NSA_Q, NSA_Q), 0)
    qry_i = lax.broadcasted_iota(jnp.int32, (NSA_Q, NSA_Q), 1)
    causal = key_i <= qry_i
    kvs = range(KV_PER_STEP)
    heads = range(KV_PER_STEP * Q_PER_KV)

    def scores(k, q_t):
        return jnp.dot(k, q_t, preferred_element_type=F32)

    def softmax_step(h, s, v_t):
        m_prev = m_sc[h]
        m_new = jnp.maximum(m_prev, jnp.max(s, axis=0, keepdims=True))
        p = jnp.exp2(s - m_new).astype(BF16)
        acc_sc[h] = jnp.exp2(m_prev - m_new) * acc_sc[h] + jnp.dot(v_t, p, preferred_element_type=F32)
        m_sc[h] = m_new

    q_sel = [jnp.concatenate([qrot_ref[0, h], bias_ref[0, h // Q_PER_KV]], axis=0) for h in heads]
    m_sc[...] = jnp.full_like(m_sc, MASK_VALUE)
    acc_sc[...] = jnp.zeros_like(acc_sc)

    def sel_scores(kt, dst):
        rows = pl.ds(pl.multiple_of(kt * NSA_Q, NSA_Q), NSA_Q)
        k = [ks_ref[0, kv, rows, :] for kv in kvs]
        for h in heads:
            dst[h] = scores(k[h // Q_PER_KV], q_sel[h])

    def sel_softmax(kt, src, diagonal=False):
        v_t = [vs_ref[0, kv, kt] for kv in kvs]
        for h in heads:
            s = src[h]
            softmax_step(h, jnp.where(causal, s, MASK_VALUE) if diagonal else s, v_t[h // Q_PER_KV])

    def window_keys(kv, kt):
        return kw_ref[0, kv, pl.ds(pl.multiple_of(kt * NSA_Q, NSA_Q), NSA_Q), :]

    def penalty_rows(penalty):
        rows = jnp.where(lax.broadcasted_iota(jnp.int32, (HEAD_DIM, NSA_Q), 0) == 0, penalty, 0.0)
        return rows.astype(BF16)

    win_mid, win_far = jnp.maximum(qi - 1, 0), jnp.maximum(qi - 2, 0)
    k_diag = [window_keys(kv, qi)[:, 0:HEAD_DIM] for kv in kvs]
    k_mid = [window_keys(kv, win_mid) for kv in kvs]
    k_far = [window_keys(kv, win_far) for kv in kvs]
    pen_mid = penalty_rows(jnp.where(qi >= 1, 0.0, MASK_VALUE))
    pen_far = penalty_rows(jnp.where(qi >= 2, 0.0, MASK_VALUE))
    s_win_diag, s_win_mid, s_win_far = [], [], []
    for h in heads:
        q_t = qrot_ref[0, h]
        kv = h // Q_PER_KV
        s_win_diag.append(scores(k_diag[kv], q_t))
        s_win_mid.append(scores(k_mid[kv], jnp.concatenate([q_t, pen_mid], axis=0)))
        s_win_far.append(scores(k_far[kv], jnp.concatenate([q_t, pen_far], axis=0)))

    sel_scores(0, sa_sc)

    for h in heads:
        s_edge = jnp.where(causal, s_win_diag[h], s_win_far[h])
        s_mid = s_win_mid[h]
        m_w = jnp.max(jnp.maximum(s_edge, s_mid), axis=0, keepdims=True)
        p_edge = jnp.exp2(s_edge - m_w).astype(BF16)
        p_mid = jnp.exp2(s_mid - m_w).astype(BF16)
        zero = jnp.zeros_like(p_edge)
        kv = h // Q_PER_KV
        acc = (jnp.dot(vw_ref[0, kv, qi], jnp.where(causal, p_edge, zero), preferred_element_type=F32)
               + jnp.dot(vw_ref[0, kv, win_far], jnp.where(causal, zero, p_edge), preferred_element_type=F32)
               + jnp.dot(vw_ref[0, kv, win_mid], p_mid, preferred_element_type=F32))
        owin_sc[h] = acc[:HEAD_DIM] / acc[HEAD_DIM:HEAD_DIM + 1]

    def sel_pair(first):
        sel_scores(first + 1, sb_sc)
        sel_softmax(first, sa_sc)
        sel_scores(first + 2, sa_sc)
        sel_softmax(first + 1, sb_sc)

    def sel_quad(t, carry):
        sel_pair(4 * t)
        sel_pair(4 * t + 2)
        return carry

    lax.fori_loop(0, lax.shift_right_logical(qi, 2), sel_quad, 0)

    @pl.when((qi & 2) != 0)
    def _():
        sel_pair(qi & ~3)

    def finish():
        gates_t = [gate_ref[kv].T for kv in kvs]
        outs = []
        for h in heads:
            kv, hh = divmod(h, Q_PER_KV)
            g = gates_t[kv]
            c = hh * N_NSA_BRANCH
            acc = acc_sc[h]
            o_sel = acc[:HEAD_DIM] / acc[HEAD_DIM:HEAD_DIM + 1]
            o_cmp = ocmp_ref[0, kv, hh * HEAD_DIM:(hh + 1) * HEAD_DIM, :]
            outs.append(g[c:c + 1] * o_cmp + g[c + 1:c + 2] * o_sel + g[c + 2:c + 3] * owin_sc[h])
        o_ref[0] = jnp.concatenate(outs, axis=0).T.astype(BF16)

    @pl.when((qi & 1) == 1)
    def _():
        sel_scores(qi, sb_sc)
        sel_softmax(qi - 1, sa_sc)
        sel_softmax(qi, sb_sc, diagonal=True)
        finish()

    @pl.when((qi & 1) == 0)
    def _():
        sel_softmax(qi, sa_sc, diagonal=True)
        finish()


def _attend(q_rot_t, bias, o_cmp_t, k_sel, v_sel_t, k_win, v_win_t, gates):
    nq = SEQ // NSA_Q
    n_heads = KV_PER_STEP * Q_PER_KV
    step = lambda rows: pl.BlockSpec((1, KV_PER_STEP, rows, NSA_Q), lambda b, k, i: (b, k, 0, i))
    whole = lambda *tile: pl.BlockSpec((1, KV_PER_STEP) + tile, lambda b, k, i: (b, k) + (0,) * len(tile))
    return pl.pallas_call(
        _attend_kernel,
        grid=(BATCH, N_KV_HEADS // KV_PER_STEP, nq),
        in_specs=[
            pl.BlockSpec((1, n_heads, HEAD_DIM, NSA_Q), lambda b, k, i: (b, k, 0, i)),
            step(N_SEL), step(Q_PER_KV * HEAD_DIM),
            whole(SEQ, LANES), whole(nq, V_ROWS, NSA_Q),
            whole(SEQ, LANES), whole(nq, V_ROWS, NSA_Q),
            pl.BlockSpec((KV_PER_STEP, NSA_Q, LANES), lambda b, k, i: (k, b * nq + i, 0)),
        ],
        out_specs=pl.BlockSpec((1, NSA_Q, n_heads * HEAD_DIM), lambda b, k, i: (b, i, k)),
        out_shape=jax.ShapeDtypeStruct((BATCH, SEQ, N_HEADS * HEAD_DIM), BF16),
        scratch_shapes=[pltpu.VMEM((n_heads, 1, NSA_Q), F32),
                        pltpu.VMEM((n_heads, V_ROWS, NSA_Q), F32),
                        pltpu.VMEM((n_heads, NSA_Q, NSA_Q), F32),
                        pltpu.VMEM((n_heads, NSA_Q, NSA_Q), F32),
                        pltpu.VMEM((n_heads, HEAD_DIM, NSA_Q), F32)],
        compiler_params=pltpu.CompilerParams(
            dimension_semantics=("parallel", "parallel", "arbitrary"),
            vmem_limit_bytes=VMEM_LIMIT),
        name="attend",
    )(q_rot_t, bias, o_cmp_t, k_sel, v_sel_t, k_win, v_win_t, gates)


def _tail_kernel(h_ref, gu_ref, vn_ref, ob_ref, ga_ref, gb_ref, ws_ref, bs_ref,
                 wa_ref, wb_ref, wo_ref, gn_ref, wi_ref, wf_ref, p_ref, gp_ref, wg_ref, wp_ref,
                 gf_ref, o_ref, *, tm):
    r = lax.broadcasted_iota(jnp.int32, (GM_CHUNK, GM_CHUNK), 0)
    c = lax.broadcasted_iota(jnp.int32, (GM_CHUNK, GM_CHUNK), 1)
    w_tril = [jnp.where(c <= r, ws_ref[g], 0.0).astype(BF16) for g in range(GM_GROUPS)]
    bias = bs_ref[...]
    rows = []
    for ch in range(tm // GM_CHUNK):
        rs = slice(ch * GM_CHUNK, (ch + 1) * GM_CHUNK)
        mix = jnp.concatenate(
            [jnp.dot(w_tril[g], vn_ref[rs, g * LANES:(g + 1) * LANES], preferred_element_type=F32)
             for g in range(GM_GROUPS)], axis=1)
        rows.append(gu_ref[rs, :].astype(F32) * (mix + bias))
    z = jnp.concatenate(rows, axis=0).astype(BF16)
    y_a = jnp.dot(z, wa_ref[...], preferred_element_type=F32)
    y_b = jnp.dot(ob_ref[...], wb_ref[...], preferred_element_type=F32)
    merged = (ga_ref[...].astype(F32) * y_a + gb_ref[...].astype(F32) * y_b).astype(BF16)
    h = h_ref[...] + jnp.dot(merged, wo_ref[...], preferred_element_type=F32)

    h = _half_step_ffn(h, gn_ref[...], wi_ref, wf_ref)

    gate = jax.nn.sigmoid(jnp.dot(_rms(h, gp_ref[...]).astype(BF16), wg_ref[...],
                                  preferred_element_type=F32))
    proj = jnp.dot(p_ref[...].astype(BF16), wp_ref[...], preferred_element_type=F32)
    o_ref[...] = _rms(h + gate * proj, gf_ref[...])


def _tail(h, gu, vn, o_b, ga, gb, w_s, b_s_exp, w_a, w_b, w_o, ffn_norm, ffn_w_in, ffn_w_out,
          p, ple_norm, ple_w_gate, ple_w_proj, final_norm, *, tm=512):
    row = pl.BlockSpec((tm, D_MODEL), lambda i: (i, 0))
    square = _const_spec((D_MODEL, D_MODEL))
    gain = _const_spec((1, D_MODEL))
    return pl.pallas_call(
        functools.partial(_tail_kernel, tm=tm),
        grid=(TOKENS // tm,),
        in_specs=[row, row, row, row, row, row,
                  _const_spec((GM_GROUPS, GM_CHUNK, GM_CHUNK)),
                  _const_spec((GM_CHUNK, GM_WIDTH)),
                  square, square, square,
                  gain, _const_spec((D_MODEL, 2 * D_FF)), _const_spec((D_FF, D_MODEL)),
                  pl.BlockSpec((tm, PLE_DIM), lambda i: (i, 0)), gain, square,
                  _const_spec((PLE_DIM, D_MODEL)), gain],
        out_specs=row,
        out_shape=jax.ShapeDtypeStruct((TOKENS, D_MODEL), F32),
        compiler_params=pltpu.CompilerParams(
            dimension_semantics=("parallel",), vmem_limit_bytes=VMEM_LIMIT),
        name="tail",
    )(h, gu, vn, o_b, ga, gb, w_s, b_s_exp, w_a, w_b, w_o, ffn_norm, ffn_w_in, ffn_w_out,
      p, ple_norm, ple_w_gate, ple_w_proj, final_norm)


def _rope_tables():
    inv_freq = ROPE_THETA ** (-jnp.arange(0, ROPE_DIM, 2, dtype=jnp.float32) / ROPE_DIM)
    ang = jnp.arange(SEQ).astype(jnp.float32)[:, None] * inv_freq[None, :]
    cos, sin = jnp.cos(ang), jnp.sin(ang)
    zero = jnp.zeros_like(cos)
    rest = HEAD_DIM - ROPE_DIM
    c = jnp.concatenate([cos, cos, jnp.ones((SEQ, rest), F32)], axis=1)
    sa = jnp.concatenate([zero, sin, jnp.zeros((SEQ, rest), F32)], axis=1)
    sb = jnp.concatenate([-sin, zero, jnp.zeros((SEQ, rest), F32)], axis=1)
    return [jnp.tile(t, (1, LANES // HEAD_DIM)) for t in (c, sa, sb)]


def _overlap_matrix_t():
    j = jnp.arange(N_SEL)[:, None]
    i = jnp.arange(N_CHUNKS)[None, :]
    r_sel = SEL_LEN // CMP_STRIDE
    l_cmp = CMP_LEN // CMP_STRIDE
    return ((i >= r_sel * j - (l_cmp - 1)) & (i <= r_sel * j + r_sel - 1)).astype(BF16)


def _split_w_in(w_in):
    gate = w_in[:, OFF_NSA_GATE:OFF_MERGE].reshape(D_MODEL, N_KV_HEADS, Q_PER_KV * N_NSA_BRANCH)
    gate = jnp.pad(gate, ((0, 0), (0, 0), (0, LANES - Q_PER_KV * N_NSA_BRANCH)))
    return (w_in[:, :OFF_NSA_GATE].astype(BF16), w_in[:, OFF_MERGE:].astype(BF16),
            gate.reshape(D_MODEL, N_KV_HEADS * LANES).astype(BF16))


def kernel(x, p, ffn1_norm, ffn1_w_in, ffn1_w_out, mix_norm, w_in, gm_ln_g, gm_ln_b, gm_w_s, gm_b_s,
           w_branch_a, cmp_pos_k, cmp_k_w1, cmp_k_w2, cmp_pos_v, cmp_v_w1, cmp_v_w2, w_branch_b, w_out,
           ffn2_norm, ffn2_w_in, ffn2_w_out, ple_norm, ple_w_gate, ple_w_proj, final_norm):
    assert x.shape == (BATCH, SEQ, D_MODEL) and p.shape == (1, BATCH, SEQ, PLE_DIM)
    row = lambda a: a.reshape(1, -1)
    h = x.reshape(TOKENS, D_MODEL)

    h = _ffn(h, row(ffn1_norm[0]), ffn1_w_in[0].astype(BF16), ffn1_w_out[0].astype(BF16))

    rope_c, rope_sa, rope_sb = _rope_tables()
    (gu, vn, q_raw, q_rot, k_c, v_c, k_sel, v_sel, k_win, v_win, gates, g_a, g_b) = _inproj(
        h, row(mix_norm[0]), *_split_w_in(w_in[0]), row(gm_ln_g[0]), row(gm_ln_b[0]),
        rope_c, rope_sa, rope_sb)

    half = CMP_STRIDE * HEAD_DIM

    def compress(x_heads, pos, w1, w2, feature_major):
        w2 = (w2.T if feature_major else w2).astype(BF16)
        return _compress(x_heads, pos[:CMP_STRIDE].reshape(1, half), pos[CMP_STRIDE:].reshape(1, half),
                         w1[:half].astype(BF16), w1[half:].astype(BF16), w2, feature_major=feature_major)

    k_cmp = compress(k_c, cmp_pos_k[0], cmp_k_w1[0], cmp_k_w2[0], False)
    v_cmp = compress(v_c, cmp_pos_v[0], cmp_v_w1[0], cmp_v_w2[0], True)

    sel_bias, o_cmp = _select(q_raw, k_cmp, v_cmp, _overlap_matrix_t())
    o_b = _attend(q_rot, sel_bias, o_cmp, k_sel, v_sel, k_win, v_win, gates)

    b_s_exp = jnp.repeat(gm_b_s[0].T, GM_WIDTH // GM_GROUPS, axis=1)
    out = _tail(h, gu, vn, o_b.reshape(TOKENS, N_HEADS * HEAD_DIM), g_a, g_b, gm_w_s[0], b_s_exp,
                w_branch_a[0].astype(BF16), w_branch_b[0].astype(BF16), w_out[0].astype(BF16),
                row(ffn2_norm[0]), ffn2_w_in[0].astype(BF16), ffn2_w_out[0].astype(BF16),
                p[0].reshape(TOKENS, PLE_DIM), row(ple_norm[0]), ple_w_gate[0].astype(BF16),
                ple_w_proj[0].astype(BF16), row(final_norm))
    return out.reshape(BATCH, SEQ, D_MODEL)
```

```python
import functools

import jax
import jax.numpy as jnp
from jax import lax
from jax.experimental import pallas as pl
from jax.experimental.pallas import tpu as pltpu

D_MODEL = 1024
BATCH = 4
SEQ = 4096
PLE_DIM = 256
D_FF = 2816
NORM_EPS = 1e-6
GM_WIDTH = 1024
GM_GROUPS = 8
GM_CHUNK = 128
N_HEADS = 16
N_KV_HEADS = 4
HEAD_DIM = 64
Q_PER_KV = N_HEADS // N_KV_HEADS
KV_WIDTH = N_KV_HEADS * HEAD_DIM
ROPE_DIM = HEAD_DIM // 4
ROPE_HALF = ROPE_DIM // 2
ROPE_THETA = 500000.0
CMP_LEN = 32
CMP_STRIDE = 16
CMP_HIDDEN = 256
SEL_LEN = 64
SEL_TOP = 16
WINDOW = 512
N_NSA_BRANCH = 3
MASK_VALUE = -1e30
FORCE_SCORE = 1e9
LOG2_E = 1.4426950408889634

TOKENS = BATCH * SEQ
N_CHUNKS = SEQ // CMP_STRIDE
N_SEL = SEQ // SEL_LEN
LANES = 128
SUBLANES = 8
VMEM_LIMIT = 56 * 1024 * 1024

OFF_U = 0
OFF_V = OFF_U + GM_WIDTH
OFF_Q = OFF_V + GM_WIDTH
OFF_KV = OFF_Q + N_HEADS * HEAD_DIM
OFF_NSA_GATE = OFF_KV + 6 * KV_WIDTH
OFF_MERGE = OFF_NSA_GATE + N_HEADS * N_NSA_BRANCH
SEL_SHIFT = SEL_LEN.bit_length() - 1

F32 = jnp.float32
BF16 = jnp.bfloat16

NT_DIMS = (((1,), (1,)), ((), ()))


def _const_spec(shape):
    nd = len(shape)
    return pl.BlockSpec(shape, lambda *_: (0,) * nd, pipeline_mode=pl.Buffered(1))


def _rms(x, g):
    return x * lax.rsqrt(jnp.mean(x * x, axis=-1, keepdims=True) + NORM_EPS) * g


MXU_WIDTH = 256


def _half_step_ffn(x, g, wi_ref, wo_ref):
    xn = _rms(x, g).astype(BF16)
    acc = None
    for c in range(D_FF // MXU_WIDTH):
        lo = c * MXU_WIDTH
        gate = jnp.dot(xn, wi_ref[:, lo:lo + MXU_WIDTH], preferred_element_type=F32)
        up = jnp.dot(xn, wi_ref[:, D_FF + lo:D_FF + lo + MXU_WIDTH], preferred_element_type=F32)
        act = (gate * jax.nn.sigmoid(gate) * up).astype(BF16)
        part = jnp.dot(act, wo_ref[lo:lo + MXU_WIDTH, :], preferred_element_type=F32)
        acc = part if acc is None else acc + part
    return x + 0.5 * acc


def _ffn_kernel(x_ref, g_ref, wi_ref, wo_ref, o_ref):
    o_ref[...] = _half_step_ffn(x_ref[...], g_ref[...], wi_ref, wo_ref)


def _ffn(x, g, w_in, w_out, *, tm=512):
    row = pl.BlockSpec((tm, D_MODEL), lambda i: (i, 0))
    return pl.pallas_call(
        _ffn_kernel,
        grid=(TOKENS // tm,),
        in_specs=[row, _const_spec((1, D_MODEL)), _const_spec((D_MODEL, 2 * D_FF)),
                  _const_spec((D_FF, D_MODEL))],
        out_specs=row,
        out_shape=jax.ShapeDtypeStruct((TOKENS, D_MODEL), F32),
        compiler_params=pltpu.CompilerParams(
            dimension_semantics=("parallel",), vmem_limit_bytes=VMEM_LIMIT),
        name="ffn",
    )(x, g, w_in, w_out)


NSA_Q = 256
BF16_ROWS = 16
V_ROWS = HEAD_DIM + BF16_ROWS


def _rope(x, c, sa, sb):
    w = x.shape[1]
    return x * c + pltpu.roll(x, ROPE_HALF, 1) * sa + pltpu.roll(x, w - ROPE_HALF, 1) * sb


def _inproj_kernel(h_ref, g_ref, w_ref, wm_ref, wg_ref, lng_ref, lnb_ref, c_ref, sa_ref, sb_ref,
                   gu_ref, vn_ref, qraw_ref, qrot_ref, kc_ref, vc_ref, ks_ref, vs_ref,
                   kw_ref, vw_ref, gate_ref, ga_ref, gb_ref, *, tm):
    n = _rms(h_ref[...], g_ref[...]).astype(BF16)

    def seg(lo, width):
        return jnp.dot(n, w_ref[:, lo:lo + width], preferred_element_type=F32)

    gu_ref[...] = jax.nn.gelu(seg(OFF_U, GM_WIDTH)).astype(BF16)
    v = jax.nn.gelu(seg(OFF_V, GM_WIDTH))
    mu = jnp.mean(v, axis=-1, keepdims=True)
    vc = v - mu
    var = jnp.mean(vc * vc, axis=-1, keepdims=True)
    vn_ref[...] = (vc * lax.rsqrt(var + NORM_EPS) * lng_ref[...] + lnb_ref[...]).astype(BF16)

    c, sa, sb = c_ref[...], sa_ref[...], sb_ref[...]

    def tile_lanes(t, reps):
        return jnp.concatenate([t] * reps, axis=1)

    q = seg(OFF_Q, N_HEADS * HEAD_DIM) * (HEAD_DIM ** -0.5 * LOG2_E)
    reps = N_HEADS * HEAD_DIM // LANES
    q_rot = _rope(q, tile_lanes(c, reps), tile_lanes(sa, reps), tile_lanes(sb, reps))
    qraw_ref[0] = q.T.astype(BF16).reshape(N_HEADS, HEAD_DIM, tm)
    qrot_ref[0] = q_rot.T.astype(BF16).reshape(N_HEADS, HEAD_DIM, tm)

    kv = seg(OFF_KV, 6 * KV_WIDTH)
    k_c, v_c, k_s, v_s, k_w, v_w = [kv[:, i * KV_WIDTH:(i + 1) * KV_WIDTH] for i in range(6)]
    reps = KV_WIDTH // LANES
    ck, sak, sbk = tile_lanes(c, reps), tile_lanes(sa, reps), tile_lanes(sb, reps)
    k_s = _rope(k_s, ck, sak, sbk)
    k_w = _rope(k_w, ck, sak, sbk)
    pos = (pl.program_id(0) % (SEQ // tm)) * tm + lax.broadcasted_iota(jnp.int32, (tm, N_SEL), 0)
    blk = lax.broadcasted_iota(jnp.int32, (tm, N_SEL), 1)
    onehot = jnp.where(lax.shift_right_logical(pos, SEL_SHIFT) == blk, 1.0, 0.0).astype(F32)
    ones_t = jnp.ones((V_ROWS - HEAD_DIM, tm), F32)
    ones_col = jnp.where(lax.broadcasted_iota(jnp.int32, (tm, HEAD_DIM), 1) == 0, 1.0, 0.0)
    v_s_t, v_w_t = v_s.T, v_w.T
    for h in range(N_KV_HEADS):
        sl = slice(h * HEAD_DIM, (h + 1) * HEAD_DIM)
        kc_ref[0, h] = k_c[:, sl]
        vc_ref[0, h] = v_c[:, sl]
        ks_ref[0, h] = jnp.concatenate([k_s[:, sl], onehot], axis=1).astype(BF16)
        vs_ref[0, h, 0] = jnp.concatenate([v_s_t[sl], ones_t], axis=0).astype(BF16)
        kw_ref[0, h] = jnp.concatenate([k_w[:, sl], ones_col], axis=1).astype(BF16)
        vw_ref[0, h, 0] = jnp.concatenate([v_w_t[sl], ones_t], axis=0).astype(BF16)

    merge = jax.nn.sigmoid(jnp.dot(n, wm_ref[...], preferred_element_type=F32))
    ga_ref[...] = merge[:, :D_MODEL].astype(BF16)
    gb_ref[...] = merge[:, D_MODEL:].astype(BF16)
    gates = jax.nn.sigmoid(jnp.dot(n, wg_ref[...], preferred_element_type=F32))
    group_cols = Q_PER_KV * N_NSA_BRANCH
    for h in range(N_KV_HEADS):
        gate_ref[h] = gates if h == 0 else pltpu.roll(gates, LANES - h * group_cols, 1)


def _inproj(h, g, w_main, w_merge, w_gate, ln_g, ln_b, rope_c, rope_sa, rope_sb):
    tm = NSA_Q
    nq = SEQ // tm
    row = lambda i: (i, 0)
    head = lambda i: (i // nq, 0, i % nq, 0)
    tok_bf = jax.ShapeDtypeStruct((TOKENS, D_MODEL), BF16)

    def head_shape(nh, width, dtype):
        return jax.ShapeDtypeStruct((BATCH, nh, SEQ, width), dtype)

    def head_spec(nh, width):
        return pl.BlockSpec((1, nh, tm, width), head)

    qt_spec = pl.BlockSpec((1, N_HEADS, HEAD_DIM, tm), lambda i: (i // nq, 0, 0, i % nq))
    qt_shape = jax.ShapeDtypeStruct((BATCH, N_HEADS, HEAD_DIM, SEQ), BF16)
    vt_spec = pl.BlockSpec((1, N_KV_HEADS, 1, V_ROWS, tm), lambda i: (i // nq, 0, i % nq, 0, 0))
    vt_shape = jax.ShapeDtypeStruct((BATCH, N_KV_HEADS, nq, V_ROWS, tm), BF16)

    return pl.pallas_call(
        functools.partial(_inproj_kernel, tm=tm),
        grid=(TOKENS // tm,),
        in_specs=[
            pl.BlockSpec((tm, D_MODEL), row),
            _const_spec((1, D_MODEL)),
            _const_spec((D_MODEL, OFF_NSA_GATE)),
            _const_spec((D_MODEL, 2 * D_MODEL)),
            _const_spec((D_MODEL, LANES)),
            _const_spec((1, GM_WIDTH)),
            _const_spec((1, GM_WIDTH)),
            pl.BlockSpec((tm, LANES), lambda i: (i % nq, 0)),
            pl.BlockSpec((tm, LANES), lambda i: (i % nq, 0)),
            pl.BlockSpec((tm, LANES), lambda i: (i % nq, 0)),
        ],
        out_specs=[
            pl.BlockSpec((tm, GM_WIDTH), row),
            pl.BlockSpec((tm, GM_WIDTH), row),
            qt_spec,
            qt_spec,
            head_spec(N_KV_HEADS, HEAD_DIM),
            head_spec(N_KV_HEADS, HEAD_DIM),
            head_spec(N_KV_HEADS, LANES),
            vt_spec,
            head_spec(N_KV_HEADS, LANES),
            vt_spec,
            pl.BlockSpec((N_KV_HEADS, tm, LANES), lambda i: (0, i, 0)),
            pl.BlockSpec((tm, D_MODEL), row),
            pl.BlockSpec((tm, D_MODEL), row),
        ],
        out_shape=[
            tok_bf, tok_bf,
            qt_shape, qt_shape,
            head_shape(N_KV_HEADS, HEAD_DIM, F32), head_shape(N_KV_HEADS, HEAD_DIM, F32),
            head_shape(N_KV_HEADS, LANES, BF16), vt_shape,
            head_shape(N_KV_HEADS, LANES, BF16), vt_shape,
            jax.ShapeDtypeStruct((N_KV_HEADS, TOKENS, LANES), F32),
            tok_bf, tok_bf,
        ],
        compiler_params=pltpu.CompilerParams(
            dimension_semantics=("parallel",), vmem_limit_bytes=VMEM_LIMIT),
        name="inproj",
    )(h, g, w_main, w_merge, w_gate, ln_g, ln_b, rope_c, rope_sa, rope_sb)


def _compress_kernel(x_ref, ptop_ref, pbot_ref, w1t_ref, w1b_ref, w2_ref, o_ref, *, feature_major):
    x = jnp.concatenate([x_ref[0, 0, pl.ds(j, N_CHUNKS, stride=CMP_STRIDE), :]
                         for j in range(CMP_STRIDE)], axis=1)
    top = jnp.dot((x + ptop_ref[...]).astype(BF16), w1t_ref[...], preferred_element_type=F32)
    bot = jnp.dot((x + pbot_ref[...]).astype(BF16), w1b_ref[...], preferred_element_type=F32)
    hidden = jax.nn.gelu(top + pltpu.roll(bot, N_CHUNKS - 1, 0)).astype(BF16)
    if feature_major:
        out = lax.dot_general(w2_ref[...], hidden, NT_DIMS, preferred_element_type=F32)
    else:
        out = jnp.dot(hidden, w2_ref[...], preferred_element_type=F32)
    o_ref[0, 0] = out.astype(BF16)


def _compress(x_heads, pos_top, pos_bot, w1_top, w1_bot, w2, *, feature_major):
    half = CMP_STRIDE * HEAD_DIM
    out_tile = (HEAD_DIM, N_CHUNKS) if feature_major else (N_CHUNKS, HEAD_DIM)
    return pl.pallas_call(
        functools.partial(_compress_kernel, feature_major=feature_major),
        grid=(BATCH, N_KV_HEADS),
        in_specs=[
            pl.BlockSpec((1, 1, SEQ, HEAD_DIM), lambda b, k: (b, k, 0, 0)),
            _const_spec((1, half)),
            _const_spec((1, half)),
            _const_spec((half, CMP_HIDDEN)),
            _const_spec((half, CMP_HIDDEN)),
            _const_spec(w2.shape),
        ],
        out_specs=pl.BlockSpec((1, 1) + out_tile, lambda b, k: (b, k, 0, 0)),
        out_shape=jax.ShapeDtypeStruct((BATCH, N_KV_HEADS) + out_tile, BF16),
        compiler_params=pltpu.CompilerParams(dimension_semantics=("parallel", "parallel")),
        name="compress",
    )(x_heads, pos_top, pos_bot, w1_top, w1_bot, w2)


RANK_STRIDE = 16
RANK_ACCUMULATORS = 4
SELECT_Q = 1024


def _split_bf16(x):
    hi = x.astype(BF16)
    r1 = x - hi.astype(F32)
    mid = r1.astype(BF16)
    lo = (r1 - mid.astype(F32)).astype(BF16)
    return hi, mid, lo


def _selection_bias(score, n_blocks):
    n_q = score.shape[1]
    sub_iota = lax.broadcasted_iota(jnp.int32, (SUBLANES, n_q), 0)
    groups = [score[g * SUBLANES:(g + 1) * SUBLANES] for g in range(n_blocks // SUBLANES)]
    counts = [[None] * RANK_ACCUMULATORS for _ in groups]
    for jp in range(n_blocks):
        other = score[jp:jp + 1, :]
        for g, grp in enumerate(groups):
            lo = g * SUBLANES
            if lo > jp:
                before = other >= grp
            elif lo + SUBLANES - 1 <= jp:
                before = other > grp
            else:
                before = (other > grp) | ((other == grp) & (sub_iota + lo > jp))
            inc = jnp.where(before, 1, 0)
            a = jp % RANK_ACCUMULATORS
            counts[g][a] = inc if counts[g][a] is None else counts[g][a] + inc
    rank = jnp.concatenate([(c[0] + c[1]) + (c[2] + c[3]) for c in counts], axis=0)
    bias = jnp.where(rank < SEL_TOP, 0.0, MASK_VALUE)
    if n_blocks < N_SEL:
        bias = jnp.concatenate([bias, jnp.zeros((N_SEL - n_blocks, n_q), F32)], axis=0)
    return bias


def _select_kernel(qraw_ref, kcmp_ref, vcmp_ref, mt_ref, bias_ref, ocmp_ref):
    qi = pl.program_id(2)
    s0 = qi * SELECT_Q
    key_i = lax.broadcasted_iota(jnp.int32, (N_CHUNKS, SELECT_Q), 0)
    qry_i = lax.broadcasted_iota(jnp.int32, (N_CHUNKS, SELECT_Q), 1)
    heads = range(Q_PER_KV)

    k_cmp = kcmp_ref[0, 0]
    v_cmp_t = vcmp_ref[0, 0]
    s_cmp = [jnp.dot(k_cmp, qraw_ref[0, h], preferred_element_type=F32) for h in heads]
    cmp_valid = key_i * CMP_STRIDE + (CMP_LEN - 1) <= s0 + qry_i
    any_valid = s0 + lax.broadcasted_iota(jnp.int32, (1, SELECT_Q), 1) >= CMP_LEN - 1
    p_cmp = []
    for h in heads:
        s_c = jnp.where(cmp_valid, s_cmp[h], MASK_VALUE)
        e_c = jnp.exp2(s_c - jnp.max(s_c, axis=0, keepdims=True))
        p_cmp.append(e_c * jnp.where(any_valid, 1.0 / jnp.sum(e_c, axis=0, keepdims=True), 0.0))
    for h in heads:
        ocmp_ref[0, 0, h * HEAD_DIM:(h + 1) * HEAD_DIM, :] = jnp.dot(
            v_cmp_t, p_cmp[h].astype(BF16), preferred_element_type=F32)

    p_sum = (p_cmp[0] + p_cmp[1]) + (p_cmp[2] + p_cmp[3])
    mt = mt_ref[...]
    imp_t = sum(jnp.dot(mt, part, preferred_element_type=F32)
                for part in _split_bf16(p_sum))
    blk = lax.broadcasted_iota(jnp.int32, (N_SEL, SELECT_Q), 0)
    cur = lax.shift_right_logical(s0 + lax.broadcasted_iota(jnp.int32, (N_SEL, SELECT_Q), 1), SEL_SHIFT)
    forced = (blk == 0) | (blk == cur) | (blk == cur - 1)
    score = jnp.where(forced, FORCE_SCORE, jnp.where(blk > cur, -FORCE_SCORE, imp_t))

    steps_per_variant = RANK_STRIDE * SEL_LEN // SELECT_Q
    for variant in range(N_SEL // RANK_STRIDE):
        @pl.when(qi // steps_per_variant == variant)
        def _():
            bias_ref[0, 0] = _selection_bias(score, (variant + 1) * RANK_STRIDE).astype(BF16)


def _select(q_raw_t, k_cmp, v_cmp_t, overlap_t):
    nq = SEQ // SELECT_Q
    step = lambda rows: pl.BlockSpec((1, 1, rows, SELECT_Q), lambda b, k, i: (b, k, 0, i))
    whole = lambda *tile: pl.BlockSpec((1, 1) + tile, lambda b, k, i: (b, k) + (0,) * len(tile))
    return pl.pallas_call(
        _select_kernel,
        grid=(BATCH, N_KV_HEADS, nq),
        in_specs=[
            pl.BlockSpec((1, Q_PER_KV, HEAD_DIM, SELECT_Q), lambda b, k, i: (b, k, 0, i)),
            whole(N_CHUNKS, HEAD_DIM), whole(HEAD_DIM, N_CHUNKS),
            _const_spec((N_SEL, N_CHUNKS)),
        ],
        out_specs=[step(N_SEL), step(Q_PER_KV * HEAD_DIM)],
        out_shape=[jax.ShapeDtypeStruct((BATCH, N_KV_HEADS, N_SEL, SEQ), BF16),
                   jax.ShapeDtypeStruct((BATCH, N_KV_HEADS, Q_PER_KV * HEAD_DIM, SEQ), F32)],
        compiler_params=pltpu.CompilerParams(
            dimension_semantics=("parallel", "parallel", "parallel"), vmem_limit_bytes=VMEM_LIMIT),
        name="select",
    )(q_raw_t, k_cmp, v_cmp_t, overlap_t)


KV_PER_STEP = 2


def _attend_kernel(qrot_ref, bias_ref, ocmp_ref, ks_ref, vs_ref, kw_ref, vw_ref, gate_ref,
                   o_ref, m_sc, acc_sc, sa_sc, sb_sc, owin_sc):
    qi = pl.program_id(2)
    s0 = qi * NSA_Q
    key_i = lax.broadcasted_iota(jnp.int32, (NSA_Q, NSA_Q), 0)
    qry_i = lax.broadcasted_iota(jnp.int32, (NSA_Q, NSA_Q), 1)
    causal = key_i <= qry_i
    kvs = range(KV_PER_STEP)
    heads = range(KV_PER_STEP * Q_PER_KV)

    def scores(k, q_t):
        return jnp.dot(k, q_t, preferred_element_type=F32)

    def softmax_step(h, s, v_t):
        m_prev = m_sc[h]
        m_new = jnp.maximum(m_prev, jnp.max(s, axis=0, keepdims=True))
        p = jnp.exp2(s - m_new).astype(BF16)
        acc_sc[h] = jnp.exp2(m_prev - m_new) * acc_sc[h] + jnp.dot(v_t, p, preferred_element_type=F32)
        m_sc[h] = m_new

    q_sel = [jnp.concatenate([qrot_ref[0, h], bias_ref[0, h // Q_PER_KV]], axis=0) for h in heads]
    m_sc[...] = jnp.full_like(m_sc, MASK_VALUE)
    acc_sc[...] = jnp.zeros_like(acc_sc)

    def sel_scores(kt, dst):
        rows = pl.ds(pl.multiple_of(kt * NSA_Q, NSA_Q), NSA_Q)
        k = [ks_ref[0, kv, rows, :] for kv in kvs]
        for h in heads:
            dst[h] = scores(k[h // Q_PER_KV], q_sel[h])

    def sel_softmax(kt, src, diagonal=False):
        v_t = [vs_ref[0, kv, kt] for kv in kvs]
        for h in heads:
            s = src[h]
            softmax_step(h, jnp.where(causal, s, MASK_VALUE) if diagonal else s, v_t[h // Q_PER_KV])

    def window_keys(kv, kt):
        return kw_ref[0, kv, pl.ds(pl.multiple_of(kt * NSA_Q, NSA_Q), NSA_Q), :]

    def penalty_rows(penalty):
        rows = jnp.where(lax.broadcasted_iota(jnp.int32, (HEAD_DIM, NSA_Q), 0) == 0, penalty, 0.0)
        return rows.astype(BF16)

    win_mid, win_far = jnp.maximum(qi - 1, 0), jnp.maximum(qi - 2, 0)
    k_diag = [window_keys(kv, qi)[:, 0:HEAD_DIM] for kv in kvs]
    k_mid = [window_keys(kv, win_mid) for kv in kvs]
    k_far = [window_keys(kv, win_far) for kv in kvs]
    pen_mid = penalty_rows(jnp.where(qi >= 1, 0.0, MASK_VALUE))
    pen_far = penalty_rows(jnp.where(qi >= 2, 0.0, MASK_VALUE))
    s_win_diag, s_win_mid, s_win_far = [], [], []
    for h in heads:
        q_t = qrot_ref[0, h]
        kv = h // Q_PER_KV
        s_win_diag.append(scores(k_diag[kv], q_t))
        s_win_mid.append(scores(k_mid[kv], jnp.concatenate([q_t, pen_mid], axis=0)))
        s_win_far.append(scores(k_far[kv], jnp.concatenate([q_t, pen_far], axis=0)))

    sel_scores(0, sa_sc)

    for h in heads:
        s_edge = jnp.where(causal, s_win_diag[h], s_win_far[h])
        s_mid = s_win_mid[h]
        m_w = jnp.max(jnp.maximum(s_edge, s_mid), axis=0, keepdims=True)
        p_edge = jnp.exp2(s_edge - m_w).astype(BF16)
        p_mid = jnp.exp2(s_mid - m_w).astype(BF16)
        zero = jnp.zeros_like(p_edge)
        kv = h // Q_PER_KV
        acc = (jnp.dot(vw_ref[0, kv, qi], jnp.where(causal, p_edge, zero), preferred_element_type=F32)
               + jnp.dot(vw_ref[0, kv, win_far], jnp.where(causal, zero, p_edge), preferred_element_type=F32)
               + jnp.dot(vw_ref[0, kv, win_mid], p_mid, preferred_element_type=F32))
        owin_sc[h] = acc[:HEAD_DIM] / acc[HEAD_DIM:HEAD_DIM + 1]

    def sel_pair(first):
        sel_scores(first + 1, sb_sc)
        sel_softmax(first, sa_sc)
        sel_scores(first + 2, sa_sc)
        sel_softmax(first + 1, sb_sc)

    def sel_quad(t, carry):
        sel_pair(4 * t)
        sel_pair(4 * t + 2)
        return carry

    lax.fori_loop(0, lax.shift_right_logical(qi, 2), sel_quad, 0)

    @pl.when((qi & 2) != 0)
    def _():
        sel_pair(qi & ~3)

    def finish():
        gates_t = [gate_ref[kv].T for kv in kvs]
        outs = []
        for h in heads:
            kv, hh = divmod(h, Q_PER_KV)
            g = gates_t[kv]
            c = hh * N_NSA_BRANCH
            acc = acc_sc[h]
            o_sel = acc[:HEAD_DIM] / acc[HEAD_DIM:HEAD_DIM + 1]
            o_cmp = ocmp_ref[0, kv, hh * HEAD_DIM:(hh + 1) * HEAD_DIM, :]
            outs.append(g[c:c + 1] * o_cmp + g[c + 1:c + 2] * o_sel + g[c + 2:c + 3] * owin_sc[h])
        o_ref[0] = jnp.concatenate(outs, axis=0).T.astype(BF16)

    @pl.when((qi & 1) == 1)
    def _():
        sel_scores(qi, sb_sc)
        sel_softmax(qi - 1, sa_sc)
        sel_softmax(qi, sb_sc, diagonal=True)
        finish()

    @pl.when((qi & 1) == 0)
    def _():
        sel_softmax(qi, sa_sc, diagonal=True)
        finish()


def _attend(q_rot_t, bias, o_cmp_t, k_sel, v_sel_t, k_win, v_win_t, gates):
    nq = SEQ // NSA_Q
    n_heads = KV_PER_STEP * Q_PER_KV
    step = lambda rows: pl.BlockSpec((1, KV_PER_STEP, rows, NSA_Q), lambda b, k, i: (b, k, 0, i))
    whole = lambda *tile: pl.BlockSpec((1, KV_PER_STEP) + tile, lambda b, k, i: (b, k) + (0,) * len(tile))
    return pl.pallas_call(
        _attend_kernel,
        grid=(BATCH, N_KV_HEADS // KV_PER_STEP, nq),
        in_specs=[
            pl.BlockSpec((1, n_heads, HEAD_DIM, NSA_Q), lambda b, k, i: (b, k, 0, i)),
            step(N_SEL), step(Q_PER_KV * HEAD_DIM),
            whole(SEQ, LANES), whole(nq, V_ROWS, NSA_Q),
            whole(SEQ, LANES), whole(nq, V_ROWS, NSA_Q),
            pl.BlockSpec((KV_PER_STEP, NSA_Q, LANES), lambda b, k, i: (k, b * nq + i, 0)),
        ],
        out_specs=pl.BlockSpec((1, NSA_Q, n_heads * HEAD_DIM), lambda b, k, i: (b, i, k)),
        out_shape=jax.ShapeDtypeStruct((BATCH, SEQ, N_HEADS * HEAD_DIM), BF16),
        scratch_shapes=[pltpu.VMEM((n_heads, 1, NSA_Q), F32),
                        pltpu.VMEM((n_heads, V_ROWS, NSA_Q), F32),
                        pltpu.VMEM((n_heads, NSA_Q, NSA_Q), F32),
                        pltpu.VMEM((n_heads, NSA_Q, NSA_Q), F32),
                        pltpu.VMEM((n_heads, HEAD_DIM, NSA_Q), F32)],
        compiler_params=pltpu.CompilerParams(
            dimension_semantics=("parallel", "parallel", "arbitrary"),
            vmem_limit_bytes=VMEM_LIMIT),
        name="attend",
    )(q_rot_t, bias, o_cmp_t, k_sel, v_sel_t, k_win, v_win_t, gates)


def _tail_kernel(h_ref, gu_ref, vn_ref, ob_ref, ga_ref, gb_ref, ws_ref, bs_ref,
                 wa_ref, wb_ref, wo_ref, gn_ref, wi_ref, wf_ref, p_ref, gp_ref, wg_ref, wp_ref,
                 gf_ref, o_ref, *, tm):
    r = lax.broadcasted_iota(jnp.int32, (GM_CHUNK, GM_CHUNK), 0)
    c = lax.broadcasted_iota(jnp.int32, (GM_CHUNK, GM_CHUNK), 1)
    w_tril = [jnp.where(c <= r, ws_ref[g], 0.0).astype(BF16) for g in range(GM_GROUPS)]
    bias = bs_ref[...]
    rows = []
    for ch in range(tm // GM_CHUNK):
        rs = slice(ch * GM_CHUNK, (ch + 1) * GM_CHUNK)
        mix = jnp.concatenate(
            [jnp.dot(w_tril[g], vn_ref[rs, g * LANES:(g + 1) * LANES], preferred_element_type=F32)
             for g in range(GM_GROUPS)], axis=1)
        rows.append(gu_ref[rs, :].astype(F32) * (mix + bias))
    z = jnp.concatenate(rows, axis=0).astype(BF16)
    y_a = jnp.dot(z, wa_ref[...], preferred_element_type=F32)
    y_b = jnp.dot(ob_ref[...], wb_ref[...], preferred_element_type=F32)
    merged = (ga_ref[...].astype(F32) * y_a + gb_ref[...].astype(F32) * y_b).astype(BF16)
    h = h_ref[...] + jnp.dot(merged, wo_ref[...], preferred_element_type=F32)

    h = _half_step_ffn(h, gn_ref[...], wi_ref, wf_ref)

    gate = jax.nn.sigmoid(jnp.dot(_rms(h, gp_ref[...]).astype(BF16), wg_ref[...],
                                  preferred_element_type=F32))
    proj = jnp.dot(p_ref[...].astype(BF16), wp_ref[...], preferred_element_type=F32)
    o_ref[...] = _rms(h + gate * proj, gf_ref[...])


def _tail(h, gu, vn, o_b, ga, gb, w_s, b_s_exp, w_a, w_b, w_o, ffn_norm, ffn_w_in, ffn_w_out,
          p, ple_norm, ple_w_gate, ple_w_proj, final_norm, *, tm=512):
    row = pl.BlockSpec((tm, D_MODEL), lambda i: (i, 0))
    square = _const_spec((D_MODEL, D_MODEL))
    gain = _const_spec((1, D_MODEL))
    return pl.pallas_call(
        functools.partial(_tail_kernel, tm=tm),
        grid=(TOKENS // tm,),
        in_specs=[row, row, row, row, row, row,
                  _const_spec((GM_GROUPS, GM_CHUNK, GM_CHUNK)),
                  _const_spec((GM_CHUNK, GM_WIDTH)),
                  square, square, square,
                  gain, _const_spec((D_MODEL, 2 * D_FF)), _const_spec((D_FF, D_MODEL)),
                  pl.BlockSpec((tm, PLE_DIM), lambda i: (i, 0)), gain, square,
                  _const_spec((PLE_DIM, D_MODEL)), gain],
        out_specs=row,
        out_shape=jax.ShapeDtypeStruct((TOKENS, D_MODEL), F32),
        compiler_params=pltpu.CompilerParams(
            dimension_semantics=("parallel",), vmem_limit_bytes=VMEM_LIMIT),
        name="tail",
    )(h, gu, vn, o_b, ga, gb, w_s, b_s_exp, w_a, w_b, w_o, ffn_norm, ffn_w_in, ffn_w_out,
      p, ple_norm, ple_w_gate, ple_w_proj, final_norm)


def _rope_tables():
    inv_freq = ROPE_THETA ** (-jnp.arange(0, ROPE_DIM, 2, dtype=jnp.float32) / ROPE_DIM)
    ang = jnp.arange(SEQ).astype(jnp.float32)[:, None] * inv_freq[None, :]
    cos, sin = jnp.cos(ang), jnp.sin(ang)
    zero = jnp.zeros_like(cos)
    rest = HEAD_DIM - ROPE_DIM
    c = jnp.concatenate([cos, cos, jnp.ones((SEQ, rest), F32)], axis=1)
    sa = jnp.concatenate([zero, sin, jnp.zeros((SEQ, rest), F32)], axis=1)
    sb = jnp.concatenate([-sin, zero, jnp.zeros((SEQ, rest), F32)], axis=1)
    return [jnp.tile(t, (1, LANES // HEAD_DIM)) for t in (c, sa, sb)]


def _overlap_matrix_t():
    j = jnp.arange(N_SEL)[:, None]
    i = jnp.arange(N_CHUNKS)[None, :]
    r_sel = SEL_LEN // CMP_STRIDE
    l_cmp = CMP_LEN // CMP_STRIDE
    return ((i >= r_sel * j - (l_cmp - 1)) & (i <= r_sel * j + r_sel - 1)).astype(BF16)


def _split_w_in(w_in):
    gate = jnp.pad(w_in[:, OFF_NSA_GATE:OFF_MERGE], ((0, 0), (0, LANES - N_HEADS * N_NSA_BRANCH)))
    return (w_in[:, :OFF_NSA_GATE].astype(BF16), w_in[:, OFF_MERGE:].astype(BF16), gate.astype(BF16))


def kernel(x, p, ffn1_norm, ffn1_w_in, ffn1_w_out, mix_norm, w_in, gm_ln_g, gm_ln_b, gm_w_s, gm_b_s,
           w_branch_a, cmp_pos_k, cmp_k_w1, cmp_k_w2, cmp_pos_v, cmp_v_w1, cmp_v_w2, w_branch_b, w_out,
           ffn2_norm, ffn2_w_in, ffn2_w_out, ple_norm, ple_w_gate, ple_w_proj, final_norm):
    assert x.shape == (BATCH, SEQ, D_MODEL) and p.shape == (1, BATCH, SEQ, PLE_DIM)
    row = lambda a: a.reshape(1, -1)
    h = x.reshape(TOKENS, D_MODEL)

    h = _ffn(h, row(ffn1_norm[0]), ffn1_w_in[0].astype(BF16), ffn1_w_out[0].astype(BF16))

    rope_c, rope_sa, rope_sb = _rope_tables()
    (gu, vn, q_raw, q_rot, k_c, v_c, k_sel, v_sel, k_win, v_win, gates, g_a, g_b) = _inproj(
        h, row(mix_norm[0]), *_split_w_in(w_in[0]), row(gm_ln_g[0]), row(gm_ln_b[0]),
        rope_c, rope_sa, rope_sb)

    half = CMP_STRIDE * HEAD_DIM

    def compress(x_heads, pos, w1, w2, feature_major):
        w2 = (w2.T if feature_major else w2).astype(BF16)
        return _compress(x_heads, pos[:CMP_STRIDE].reshape(1, half), pos[CMP_STRIDE:].reshape(1, half),
                         w1[:half].astype(BF16), w1[half:].astype(BF16), w2, feature_major=feature_major)

    k_cmp = compress(k_c, cmp_pos_k[0], cmp_k_w1[0], cmp_k_w2[0], False)
    v_cmp = compress(v_c, cmp_pos_v[0], cmp_v_w1[0], cmp_v_w2[0], True)

    sel_bias, o_cmp = _select(q_raw, k_cmp, v_cmp, _overlap_matrix_t())
    o_b = _attend(q_rot, sel_bias, o_cmp, k_sel, v_sel, k_win, v_win, gates)

    b_s_exp = jnp.repeat(gm_b_s[0].T, GM_WIDTH // GM_GROUPS, axis=1)
    out = _tail(h, gu, vn, o_b.reshape(TOKENS, N_HEADS * HEAD_DIM), g_a, g_b, gm_w_s[0], b_s_exp,
                w_branch_a[0].astype(BF16), w_branch_b[0].astype(BF16), w_out[0].astype(BF16),
                row(ffn2_norm[0]), ffn2_w_in[0].astype(BF16), ffn2_w_out[0].astype(BF16),
                p[0].reshape(TOKENS, PLE_DIM), row(ple_norm[0]), ple_w_gate[0].astype(BF16),
                ple_w_proj[0].astype(BF16), row(final_norm))
    return out.reshape(BATCH, SEQ, D_MODEL)
```

```python
import functools

import jax
import jax.numpy as jnp
from jax import lax
from jax.experimental import pallas as pl
from jax.experimental.pallas import tpu as pltpu

D_MODEL = 1024
BATCH = 4
SEQ = 4096
PLE_DIM = 256
D_FF = 2816
NORM_EPS = 1e-6
GM_WIDTH = 1024
GM_GROUPS = 8
GM_CHUNK = 128
N_HEADS = 16
N_KV_HEADS = 4
HEAD_DIM = 64
Q_PER_KV = N_HEADS // N_KV_HEADS
KV_WIDTH = N_KV_HEADS * HEAD_DIM
ROPE_DIM = HEAD_DIM // 4
ROPE_HALF = ROPE_DIM // 2
ROPE_THETA = 500000.0
CMP_LEN = 32
CMP_STRIDE = 16
CMP_HIDDEN = 256
SEL_LEN = 64
SEL_TOP = 16
WINDOW = 512
N_NSA_BRANCH = 3
MASK_VALUE = -1e30
FORCE_SCORE = 1e9
LOG2_E = 1.4426950408889634

TOKENS = BATCH * SEQ
N_CHUNKS = SEQ // CMP_STRIDE
N_SEL = SEQ // SEL_LEN
LANES = 128
SUBLANES = 8
VMEM_LIMIT = 56 * 1024 * 1024

OFF_U = 0
OFF_V = OFF_U + GM_WIDTH
OFF_Q = OFF_V + GM_WIDTH
OFF_KV = OFF_Q + N_HEADS * HEAD_DIM
OFF_NSA_GATE = OFF_KV + 6 * KV_WIDTH
OFF_MERGE = OFF_NSA_GATE + N_HEADS * N_NSA_BRANCH
SEL_SHIFT = SEL_LEN.bit_length() - 1

F32 = jnp.float32
BF16 = jnp.bfloat16

NT_DIMS = (((1,), (1,)), ((), ()))


def _const_spec(shape):
    nd = len(shape)
    return pl.BlockSpec(shape, lambda *_: (0,) * nd, pipeline_mode=pl.Buffered(1))


def _rms(x, g):
    return x * lax.rsqrt(jnp.mean(x * x, axis=-1, keepdims=True) + NORM_EPS) * g


MXU_WIDTH = 256


def _half_step_ffn(x, g, wi_ref, wo_ref):
    xn = _rms(x, g).astype(BF16)
    acc = None
    for c in range(D_FF // MXU_WIDTH):
        lo = c * MXU_WIDTH
        gate = jnp.dot(xn, wi_ref[:, lo:lo + MXU_WIDTH], preferred_element_type=F32)
        up = jnp.dot(xn, wi_ref[:, D_FF + lo:D_FF + lo + MXU_WIDTH], preferred_element_type=F32)
        act = (gate * jax.nn.sigmoid(gate) * up).astype(BF16)
        part = jnp.dot(act, wo_ref[lo:lo + MXU_WIDTH, :], preferred_element_type=F32)
        acc = part if acc is None else acc + part
    return x + 0.5 * acc


def _ffn_kernel(x_ref, g_ref, wi_ref, wo_ref, o_ref):
    o_ref[...] = _half_step_ffn(x_ref[...], g_ref[...], wi_ref, wo_ref)


def _ffn(x, g, w_in, w_out, *, tm=512):
    row = pl.BlockSpec((tm, D_MODEL), lambda i: (i, 0))
    return pl.pallas_call(
        _ffn_kernel,
        grid=(TOKENS // tm,),
        in_specs=[row, _const_spec((1, D_MODEL)), _const_spec((D_MODEL, 2 * D_FF)),
                  _const_spec((D_FF, D_MODEL))],
        out_specs=row,
        out_shape=jax.ShapeDtypeStruct((TOKENS, D_MODEL), F32),
        compiler_params=pltpu.CompilerParams(
            dimension_semantics=("parallel",), vmem_limit_bytes=VMEM_LIMIT),
        name="ffn",
    )(x, g, w_in, w_out)


NSA_Q = 256
BF16_ROWS = 16
V_ROWS = HEAD_DIM + BF16_ROWS
INPROJ_KEY_TILES = 2


def _rope(x, c, sa, sb):
    w = x.shape[1]
    return x * c + pltpu.roll(x, ROPE_HALF, 1) * sa + pltpu.roll(x, w - ROPE_HALF, 1) * sb


def _inproj_kernel(h_ref, g_ref, w_ref, wm_ref, wg_ref, lng_ref, lnb_ref, c_ref, sa_ref, sb_ref,
                   gu_ref, vn_ref, qraw_ref, qrot_ref, kc_ref, vc_ref, ks_ref, vs_ref,
                   kw_ref, vw_ref, gate_ref, ga_ref, gb_ref, *, tm):
    n = _rms(h_ref[...], g_ref[...]).astype(BF16)

    def seg(lo, width):
        return jnp.dot(n, w_ref[:, lo:lo + width], preferred_element_type=F32)

    gu_ref[...] = jax.nn.gelu(seg(OFF_U, GM_WIDTH)).astype(BF16)
    v = jax.nn.gelu(seg(OFF_V, GM_WIDTH))
    mu = jnp.mean(v, axis=-1, keepdims=True)
    vc = v - mu
    var = jnp.mean(vc * vc, axis=-1, keepdims=True)
    vn_ref[...] = (vc * lax.rsqrt(var + NORM_EPS) * lng_ref[...] + lnb_ref[...]).astype(BF16)

    c, sa, sb = c_ref[...], sa_ref[...], sb_ref[...]

    def tile_lanes(t, reps):
        return jnp.concatenate([t] * reps, axis=1)

    q = seg(OFF_Q, N_HEADS * HEAD_DIM) * (HEAD_DIM ** -0.5 * LOG2_E)
    reps = N_HEADS * HEAD_DIM // LANES
    q_rot = _rope(q, tile_lanes(c, reps), tile_lanes(sa, reps), tile_lanes(sb, reps))
    qraw_ref[0] = q.T.astype(BF16).reshape(N_HEADS, HEAD_DIM, tm)
    qrot_ref[0] = q_rot.T.astype(BF16).reshape(N_HEADS, HEAD_DIM, tm)

    kv = seg(OFF_KV, 6 * KV_WIDTH)
    k_c, v_c, k_s, v_s, k_w, v_w = [kv[:, i * KV_WIDTH:(i + 1) * KV_WIDTH] for i in range(6)]
    reps = KV_WIDTH // LANES
    ck, sak, sbk = tile_lanes(c, reps), tile_lanes(sa, reps), tile_lanes(sb, reps)
    k_s = _rope(k_s, ck, sak, sbk)
    k_w = _rope(k_w, ck, sak, sbk)
    pos = (pl.program_id(0) % (SEQ // tm)) * tm + lax.broadcasted_iota(jnp.int32, (tm, N_SEL), 0)
    blk = lax.broadcasted_iota(jnp.int32, (tm, N_SEL), 1)
    onehot = jnp.where(lax.shift_right_logical(pos, SEL_SHIFT) == blk, 1.0, 0.0).astype(F32)
    ones_t = jnp.ones((V_ROWS - HEAD_DIM, NSA_Q), F32)
    ones_col = jnp.where(lax.broadcasted_iota(jnp.int32, (tm, HEAD_DIM), 1) == 0, 1.0, 0.0)
    v_s_t, v_w_t = v_s.T, v_w.T
    for h in range(N_KV_HEADS):
        sl = slice(h * HEAD_DIM, (h + 1) * HEAD_DIM)
        kc_ref[0, h] = k_c[:, sl]
        vc_ref[0, h] = v_c[:, sl]
        ks_ref[0, h] = jnp.concatenate([k_s[:, sl], onehot], axis=1).astype(BF16)
        kw_ref[0, h] = jnp.concatenate([k_w[:, sl], ones_col], axis=1).astype(BF16)
        for t in range(tm // NSA_Q):
            keys = slice(t * NSA_Q, (t + 1) * NSA_Q)
            vs_ref[0, h, t] = jnp.concatenate([v_s_t[sl, keys], ones_t], axis=0).astype(BF16)
            vw_ref[0, h, t] = jnp.concatenate([v_w_t[sl, keys], ones_t], axis=0).astype(BF16)

    merge = jax.nn.sigmoid(jnp.dot(n, wm_ref[...], preferred_element_type=F32))
    ga_ref[...] = merge[:, :D_MODEL].astype(BF16)
    gb_ref[...] = merge[:, D_MODEL:].astype(BF16)
    gates = jax.nn.sigmoid(jnp.dot(n, wg_ref[...], preferred_element_type=F32))
    group_cols = Q_PER_KV * N_NSA_BRANCH
    for h in range(N_KV_HEADS):
        gate_ref[h] = gates if h == 0 else pltpu.roll(gates, LANES - h * group_cols, 1)


def _inproj(h, g, w_main, w_merge, w_gate, ln_g, ln_b, rope_c, rope_sa, rope_sb):
    tm = INPROJ_KEY_TILES * NSA_Q
    nq = SEQ // tm
    row = lambda i: (i, 0)
    head = lambda i: (i // nq, 0, i % nq, 0)
    tok_bf = jax.ShapeDtypeStruct((TOKENS, D_MODEL), BF16)

    def head_shape(nh, width, dtype):
        return jax.ShapeDtypeStruct((BATCH, nh, SEQ, width), dtype)

    def head_spec(nh, width):
        return pl.BlockSpec((1, nh, tm, width), head)

    qt_spec = pl.BlockSpec((1, N_HEADS, HEAD_DIM, tm), lambda i: (i // nq, 0, 0, i % nq))
    qt_shape = jax.ShapeDtypeStruct((BATCH, N_HEADS, HEAD_DIM, SEQ), BF16)
    vt_spec = pl.BlockSpec((1, N_KV_HEADS, INPROJ_KEY_TILES, V_ROWS, NSA_Q),
                           lambda i: (i // nq, 0, i % nq, 0, 0))
    vt_shape = jax.ShapeDtypeStruct((BATCH, N_KV_HEADS, SEQ // NSA_Q, V_ROWS, NSA_Q), BF16)

    return pl.pallas_call(
        functools.partial(_inproj_kernel, tm=tm),
        grid=(TOKENS // tm,),
        in_specs=[
            pl.BlockSpec((tm, D_MODEL), row),
            _const_spec((1, D_MODEL)),
            _const_spec((D_MODEL, OFF_NSA_GATE)),
            _const_spec((D_MODEL, 2 * D_MODEL)),
            _const_spec((D_MODEL, LANES)),
            _const_spec((1, GM_WIDTH)),
            _const_spec((1, GM_WIDTH)),
            pl.BlockSpec((tm, LANES), lambda i: (i % nq, 0)),
            pl.BlockSpec((tm, LANES), lambda i: (i % nq, 0)),
            pl.BlockSpec((tm, LANES), lambda i: (i % nq, 0)),
        ],
        out_specs=[
            pl.BlockSpec((tm, GM_WIDTH), row),
            pl.BlockSpec((tm, GM_WIDTH), row),
            qt_spec,
            qt_spec,
            head_spec(N_KV_HEADS, HEAD_DIM),
            head_spec(N_KV_HEADS, HEAD_DIM),
            head_spec(N_KV_HEADS, LANES),
            vt_spec,
            head_spec(N_KV_HEADS, LANES),
            vt_spec,
            pl.BlockSpec((N_KV_HEADS, tm, LANES), lambda i: (0, i, 0)),
            pl.BlockSpec((tm, D_MODEL), row),
            pl.BlockSpec((tm, D_MODEL), row),
        ],
        out_shape=[
            tok_bf, tok_bf,
            qt_shape, qt_shape,
            head_shape(N_KV_HEADS, HEAD_DIM, F32), head_shape(N_KV_HEADS, HEAD_DIM, F32),
            head_shape(N_KV_HEADS, LANES, BF16), vt_shape,
            head_shape(N_KV_HEADS, LANES, BF16), vt_shape,
            jax.ShapeDtypeStruct((N_KV_HEADS, TOKENS, LANES), F32),
            tok_bf, tok_bf,
        ],
        compiler_params=pltpu.CompilerParams(
            dimension_semantics=("parallel",), vmem_limit_bytes=VMEM_LIMIT),
        name="inproj",
    )(h, g, w_main, w_merge, w_gate, ln_g, ln_b, rope_c, rope_sa, rope_sb)


def _compress_kernel(x_ref, ptop_ref, pbot_ref, w1t_ref, w1b_ref, w2_ref, o_ref, *, feature_major):
    x = jnp.concatenate([x_ref[0, 0, pl.ds(j, N_CHUNKS, stride=CMP_STRIDE), :]
                         for j in range(CMP_STRIDE)], axis=1)
    top = jnp.dot((x + ptop_ref[...]).astype(BF16), w1t_ref[...], preferred_element_type=F32)
    bot = jnp.dot((x + pbot_ref[...]).astype(BF16), w1b_ref[...], preferred_element_type=F32)
    hidden = jax.nn.gelu(top + pltpu.roll(bot, N_CHUNKS - 1, 0)).astype(BF16)
    if feature_major:
        out = lax.dot_general(w2_ref[...], hidden, NT_DIMS, preferred_element_type=F32)
    else:
        out = jnp.dot(hidden, w2_ref[...], preferred_element_type=F32)
    o_ref[0, 0] = out.astype(BF16)


def _compress(x_heads, pos_top, pos_bot, w1_top, w1_bot, w2, *, feature_major):
    half = CMP_STRIDE * HEAD_DIM
    out_tile = (HEAD_DIM, N_CHUNKS) if feature_major else (N_CHUNKS, HEAD_DIM)
    return pl.pallas_call(
        functools.partial(_compress_kernel, feature_major=feature_major),
        grid=(BATCH, N_KV_HEADS),
        in_specs=[
            pl.BlockSpec((1, 1, SEQ, HEAD_DIM), lambda b, k: (b, k, 0, 0)),
            _const_spec((1, half)),
            _const_spec((1, half)),
            _const_spec((half, CMP_HIDDEN)),
            _const_spec((half, CMP_HIDDEN)),
            _const_spec(w2.shape),
        ],
        out_specs=pl.BlockSpec((1, 1) + out_tile, lambda b, k: (b, k, 0, 0)),
        out_shape=jax.ShapeDtypeStruct((BATCH, N_KV_HEADS) + out_tile, BF16),
        compiler_params=pltpu.CompilerParams(dimension_semantics=("parallel", "parallel")),
        name="compress",
    )(x_heads, pos_top, pos_bot, w1_top, w1_bot, w2)


RANK_STRIDE = 16
RANK_ACCUMULATORS = 4
SELECT_Q = 1024


def _selection_bias(score, n_blocks):
    n_q = score.shape[1]
    sub_iota = lax.broadcasted_iota(jnp.int32, (SUBLANES, n_q), 0)
    groups = [score[g * SUBLANES:(g + 1) * SUBLANES] for g in range(n_blocks // SUBLANES)]
    counts = [[None] * RANK_ACCUMULATORS for _ in groups]
    for jp in range(n_blocks):
        other = score[jp:jp + 1, :]
        for g, grp in enumerate(groups):
            lo = g * SUBLANES
            if lo > jp:
                before = other >= grp
            elif lo + SUBLANES - 1 <= jp:
                before = other > grp
            else:
                before = (other > grp) | ((other == grp) & (sub_iota + lo > jp))
            inc = jnp.where(before, 1, 0)
            a = jp % RANK_ACCUMULATORS
            counts[g][a] = inc if counts[g][a] is None else counts[g][a] + inc
    rank = jnp.concatenate([(c[0] + c[1]) + (c[2] + c[3]) for c in counts], axis=0)
    bias = jnp.where(rank < SEL_TOP, 0.0, MASK_VALUE)
    if n_blocks < N_SEL:
        bias = jnp.concatenate([bias, jnp.zeros((N_SEL - n_blocks, n_q), F32)], axis=0)
    return bias


def _select_kernel(qraw_ref, kcmp_ref, vcmp_ref, bias_ref, ocmp_ref, psum_sc):
    qi = pl.program_id(2)
    s0 = qi * SELECT_Q
    key_i = lax.broadcasted_iota(jnp.int32, (N_CHUNKS, SELECT_Q), 0)
    qry_i = lax.broadcasted_iota(jnp.int32, (N_CHUNKS, SELECT_Q), 1)
    heads = range(Q_PER_KV)

    k_cmp = kcmp_ref[0, 0]
    v_cmp_t = vcmp_ref[0, 0]
    s_cmp = [jnp.dot(k_cmp, qraw_ref[0, h], preferred_element_type=F32) for h in heads]
    cmp_valid = key_i * CMP_STRIDE + (CMP_LEN - 1) <= s0 + qry_i
    any_valid = s0 + lax.broadcasted_iota(jnp.int32, (1, SELECT_Q), 1) >= CMP_LEN - 1
    p_cmp = []
    for h in heads:
        s_c = jnp.where(cmp_valid, s_cmp[h], MASK_VALUE)
        e_c = jnp.exp2(s_c - jnp.max(s_c, axis=0, keepdims=True))
        p_cmp.append(e_c * jnp.where(any_valid, 1.0 / jnp.sum(e_c, axis=0, keepdims=True), 0.0))
    for h in heads:
        ocmp_ref[0, 0, h * HEAD_DIM:(h + 1) * HEAD_DIM, :] = jnp.dot(
            v_cmp_t, p_cmp[h].astype(BF16), preferred_element_type=F32)

    p_sum = (p_cmp[0] + p_cmp[1]) + (p_cmp[2] + p_cmp[3])
    lane_tiles = range(SELECT_Q // LANES)
    for t in lane_tiles:
        psum_sc[t] = p_sum[:, t * LANES:(t + 1) * LANES]
    per_sel = SEL_LEN // CMP_STRIDE
    terms = [jnp.concatenate([psum_sc[t, pl.ds(r, N_SEL, stride=per_sel), :] for t in lane_tiles], axis=1)
             for r in range(per_sel)]
    blk = lax.broadcasted_iota(jnp.int32, (N_SEL, SELECT_Q), 0)
    before = jnp.where(blk == 0, 0.0, pltpu.roll(terms[per_sel - 1], 1, 0))
    imp_t = (before + terms[0]) + (terms[1] + terms[2]) + terms[3]
    cur = lax.shift_right_logical(s0 + lax.broadcasted_iota(jnp.int32, (N_SEL, SELECT_Q), 1), SEL_SHIFT)
    forced = (blk == 0) | (blk == cur) | (blk == cur - 1)
    score = jnp.where(forced, FORCE_SCORE, jnp.where(blk > cur, -FORCE_SCORE, imp_t))

    steps_per_variant = RANK_STRIDE * SEL_LEN // SELECT_Q
    for variant in range(N_SEL // RANK_STRIDE):
        @pl.when(qi // steps_per_variant == variant)
        def _():
            bias_ref[0, 0] = _selection_bias(score, (variant + 1) * RANK_STRIDE).astype(BF16)


def _select(q_raw_t, k_cmp, v_cmp_t):
    assert CMP_LEN == 2 * CMP_STRIDE
    nq = SEQ // SELECT_Q
    step = lambda rows: pl.BlockSpec((1, 1, rows, SELECT_Q), lambda b, k, i: (b, k, 0, i))
    whole = lambda *tile: pl.BlockSpec((1, 1) + tile, lambda b, k, i: (b, k) + (0,) * len(tile))
    return pl.pallas_call(
        _select_kernel,
        grid=(BATCH, N_KV_HEADS, nq),
        in_specs=[
            pl.BlockSpec((1, Q_PER_KV, HEAD_DIM, SELECT_Q), lambda b, k, i: (b, k, 0, i)),
            whole(N_CHUNKS, HEAD_DIM), whole(HEAD_DIM, N_CHUNKS),
        ],
        out_specs=[step(N_SEL), step(Q_PER_KV * HEAD_DIM)],
        out_shape=[jax.ShapeDtypeStruct((BATCH, N_KV_HEADS, N_SEL, SEQ), BF16),
                   jax.ShapeDtypeStruct((BATCH, N_KV_HEADS, Q_PER_KV * HEAD_DIM, SEQ), F32)],
        scratch_shapes=[pltpu.VMEM((SELECT_Q // LANES, N_CHUNKS, LANES), F32)],
        compiler_params=pltpu.CompilerParams(
            dimension_semantics=("parallel", "parallel", "parallel"), vmem_limit_bytes=VMEM_LIMIT),
        name="select",
    )(q_raw_t, k_cmp, v_cmp_t)


KV_PER_STEP = 2


def _attend_kernel(qrot_ref, bias_ref, ocmp_ref, ks_ref, vs_ref, kw_ref, vw_ref, gate_ref,
                   o_ref, m_sc, acc_sc, sa_sc, sb_sc, owin_sc):
    qi = pl.program_id(2)
    s0 = qi * NSA_Q
    key_i = lax.broadcasted_iota(jnp.int32, (NSA_Q, NSA_Q), 0)
    qry_i = lax.broadcasted_iota(jnp.int32, (NSA_Q, NSA_Q), 1)
    causal = key_i <= qry_i
    kvs = range(KV_PER_STEP)
    heads = range(KV_PER_STEP * Q_PER_KV)

    def scores(k, q_t):
        return jnp.dot(k, q_t, preferred_element_type=F32)

    def softmax_step(h, s, v_t):
        m_prev = m_sc[h]
        m_new = jnp.maximum(m_prev, jnp.max(s, axis=0, keepdims=True))
        p = jnp.exp2(s - m_new).astype(BF16)
        acc_sc[h] = jnp.exp2(m_prev - m_new) * acc_sc[h] + jnp.dot(v_t, p, preferred_element_type=F32)
        m_sc[h] = m_new

    q_sel = [jnp.concatenate([qrot_ref[0, h], bias_ref[0, h // Q_PER_KV]], axis=0) for h in heads]
    m_sc[...] = jnp.full_like(m_sc, MASK_VALUE)
    acc_sc[...] = jnp.zeros_like(acc_sc)

    def sel_scores(kt, dst):
        rows = pl.ds(pl.multiple_of(kt * NSA_Q, NSA_Q), NSA_Q)
        k = [ks_ref[0, kv, rows, :] for kv in kvs]
        for h in heads:
            dst[h] = scores(k[h // Q_PER_KV], q_sel[h])

    def sel_softmax(kt, src, diagonal=False):
        v_t = [vs_ref[0, kv, kt] for kv in kvs]
        for h in heads:
            s = src[h]
            softmax_step(h, jnp.where(causal, s, MASK_VALUE) if diagonal else s, v_t[h // Q_PER_KV])

    def window_keys(kv, kt):
        return kw_ref[0, kv, pl.ds(pl.multiple_of(kt * NSA_Q, NSA_Q), NSA_Q), :]

    def penalty_rows(penalty):
        rows = jnp.where(lax.broadcasted_iota(jnp.int32, (HEAD_DIM, NSA_Q), 0) == 0, penalty, 0.0)
        return rows.astype(BF16)

    win_mid, win_far = jnp.maximum(qi - 1, 0), jnp.maximum(qi - 2, 0)
    k_diag = [window_keys(kv, qi)[:, 0:HEAD_DIM] for kv in kvs]
    k_mid = [window_keys(kv, win_mid) for kv in kvs]
    k_far = [window_keys(kv, win_far) for kv in kvs]
    pen_mid = penalty_rows(jnp.where(qi >= 1, 0.0, MASK_VALUE))
    pen_far = penalty_rows(jnp.where(qi >= 2, 0.0, MASK_VALUE))
    s_win_diag, s_win_mid, s_win_far = [], [], []
    for h in heads:
        q_t = qrot_ref[0, h]
        kv = h // Q_PER_KV
        s_win_diag.append(scores(k_diag[kv], q_t))
        s_win_mid.append(scores(k_mid[kv], jnp.concatenate([q_t, pen_mid], axis=0)))
        s_win_far.append(scores(k_far[kv], jnp.concatenate([q_t, pen_far], axis=0)))

    sel_scores(0, sa_sc)

    for h in heads:
        s_edge = jnp.where(causal, s_win_diag[h], s_win_far[h])
        s_mid = s_win_mid[h]
        m_w = jnp.max(jnp.maximum(s_edge, s_mid), axis=0, keepdims=True)
        p_edge = jnp.exp2(s_edge - m_w).astype(BF16)
        p_mid = jnp.exp2(s_mid - m_w).astype(BF16)
        zero = jnp.zeros_like(p_edge)
        kv = h // Q_PER_KV
        acc = (jnp.dot(vw_ref[0, kv, qi], jnp.where(causal, p_edge, zero), preferred_element_type=F32)
               + jnp.dot(vw_ref[0, kv, win_far], jnp.where(causal, zero, p_edge), preferred_element_type=F32)
               + jnp.dot(vw_ref[0, kv, win_mid], p_mid, preferred_element_type=F32))
        owin_sc[h] = acc[:HEAD_DIM] / acc[HEAD_DIM:HEAD_DIM + 1]

    def sel_pair(first):
        sel_scores(first + 1, sb_sc)
        sel_softmax(first, sa_sc)
        sel_scores(first + 2, sa_sc)
        sel_softmax(first + 1, sb_sc)

    def sel_quad(t, carry):
        sel_pair(4 * t)
        sel_pair(4 * t + 2)
        return carry

    lax.fori_loop(0, lax.shift_right_logical(qi, 2), sel_quad, 0)

    @pl.when((qi & 2) != 0)
    def _():
        sel_pair(qi & ~3)

    def finish():
        gates_t = [gate_ref[kv].T for kv in kvs]
        outs = []
        for h in heads:
            kv, hh = divmod(h, Q_PER_KV)
            g = gates_t[kv]
            c = hh * N_NSA_BRANCH
            acc = acc_sc[h]
            o_sel = acc[:HEAD_DIM] / acc[HEAD_DIM:HEAD_DIM + 1]
            o_cmp = ocmp_ref[0, kv, hh * HEAD_DIM:(hh + 1) * HEAD_DIM, :]
            outs.append(g[c:c + 1] * o_cmp + g[c + 1:c + 2] * o_sel + g[c + 2:c + 3] * owin_sc[h])
        o_ref[0] = jnp.concatenate(outs, axis=0).T.astype(BF16)

    @pl.when((qi & 1) == 1)
    def _():
        sel_scores(qi, sb_sc)
        sel_softmax(qi - 1, sa_sc)
        sel_softmax(qi, sb_sc, diagonal=True)
        finish()

    @pl.when((qi & 1) == 0)
    def _():
        sel_softmax(qi, sa_sc, diagonal=True)
        finish()


def _attend(q_rot_t, bias, o_cmp_t, k_sel, v_sel_t, k_win, v_win_t, gates):
    nq = SEQ // NSA_Q
    n_heads = KV_PER_STEP * Q_PER_KV
    step = lambda rows: pl.BlockSpec((1, KV_PER_STEP, rows, NSA_Q), lambda b, k, i: (b, k, 0, i))
    whole = lambda *tile: pl.BlockSpec((1, KV_PER_STEP) + tile, lambda b, k, i: (b, k) + (0,) * len(tile))
    return pl.pallas_call(
        _attend_kernel,
        grid=(BATCH, N_KV_HEADS // KV_PER_STEP, nq),
        in_specs=[
            pl.BlockSpec((1, n_heads, HEAD_DIM, NSA_Q), lambda b, k, i: (b, k, 0, i)),
            step(N_SEL), step(Q_PER_KV * HEAD_DIM),
            whole(SEQ, LANES), whole(nq, V_ROWS, NSA_Q),
            whole(SEQ, LANES), whole(nq, V_ROWS, NSA_Q),
            pl.BlockSpec((KV_PER_STEP, NSA_Q, LANES), lambda b, k, i: (k, b * nq + i, 0)),
        ],
        out_specs=pl.BlockSpec((1, NSA_Q, n_heads * HEAD_DIM), lambda b, k, i: (b, i, k)),
        out_shape=jax.ShapeDtypeStruct((BATCH, SEQ, N_HEADS * HEAD_DIM), BF16),
        scratch_shapes=[pltpu.VMEM((n_heads, 1, NSA_Q), F32),
                        pltpu.VMEM((n_heads, V_ROWS, NSA_Q), F32),
                        pltpu.VMEM((n_heads, NSA_Q, NSA_Q), F32),
                        pltpu.VMEM((n_heads, NSA_Q, NSA_Q), F32),
                        pltpu.VMEM((n_heads, HEAD_DIM, NSA_Q), F32)],
        compiler_params=pltpu.CompilerParams(
            dimension_semantics=("parallel", "parallel", "arbitrary"),
            vmem_limit_bytes=VMEM_LIMIT),
        name="attend",
    )(q_rot_t, bias, o_cmp_t, k_sel, v_sel_t, k_win, v_win_t, gates)


def _tail_kernel(h_ref, gu_ref, vn_ref, ob_ref, ga_ref, gb_ref, ws_ref, bs_ref,
                 wa_ref, wb_ref, wo_ref, gn_ref, wi_ref, wf_ref, p_ref, gp_ref, wg_ref, wp_ref,
                 gf_ref, o_ref, *, tm):
    r = lax.broadcasted_iota(jnp.int32, (GM_CHUNK, GM_CHUNK), 0)
    c = lax.broadcasted_iota(jnp.int32, (GM_CHUNK, GM_CHUNK), 1)
    w_tril = [jnp.where(c <= r, ws_ref[g], 0.0).astype(BF16) for g in range(GM_GROUPS)]
    bias = bs_ref[...]
    rows = []
    for ch in range(tm // GM_CHUNK):
        rs = slice(ch * GM_CHUNK, (ch + 1) * GM_CHUNK)
        mix = jnp.concatenate(
            [jnp.dot(w_tril[g], vn_ref[rs, g * LANES:(g + 1) * LANES], preferred_element_type=F32)
             for g in range(GM_GROUPS)], axis=1)
        rows.append(gu_ref[rs, :].astype(F32) * (mix + bias))
    z = jnp.concatenate(rows, axis=0).astype(BF16)
    y_a = jnp.dot(z, wa_ref[...], preferred_element_type=F32)
    y_b = jnp.dot(ob_ref[...], wb_ref[...], preferred_element_type=F32)
    merged = (ga_ref[...].astype(F32) * y_a + gb_ref[...].astype(F32) * y_b).astype(BF16)
    h = h_ref[...] + jnp.dot(merged, wo_ref[...], preferred_element_type=F32)

    h = _half_step_ffn(h, gn_ref[...], wi_ref, wf_ref)

    gate = jax.nn.sigmoid(jnp.dot(_rms(h, gp_ref[...]).astype(BF16), wg_ref[...],
                                  preferred_element_type=F32))
    proj = jnp.dot(p_ref[...].astype(BF16), wp_ref[...], preferred_element_type=F32)
    o_ref[...] = _rms(h + gate * proj, gf_ref[...])


def _tail(h, gu, vn, o_b, ga, gb, w_s, b_s_exp, w_a, w_b, w_o, ffn_norm, ffn_w_in, ffn_w_out,
          p, ple_norm, ple_w_gate, ple_w_proj, final_norm, *, tm=512):
    row = pl.BlockSpec((tm, D_MODEL), lambda i: (i, 0))
    square = _const_spec((D_MODEL, D_MODEL))
    gain = _const_spec((1, D_MODEL))
    return pl.pallas_call(
        functools.partial(_tail_kernel, tm=tm),
        grid=(TOKENS // tm,),
        in_specs=[row, row, row, row, row, row,
                  _const_spec((GM_GROUPS, GM_CHUNK, GM_CHUNK)),
                  _const_spec((GM_CHUNK, GM_WIDTH)),
                  square, square, square,
                  gain, _const_spec((D_MODEL, 2 * D_FF)), _const_spec((D_FF, D_MODEL)),
                  pl.BlockSpec((tm, PLE_DIM), lambda i: (i, 0)), gain, square,
                  _const_spec((PLE_DIM, D_MODEL)), gain],
        out_specs=row,
        out_shape=jax.ShapeDtypeStruct((TOKENS, D_MODEL), F32),
        compiler_params=pltpu.CompilerParams(
            dimension_semantics=("parallel",), vmem_limit_bytes=VMEM_LIMIT),
        name="tail",
    )(h, gu, vn, o_b, ga, gb, w_s, b_s_exp, w_a, w_b, w_o, ffn_norm, ffn_w_in, ffn_w_out,
      p, ple_norm, ple_w_gate, ple_w_proj, final_norm)


def _rope_tables():
    inv_freq = ROPE_THETA ** (-jnp.arange(0, ROPE_DIM, 2, dtype=jnp.float32) / ROPE_DIM)
    ang = jnp.arange(SEQ).astype(jnp.float32)[:, None] * inv_freq[None, :]
    cos, sin = jnp.cos(ang), jnp.sin(ang)
    zero = jnp.zeros_like(cos)
    rest = HEAD_DIM - ROPE_DIM
    c = jnp.concatenate([cos, cos, jnp.ones((SEQ, rest), F32)], axis=1)
    sa = jnp.concatenate([zero, sin, jnp.zeros((SEQ, rest), F32)], axis=1)
    sb = jnp.concatenate([-sin, zero, jnp.zeros((SEQ, rest), F32)], axis=1)
    return [jnp.tile(t, (1, LANES // HEAD_DIM)) for t in (c, sa, sb)]


def _split_w_in(w_in):
    gate = jnp.pad(w_in[:, OFF_NSA_GATE:OFF_MERGE], ((0, 0), (0, LANES - N_HEADS * N_NSA_BRANCH)))
    return (w_in[:, :OFF_NSA_GATE].astype(BF16), w_in[:, OFF_MERGE:].astype(BF16), gate.astype(BF16))


def kernel(x, p, ffn1_norm, ffn1_w_in, ffn1_w_out, mix_norm, w_in, gm_ln_g, gm_ln_b, gm_w_s, gm_b_s,
           w_branch_a, cmp_pos_k, cmp_k_w1, cmp_k_w2, cmp_pos_v, cmp_v_w1, cmp_v_w2, w_branch_b, w_out,
           ffn2_norm, ffn2_w_in, ffn2_w_out, ple_norm, ple_w_gate, ple_w_proj, final_norm):
    assert x.shape == (BATCH, SEQ, D_MODEL) and p.shape == (1, BATCH, SEQ, PLE_DIM)
    row = lambda a: a.reshape(1, -1)
    h = x.reshape(TOKENS, D_MODEL)

    h = _ffn(h, row(ffn1_norm[0]), ffn1_w_in[0].astype(BF16), ffn1_w_out[0].astype(BF16))

    rope_c, rope_sa, rope_sb = _rope_tables()
    (gu, vn, q_raw, q_rot, k_c, v_c, k_sel, v_sel, k_win, v_win, gates, g_a, g_b) = _inproj(
        h, row(mix_norm[0]), *_split_w_in(w_in[0]), row(gm_ln_g[0]), row(gm_ln_b[0]),
        rope_c, rope_sa, rope_sb)

    half = CMP_STRIDE * HEAD_DIM

    def compress(x_heads, pos, w1, w2, feature_major):
        w2 = (w2.T if feature_major else w2).astype(BF16)
        return _compress(x_heads, pos[:CMP_STRIDE].reshape(1, half), pos[CMP_STRIDE:].reshape(1, half),
                         w1[:half].astype(BF16), w1[half:].astype(BF16), w2, feature_major=feature_major)

    k_cmp = compress(k_c, cmp_pos_k[0], cmp_k_w1[0], cmp_k_w2[0], False)
    v_cmp = compress(v_c, cmp_pos_v[0], cmp_v_w1[0], cmp_v_w2[0], True)

    sel_bias, o_cmp = _select(q_raw, k_cmp, v_cmp)
    o_b = _attend(q_rot, sel_bias, o_cmp, k_sel, v_sel, k_win, v_win, gates)

    b_s_exp = jnp.repeat(gm_b_s[0].T, GM_WIDTH // GM_GROUPS, axis=1)
    out = _tail(h, gu, vn, o_b.reshape(TOKENS, N_HEADS * HEAD_DIM), g_a, g_b, gm_w_s[0], b_s_exp,
                w_branch_a[0].astype(BF16), w_branch_b[0].astype(BF16), w_out[0].astype(BF16),
                row(ffn2_norm[0]), ffn2_w_in[0].astype(BF16), ffn2_w_out[0].astype(BF16),
                p[0].reshape(TOKENS, PLE_DIM), row(ple_norm[0]), ple_w_gate[0].astype(BF16),
                ple_w_proj[0].astype(BF16), row(final_norm))
    return out.reshape(BATCH, SEQ, D_MODEL)
```

```python
import functools

import jax
import jax.numpy as jnp
from jax import lax
from jax.experimental import pallas as pl
from jax.experimental.pallas import tpu as pltpu

D_MODEL = 1024
BATCH = 4
SEQ = 4096
PLE_DIM = 256
D_FF = 2816
NORM_EPS = 1e-6
GM_WIDTH = 1024
GM_GROUPS = 8
GM_CHUNK = 128
N_HEADS = 16
N_KV_HEADS = 4
HEAD_DIM = 64
Q_PER_KV = N_HEADS // N_KV_HEADS
KV_WIDTH = N_KV_HEADS * HEAD_DIM
ROPE_DIM = HEAD_DIM // 4
ROPE_HALF = ROPE_DIM // 2
ROPE_THETA = 500000.0
CMP_LEN = 32
CMP_STRIDE = 16
CMP_HIDDEN = 256
SEL_LEN = 64
SEL_TOP = 16
WINDOW = 512
N_NSA_BRANCH = 3
MASK_VALUE = -1e30
FORCE_SCORE = 1e9
LOG2_E = 1.4426950408889634

TOKENS = BATCH * SEQ
N_CHUNKS = SEQ // CMP_STRIDE
N_SEL = SEQ // SEL_LEN
LANES = 128
SUBLANES = 8
VMEM_LIMIT = 56 * 1024 * 1024

OFF_U = 0
OFF_V = OFF_U + GM_WIDTH
OFF_Q = OFF_V + GM_WIDTH
OFF_KV = OFF_Q + N_HEADS * HEAD_DIM
OFF_NSA_GATE = OFF_KV + 6 * KV_WIDTH
OFF_MERGE = OFF_NSA_GATE + N_HEADS * N_NSA_BRANCH
SEL_SHIFT = SEL_LEN.bit_length() - 1

F32 = jnp.float32
BF16 = jnp.bfloat16

NT_DIMS = (((1,), (1,)), ((), ()))


def _const_spec(shape):
    nd = len(shape)
    return pl.BlockSpec(shape, lambda *_: (0,) * nd, pipeline_mode=pl.Buffered(1))


def _rms(x, g):
    return x * lax.rsqrt(jnp.mean(x * x, axis=-1, keepdims=True) + NORM_EPS) * g


MXU_WIDTH = 256


def _half_step_ffn(x, g, wi_ref, wo_ref):
    xn = _rms(x, g).astype(BF16)
    acc = None
    for c in range(D_FF // MXU_WIDTH):
        lo = c * MXU_WIDTH
        gate = jnp.dot(xn, wi_ref[:, lo:lo + MXU_WIDTH], preferred_element_type=F32)
        up = jnp.dot(xn, wi_ref[:, D_FF + lo:D_FF + lo + MXU_WIDTH], preferred_element_type=F32)
        act = (gate * jax.nn.sigmoid(gate) * up).astype(BF16)
        part = jnp.dot(act, wo_ref[lo:lo + MXU_WIDTH, :], preferred_element_type=F32)
        acc = part if acc is None else acc + part
    return x + 0.5 * acc


def _ffn_kernel(x_ref, g_ref, wi_ref, wo_ref, o_ref):
    o_ref[...] = _half_step_ffn(x_ref[...], g_ref[...], wi_ref, wo_ref)


def _ffn(x, g, w_in, w_out, *, tm=512):
    row = pl.BlockSpec((tm, D_MODEL), lambda i: (i, 0))
    return pl.pallas_call(
        _ffn_kernel,
        grid=(TOKENS // tm,),
        in_specs=[row, _const_spec((1, D_MODEL)), _const_spec((D_MODEL, 2 * D_FF)),
                  _const_spec((D_FF, D_MODEL))],
        out_specs=row,
        out_shape=jax.ShapeDtypeStruct((TOKENS, D_MODEL), F32),
        compiler_params=pltpu.CompilerParams(
            dimension_semantics=("parallel",), vmem_limit_bytes=VMEM_LIMIT),
        name="ffn",
    )(x, g, w_in, w_out)


NSA_Q = 256
BF16_ROWS = 16
V_ROWS = HEAD_DIM + BF16_ROWS
INPROJ_KEY_TILES = 2


def _rope(x, c, sa, sb):
    w = x.shape[1]
    return x * c + pltpu.roll(x, ROPE_HALF, 1) * sa + pltpu.roll(x, w - ROPE_HALF, 1) * sb


def _inproj_kernel(h_ref, g_ref, w_ref, wm_ref, wg_ref, lng_ref, lnb_ref, c_ref, sa_ref, sb_ref,
                   gu_ref, vn_ref, qraw_ref, qrot_ref, kc_ref, vc_ref, ks_ref, vs_ref,
                   kw_ref, vw_ref, gate_ref, ga_ref, gb_ref, *, tm):
    n = _rms(h_ref[...], g_ref[...]).astype(BF16)

    def seg(lo, width):
        return jnp.dot(n, w_ref[:, lo:lo + width], preferred_element_type=F32)

    gu_ref[...] = jax.nn.gelu(seg(OFF_U, GM_WIDTH)).astype(BF16)
    v = jax.nn.gelu(seg(OFF_V, GM_WIDTH))
    mu = jnp.mean(v, axis=-1, keepdims=True)
    vc = v - mu
    var = jnp.mean(vc * vc, axis=-1, keepdims=True)
    vn_ref[...] = (vc * lax.rsqrt(var + NORM_EPS) * lng_ref[...] + lnb_ref[...]).astype(BF16)

    c, sa, sb = c_ref[...], sa_ref[...], sb_ref[...]

    def tile_lanes(t, reps):
        return jnp.concatenate([t] * reps, axis=1)

    q = seg(OFF_Q, N_HEADS * HEAD_DIM) * (HEAD_DIM ** -0.5 * LOG2_E)
    reps = N_HEADS * HEAD_DIM // LANES
    q_rot = _rope(q, tile_lanes(c, reps), tile_lanes(sa, reps), tile_lanes(sb, reps))
    qraw_ref[0] = q.T.astype(BF16).reshape(N_HEADS, HEAD_DIM, tm)
    qrot_ref[0] = q_rot.T.astype(BF16).reshape(N_HEADS, HEAD_DIM, tm)

    kv = seg(OFF_KV, 6 * KV_WIDTH)
    k_c, v_c, k_s, v_s, k_w, v_w = [kv[:, i * KV_WIDTH:(i + 1) * KV_WIDTH] for i in range(6)]
    reps = KV_WIDTH // LANES
    ck, sak, sbk = tile_lanes(c, reps), tile_lanes(sa, reps), tile_lanes(sb, reps)
    k_s = _rope(k_s, ck, sak, sbk)
    k_w = _rope(k_w, ck, sak, sbk)
    pos = (pl.program_id(0) % (SEQ // tm)) * tm + lax.broadcasted_iota(jnp.int32, (tm, N_SEL), 0)
    blk = lax.broadcasted_iota(jnp.int32, (tm, N_SEL), 1)
    onehot = jnp.where(lax.shift_right_logical(pos, SEL_SHIFT) == blk, 1.0, 0.0).astype(F32)
    ones_t = jnp.ones((V_ROWS - HEAD_DIM, NSA_Q), F32)
    ones_col = jnp.where(lax.broadcasted_iota(jnp.int32, (tm, HEAD_DIM), 1) == 0, 1.0, 0.0)
    v_s_t, v_w_t = v_s.T, v_w.T
    for h in range(N_KV_HEADS):
        sl = slice(h * HEAD_DIM, (h + 1) * HEAD_DIM)
        kc_ref[0, h] = k_c[:, sl]
        vc_ref[0, h] = v_c[:, sl]
        ks_ref[0, h] = jnp.concatenate([k_s[:, sl], onehot], axis=1).astype(BF16)
        kw_ref[0, h] = jnp.concatenate([k_w[:, sl], ones_col], axis=1).astype(BF16)
        for t in range(tm // NSA_Q):
            keys = slice(t * NSA_Q, (t + 1) * NSA_Q)
            vs_ref[0, h, t] = jnp.concatenate([v_s_t[sl, keys], ones_t], axis=0).astype(BF16)
            vw_ref[0, h, t] = jnp.concatenate([v_w_t[sl, keys], ones_t], axis=0).astype(BF16)

    merge = jax.nn.sigmoid(jnp.dot(n, wm_ref[...], preferred_element_type=F32))
    ga_ref[...] = merge[:, :D_MODEL].astype(BF16)
    gb_ref[...] = merge[:, D_MODEL:].astype(BF16)
    gates = jax.nn.sigmoid(jnp.dot(n, wg_ref[...], preferred_element_type=F32))
    group_cols = Q_PER_KV * N_NSA_BRANCH
    for h in range(N_KV_HEADS):
        gate_ref[h] = gates if h == 0 else pltpu.roll(gates, LANES - h * group_cols, 1)


def _inproj(h, g, w_main, w_merge, w_gate, ln_g, ln_b, rope_c, rope_sa, rope_sb):
    tm = INPROJ_KEY_TILES * NSA_Q
    nq = SEQ // tm
    row = lambda i: (i, 0)
    head = lambda i: (i // nq, 0, i % nq, 0)
    tok_bf = jax.ShapeDtypeStruct((TOKENS, D_MODEL), BF16)

    def head_shape(nh, width, dtype):
        return jax.ShapeDtypeStruct((BATCH, nh, SEQ, width), dtype)

    def head_spec(nh, width):
        return pl.BlockSpec((1, nh, tm, width), head)

    qt_spec = pl.BlockSpec((1, N_HEADS, HEAD_DIM, tm), lambda i: (i // nq, 0, 0, i % nq))
    qt_shape = jax.ShapeDtypeStruct((BATCH, N_HEADS, HEAD_DIM, SEQ), BF16)
    vt_spec = pl.BlockSpec((1, N_KV_HEADS, INPROJ_KEY_TILES, V_ROWS, NSA_Q),
                           lambda i: (i // nq, 0, i % nq, 0, 0))
    vt_shape = jax.ShapeDtypeStruct((BATCH, N_KV_HEADS, SEQ // NSA_Q, V_ROWS, NSA_Q), BF16)

    return pl.pallas_call(
        functools.partial(_inproj_kernel, tm=tm),
        grid=(TOKENS // tm,),
        in_specs=[
            pl.BlockSpec((tm, D_MODEL), row),
            _const_spec((1, D_MODEL)),
            _const_spec((D_MODEL, OFF_NSA_GATE)),
            _const_spec((D_MODEL, 2 * D_MODEL)),
            _const_spec((D_MODEL, LANES)),
            _const_spec((1, GM_WIDTH)),
            _const_spec((1, GM_WIDTH)),
            pl.BlockSpec((tm, LANES), lambda i: (i % nq, 0)),
            pl.BlockSpec((tm, LANES), lambda i: (i % nq, 0)),
            pl.BlockSpec((tm, LANES), lambda i: (i % nq, 0)),
        ],
        out_specs=[
            pl.BlockSpec((tm, GM_WIDTH), row),
            pl.BlockSpec((tm, GM_WIDTH), row),
            qt_spec,
            qt_spec,
            head_spec(N_KV_HEADS, HEAD_DIM),
            head_spec(N_KV_HEADS, HEAD_DIM),
            head_spec(N_KV_HEADS, LANES),
            vt_spec,
            head_spec(N_KV_HEADS, LANES),
            vt_spec,
            pl.BlockSpec((N_KV_HEADS, tm, LANES), lambda i: (0, i, 0)),
            pl.BlockSpec((tm, D_MODEL), row),
            pl.BlockSpec((tm, D_MODEL), row),
        ],
        out_shape=[
            tok_bf, tok_bf,
            qt_shape, qt_shape,
            head_shape(N_KV_HEADS, HEAD_DIM, F32), head_shape(N_KV_HEADS, HEAD_DIM, F32),
            head_shape(N_KV_HEADS, LANES, BF16), vt_shape,
            head_shape(N_KV_HEADS, LANES, BF16), vt_shape,
            jax.ShapeDtypeStruct((N_KV_HEADS, TOKENS, LANES), F32),
            tok_bf, tok_bf,
        ],
        compiler_params=pltpu.CompilerParams(
            dimension_semantics=("parallel",), vmem_limit_bytes=VMEM_LIMIT),
        name="inproj",
    )(h, g, w_main, w_merge, w_gate, ln_g, ln_b, rope_c, rope_sa, rope_sb)


def _compress_kernel(x_ref, ptop_ref, pbot_ref, w1t_ref, w1b_ref, w2_ref, o_ref, *, feature_major):
    x = jnp.concatenate([x_ref[0, 0, pl.ds(j, N_CHUNKS, stride=CMP_STRIDE), :]
                         for j in range(CMP_STRIDE)], axis=1)
    top = jnp.dot((x + ptop_ref[...]).astype(BF16), w1t_ref[...], preferred_element_type=F32)
    bot = jnp.dot((x + pbot_ref[...]).astype(BF16), w1b_ref[...], preferred_element_type=F32)
    hidden = jax.nn.gelu(top + pltpu.roll(bot, N_CHUNKS - 1, 0)).astype(BF16)
    if feature_major:
        out = lax.dot_general(w2_ref[...], hidden, NT_DIMS, preferred_element_type=F32)
    else:
        out = jnp.dot(hidden, w2_ref[...], preferred_element_type=F32)
    o_ref[0, 0] = out.astype(BF16)


def _compress(x_heads, pos_top, pos_bot, w1_top, w1_bot, w2, *, feature_major):
    half = CMP_STRIDE * HEAD_DIM
    out_tile = (HEAD_DIM, N_CHUNKS) if feature_major else (N_CHUNKS, HEAD_DIM)
    return pl.pallas_call(
        functools.partial(_compress_kernel, feature_major=feature_major),
        grid=(BATCH, N_KV_HEADS),
        in_specs=[
            pl.BlockSpec((1, 1, SEQ, HEAD_DIM), lambda b, k: (b, k, 0, 0)),
            _const_spec((1, half)),
            _const_spec((1, half)),
            _const_spec((half, CMP_HIDDEN)),
            _const_spec((half, CMP_HIDDEN)),
            _const_spec(w2.shape),
        ],
        out_specs=pl.BlockSpec((1, 1) + out_tile, lambda b, k: (b, k, 0, 0)),
        out_shape=jax.ShapeDtypeStruct((BATCH, N_KV_HEADS) + out_tile, BF16),
        compiler_params=pltpu.CompilerParams(dimension_semantics=("parallel", "parallel")),
        name="compress",
    )(x_heads, pos_top, pos_bot, w1_top, w1_bot, w2)


RANK_STRIDE = 16
RANK_ACCUMULATORS = 4
SELECT_Q = 1024


def _selection_bias(score, n_blocks):
    n_q = score.shape[1]
    sub_iota = lax.broadcasted_iota(jnp.int32, (SUBLANES, n_q), 0)
    groups = [score[g * SUBLANES:(g + 1) * SUBLANES] for g in range(n_blocks // SUBLANES)]
    counts = [[None] * RANK_ACCUMULATORS for _ in groups]
    for jp in range(n_blocks):
        other = score[jp:jp + 1, :]
        for g, grp in enumerate(groups):
            lo = g * SUBLANES
            if lo > jp:
                before = other >= grp
            elif lo + SUBLANES - 1 <= jp:
                before = other > grp
            else:
                before = (other > grp) | ((other == grp) & (sub_iota + lo > jp))
            inc = jnp.where(before, 1, 0)
            a = jp % RANK_ACCUMULATORS
            counts[g][a] = inc if counts[g][a] is None else counts[g][a] + inc
    rank = jnp.concatenate([(c[0] + c[1]) + (c[2] + c[3]) for c in counts], axis=0)
    bias = jnp.where(rank < SEL_TOP, 0.0, MASK_VALUE)
    if n_blocks < N_SEL:
        bias = jnp.concatenate([bias, jnp.zeros((N_SEL - n_blocks, n_q), F32)], axis=0)
    return bias


def _select_kernel(qraw_ref, kcmp_ref, vcmp_ref, bias_ref, ocmp_ref, psum_sc):
    qi = pl.program_id(2)
    s0 = qi * SELECT_Q
    key_i = lax.broadcasted_iota(jnp.int32, (N_CHUNKS, SELECT_Q), 0)
    qry_i = lax.broadcasted_iota(jnp.int32, (N_CHUNKS, SELECT_Q), 1)
    heads = range(Q_PER_KV)

    k_cmp = kcmp_ref[0, 0]
    v_cmp_t = vcmp_ref[0, 0]
    s_cmp = [jnp.dot(k_cmp, qraw_ref[0, h], preferred_element_type=F32) for h in heads]
    cmp_valid = key_i * CMP_STRIDE + (CMP_LEN - 1) <= s0 + qry_i
    any_valid = s0 + lax.broadcasted_iota(jnp.int32, (1, SELECT_Q), 1) >= CMP_LEN - 1
    p_cmp = []
    for h in heads:
        s_c = jnp.where(cmp_valid, s_cmp[h], MASK_VALUE)
        e_c = jnp.exp2(s_c - jnp.max(s_c, axis=0, keepdims=True))
        p_cmp.append(e_c * jnp.where(any_valid, 1.0 / jnp.sum(e_c, axis=0, keepdims=True), 0.0))
    for h in heads:
        ocmp_ref[0, 0, h * HEAD_DIM:(h + 1) * HEAD_DIM, :] = jnp.dot(
            v_cmp_t, p_cmp[h].astype(BF16), preferred_element_type=F32)

    p_sum = (p_cmp[0] + p_cmp[1]) + (p_cmp[2] + p_cmp[3])
    lane_tiles = range(SELECT_Q // LANES)
    for t in lane_tiles:
        psum_sc[t] = p_sum[:, t * LANES:(t + 1) * LANES]
    per_sel = SEL_LEN // CMP_STRIDE
    terms = [jnp.concatenate([psum_sc[t, pl.ds(r, N_SEL, stride=per_sel), :] for t in lane_tiles], axis=1)
             for r in range(per_sel)]
    blk = lax.broadcasted_iota(jnp.int32, (N_SEL, SELECT_Q), 0)
    before = jnp.where(blk == 0, 0.0, pltpu.roll(terms[per_sel - 1], 1, 0))
    imp_t = (before + terms[0]) + (terms[1] + terms[2]) + terms[3]
    cur = lax.shift_right_logical(s0 + lax.broadcasted_iota(jnp.int32, (N_SEL, SELECT_Q), 1), SEL_SHIFT)
    forced = (blk == 0) | (blk == cur) | (blk == cur - 1)
    score = jnp.where(forced, FORCE_SCORE, jnp.where(blk > cur, -FORCE_SCORE, imp_t))

    steps_per_variant = RANK_STRIDE * SEL_LEN // SELECT_Q
    for variant in range(N_SEL // RANK_STRIDE):
        @pl.when(qi // steps_per_variant == variant)
        def _():
            bias_ref[0, 0] = _selection_bias(score, (variant + 1) * RANK_STRIDE).astype(BF16)


def _select(q_raw_t, k_cmp, v_cmp_t):
    assert CMP_LEN == 2 * CMP_STRIDE
    nq = SEQ // SELECT_Q
    step = lambda rows: pl.BlockSpec((1, 1, rows, SELECT_Q), lambda b, k, i: (b, k, 0, i))
    whole = lambda *tile: pl.BlockSpec((1, 1) + tile, lambda b, k, i: (b, k) + (0,) * len(tile))
    return pl.pallas_call(
        _select_kernel,
        grid=(BATCH, N_KV_HEADS, nq),
        in_specs=[
            pl.BlockSpec((1, Q_PER_KV, HEAD_DIM, SELECT_Q), lambda b, k, i: (b, k, 0, i)),
            whole(N_CHUNKS, HEAD_DIM), whole(HEAD_DIM, N_CHUNKS),
        ],
        out_specs=[step(N_SEL), step(Q_PER_KV * HEAD_DIM)],
        out_shape=[jax.ShapeDtypeStruct((BATCH, N_KV_HEADS, N_SEL, SEQ), BF16),
                   jax.ShapeDtypeStruct((BATCH, N_KV_HEADS, Q_PER_KV * HEAD_DIM, SEQ), F32)],
        scratch_shapes=[pltpu.VMEM((SELECT_Q // LANES, N_CHUNKS, LANES), F32)],
        compiler_params=pltpu.CompilerParams(
            dimension_semantics=("parallel", "parallel", "parallel"), vmem_limit_bytes=VMEM_LIMIT),
        name="select",
    )(q_raw_t, k_cmp, v_cmp_t)


KV_PER_STEP = 2


def _attend_kernel(qrot_ref, bias_ref, ocmp_ref, ks_ref, vs_ref, kw_ref, vw_ref, gate_ref,
                   o_ref, m_sc, acc_sc, sa_sc, sb_sc, owin_sc):
    qi = pl.program_id(2)
    s0 = qi * NSA_Q
    key_i = lax.broadcasted_iota(jnp.int32, (NSA_Q, NSA_Q), 0)
    qry_i = lax.broadcasted_iota(jnp.int32, (NSA_Q, NSA_Q), 1)
    causal = key_i <= qry_i
    kvs = range(KV_PER_STEP)
    heads = range(KV_PER_STEP * Q_PER_KV)

    def scores(k, q_t):
        return jnp.dot(k, q_t, preferred_element_type=F32)

    def softmax_step(h, s, v_t):
        m_prev = m_sc[h]
        m_new = jnp.maximum(m_prev, jnp.max(s, axis=0, keepdims=True))
        p = jnp.exp2(s - m_new).astype(BF16)
        acc_sc[h] = jnp.exp2(m_prev - m_new) * acc_sc[h] + jnp.dot(v_t, p, preferred_element_type=F32)
        m_sc[h] = m_new

    q_sel = [jnp.concatenate([qrot_ref[0, h], bias_ref[0, h // Q_PER_KV]], axis=0) for h in heads]
    m_sc[...] = jnp.full_like(m_sc, MASK_VALUE)
    acc_sc[...] = jnp.zeros_like(acc_sc)

    def sel_scores(kt, dst):
        rows = pl.ds(pl.multiple_of(kt * NSA_Q, NSA_Q), NSA_Q)
        k = [ks_ref[0, kv, rows, :] for kv in kvs]
        for h in heads:
            dst[h] = scores(k[h // Q_PER_KV], q_sel[h])

    def sel_softmax(kt, src, diagonal=False):
        v_t = [vs_ref[0, kv, kt] for kv in kvs]
        for h in heads:
            s = src[h]
            softmax_step(h, jnp.where(causal, s, MASK_VALUE) if diagonal else s, v_t[h // Q_PER_KV])

    def window_keys(kv, kt):
        return kw_ref[0, kv, pl.ds(pl.multiple_of(kt * NSA_Q, NSA_Q), NSA_Q), :]

    def penalty_rows(penalty):
        rows = jnp.where(lax.broadcasted_iota(jnp.int32, (HEAD_DIM, NSA_Q), 0) == 0, penalty, 0.0)
        return rows.astype(BF16)

    win_mid, win_far = jnp.maximum(qi - 1, 0), jnp.maximum(qi - 2, 0)
    k_diag = [window_keys(kv, qi)[:, 0:HEAD_DIM] for kv in kvs]
    k_mid = [window_keys(kv, win_mid) for kv in kvs]
    k_far = [window_keys(kv, win_far) for kv in kvs]
    pen_mid = penalty_rows(jnp.where(qi >= 1, 0.0, MASK_VALUE))
    pen_far = penalty_rows(jnp.where(qi >= 2, 0.0, MASK_VALUE))
    s_win_diag, s_win_mid, s_win_far = [], [], []
    for h in heads:
        q_t = qrot_ref[0, h]
        kv = h // Q_PER_KV
        s_win_diag.append(scores(k_diag[kv], q_t))
        s_win_mid.append(scores(k_mid[kv], jnp.concatenate([q_t, pen_mid], axis=0)))
        s_win_far.append(scores(k_far[kv], jnp.concatenate([q_t, pen_far], axis=0)))

    sel_scores(0, sa_sc)

    for h in heads:
        s_edge = jnp.where(causal, s_win_diag[h], s_win_far[h])
        s_mid = s_win_mid[h]
        m_w = jnp.max(jnp.maximum(s_edge, s_mid), axis=0, keepdims=True)
        p_edge = jnp.exp2(s_edge - m_w).astype(BF16)
        p_mid = jnp.exp2(s_mid - m_w).astype(BF16)
        zero = jnp.zeros_like(p_edge)
        kv = h // Q_PER_KV
        acc = (jnp.dot(vw_ref[0, kv, qi], jnp.where(causal, p_edge, zero), preferred_element_type=F32)
               + jnp.dot(vw_ref[0, kv, win_far], jnp.where(causal, zero, p_edge), preferred_element_type=F32)
               + jnp.dot(vw_ref[0, kv, win_mid], p_mid, preferred_element_type=F32))
        owin_sc[h] = acc[:HEAD_DIM] / acc[HEAD_DIM:HEAD_DIM + 1]

    def sel_pair(first):
        sel_scores(first + 1, sb_sc)
        sel_softmax(first, sa_sc)
        sel_scores(first + 2, sa_sc)
        sel_softmax(first + 1, sb_sc)

    def sel_quad(t, carry):
        sel_pair(4 * t)
        sel_pair(4 * t + 2)
        return carry

    lax.fori_loop(0, lax.shift_right_logical(qi, 2), sel_quad, 0)

    @pl.when((qi & 2) != 0)
    def _():
        sel_pair(qi & ~3)

    def finish():
        gates_t = [gate_ref[kv].T for kv in kvs]
        outs = []
        for h in heads:
            kv, hh = divmod(h, Q_PER_KV)
            g = gates_t[kv]
            c = hh * N_NSA_BRANCH
            acc = acc_sc[h]
            o_sel = acc[:HEAD_DIM] / acc[HEAD_DIM:HEAD_DIM + 1]
            o_cmp = ocmp_ref[0, kv, hh * HEAD_DIM:(hh + 1) * HEAD_DIM, :]
            outs.append(g[c:c + 1] * o_cmp + g[c + 1:c + 2] * o_sel + g[c + 2:c + 3] * owin_sc[h])
        o_ref[0] = jnp.concatenate(outs, axis=0).T.astype(BF16)

    @pl.when((qi & 1) == 1)
    def _():
        sel_scores(qi, sb_sc)
        sel_softmax(qi - 1, sa_sc)
        sel_softmax(qi, sb_sc, diagonal=True)
        finish()

    @pl.when((qi & 1) == 0)
    def _():
        sel_softmax(qi, sa_sc, diagonal=True)
        finish()


def _attend(q_rot_t, bias, o_cmp_t, k_sel, v_sel_t, k_win, v_win_t, gates):
    nq = SEQ // NSA_Q
    n_heads = KV_PER_STEP * Q_PER_KV
    step = lambda rows: pl.BlockSpec((1, KV_PER_STEP, rows, NSA_Q), lambda b, k, i: (b, k, 0, i))
    whole = lambda *tile: pl.BlockSpec((1, KV_PER_STEP) + tile, lambda b, k, i: (b, k) + (0,) * len(tile))
    return pl.pallas_call(
        _attend_kernel,
        grid=(BATCH, N_KV_HEADS // KV_PER_STEP, nq),
        in_specs=[
            pl.BlockSpec((1, n_heads, HEAD_DIM, NSA_Q), lambda b, k, i: (b, k, 0, i)),
            step(N_SEL), step(Q_PER_KV * HEAD_DIM),
            whole(SEQ, LANES), whole(nq, V_ROWS, NSA_Q),
            whole(SEQ, LANES), whole(nq, V_ROWS, NSA_Q),
            pl.BlockSpec((KV_PER_STEP, NSA_Q, LANES), lambda b, k, i: (k, b * nq + i, 0)),
        ],
        out_specs=pl.BlockSpec((1, NSA_Q, n_heads * HEAD_DIM), lambda b, k, i: (b, i, k)),
        out_shape=jax.ShapeDtypeStruct((BATCH, SEQ, N_HEADS * HEAD_DIM), BF16),
        scratch_shapes=[pltpu.VMEM((n_heads, 1, NSA_Q), F32),
                        pltpu.VMEM((n_heads, V_ROWS, NSA_Q), F32),
                        pltpu.VMEM((n_heads, NSA_Q, NSA_Q), F32),
                        pltpu.VMEM((n_heads, NSA_Q, NSA_Q), F32),
                        pltpu.VMEM((n_heads, HEAD_DIM, NSA_Q), F32)],
        compiler_params=pltpu.CompilerParams(
            dimension_semantics=("parallel", "parallel", "arbitrary"),
            vmem_limit_bytes=VMEM_LIMIT),
        name="attend",
    )(q_rot_t, bias, o_cmp_t, k_sel, v_sel_t, k_win, v_win_t, gates)


def _tail_kernel(h_ref, gu_ref, vn_ref, ob_ref, ga_ref, gb_ref, ws_ref, bs_ref,
                 wa_ref, wb_ref, wo_ref, gn_ref, wi_ref, wf_ref, p_ref, gp_ref, wg_ref, wp_ref,
                 gf_ref, o_ref, *, tm):
    r = lax.broadcasted_iota(jnp.int32, (GM_CHUNK, GM_CHUNK), 0)
    c = lax.broadcasted_iota(jnp.int32, (GM_CHUNK, GM_CHUNK), 1)
    w_tril = [jnp.where(c <= r, ws_ref[g], 0.0).astype(BF16) for g in range(GM_GROUPS)]
    bias = bs_ref[...]
    rows = []
    for ch in range(tm // GM_CHUNK):
        rs = slice(ch * GM_CHUNK, (ch + 1) * GM_CHUNK)
        mix = jnp.concatenate(
            [jnp.dot(w_tril[g], vn_ref[rs, g * LANES:(g + 1) * LANES], preferred_element_type=F32)
             for g in range(GM_GROUPS)], axis=1)
        rows.append(gu_ref[rs, :].astype(F32) * (mix + bias))
    z = jnp.concatenate(rows, axis=0).astype(BF16)
    y_a = jnp.dot(z, wa_ref[...], preferred_element_type=F32)
    y_b = jnp.dot(ob_ref[...], wb_ref[...], preferred_element_type=F32)
    merged = (ga_ref[...].astype(F32) * y_a + gb_ref[...].astype(F32) * y_b).astype(BF16)
    h = h_ref[...] + jnp.dot(merged, wo_ref[...], preferred_element_type=F32)

    h = _half_step_ffn(h, gn_ref[...], wi_ref, wf_ref)

    gate = jax.nn.sigmoid(jnp.dot(_rms(h, gp_ref[...]).astype(BF16), wg_ref[...],
                                  preferred_element_type=F32))
    proj = jnp.dot(p_ref[...].astype(BF16), wp_ref[...], preferred_element_type=F32)
    o_ref[...] = _rms(h + gate * proj, gf_ref[...])


def _tail(h, gu, vn, o_b, ga, gb, w_s, b_s_exp, w_a, w_b, w_o, ffn_norm, ffn_w_in, ffn_w_out,
          p, ple_norm, ple_w_gate, ple_w_proj, final_norm, *, tm=512):
    row = pl.BlockSpec((tm, D_MODEL), lambda i: (i, 0))
    square = _const_spec((D_MODEL, D_MODEL))
    gain = _const_spec((1, D_MODEL))
    return pl.pallas_call(
        functools.partial(_tail_kernel, tm=tm),
        grid=(TOKENS // tm,),
        in_specs=[row, row, row, row, row, row,
                  _const_spec((GM_GROUPS, GM_CHUNK, GM_CHUNK)),
                  _const_spec((GM_CHUNK, GM_WIDTH)),
                  square, square, square,
                  gain, _const_spec((D_MODEL, 2 * D_FF)), _const_spec((D_FF, D_MODEL)),
                  pl.BlockSpec((tm, PLE_DIM), lambda i: (i, 0)), gain, square,
                  _const_spec((PLE_DIM, D_MODEL)), gain],
        out_specs=row,
        out_shape=jax.ShapeDtypeStruct((TOKENS, D_MODEL), F32),
        compiler_params=pltpu.CompilerParams(
            dimension_semantics=("parallel",), vmem_limit_bytes=VMEM_LIMIT),
        name="tail",
    )(h, gu, vn, o_b, ga, gb, w_s, b_s_exp, w_a, w_b, w_o, ffn_norm, ffn_w_in, ffn_w_out,
      p, ple_norm, ple_w_gate, ple_w_proj, final_norm)


def _rope_tables():
    inv_freq = ROPE_THETA ** (-jnp.arange(0, ROPE_DIM, 2, dtype=jnp.float32) / ROPE_DIM)
    ang = jnp.arange(SEQ).astype(jnp.float32)[:, None] * inv_freq[None, :]
    cos, sin = jnp.cos(ang), jnp.sin(ang)
    zero = jnp.zeros_like(cos)
    rest = HEAD_DIM - ROPE_DIM
    c = jnp.concatenate([cos, cos, jnp.ones((SEQ, rest), F32)], axis=1)
    sa = jnp.concatenate([zero, sin, jnp.zeros((SEQ, rest), F32)], axis=1)
    sb = jnp.concatenate([-sin, zero, jnp.zeros((SEQ, rest), F32)], axis=1)
    return [jnp.tile(t, (1, LANES // HEAD_DIM)) for t in (c, sa, sb)]


def _split_w_in_kernel(w_ref, main_ref, merge_ref, gate_ref):
    w = w_ref[...]
    main_ref[...] = w[:, :OFF_NSA_GATE].astype(BF16)
    merge_ref[...] = w[:, OFF_MERGE:].astype(BF16)
    n_gate = N_HEADS * N_NSA_BRANCH
    gate = jnp.concatenate([w[:, OFF_NSA_GATE:OFF_MERGE],
                            jnp.zeros((w.shape[0], LANES - n_gate), F32)], axis=1)
    gate_ref[...] = gate.astype(BF16)


def _split_w_in(w_in, *, tr=128):
    width = w_in.shape[1]
    row = lambda cols: pl.BlockSpec((tr, cols), lambda i: (i, 0))
    shape = lambda cols: jax.ShapeDtypeStruct((D_MODEL, cols), BF16)
    return pl.pallas_call(
        _split_w_in_kernel,
        grid=(D_MODEL // tr,),
        in_specs=[row(width)],
        out_specs=[row(OFF_NSA_GATE), row(width - OFF_MERGE), row(LANES)],
        out_shape=[shape(OFF_NSA_GATE), shape(width - OFF_MERGE), shape(LANES)],
        compiler_params=pltpu.CompilerParams(dimension_semantics=("parallel",)),
        name="split_w_in",
    )(w_in)


def kernel(x, p, ffn1_norm, ffn1_w_in, ffn1_w_out, mix_norm, w_in, gm_ln_g, gm_ln_b, gm_w_s, gm_b_s,
           w_branch_a, cmp_pos_k, cmp_k_w1, cmp_k_w2, cmp_pos_v, cmp_v_w1, cmp_v_w2, w_branch_b, w_out,
           ffn2_norm, ffn2_w_in, ffn2_w_out, ple_norm, ple_w_gate, ple_w_proj, final_norm):
    assert x.shape == (BATCH, SEQ, D_MODEL) and p.shape == (1, BATCH, SEQ, PLE_DIM)
    row = lambda a: a.reshape(1, -1)
    h = x.reshape(TOKENS, D_MODEL)

    h = _ffn(h, row(ffn1_norm[0]), ffn1_w_in[0].astype(BF16), ffn1_w_out[0].astype(BF16))

    rope_c, rope_sa, rope_sb = _rope_tables()
    (gu, vn, q_raw, q_rot, k_c, v_c, k_sel, v_sel, k_win, v_win, gates, g_a, g_b) = _inproj(
        h, row(mix_norm[0]), *_split_w_in(w_in[0]), row(gm_ln_g[0]), row(gm_ln_b[0]),
        rope_c, rope_sa, rope_sb)

    half = CMP_STRIDE * HEAD_DIM

    def compress(x_heads, pos, w1, w2, feature_major):
        w2 = (w2.T if feature_major else w2).astype(BF16)
        return _compress(x_heads, pos[:CMP_STRIDE].reshape(1, half), pos[CMP_STRIDE:].reshape(1, half),
                         w1[:half].astype(BF16), w1[half:].astype(BF16), w2, feature_major=feature_major)

    k_cmp = compress(k_c, cmp_pos_k[0], cmp_k_w1[0], cmp_k_w2[0], False)
    v_cmp = compress(v_c, cmp_pos_v[0], cmp_v_w1[0], cmp_v_w2[0], True)

    sel_bias, o_cmp = _select(q_raw, k_cmp, v_cmp)
    o_b = _attend(q_rot, sel_bias, o_cmp, k_sel, v_sel, k_win, v_win, gates)

    b_s_exp = jnp.repeat(gm_b_s[0].T, GM_WIDTH // GM_GROUPS, axis=1)
    out = _tail(h, gu, vn, o_b.reshape(TOKENS, N_HEADS * HEAD_DIM), g_a, g_b, gm_w_s[0], b_s_exp,
                w_branch_a[0].astype(BF16), w_branch_b[0].astype(BF16), w_out[0].astype(BF16),
                row(ffn2_norm[0]), ffn2_w_in[0].astype(BF16), ffn2_w_out[0].astype(BF16),
                p[0].reshape(TOKENS, PLE_DIM), row(ple_norm[0]), ple_w_gate[0].astype(BF16),
                ple_w_proj[0].astype(BF16), row(final_norm))
    return out.reshape(BATCH, SEQ, D_MODEL)
```

```python
import functools

import jax
import jax.numpy as jnp
from jax import lax
from jax.experimental import pallas as pl
from jax.experimental.pallas import tpu as pltpu

D_MODEL = 1024
BATCH = 4
SEQ = 4096
PLE_DIM = 256
D_FF = 2816
NORM_EPS = 1e-6
GM_WIDTH = 1024
GM_GROUPS = 8
GM_CHUNK = 128
N_HEADS = 16
N_KV_HEADS = 4
HEAD_DIM = 64
Q_PER_KV = N_HEADS // N_KV_HEADS
KV_WIDTH = N_KV_HEADS * HEAD_DIM
ROPE_DIM = HEAD_DIM // 4
ROPE_HALF = ROPE_DIM // 2
ROPE_THETA = 500000.0
CMP_LEN = 32
CMP_STRIDE = 16
CMP_HIDDEN = 256
SEL_LEN = 64
SEL_TOP = 16
WINDOW = 512
N_NSA_BRANCH = 3
MASK_VALUE = -1e30
FORCE_SCORE = 1e9
LOG2_E = 1.4426950408889634

TOKENS = BATCH * SEQ
N_CHUNKS = SEQ // CMP_STRIDE
N_SEL = SEQ // SEL_LEN
LANES = 128
SUBLANES = 8
VMEM_LIMIT = 56 * 1024 * 1024

OFF_U = 0
OFF_V = OFF_U + GM_WIDTH
OFF_Q = OFF_V + GM_WIDTH
OFF_KV = OFF_Q + N_HEADS * HEAD_DIM
OFF_NSA_GATE = OFF_KV + 6 * KV_WIDTH
OFF_MERGE = OFF_NSA_GATE + N_HEADS * N_NSA_BRANCH
SEL_SHIFT = SEL_LEN.bit_length() - 1

F32 = jnp.float32
BF16 = jnp.bfloat16

NT_DIMS = (((1,), (1,)), ((), ()))


def _const_spec(shape):
    nd = len(shape)
    return pl.BlockSpec(shape, lambda *_: (0,) * nd, pipeline_mode=pl.Buffered(1))


def _rms(x, g):
    return x * lax.rsqrt(jnp.mean(x * x, axis=-1, keepdims=True) + NORM_EPS) * g


MXU_WIDTH = 256


def _half_step_ffn(x, g, wi_ref, wo_ref):
    xn = _rms(x, g).astype(BF16)
    acc = None
    for c in range(D_FF // MXU_WIDTH):
        lo = c * MXU_WIDTH
        gate = jnp.dot(xn, wi_ref[:, lo:lo + MXU_WIDTH], preferred_element_type=F32)
        up = jnp.dot(xn, wi_ref[:, D_FF + lo:D_FF + lo + MXU_WIDTH], preferred_element_type=F32)
        act = (gate * jax.nn.sigmoid(gate) * up).astype(BF16)
        part = jnp.dot(act, wo_ref[lo:lo + MXU_WIDTH, :], preferred_element_type=F32)
        acc = part if acc is None else acc + part
    return x + 0.5 * acc


def _ffn_kernel(x_ref, g_ref, wi_ref, wo_ref, o_ref):
    o_ref[...] = _half_step_ffn(x_ref[...], g_ref[...], wi_ref, wo_ref)


def _ffn(x, g, w_in, w_out, *, tm=512):
    row = pl.BlockSpec((tm, D_MODEL), lambda i: (i, 0))
    return pl.pallas_call(
        _ffn_kernel,
        grid=(TOKENS // tm,),
        in_specs=[row, _const_spec((1, D_MODEL)), _const_spec((D_MODEL, 2 * D_FF)),
                  _const_spec((D_FF, D_MODEL))],
        out_specs=row,
        out_shape=jax.ShapeDtypeStruct((TOKENS, D_MODEL), F32),
        compiler_params=pltpu.CompilerParams(
            dimension_semantics=("parallel",), vmem_limit_bytes=VMEM_LIMIT),
        name="ffn",
    )(x, g, w_in, w_out)


NSA_Q = 256
BF16_ROWS = 16
V_ROWS = HEAD_DIM + BF16_ROWS
INPROJ_KEY_TILES = 2


def _rope(x, c, sa, sb):
    w = x.shape[1]
    return x * c + pltpu.roll(x, ROPE_HALF, 1) * sa + pltpu.roll(x, w - ROPE_HALF, 1) * sb


def _inproj_kernel(h_ref, g_ref, wt_ref, lng_ref, lnb_ref, c_ref, sa_ref, sb_ref,
                   gu_ref, vn_ref, qraw_ref, qrot_ref, kc_ref, vc_ref, ks_ref, vs_ref,
                   kw_ref, vw_ref, gate_ref, ga_ref, gb_ref, *, tm):
    n = _rms(h_ref[...], g_ref[...]).astype(BF16)

    def seg(lo, width):
        return lax.dot_general(n, wt_ref[lo:lo + width, :], NT_DIMS, preferred_element_type=F32)

    gu_ref[...] = jax.nn.gelu(seg(OFF_U, GM_WIDTH)).astype(BF16)
    v = jax.nn.gelu(seg(OFF_V, GM_WIDTH))
    mu = jnp.mean(v, axis=-1, keepdims=True)
    vc = v - mu
    var = jnp.mean(vc * vc, axis=-1, keepdims=True)
    vn_ref[...] = (vc * lax.rsqrt(var + NORM_EPS) * lng_ref[...] + lnb_ref[...]).astype(BF16)

    c, sa, sb = c_ref[...], sa_ref[...], sb_ref[...]

    def tile_lanes(t, reps):
        return jnp.concatenate([t] * reps, axis=1)

    q = seg(OFF_Q, N_HEADS * HEAD_DIM) * (HEAD_DIM ** -0.5 * LOG2_E)
    reps = N_HEADS * HEAD_DIM // LANES
    q_rot = _rope(q, tile_lanes(c, reps), tile_lanes(sa, reps), tile_lanes(sb, reps))
    qraw_ref[0] = q.T.astype(BF16).reshape(N_HEADS, HEAD_DIM, tm)
    qrot_ref[0] = q_rot.T.astype(BF16).reshape(N_HEADS, HEAD_DIM, tm)

    kv = seg(OFF_KV, 6 * KV_WIDTH)
    k_c, v_c, k_s, v_s, k_w, v_w = [kv[:, i * KV_WIDTH:(i + 1) * KV_WIDTH] for i in range(6)]
    reps = KV_WIDTH // LANES
    ck, sak, sbk = tile_lanes(c, reps), tile_lanes(sa, reps), tile_lanes(sb, reps)
    k_s = _rope(k_s, ck, sak, sbk)
    k_w = _rope(k_w, ck, sak, sbk)
    pos = (pl.program_id(0) % (SEQ // tm)) * tm + lax.broadcasted_iota(jnp.int32, (tm, N_SEL), 0)
    blk = lax.broadcasted_iota(jnp.int32, (tm, N_SEL), 1)
    onehot = jnp.where(lax.shift_right_logical(pos, SEL_SHIFT) == blk, 1.0, 0.0).astype(F32)
    ones_t = jnp.ones((V_ROWS - HEAD_DIM, NSA_Q), F32)
    ones_col = jnp.where(lax.broadcasted_iota(jnp.int32, (tm, HEAD_DIM), 1) == 0, 1.0, 0.0)
    v_s_t, v_w_t = v_s.T, v_w.T
    for h in range(N_KV_HEADS):
        sl = slice(h * HEAD_DIM, (h + 1) * HEAD_DIM)
        kc_ref[0, h] = k_c[:, sl]
        vc_ref[0, h] = v_c[:, sl]
        ks_ref[0, h] = jnp.concatenate([k_s[:, sl], onehot], axis=1).astype(BF16)
        kw_ref[0, h] = jnp.concatenate([k_w[:, sl], ones_col], axis=1).astype(BF16)
        for t in range(tm // NSA_Q):
            keys = slice(t * NSA_Q, (t + 1) * NSA_Q)
            vs_ref[0, h, t] = jnp.concatenate([v_s_t[sl, keys], ones_t], axis=0).astype(BF16)
            vw_ref[0, h, t] = jnp.concatenate([v_w_t[sl, keys], ones_t], axis=0).astype(BF16)

    merge = jax.nn.sigmoid(seg(OFF_MERGE, 2 * D_MODEL))
    ga_ref[...] = merge[:, :D_MODEL].astype(BF16)
    gb_ref[...] = merge[:, D_MODEL:].astype(BF16)
    gates = jax.nn.sigmoid(seg(OFF_NSA_GATE, LANES))
    group_cols = Q_PER_KV * N_NSA_BRANCH
    for h in range(N_KV_HEADS):
        gate_ref[h] = gates if h == 0 else pltpu.roll(gates, LANES - h * group_cols, 1)


def _inproj(h, g, w_in_t, ln_g, ln_b, rope_c, rope_sa, rope_sb):
    tm = INPROJ_KEY_TILES * NSA_Q
    nq = SEQ // tm
    row = lambda i: (i, 0)
    head = lambda i: (i // nq, 0, i % nq, 0)
    tok_bf = jax.ShapeDtypeStruct((TOKENS, D_MODEL), BF16)

    def head_shape(nh, width, dtype):
        return jax.ShapeDtypeStruct((BATCH, nh, SEQ, width), dtype)

    def head_spec(nh, width):
        return pl.BlockSpec((1, nh, tm, width), head)

    qt_spec = pl.BlockSpec((1, N_HEADS, HEAD_DIM, tm), lambda i: (i // nq, 0, 0, i % nq))
    qt_shape = jax.ShapeDtypeStruct((BATCH, N_HEADS, HEAD_DIM, SEQ), BF16)
    vt_spec = pl.BlockSpec((1, N_KV_HEADS, INPROJ_KEY_TILES, V_ROWS, NSA_Q),
                           lambda i: (i // nq, 0, i % nq, 0, 0))
    vt_shape = jax.ShapeDtypeStruct((BATCH, N_KV_HEADS, SEQ // NSA_Q, V_ROWS, NSA_Q), BF16)

    return pl.pallas_call(
        functools.partial(_inproj_kernel, tm=tm),
        grid=(TOKENS // tm,),
        in_specs=[
            pl.BlockSpec((tm, D_MODEL), row),
            _const_spec((1, D_MODEL)),
            _const_spec(w_in_t.shape),
            _const_spec((1, GM_WIDTH)),
            _const_spec((1, GM_WIDTH)),
            pl.BlockSpec((tm, LANES), lambda i: (i % nq, 0)),
            pl.BlockSpec((tm, LANES), lambda i: (i % nq, 0)),
            pl.BlockSpec((tm, LANES), lambda i: (i % nq, 0)),
        ],
        out_specs=[
            pl.BlockSpec((tm, GM_WIDTH), row),
            pl.BlockSpec((tm, GM_WIDTH), row),
            qt_spec,
            qt_spec,
            head_spec(N_KV_HEADS, HEAD_DIM),
            head_spec(N_KV_HEADS, HEAD_DIM),
            head_spec(N_KV_HEADS, LANES),
            vt_spec,
            head_spec(N_KV_HEADS, LANES),
            vt_spec,
            pl.BlockSpec((N_KV_HEADS, tm, LANES), lambda i: (0, i, 0)),
            pl.BlockSpec((tm, D_MODEL), row),
            pl.BlockSpec((tm, D_MODEL), row),
        ],
        out_shape=[
            tok_bf, tok_bf,
            qt_shape, qt_shape,
            head_shape(N_KV_HEADS, HEAD_DIM, F32), head_shape(N_KV_HEADS, HEAD_DIM, F32),
            head_shape(N_KV_HEADS, LANES, BF16), vt_shape,
            head_shape(N_KV_HEADS, LANES, BF16), vt_shape,
            jax.ShapeDtypeStruct((N_KV_HEADS, TOKENS, LANES), F32),
            tok_bf, tok_bf,
        ],
        compiler_params=pltpu.CompilerParams(
            dimension_semantics=("parallel",), vmem_limit_bytes=VMEM_LIMIT),
        name="inproj",
    )(h, g, w_in_t, ln_g, ln_b, rope_c, rope_sa, rope_sb)


def _compress_kernel(x_ref, ptop_ref, pbot_ref, w1t_ref, w1b_ref, w2_ref, o_ref, *, feature_major):
    x = jnp.concatenate([x_ref[0, 0, pl.ds(j, N_CHUNKS, stride=CMP_STRIDE), :]
                         for j in range(CMP_STRIDE)], axis=1)
    top = jnp.dot((x + ptop_ref[...]).astype(BF16), w1t_ref[...], preferred_element_type=F32)
    bot = jnp.dot((x + pbot_ref[...]).astype(BF16), w1b_ref[...], preferred_element_type=F32)
    hidden = jax.nn.gelu(top + pltpu.roll(bot, N_CHUNKS - 1, 0)).astype(BF16)
    if feature_major:
        out = lax.dot_general(w2_ref[...], hidden, NT_DIMS, preferred_element_type=F32)
    else:
        out = jnp.dot(hidden, w2_ref[...], preferred_element_type=F32)
    o_ref[0, 0] = out.astype(BF16)


def _compress(x_heads, pos_top, pos_bot, w1_top, w1_bot, w2, *, feature_major):
    half = CMP_STRIDE * HEAD_DIM
    out_tile = (HEAD_DIM, N_CHUNKS) if feature_major else (N_CHUNKS, HEAD_DIM)
    return pl.pallas_call(
        functools.partial(_compress_kernel, feature_major=feature_major),
        grid=(BATCH, N_KV_HEADS),
        in_specs=[
            pl.BlockSpec((1, 1, SEQ, HEAD_DIM), lambda b, k: (b, k, 0, 0)),
            _const_spec((1, half)),
            _const_spec((1, half)),
            _const_spec((half, CMP_HIDDEN)),
            _const_spec((half, CMP_HIDDEN)),
            _const_spec(w2.shape),
        ],
        out_specs=pl.BlockSpec((1, 1) + out_tile, lambda b, k: (b, k, 0, 0)),
        out_shape=jax.ShapeDtypeStruct((BATCH, N_KV_HEADS) + out_tile, BF16),
        compiler_params=pltpu.CompilerParams(dimension_semantics=("parallel", "parallel")),
        name="compress",
    )(x_heads, pos_top, pos_bot, w1_top, w1_bot, w2)


RANK_STRIDE = 16
RANK_ACCUMULATORS = 4
SELECT_Q = 1024


def _selection_bias(score, n_blocks):
    n_q = score.shape[1]
    sub_iota = lax.broadcasted_iota(jnp.int32, (SUBLANES, n_q), 0)
    groups = [score[g * SUBLANES:(g + 1) * SUBLANES] for g in range(n_blocks // SUBLANES)]
    counts = [[None] * RANK_ACCUMULATORS for _ in groups]
    for jp in range(n_blocks):
        other = score[jp:jp + 1, :]
        for g, grp in enumerate(groups):
            lo = g * SUBLANES
            if lo > jp:
                before = other >= grp
            elif lo + SUBLANES - 1 <= jp:
                before = other > grp
            else:
                before = (other > grp) | ((other == grp) & (sub_iota + lo > jp))
            inc = jnp.where(before, 1, 0)
            a = jp % RANK_ACCUMULATORS
            counts[g][a] = inc if counts[g][a] is None else counts[g][a] + inc
    rank = jnp.concatenate([(c[0] + c[1]) + (c[2] + c[3]) for c in counts], axis=0)
    bias = jnp.where(rank < SEL_TOP, 0.0, MASK_VALUE)
    if n_blocks < N_SEL:
        bias = jnp.concatenate([bias, jnp.zeros((N_SEL - n_blocks, n_q), F32)], axis=0)
    return bias


def _select_kernel(qraw_ref, kcmp_ref, vcmp_ref, bias_ref, ocmp_ref, psum_sc):
    qi = pl.program_id(2)
    s0 = qi * SELECT_Q
    key_i = lax.broadcasted_iota(jnp.int32, (N_CHUNKS, SELECT_Q), 0)
    qry_i = lax.broadcasted_iota(jnp.int32, (N_CHUNKS, SELECT_Q), 1)
    heads = range(Q_PER_KV)

    k_cmp = kcmp_ref[0, 0]
    v_cmp_t = vcmp_ref[0, 0]
    s_cmp = [jnp.dot(k_cmp, qraw_ref[0, h], preferred_element_type=F32) for h in heads]
    cmp_valid = key_i * CMP_STRIDE + (CMP_LEN - 1) <= s0 + qry_i
    any_valid = s0 + lax.broadcasted_iota(jnp.int32, (1, SELECT_Q), 1) >= CMP_LEN - 1
    p_cmp = []
    for h in heads:
        s_c = jnp.where(cmp_valid, s_cmp[h], MASK_VALUE)
        e_c = jnp.exp2(s_c - jnp.max(s_c, axis=0, keepdims=True))
        p_cmp.append(e_c * jnp.where(any_valid, 1.0 / jnp.sum(e_c, axis=0, keepdims=True), 0.0))
    for h in heads:
        ocmp_ref[0, 0, h * HEAD_DIM:(h + 1) * HEAD_DIM, :] = jnp.dot(
            v_cmp_t, p_cmp[h].astype(BF16), preferred_element_type=F32)

    p_sum = (p_cmp[0] + p_cmp[1]) + (p_cmp[2] + p_cmp[3])
    lane_tiles = range(SELECT_Q // LANES)
    for t in lane_tiles:
        psum_sc[t] = p_sum[:, t * LANES:(t + 1) * LANES]
    per_sel = SEL_LEN // CMP_STRIDE
    terms = [jnp.concatenate([psum_sc[t, pl.ds(r, N_SEL, stride=per_sel), :] for t in lane_tiles], axis=1)
             for r in range(per_sel)]
    blk = lax.broadcasted_iota(jnp.int32, (N_SEL, SELECT_Q), 0)
    before = jnp.where(blk == 0, 0.0, pltpu.roll(terms[per_sel - 1], 1, 0))
    imp_t = (before + terms[0]) + (terms[1] + terms[2]) + terms[3]
    cur = lax.shift_right_logical(s0 + lax.broadcasted_iota(jnp.int32, (N_SEL, SELECT_Q), 1), SEL_SHIFT)
    forced = (blk == 0) | (blk == cur) | (blk == cur - 1)
    score = jnp.where(forced, FORCE_SCORE, jnp.where(blk > cur, -FORCE_SCORE, imp_t))

    steps_per_variant = RANK_STRIDE * SEL_LEN // SELECT_Q
    for variant in range(N_SEL // RANK_STRIDE):
        @pl.when(qi // steps_per_variant == variant)
        def _():
            bias_ref[0, 0] = _selection_bias(score, (variant + 1) * RANK_STRIDE).astype(BF16)


def _select(q_raw_t, k_cmp, v_cmp_t):
    assert CMP_LEN == 2 * CMP_STRIDE
    nq = SEQ // SELECT_Q
    step = lambda rows: pl.BlockSpec((1, 1, rows, SELECT_Q), lambda b, k, i: (b, k, 0, i))
    whole = lambda *tile: pl.BlockSpec((1, 1) + tile, lambda b, k, i: (b, k) + (0,) * len(tile))
    return pl.pallas_call(
        _select_kernel,
        grid=(BATCH, N_KV_HEADS, nq),
        in_specs=[
            pl.BlockSpec((1, Q_PER_KV, HEAD_DIM, SELECT_Q), lambda b, k, i: (b, k, 0, i)),
            whole(N_CHUNKS, HEAD_DIM), whole(HEAD_DIM, N_CHUNKS),
        ],
        out_specs=[step(N_SEL), step(Q_PER_KV * HEAD_DIM)],
        out_shape=[jax.ShapeDtypeStruct((BATCH, N_KV_HEADS, N_SEL, SEQ), BF16),
                   jax.ShapeDtypeStruct((BATCH, N_KV_HEADS, Q_PER_KV * HEAD_DIM, SEQ), F32)],
        scratch_shapes=[pltpu.VMEM((SELECT_Q // LANES, N_CHUNKS, LANES), F32)],
        compiler_params=pltpu.CompilerParams(
            dimension_semantics=("parallel", "parallel", "parallel"), vmem_limit_bytes=VMEM_LIMIT),
        name="select",
    )(q_raw_t, k_cmp, v_cmp_t)


KV_PER_STEP = 2


def _attend_kernel(qrot_ref, bias_ref, ocmp_ref, ks_ref, vs_ref, kw_ref, vw_ref, gate_ref,
                   o_ref, m_sc, acc_sc, sa_sc, sb_sc, owin_sc):
    qi = pl.program_id(2)
    s0 = qi * NSA_Q
    key_i = lax.broadcasted_iota(jnp.int32, (NSA_Q, NSA_Q), 0)
    qry_i = lax.broadcasted_iota(jnp.int32, (NSA_Q, NSA_Q), 1)
    causal = key_i <= qry_i
    kvs = range(KV_PER_STEP)
    heads = range(KV_PER_STEP * Q_PER_KV)

    def scores(k, q_t):
        return jnp.dot(k, q_t, preferred_element_type=F32)

    def softmax_step(h, s, v_t):
        m_prev = m_sc[h]
        m_new = jnp.maximum(m_prev, jnp.max(s, axis=0, keepdims=True))
        p = jnp.exp2(s - m_new).astype(BF16)
        acc_sc[h] = jnp.exp2(m_prev - m_new) * acc_sc[h] + jnp.dot(v_t, p, preferred_element_type=F32)
        m_sc[h] = m_new

    q_sel = [jnp.concatenate([qrot_ref[0, h], bias_ref[0, h // Q_PER_KV]], axis=0) for h in heads]
    m_sc[...] = jnp.full_like(m_sc, MASK_VALUE)
    acc_sc[...] = jnp.zeros_like(acc_sc)

    def sel_scores(kt, dst):
        rows = pl.ds(pl.multiple_of(kt * NSA_Q, NSA_Q), NSA_Q)
        k = [ks_ref[0, kv, rows, :] for kv in kvs]
        for h in heads:
            dst[h] = scores(k[h // Q_PER_KV], q_sel[h])

    def sel_softmax(kt, src, diagonal=False):
        v_t = [vs_ref[0, kv, kt] for kv in kvs]
        for h in heads:
            s = src[h]
            softmax_step(h, jnp.where(causal, s, MASK_VALUE) if diagonal else s, v_t[h // Q_PER_KV])

    def window_keys(kv, kt):
        return kw_ref[0, kv, pl.ds(pl.multiple_of(kt * NSA_Q, NSA_Q), NSA_Q), :]

    def penalty_rows(penalty):
        rows = jnp.where(lax.broadcasted_iota(jnp.int32, (HEAD_DIM, NSA_Q), 0) == 0, penalty, 0.0)
        return rows.astype(BF16)

    win_mid, win_far = jnp.maximum(qi - 1, 0), jnp.maximum(qi - 2, 0)
    k_diag = [window_keys(kv, qi)[:, 0:HEAD_DIM] for kv in kvs]
    k_mid = [window_keys(kv, win_mid) for kv in kvs]
    k_far = [window_keys(kv, win_far) for kv in kvs]
    pen_mid = penalty_rows(jnp.where(qi >= 1, 0.0, MASK_VALUE))
    pen_far = penalty_rows(jnp.where(qi >= 2, 0.0, MASK_VALUE))
    s_win_diag, s_win_mid, s_win_far = [], [], []
    for h in heads:
        q_t = qrot_ref[0, h]
        kv = h // Q_PER_KV
        s_win_diag.append(scores(k_diag[kv], q_t))
        s_win_mid.append(scores(k_mid[kv], jnp.concatenate([q_t, pen_mid], axis=0)))
        s_win_far.append(scores(k_far[kv], jnp.concatenate([q_t, pen_far], axis=0)))

    sel_scores(0, sa_sc)

    for h in heads:
        s_edge = jnp.where(causal, s_win_diag[h], s_win_far[h])
        s_mid = s_win_mid[h]
        m_w = jnp.max(jnp.maximum(s_edge, s_mid), axis=0, keepdims=True)
        p_edge = jnp.exp2(s_edge - m_w).astype(BF16)
        p_mid = jnp.exp2(s_mid - m_w).astype(BF16)
        zero = jnp.zeros_like(p_edge)
        kv = h // Q_PER_KV
        acc = (jnp.dot(vw_ref[0, kv, qi], jnp.where(causal, p_edge, zero), preferred_element_type=F32)
               + jnp.dot(vw_ref[0, kv, win_far], jnp.where(causal, zero, p_edge), preferred_element_type=F32)
               + jnp.dot(vw_ref[0, kv, win_mid], p_mid, preferred_element_type=F32))
        owin_sc[h] = acc[:HEAD_DIM] / acc[HEAD_DIM:HEAD_DIM + 1]

    def sel_pair(first):
        sel_scores(first + 1, sb_sc)
        sel_softmax(first, sa_sc)
        sel_scores(first + 2, sa_sc)
        sel_softmax(first + 1, sb_sc)

    def sel_quad(t, carry):
        sel_pair(4 * t)
        sel_pair(4 * t + 2)
        return carry

    lax.fori_loop(0, lax.shift_right_logical(qi, 2), sel_quad, 0)

    @pl.when((qi & 2) != 0)
    def _():
        sel_pair(qi & ~3)

    def finish():
        gates_t = [gate_ref[kv].T for kv in kvs]
        outs = []
        for h in heads:
            kv, hh = divmod(h, Q_PER_KV)
            g = gates_t[kv]
            c = hh * N_NSA_BRANCH
            acc = acc_sc[h]
            o_sel = acc[:HEAD_DIM] / acc[HEAD_DIM:HEAD_DIM + 1]
            o_cmp = ocmp_ref[0, kv, hh * HEAD_DIM:(hh + 1) * HEAD_DIM, :]
            outs.append(g[c:c + 1] * o_cmp + g[c + 1:c + 2] * o_sel + g[c + 2:c + 3] * owin_sc[h])
        o_ref[0] = jnp.concatenate(outs, axis=0).T.astype(BF16)

    @pl.when((qi & 1) == 1)
    def _():
        sel_scores(qi, sb_sc)
        sel_softmax(qi - 1, sa_sc)
        sel_softmax(qi, sb_sc, diagonal=True)
        finish()

    @pl.when((qi & 1) == 0)
    def _():
        sel_softmax(qi, sa_sc, diagonal=True)
        finish()


def _attend(q_rot_t, bias, o_cmp_t, k_sel, v_sel_t, k_win, v_win_t, gates):
    nq = SEQ // NSA_Q
    n_heads = KV_PER_STEP * Q_PER_KV
    step = lambda rows: pl.BlockSpec((1, KV_PER_STEP, rows, NSA_Q), lambda b, k, i: (b, k, 0, i))
    whole = lambda *tile: pl.BlockSpec((1, KV_PER_STEP) + tile, lambda b, k, i: (b, k) + (0,) * len(tile))
    return pl.pallas_call(
        _attend_kernel,
        grid=(BATCH, N_KV_HEADS // KV_PER_STEP, nq),
        in_specs=[
            pl.BlockSpec((1, n_heads, HEAD_DIM, NSA_Q), lambda b, k, i: (b, k, 0, i)),
            step(N_SEL), step(Q_PER_KV * HEAD_DIM),
            whole(SEQ, LANES), whole(nq, V_ROWS, NSA_Q),
            whole(SEQ, LANES), whole(nq, V_ROWS, NSA_Q),
            pl.BlockSpec((KV_PER_STEP, NSA_Q, LANES), lambda b, k, i: (k, b * nq + i, 0)),
        ],
        out_specs=pl.BlockSpec((1, NSA_Q, n_heads * HEAD_DIM), lambda b, k, i: (b, i, k)),
        out_shape=jax.ShapeDtypeStruct((BATCH, SEQ, N_HEADS * HEAD_DIM), BF16),
        scratch_shapes=[pltpu.VMEM((n_heads, 1, NSA_Q), F32),
                        pltpu.VMEM((n_heads, V_ROWS, NSA_Q), F32),
                        pltpu.VMEM((n_heads, NSA_Q, NSA_Q), F32),
                        pltpu.VMEM((n_heads, NSA_Q, NSA_Q), F32),
                        pltpu.VMEM((n_heads, HEAD_DIM, NSA_Q), F32)],
        compiler_params=pltpu.CompilerParams(
            dimension_semantics=("parallel", "parallel", "arbitrary"),
            vmem_limit_bytes=VMEM_LIMIT),
        name="attend",
    )(q_rot_t, bias, o_cmp_t, k_sel, v_sel_t, k_win, v_win_t, gates)


def _tail_kernel(h_ref, gu_ref, vn_ref, ob_ref, ga_ref, gb_ref, ws_ref, bs_ref,
                 wa_ref, wb_ref, wo_ref, gn_ref, wi_ref, wf_ref, p_ref, gp_ref, wg_ref, wp_ref,
                 gf_ref, o_ref, *, tm):
    r = lax.broadcasted_iota(jnp.int32, (GM_CHUNK, GM_CHUNK), 0)
    c = lax.broadcasted_iota(jnp.int32, (GM_CHUNK, GM_CHUNK), 1)
    w_tril = [jnp.where(c <= r, ws_ref[g], 0.0).astype(BF16) for g in range(GM_GROUPS)]
    bias = bs_ref[...]
    rows = []
    for ch in range(tm // GM_CHUNK):
        rs = slice(ch * GM_CHUNK, (ch + 1) * GM_CHUNK)
        mix = jnp.concatenate(
            [jnp.dot(w_tril[g], vn_ref[rs, g * LANES:(g + 1) * LANES], preferred_element_type=F32)
             for g in range(GM_GROUPS)], axis=1)
        rows.append(gu_ref[rs, :].astype(F32) * (mix + bias))
    z = jnp.concatenate(rows, axis=0).astype(BF16)
    y_a = jnp.dot(z, wa_ref[...], preferred_element_type=F32)
    y_b = jnp.dot(ob_ref[...], wb_ref[...], preferred_element_type=F32)
    merged = (ga_ref[...].astype(F32) * y_a + gb_ref[...].astype(F32) * y_b).astype(BF16)
    h = h_ref[...] + jnp.dot(merged, wo_ref[...], preferred_element_type=F32)

    h = _half_step_ffn(h, gn_ref[...], wi_ref, wf_ref)

    gate = jax.nn.sigmoid(jnp.dot(_rms(h, gp_ref[...]).astype(BF16), wg_ref[...],
                                  preferred_element_type=F32))
    proj = jnp.dot(p_ref[...].astype(BF16), wp_ref[...], preferred_element_type=F32)
    o_ref[...] = _rms(h + gate * proj, gf_ref[...])


def _tail(h, gu, vn, o_b, ga, gb, w_s, b_s_exp, w_a, w_b, w_o, ffn_norm, ffn_w_in, ffn_w_out,
          p, ple_norm, ple_w_gate, ple_w_proj, final_norm, *, tm=512):
    row = pl.BlockSpec((tm, D_MODEL), lambda i: (i, 0))
    square = _const_spec((D_MODEL, D_MODEL))
    gain = _const_spec((1, D_MODEL))
    return pl.pallas_call(
        functools.partial(_tail_kernel, tm=tm),
        grid=(TOKENS // tm,),
        in_specs=[row, row, row, row, row, row,
                  _const_spec((GM_GROUPS, GM_CHUNK, GM_CHUNK)),
                  _const_spec((GM_CHUNK, GM_WIDTH)),
                  square, square, square,
                  gain, _const_spec((D_MODEL, 2 * D_FF)), _const_spec((D_FF, D_MODEL)),
                  pl.BlockSpec((tm, PLE_DIM), lambda i: (i, 0)), gain, square,
                  _const_spec((PLE_DIM, D_MODEL)), gain],
        out_specs=row,
        out_shape=jax.ShapeDtypeStruct((TOKENS, D_MODEL), F32),
        compiler_params=pltpu.CompilerParams(
            dimension_semantics=("parallel",), vmem_limit_bytes=VMEM_LIMIT),
        name="tail",
    )(h, gu, vn, o_b, ga, gb, w_s, b_s_exp, w_a, w_b, w_o, ffn_norm, ffn_w_in, ffn_w_out,
      p, ple_norm, ple_w_gate, ple_w_proj, final_norm)


def _rope_tables():
    inv_freq = ROPE_THETA ** (-jnp.arange(0, ROPE_DIM, 2, dtype=jnp.float32) / ROPE_DIM)
    ang = jnp.arange(SEQ).astype(jnp.float32)[:, None] * inv_freq[None, :]
    cos, sin = jnp.cos(ang), jnp.sin(ang)
    zero = jnp.zeros_like(cos)
    rest = HEAD_DIM - ROPE_DIM
    c = jnp.concatenate([cos, cos, jnp.ones((SEQ, rest), F32)], axis=1)
    sa = jnp.concatenate([zero, sin, jnp.zeros((SEQ, rest), F32)], axis=1)
    sb = jnp.concatenate([-sin, zero, jnp.zeros((SEQ, rest), F32)], axis=1)
    return [jnp.tile(t, (1, LANES // HEAD_DIM)) for t in (c, sa, sb)]


def kernel(x, p, ffn1_norm, ffn1_w_in, ffn1_w_out, mix_norm, w_in, gm_ln_g, gm_ln_b, gm_w_s, gm_b_s,
           w_branch_a, cmp_pos_k, cmp_k_w1, cmp_k_w2, cmp_pos_v, cmp_v_w1, cmp_v_w2, w_branch_b, w_out,
           ffn2_norm, ffn2_w_in, ffn2_w_out, ple_norm, ple_w_gate, ple_w_proj, final_norm):
    assert x.shape == (BATCH, SEQ, D_MODEL) and p.shape == (1, BATCH, SEQ, PLE_DIM)
    row = lambda a: a.reshape(1, -1)
    h = x.reshape(TOKENS, D_MODEL)

    h = _ffn(h, row(ffn1_norm[0]), ffn1_w_in[0].astype(BF16), ffn1_w_out[0].astype(BF16))

    rope_c, rope_sa, rope_sb = _rope_tables()
    (gu, vn, q_raw, q_rot, k_c, v_c, k_sel, v_sel, k_win, v_win, gates, g_a, g_b) = _inproj(
        h, row(mix_norm[0]), w_in[0].T.astype(BF16), row(gm_ln_g[0]), row(gm_ln_b[0]),
        rope_c, rope_sa, rope_sb)

    half = CMP_STRIDE * HEAD_DIM

    def compress(x_heads, pos, w1, w2, feature_major):
        w2 = (w2.T if feature_major else w2).astype(BF16)
        return _compress(x_heads, pos[:CMP_STRIDE].reshape(1, half), pos[CMP_STRIDE:].reshape(1, half),
                         w1[:half].astype(BF16), w1[half:].astype(BF16), w2, feature_major=feature_major)

    k_cmp = compress(k_c, cmp_pos_k[0], cmp_k_w1[0], cmp_k_w2[0], False)
    v_cmp = compress(v_c, cmp_pos_v[0], cmp_v_w1[0], cmp_v_w2[0], True)

    sel_bias, o_cmp = _select(q_raw, k_cmp, v_cmp)
    o_b = _attend(q_rot, sel_bias, o_cmp, k_sel, v_sel, k_win, v_win, gates)

    b_s_exp = jnp.repeat(gm_b_s[0].T, GM_WIDTH // GM_GROUPS, axis=1)
    out = _tail(h, gu, vn, o_b.reshape(TOKENS, N_HEADS * HEAD_DIM), g_a, g_b, gm_w_s[0], b_s_exp,
                w_branch_a[0].astype(BF16), w_branch_b[0].astype(BF16), w_out[0].astype(BF16),
                row(ffn2_norm[0]), ffn2_w_in[0].astype(BF16), ffn2_w_out[0].astype(BF16),
                p[0].reshape(TOKENS, PLE_DIM), row(ple_norm[0]), ple_w_gate[0].astype(BF16),
                ple_w_proj[0].astype(BF16), row(final_norm))
    return out.reshape(BATCH, SEQ, D_MODEL)
```

```python
import functools

import jax
import jax.numpy as jnp
from jax import lax
from jax.experimental import pallas as pl
from jax.experimental.pallas import tpu as pltpu

D_MODEL = 1024
BATCH = 4
SEQ = 4096
PLE_DIM = 256
D_FF = 2816
NORM_EPS = 1e-6
GM_WIDTH = 1024
GM_GROUPS = 8
GM_CHUNK = 128
N_HEADS = 16
N_KV_HEADS = 4
HEAD_DIM = 64
Q_PER_KV = N_HEADS // N_KV_HEADS
KV_WIDTH = N_KV_HEADS * HEAD_DIM
ROPE_DIM = HEAD_DIM // 4
ROPE_HALF = ROPE_DIM // 2
ROPE_THETA = 500000.0
CMP_LEN = 32
CMP_STRIDE = 16
CMP_HIDDEN = 256
SEL_LEN = 64
SEL_TOP = 16
WINDOW = 512
N_NSA_BRANCH = 3
MASK_VALUE = -1e30
FORCE_SCORE = 1e9
LOG2_E = 1.4426950408889634

TOKENS = BATCH * SEQ
N_CHUNKS = SEQ // CMP_STRIDE
N_SEL = SEQ // SEL_LEN
LANES = 128
SUBLANES = 8
VMEM_LIMIT = 56 * 1024 * 1024

OFF_U = 0
OFF_V = OFF_U + GM_WIDTH
OFF_Q = OFF_V + GM_WIDTH
OFF_KV = OFF_Q + N_HEADS * HEAD_DIM
OFF_NSA_GATE = OFF_KV + 6 * KV_WIDTH
OFF_MERGE = OFF_NSA_GATE + N_HEADS * N_NSA_BRANCH
SEL_SHIFT = SEL_LEN.bit_length() - 1

F32 = jnp.float32
BF16 = jnp.bfloat16

NT_DIMS = (((1,), (1,)), ((), ()))


def _const_spec(shape):
    nd = len(shape)
    return pl.BlockSpec(shape, lambda *_: (0,) * nd, pipeline_mode=pl.Buffered(1))


def _rms(x, g):
    return x * lax.rsqrt(jnp.mean(x * x, axis=-1, keepdims=True) + NORM_EPS) * g


MXU_WIDTH = 256


def _half_step_ffn(x, g, wi_ref, wo_ref):
    xn = _rms(x, g).astype(BF16)
    acc = None
    for c in range(D_FF // MXU_WIDTH):
        lo = c * MXU_WIDTH
        gate = jnp.dot(xn, wi_ref[:, lo:lo + MXU_WIDTH], preferred_element_type=F32)
        up = jnp.dot(xn, wi_ref[:, D_FF + lo:D_FF + lo + MXU_WIDTH], preferred_element_type=F32)
        act = (gate * jax.nn.sigmoid(gate) * up).astype(BF16)
        part = jnp.dot(act, wo_ref[lo:lo + MXU_WIDTH, :], preferred_element_type=F32)
        acc = part if acc is None else acc + part
    return x + 0.5 * acc


def _ffn_kernel(x_ref, g_ref, wi_ref, wo_ref, o_ref):
    o_ref[...] = _half_step_ffn(x_ref[...], g_ref[...], wi_ref, wo_ref)


def _ffn(x, g, w_in, w_out, *, tm=512):
    row = pl.BlockSpec((tm, D_MODEL), lambda i: (i, 0))
    return pl.pallas_call(
        _ffn_kernel,
        grid=(TOKENS // tm,),
        in_specs=[row, _const_spec((1, D_MODEL)), _const_spec((D_MODEL, 2 * D_FF)),
                  _const_spec((D_FF, D_MODEL))],
        out_specs=row,
        out_shape=jax.ShapeDtypeStruct((TOKENS, D_MODEL), F32),
        compiler_params=pltpu.CompilerParams(
            dimension_semantics=("parallel",), vmem_limit_bytes=VMEM_LIMIT),
        name="ffn",
    )(x, g, w_in, w_out)


NSA_Q = 256
BF16_ROWS = 16
V_ROWS = HEAD_DIM + BF16_ROWS
INPROJ_KEY_TILES = 2


def _rope(x, c, sa, sb):
    w = x.shape[1]
    return x * c + pltpu.roll(x, ROPE_HALF, 1) * sa + pltpu.roll(x, w - ROPE_HALF, 1) * sb


def _inproj_kernel(h_ref, g_ref, wt_ref, lng_ref, lnb_ref, c_ref, sa_ref, sb_ref,
                   gu_ref, vn_ref, qraw_ref, qrot_ref, kc_ref, vc_ref, ks_ref, vs_ref,
                   kw_ref, vw_ref, gate_ref, ga_ref, gb_ref, *, tm):
    n = _rms(h_ref[...], g_ref[...]).astype(BF16)

    def seg(lo, width):
        return lax.dot_general(n, wt_ref[lo:lo + width, :], NT_DIMS, preferred_element_type=F32)

    gu_ref[...] = jax.nn.gelu(seg(OFF_U, GM_WIDTH)).astype(BF16)
    v = jax.nn.gelu(seg(OFF_V, GM_WIDTH))
    mu = jnp.mean(v, axis=-1, keepdims=True)
    vc = v - mu
    var = jnp.mean(vc * vc, axis=-1, keepdims=True)
    vn_ref[...] = (vc * lax.rsqrt(var + NORM_EPS) * lng_ref[...] + lnb_ref[...]).astype(BF16)

    c, sa, sb = c_ref[...], sa_ref[...], sb_ref[...]

    def tile_lanes(t, reps):
        return jnp.concatenate([t] * reps, axis=1)

    q = seg(OFF_Q, N_HEADS * HEAD_DIM) * (HEAD_DIM ** -0.5 * LOG2_E)
    reps = N_HEADS * HEAD_DIM // LANES
    q_rot = _rope(q, tile_lanes(c, reps), tile_lanes(sa, reps), tile_lanes(sb, reps))
    qraw_ref[0] = q.T.astype(BF16).reshape(N_HEADS, HEAD_DIM, tm)
    qrot_ref[0] = q_rot.T.astype(BF16).reshape(N_HEADS, HEAD_DIM, tm)

    kv = seg(OFF_KV, 6 * KV_WIDTH)
    k_c, v_c, k_s, v_s, k_w, v_w = [kv[:, i * KV_WIDTH:(i + 1) * KV_WIDTH] for i in range(6)]
    reps = KV_WIDTH // LANES
    ck, sak, sbk = tile_lanes(c, reps), tile_lanes(sa, reps), tile_lanes(sb, reps)
    k_s = _rope(k_s, ck, sak, sbk)
    k_w = _rope(k_w, ck, sak, sbk)
    pos = (pl.program_id(0) % (SEQ // tm)) * tm + lax.broadcasted_iota(jnp.int32, (tm, N_SEL), 0)
    blk = lax.broadcasted_iota(jnp.int32, (tm, N_SEL), 1)
    onehot = jnp.where(lax.shift_right_logical(pos, SEL_SHIFT) == blk, 1.0, 0.0).astype(F32)
    ones_t = jnp.ones((V_ROWS - HEAD_DIM, NSA_Q), F32)
    ones_col = jnp.where(lax.broadcasted_iota(jnp.int32, (tm, HEAD_DIM), 1) == 0, 1.0, 0.0)
    v_s_t, v_w_t = v_s.T, v_w.T
    for h in range(N_KV_HEADS):
        sl = slice(h * HEAD_DIM, (h + 1) * HEAD_DIM)
        kc_ref[0, h] = k_c[:, sl]
        vc_ref[0, h] = v_c[:, sl]
        ks_ref[0, h] = jnp.concatenate([k_s[:, sl], onehot], axis=1).astype(BF16)
        kw_ref[0, h] = jnp.concatenate([k_w[:, sl], ones_col], axis=1).astype(BF16)
        for t in range(tm // NSA_Q):
            keys = slice(t * NSA_Q, (t + 1) * NSA_Q)
            vs_ref[0, h, t] = jnp.concatenate([v_s_t[sl, keys], ones_t], axis=0).astype(BF16)
            vw_ref[0, h, t] = jnp.concatenate([v_w_t[sl, keys], ones_t], axis=0).astype(BF16)

    merge = jax.nn.sigmoid(seg(OFF_MERGE, 2 * D_MODEL))
    ga_ref[...] = merge[:, :D_MODEL].astype(BF16)
    gb_ref[...] = merge[:, D_MODEL:].astype(BF16)
    gates = jax.nn.sigmoid(seg(OFF_NSA_GATE, LANES))
    group_cols = Q_PER_KV * N_NSA_BRANCH
    for h in range(N_KV_HEADS):
        gate_ref[h] = gates if h == 0 else pltpu.roll(gates, LANES - h * group_cols, 1)


def _inproj(h, g, w_in_t, ln_g, ln_b, rope_c, rope_sa, rope_sb):
    tm = INPROJ_KEY_TILES * NSA_Q
    nq = SEQ // tm
    row = lambda i: (i, 0)
    head = lambda i: (i // nq, 0, i % nq, 0)
    tok_bf = jax.ShapeDtypeStruct((TOKENS, D_MODEL), BF16)

    def head_shape(nh, width, dtype):
        return jax.ShapeDtypeStruct((BATCH, nh, SEQ, width), dtype)

    def head_spec(nh, width):
        return pl.BlockSpec((1, nh, tm, width), head)

    qt_spec = pl.BlockSpec((1, N_HEADS, HEAD_DIM, tm), lambda i: (i // nq, 0, 0, i % nq))
    qt_shape = jax.ShapeDtypeStruct((BATCH, N_HEADS, HEAD_DIM, SEQ), BF16)
    vt_spec = pl.BlockSpec((1, N_KV_HEADS, INPROJ_KEY_TILES, V_ROWS, NSA_Q),
                           lambda i: (i // nq, 0, i % nq, 0, 0))
    vt_shape = jax.ShapeDtypeStruct((BATCH, N_KV_HEADS, SEQ // NSA_Q, V_ROWS, NSA_Q), BF16)

    return pl.pallas_call(
        functools.partial(_inproj_kernel, tm=tm),
        grid=(TOKENS // tm,),
        in_specs=[
            pl.BlockSpec((tm, D_MODEL), row),
            _const_spec((1, D_MODEL)),
            _const_spec(w_in_t.shape),
            _const_spec((1, GM_WIDTH)),
            _const_spec((1, GM_WIDTH)),
            pl.BlockSpec((tm, LANES), lambda i: (i % nq, 0)),
            pl.BlockSpec((tm, LANES), lambda i: (i % nq, 0)),
            pl.BlockSpec((tm, LANES), lambda i: (i % nq, 0)),
        ],
        out_specs=[
            pl.BlockSpec((tm, GM_WIDTH), row),
            pl.BlockSpec((tm, GM_WIDTH), row),
            qt_spec,
            qt_spec,
            head_spec(N_KV_HEADS, HEAD_DIM),
            head_spec(N_KV_HEADS, HEAD_DIM),
            head_spec(N_KV_HEADS, LANES),
            vt_spec,
            head_spec(N_KV_HEADS, LANES),
            vt_spec,
            pl.BlockSpec((N_KV_HEADS, tm, LANES), lambda i: (0, i, 0)),
            pl.BlockSpec((tm, D_MODEL), row),
            pl.BlockSpec((tm, D_MODEL), row),
        ],
        out_shape=[
            tok_bf, tok_bf,
            qt_shape, qt_shape,
            head_shape(N_KV_HEADS, HEAD_DIM, F32), head_shape(N_KV_HEADS, HEAD_DIM, F32),
            head_shape(N_KV_HEADS, LANES, BF16), vt_shape,
            head_shape(N_KV_HEADS, LANES, BF16), vt_shape,
            jax.ShapeDtypeStruct((N_KV_HEADS, TOKENS, LANES), F32),
            tok_bf, tok_bf,
        ],
        compiler_params=pltpu.CompilerParams(
            dimension_semantics=("parallel",), vmem_limit_bytes=VMEM_LIMIT),
        name="inproj",
    )(h, g, w_in_t, ln_g, ln_b, rope_c, rope_sa, rope_sb)


def _compress_kernel(x_ref, ptop_ref, pbot_ref, w1t_ref, w1b_ref, w2_ref, o_ref, *, feature_major):
    x = jnp.concatenate([x_ref[0, 0, pl.ds(j, N_CHUNKS, stride=CMP_STRIDE), :]
                         for j in range(CMP_STRIDE)], axis=1)
    top = jnp.dot((x + ptop_ref[...]).astype(BF16), w1t_ref[...], preferred_element_type=F32)
    bot = jnp.dot((x + pbot_ref[...]).astype(BF16), w1b_ref[...], preferred_element_type=F32)
    hidden = jax.nn.gelu(top + pltpu.roll(bot, N_CHUNKS - 1, 0)).astype(BF16)
    if feature_major:
        out = lax.dot_general(w2_ref[...], hidden, NT_DIMS, preferred_element_type=F32)
    else:
        out = jnp.dot(hidden, w2_ref[...], preferred_element_type=F32)
    o_ref[0, 0] = out.astype(BF16)


def _compress(x_heads, pos_top, pos_bot, w1_top, w1_bot, w2, *, feature_major):
    half = CMP_STRIDE * HEAD_DIM
    out_tile = (HEAD_DIM, N_CHUNKS) if feature_major else (N_CHUNKS, HEAD_DIM)
    return pl.pallas_call(
        functools.partial(_compress_kernel, feature_major=feature_major),
        grid=(BATCH, N_KV_HEADS),
        in_specs=[
            pl.BlockSpec((1, 1, SEQ, HEAD_DIM), lambda b, k: (b, k, 0, 0)),
            _const_spec((1, half)),
            _const_spec((1, half)),
            _const_spec((half, CMP_HIDDEN)),
            _const_spec((half, CMP_HIDDEN)),
            _const_spec(w2.shape),
        ],
        out_specs=pl.BlockSpec((1, 1) + out_tile, lambda b, k: (b, k, 0, 0)),
        out_shape=jax.ShapeDtypeStruct((BATCH, N_KV_HEADS) + out_tile, BF16),
        compiler_params=pltpu.CompilerParams(dimension_semantics=("parallel", "parallel")),
        name="compress",
    )(x_heads, pos_top, pos_bot, w1_top, w1_bot, w2)


RANK_ACCUMULATORS = 4
SELECT_Q = 1024


def _selection_bias(score, n_blocks):
    n_q = score.shape[1]
    sub_iota = lax.broadcasted_iota(jnp.int32, (SUBLANES, n_q), 0)
    groups = [score[g * SUBLANES:(g + 1) * SUBLANES] for g in range(n_blocks // SUBLANES)]
    counts = [[None] * RANK_ACCUMULATORS for _ in groups]
    for jp in range(n_blocks):
        other = score[jp:jp + 1, :]
        for g, grp in enumerate(groups):
            lo = g * SUBLANES
            if lo > jp:
                before = other >= grp
            elif lo + SUBLANES - 1 <= jp:
                before = other > grp
            else:
                before = (other > grp) | ((other == grp) & (sub_iota + lo > jp))
            inc = jnp.where(before, 1, 0)
            a = jp % RANK_ACCUMULATORS
            counts[g][a] = inc if counts[g][a] is None else counts[g][a] + inc
    rank = jnp.concatenate([(c[0] + c[1]) + (c[2] + c[3]) for c in counts], axis=0)
    bias = jnp.where(rank < SEL_TOP, 0.0, MASK_VALUE)
    if n_blocks < N_SEL:
        bias = jnp.concatenate([bias, jnp.zeros((N_SEL - n_blocks, n_q), F32)], axis=0)
    return bias


def _select_kernel(qraw_ref, kcmp_ref, vcmp_ref, bias_ref, ocmp_ref, psum_sc):
    qi = pl.program_id(2)
    s0 = qi * SELECT_Q
    key_i = lax.broadcasted_iota(jnp.int32, (N_CHUNKS, SELECT_Q), 0)
    qry_i = lax.broadcasted_iota(jnp.int32, (N_CHUNKS, SELECT_Q), 1)
    heads = range(Q_PER_KV)

    k_cmp = kcmp_ref[0, 0]
    v_cmp_t = vcmp_ref[0, 0]
    s_cmp = [jnp.dot(k_cmp, qraw_ref[0, h], preferred_element_type=F32) for h in heads]
    cmp_valid = key_i * CMP_STRIDE + (CMP_LEN - 1) <= s0 + qry_i
    any_valid = s0 + lax.broadcasted_iota(jnp.int32, (1, SELECT_Q), 1) >= CMP_LEN - 1
    p_cmp = []
    for h in heads:
        s_c = jnp.where(cmp_valid, s_cmp[h], MASK_VALUE)
        e_c = jnp.exp2(s_c - jnp.max(s_c, axis=0, keepdims=True))
        p_cmp.append(e_c * jnp.where(any_valid, 1.0 / jnp.sum(e_c, axis=0, keepdims=True), 0.0))
    for h in heads:
        ocmp_ref[0, 0, h * HEAD_DIM:(h + 1) * HEAD_DIM, :] = jnp.dot(
            v_cmp_t, p_cmp[h].astype(BF16), preferred_element_type=F32)

    p_sum = (p_cmp[0] + p_cmp[1]) + (p_cmp[2] + p_cmp[3])
    lane_tiles = range(SELECT_Q // LANES)
    for t in lane_tiles:
        psum_sc[t] = p_sum[:, t * LANES:(t + 1) * LANES]
    per_sel = SEL_LEN // CMP_STRIDE
    terms = [jnp.concatenate([psum_sc[t, pl.ds(r, N_SEL, stride=per_sel), :] for t in lane_tiles], axis=1)
             for r in range(per_sel)]
    blk = lax.broadcasted_iota(jnp.int32, (N_SEL, SELECT_Q), 0)
    before = jnp.where(blk == 0, 0.0, pltpu.roll(terms[per_sel - 1], 1, 0))
    imp_t = (before + terms[0]) + (terms[1] + terms[2]) + terms[3]
    cur = lax.shift_right_logical(s0 + lax.broadcasted_iota(jnp.int32, (N_SEL, SELECT_Q), 1), SEL_SHIFT)
    forced = (blk == 0) | (blk == cur) | (blk == cur - 1)
    score = jnp.where(forced, FORCE_SCORE, jnp.where(blk > cur, -FORCE_SCORE, imp_t))

    blocks_per_tile = NSA_Q // SEL_LEN
    for variant in range(SEQ // SELECT_Q):
        @pl.when(qi == variant)
        def _():
            for t in range(SELECT_Q // NSA_Q):
                lanes = slice(t * NSA_Q, (t + 1) * NSA_Q)
                reachable = (variant * (SELECT_Q // NSA_Q) + t + 1) * blocks_per_tile
                n_blocks = min(N_SEL, -(-reachable // SUBLANES) * SUBLANES)
                bias_ref[0, 0, :, lanes] = _selection_bias(score[:, lanes], n_blocks).astype(BF16)


def _select(q_raw_t, k_cmp, v_cmp_t):
    assert CMP_LEN == 2 * CMP_STRIDE
    nq = SEQ // SELECT_Q
    step = lambda rows: pl.BlockSpec((1, 1, rows, SELECT_Q), lambda b, k, i: (b, k, 0, i))
    whole = lambda *tile: pl.BlockSpec((1, 1) + tile, lambda b, k, i: (b, k) + (0,) * len(tile))
    return pl.pallas_call(
        _select_kernel,
        grid=(BATCH, N_KV_HEADS, nq),
        in_specs=[
            pl.BlockSpec((1, Q_PER_KV, HEAD_DIM, SELECT_Q), lambda b, k, i: (b, k, 0, i)),
            whole(N_CHUNKS, HEAD_DIM), whole(HEAD_DIM, N_CHUNKS),
        ],
        out_specs=[step(N_SEL), step(Q_PER_KV * HEAD_DIM)],
        out_shape=[jax.ShapeDtypeStruct((BATCH, N_KV_HEADS, N_SEL, SEQ), BF16),
                   jax.ShapeDtypeStruct((BATCH, N_KV_HEADS, Q_PER_KV * HEAD_DIM, SEQ), F32)],
        scratch_shapes=[pltpu.VMEM((SELECT_Q // LANES, N_CHUNKS, LANES), F32)],
        compiler_params=pltpu.CompilerParams(
            dimension_semantics=("parallel", "parallel", "parallel"), vmem_limit_bytes=VMEM_LIMIT),
        name="select",
    )(q_raw_t, k_cmp, v_cmp_t)


KV_PER_STEP = 2


def _attend_kernel(qrot_ref, bias_ref, ocmp_ref, ks_ref, vs_ref, kw_ref, vw_ref, gate_ref,
                   o_ref, m_sc, acc_sc, sa_sc, sb_sc, owin_sc):
    qi = pl.program_id(2)
    s0 = qi * NSA_Q
    key_i = lax.broadcasted_iota(jnp.int32, (NSA_Q, NSA_Q), 0)
    qry_i = lax.broadcasted_iota(jnp.int32, (NSA_Q, NSA_Q), 1)
    causal = key_i <= qry_i
    kvs = range(KV_PER_STEP)
    heads = range(KV_PER_STEP * Q_PER_KV)

    def scores(k, q_t):
        return jnp.dot(k, q_t, preferred_element_type=F32)

    def softmax_step(h, s, v_t):
        m_prev = m_sc[h]
        m_new = jnp.maximum(m_prev, jnp.max(s, axis=0, keepdims=True))
        p = jnp.exp2(s - m_new).astype(BF16)
        acc_sc[h] = jnp.exp2(m_prev - m_new) * acc_sc[h] + jnp.dot(v_t, p, preferred_element_type=F32)
        m_sc[h] = m_new

    q_sel = [jnp.concatenate([qrot_ref[0, h], bias_ref[0, h // Q_PER_KV]], axis=0) for h in heads]
    m_sc[...] = jnp.full_like(m_sc, MASK_VALUE)
    acc_sc[...] = jnp.zeros_like(acc_sc)

    def sel_scores(kt, dst):
        rows = pl.ds(pl.multiple_of(kt * NSA_Q, NSA_Q), NSA_Q)
        k = [ks_ref[0, kv, rows, :] for kv in kvs]
        for h in heads:
            dst[h] = scores(k[h // Q_PER_KV], q_sel[h])

    def sel_softmax(kt, src, diagonal=False):
        v_t = [vs_ref[0, kv, kt] for kv in kvs]
        for h in heads:
            s = src[h]
            softmax_step(h, jnp.where(causal, s, MASK_VALUE) if diagonal else s, v_t[h // Q_PER_KV])

    def window_keys(kv, kt):
        return kw_ref[0, kv, pl.ds(pl.multiple_of(kt * NSA_Q, NSA_Q), NSA_Q), :]

    def penalty_rows(penalty):
        rows = jnp.where(lax.broadcasted_iota(jnp.int32, (HEAD_DIM, NSA_Q), 0) == 0, penalty, 0.0)
        return rows.astype(BF16)

    win_mid, win_far = jnp.maximum(qi - 1, 0), jnp.maximum(qi - 2, 0)
    k_diag = [window_keys(kv, qi)[:, 0:HEAD_DIM] for kv in kvs]
    k_mid = [window_keys(kv, win_mid) for kv in kvs]
    k_far = [window_keys(kv, win_far) for kv in kvs]
    pen_mid = penalty_rows(jnp.where(qi >= 1, 0.0, MASK_VALUE))
    pen_far = penalty_rows(jnp.where(qi >= 2, 0.0, MASK_VALUE))
    s_win_diag, s_win_mid, s_win_far = [], [], []
    for h in heads:
        q_t = qrot_ref[0, h]
        kv = h // Q_PER_KV
        s_win_diag.append(scores(k_diag[kv], q_t))
        s_win_mid.append(scores(k_mid[kv], jnp.concatenate([q_t, pen_mid], axis=0)))
        s_win_far.append(scores(k_far[kv], jnp.concatenate([q_t, pen_far], axis=0)))

    sel_scores(0, sa_sc)

    for h in heads:
        s_edge = jnp.where(causal, s_win_diag[h], s_win_far[h])
        s_mid = s_win_mid[h]
        m_w = jnp.max(jnp.maximum(s_edge, s_mid), axis=0, keepdims=True)
        p_edge = jnp.exp2(s_edge - m_w).astype(BF16)
        p_mid = jnp.exp2(s_mid - m_w).astype(BF16)
        zero = jnp.zeros_like(p_edge)
        kv = h // Q_PER_KV
        acc = (jnp.dot(vw_ref[0, kv, qi], jnp.where(causal, p_edge, zero), preferred_element_type=F32)
               + jnp.dot(vw_ref[0, kv, win_far], jnp.where(causal, zero, p_edge), preferred_element_type=F32)
               + jnp.dot(vw_ref[0, kv, win_mid], p_mid, preferred_element_type=F32))
        owin_sc[h] = acc[:HEAD_DIM] / acc[HEAD_DIM:HEAD_DIM + 1]

    def sel_pair(first):
        sel_scores(first + 1, sb_sc)
        sel_softmax(first, sa_sc)
        sel_scores(first + 2, sa_sc)
        sel_softmax(first + 1, sb_sc)

    def sel_quad(t, carry):
        sel_pair(4 * t)
        sel_pair(4 * t + 2)
        return carry

    lax.fori_loop(0, lax.shift_right_logical(qi, 2), sel_quad, 0)

    @pl.when((qi & 2) != 0)
    def _():
        sel_pair(qi & ~3)

    def finish():
        gates_t = [gate_ref[kv].T for kv in kvs]
        outs = []
        for h in heads:
            kv, hh = divmod(h, Q_PER_KV)
            g = gates_t[kv]
            c = hh * N_NSA_BRANCH
            acc = acc_sc[h]
            o_sel = acc[:HEAD_DIM] / acc[HEAD_DIM:HEAD_DIM + 1]
            o_cmp = ocmp_ref[0, kv, hh * HEAD_DIM:(hh + 1) * HEAD_DIM, :]
            outs.append(g[c:c + 1] * o_cmp + g[c + 1:c + 2] * o_sel + g[c + 2:c + 3] * owin_sc[h])
        o_ref[0] = jnp.concatenate(outs, axis=0).T.astype(BF16)

    @pl.when((qi & 1) == 1)
    def _():
        sel_scores(qi, sb_sc)
        sel_softmax(qi - 1, sa_sc)
        sel_softmax(qi, sb_sc, diagonal=True)
        finish()

    @pl.when((qi & 1) == 0)
    def _():
        sel_softmax(qi, sa_sc, diagonal=True)
        finish()


def _attend(q_rot_t, bias, o_cmp_t, k_sel, v_sel_t, k_win, v_win_t, gates):
    nq = SEQ // NSA_Q
    n_heads = KV_PER_STEP * Q_PER_KV
    step = lambda rows: pl.BlockSpec((1, KV_PER_STEP, rows, NSA_Q), lambda b, k, i: (b, k, 0, i))
    whole = lambda *tile: pl.BlockSpec((1, KV_PER_STEP) + tile, lambda b, k, i: (b, k) + (0,) * len(tile))
    return pl.pallas_call(
        _attend_kernel,
        grid=(BATCH, N_KV_HEADS // KV_PER_STEP, nq),
        in_specs=[
            pl.BlockSpec((1, n_heads, HEAD_DIM, NSA_Q), lambda b, k, i: (b, k, 0, i)),
            step(N_SEL), step(Q_PER_KV * HEAD_DIM),
            whole(SEQ, LANES), whole(nq, V_ROWS, NSA_Q),
            whole(SEQ, LANES), whole(nq, V_ROWS, NSA_Q),
            pl.BlockSpec((KV_PER_STEP, NSA_Q, LANES), lambda b, k, i: (k, b * nq + i, 0)),
        ],
        out_specs=pl.BlockSpec((1, NSA_Q, n_heads * HEAD_DIM), lambda b, k, i: (b, i, k)),
        out_shape=jax.ShapeDtypeStruct((BATCH, SEQ, N_HEADS * HEAD_DIM), BF16),
        scratch_shapes=[pltpu.VMEM((n_heads, 1, NSA_Q), F32),
                        pltpu.VMEM((n_heads, V_ROWS, NSA_Q), F32),
                        pltpu.VMEM((n_heads, NSA_Q, NSA_Q), F32),
                        pltpu.VMEM((n_heads, NSA_Q, NSA_Q), F32),
                        pltpu.VMEM((n_heads, HEAD_DIM, NSA_Q), F32)],
        compiler_params=pltpu.CompilerParams(
            dimension_semantics=("parallel", "parallel", "arbitrary"),
            vmem_limit_bytes=VMEM_LIMIT),
        name="attend",
    )(q_rot_t, bias, o_cmp_t, k_sel, v_sel_t, k_win, v_win_t, gates)


def _tail_kernel(h_ref, gu_ref, vn_ref, ob_ref, ga_ref, gb_ref, ws_ref, bs_ref,
                 wa_ref, wb_ref, wo_ref, gn_ref, wi_ref, wf_ref, p_ref, gp_ref, wg_ref, wp_ref,
                 gf_ref, o_ref, *, tm):
    r = lax.broadcasted_iota(jnp.int32, (GM_CHUNK, GM_CHUNK), 0)
    c = lax.broadcasted_iota(jnp.int32, (GM_CHUNK, GM_CHUNK), 1)
    w_tril = [jnp.where(c <= r, ws_ref[g], 0.0).astype(BF16) for g in range(GM_GROUPS)]
    bias = bs_ref[...]
    rows = []
    for ch in range(tm // GM_CHUNK):
        rs = slice(ch * GM_CHUNK, (ch + 1) * GM_CHUNK)
        mix = jnp.concatenate(
            [jnp.dot(w_tril[g], vn_ref[rs, g * LANES:(g + 1) * LANES], preferred_element_type=F32)
             for g in range(GM_GROUPS)], axis=1)
        rows.append(gu_ref[rs, :].astype(F32) * (mix + bias))
    z = jnp.concatenate(rows, axis=0).astype(BF16)
    y_a = jnp.dot(z, wa_ref[...], preferred_element_type=F32)
    y_b = jnp.dot(ob_ref[...], wb_ref[...], preferred_element_type=F32)
    merged = (ga_ref[...].astype(F32) * y_a + gb_ref[...].astype(F32) * y_b).astype(BF16)
    h = h_ref[...] + jnp.dot(merged, wo_ref[...], preferred_element_type=F32)

    h = _half_step_ffn(h, gn_ref[...], wi_ref, wf_ref)

    gate = jax.nn.sigmoid(jnp.dot(_rms(h, gp_ref[...]).astype(BF16), wg_ref[...],
                                  preferred_element_type=F32))
    proj = jnp.dot(p_ref[...].astype(BF16), wp_ref[...], preferred_element_type=F32)
    o_ref[...] = _rms(h + gate * proj, gf_ref[...])


def _tail(h, gu, vn, o_b, ga, gb, w_s, b_s_exp, w_a, w_b, w_o, ffn_norm, ffn_w_in, ffn_w_out,
          p, ple_norm, ple_w_gate, ple_w_proj, final_norm, *, tm=512):
    row = pl.BlockSpec((tm, D_MODEL), lambda i: (i, 0))
    square = _const_spec((D_MODEL, D_MODEL))
    gain = _const_spec((1, D_MODEL))
    return pl.pallas_call(
        functools.partial(_tail_kernel, tm=tm),
        grid=(TOKENS // tm,),
        in_specs=[row, row, row, row, row, row,
                  _const_spec((GM_GROUPS, GM_CHUNK, GM_CHUNK)),
                  _const_spec((GM_CHUNK, GM_WIDTH)),
                  square, square, square,
                  gain, _const_spec((D_MODEL, 2 * D_FF)), _const_spec((D_FF, D_MODEL)),
                  pl.BlockSpec((tm, PLE_DIM), lambda i: (i, 0)), gain, square,
                  _const_spec((PLE_DIM, D_MODEL)), gain],
        out_specs=row,
        out_shape=jax.ShapeDtypeStruct((TOKENS, D_MODEL), F32),
        compiler_params=pltpu.CompilerParams(
            dimension_semantics=("parallel",), vmem_limit_bytes=VMEM_LIMIT),
        name="tail",
    )(h, gu, vn, o_b, ga, gb, w_s, b_s_exp, w_a, w_b, w_o, ffn_norm, ffn_w_in, ffn_w_out,
      p, ple_norm, ple_w_gate, ple_w_proj, final_norm)


def _rope_tables():
    inv_freq = ROPE_THETA ** (-jnp.arange(0, ROPE_DIM, 2, dtype=jnp.float32) / ROPE_DIM)
    ang = jnp.arange(SEQ).astype(jnp.float32)[:, None] * inv_freq[None, :]
    cos, sin = jnp.cos(ang), jnp.sin(ang)
    zero = jnp.zeros_like(cos)
    rest = HEAD_DIM - ROPE_DIM
    c = jnp.concatenate([cos, cos, jnp.ones((SEQ, rest), F32)], axis=1)
    sa = jnp.concatenate([zero, sin, jnp.zeros((SEQ, rest), F32)], axis=1)
    sb = jnp.concatenate([-sin, zero, jnp.zeros((SEQ, rest), F32)], axis=1)
    return [jnp.tile(t, (1, LANES // HEAD_DIM)) for t in (c, sa, sb)]


def kernel(x, p, ffn1_norm, ffn1_w_in, ffn1_w_out, mix_norm, w_in, gm_ln_g, gm_ln_b, gm_w_s, gm_b_s,
           w_branch_a, cmp_pos_k, cmp_k_w1, cmp_k_w2, cmp_pos_v, cmp_v_w1, cmp_v_w2, w_branch_b, w_out,
           ffn2_norm, ffn2_w_in, ffn2_w_out, ple_norm, ple_w_gate, ple_w_proj, final_norm):
    assert x.shape == (BATCH, SEQ, D_MODEL) and p.shape == (1, BATCH, SEQ, PLE_DIM)
    row = lambda a: a.reshape(1, -1)
    h = x.reshape(TOKENS, D_MODEL)

    h = _ffn(h, row(ffn1_norm[0]), ffn1_w_in[0].astype(BF16), ffn1_w_out[0].astype(BF16))

    rope_c, rope_sa, rope_sb = _rope_tables()
    (gu, vn, q_raw, q_rot, k_c, v_c, k_sel, v_sel, k_win, v_win, gates, g_a, g_b) = _inproj(
        h, row(mix_norm[0]), w_in[0].T.astype(BF16), row(gm_ln_g[0]), row(gm_ln_b[0]),
        rope_c, rope_sa, rope_sb)

    half = CMP_STRIDE * HEAD_DIM

    def compress(x_heads, pos, w1, w2, feature_major):
        w2 = (w2.T if feature_major else w2).astype(BF16)
        return _compress(x_heads, pos[:CMP_STRIDE].reshape(1, half), pos[CMP_STRIDE:].reshape(1, half),
                         w1[:half].astype(BF16), w1[half:].astype(BF16), w2, feature_major=feature_major)

    k_cmp = compress(k_c, cmp_pos_k[0], cmp_k_w1[0], cmp_k_w2[0], False)
    v_cmp = compress(v_c, cmp_pos_v[0], cmp_v_w1[0], cmp_v_w2[0], True)

    sel_bias, o_cmp = _select(q_raw, k_cmp, v_cmp)
    o_b = _attend(q_rot, sel_bias, o_cmp, k_sel, v_sel, k_win, v_win, gates)

    b_s_exp = jnp.repeat(gm_b_s[0].T, GM_WIDTH // GM_GROUPS, axis=1)
    out = _tail(h, gu, vn, o_b.reshape(TOKENS, N_HEADS * HEAD_DIM), g_a, g_b, gm_w_s[0], b_s_exp,
                w_branch_a[0].astype(BF16), w_branch_b[0].astype(BF16), w_out[0].astype(BF16),
                row(ffn2_norm[0]), ffn2_w_in[0].astype(BF16), ffn2_w_out[0].astype(BF16),
                p[0].reshape(TOKENS, PLE_DIM), row(ple_norm[0]), ple_w_gate[0].astype(BF16),
                ple_w_proj[0].astype(BF16), row(final_norm))
    return out.reshape(BATCH, SEQ, D_MODEL)
```

```python
import functools

import jax
import jax.numpy as jnp
from jax import lax
from jax.experimental import pallas as pl
from jax.experimental.pallas import tpu as pltpu

D_MODEL = 1024
BATCH = 4
SEQ = 4096
PLE_DIM = 256
D_FF = 2816
NORM_EPS = 1e-6
GM_WIDTH = 1024
GM_GROUPS = 8
GM_CHUNK = 128
N_HEADS = 16
N_KV_HEADS = 4
HEAD_DIM = 64
Q_PER_KV = N_HEADS // N_KV_HEADS
KV_WIDTH = N_KV_HEADS * HEAD_DIM
ROPE_DIM = HEAD_DIM // 4
ROPE_HALF = ROPE_DIM // 2
ROPE_THETA = 500000.0
CMP_LEN = 32
CMP_STRIDE = 16
CMP_HIDDEN = 256
SEL_LEN = 64
SEL_TOP = 16
WINDOW = 512
N_NSA_BRANCH = 3
MASK_VALUE = -1e30
FORCE_SCORE = 1e9
LOG2_E = 1.4426950408889634

TOKENS = BATCH * SEQ
N_CHUNKS = SEQ // CMP_STRIDE
N_SEL = SEQ // SEL_LEN
LANES = 128
SUBLANES = 8
VMEM_LIMIT = 56 * 1024 * 1024

OFF_U = 0
OFF_V = OFF_U + GM_WIDTH
OFF_Q = OFF_V + GM_WIDTH
OFF_KV = OFF_Q + N_HEADS * HEAD_DIM
OFF_NSA_GATE = OFF_KV + 6 * KV_WIDTH
OFF_MERGE = OFF_NSA_GATE + N_HEADS * N_NSA_BRANCH
SEL_SHIFT = SEL_LEN.bit_length() - 1

F32 = jnp.float32
BF16 = jnp.bfloat16

NT_DIMS = (((1,), (1,)), ((), ()))


def _const_spec(shape):
    nd = len(shape)
    return pl.BlockSpec(shape, lambda *_: (0,) * nd, pipeline_mode=pl.Buffered(1))


def _rms(x, g):
    return x * lax.rsqrt(jnp.mean(x * x, axis=-1, keepdims=True) + NORM_EPS) * g


MXU_WIDTH = 256


def _half_step_ffn(x, g, wi_ref, wo_ref):
    xn = _rms(x, g).astype(BF16)
    acc = None
    for c in range(D_FF // MXU_WIDTH):
        lo = c * MXU_WIDTH
        gate = jnp.dot(xn, wi_ref[:, lo:lo + MXU_WIDTH], preferred_element_type=F32)
        up = jnp.dot(xn, wi_ref[:, D_FF + lo:D_FF + lo + MXU_WIDTH], preferred_element_type=F32)
        act = (gate * jax.nn.sigmoid(gate) * up).astype(BF16)
        part = jnp.dot(act, wo_ref[lo:lo + MXU_WIDTH, :], preferred_element_type=F32)
        acc = part if acc is None else acc + part
    return x + 0.5 * acc


def _ffn_kernel(x_ref, g_ref, wi_ref, wo_ref, o_ref):
    o_ref[...] = _half_step_ffn(x_ref[...], g_ref[...], wi_ref, wo_ref)


def _ffn(x, g, w_in, w_out, *, tm=512):
    row = pl.BlockSpec((tm, D_MODEL), lambda i: (i, 0))
    return pl.pallas_call(
        _ffn_kernel,
        grid=(TOKENS // tm,),
        in_specs=[row, _const_spec((1, D_MODEL)), _const_spec((D_MODEL, 2 * D_FF)),
                  _const_spec((D_FF, D_MODEL))],
        out_specs=row,
        out_shape=jax.ShapeDtypeStruct((TOKENS, D_MODEL), F32),
        compiler_params=pltpu.CompilerParams(
            dimension_semantics=("parallel",), vmem_limit_bytes=VMEM_LIMIT),
        name="ffn",
    )(x, g, w_in, w_out)


NSA_Q = 256
BF16_ROWS = 16
V_ROWS = HEAD_DIM + BF16_ROWS
INPROJ_KEY_TILES = 2


def _rope(x, c, sa, sb):
    w = x.shape[1]
    return x * c + pltpu.roll(x, ROPE_HALF, 1) * sa + pltpu.roll(x, w - ROPE_HALF, 1) * sb


def _inproj_kernel(h_ref, g_ref, wt_ref, lng_ref, lnb_ref, c_ref, sa_ref, sb_ref,
                   gu_ref, vn_ref, qraw_ref, qrot_ref, kc_ref, vc_ref, ks_ref, vs_ref,
                   kw_ref, vw_ref, gate_ref, ga_ref, gb_ref, *, tm):
    n = _rms(h_ref[...], g_ref[...]).astype(BF16)

    def seg(lo, width):
        return lax.dot_general(n, wt_ref[lo:lo + width, :], NT_DIMS, preferred_element_type=F32)

    gu_ref[...] = jax.nn.gelu(seg(OFF_U, GM_WIDTH)).astype(BF16)
    v = jax.nn.gelu(seg(OFF_V, GM_WIDTH))
    mu = jnp.mean(v, axis=-1, keepdims=True)
    vc = v - mu
    var = jnp.mean(vc * vc, axis=-1, keepdims=True)
    vn_ref[...] = (vc * lax.rsqrt(var + NORM_EPS) * lng_ref[...] + lnb_ref[...]).astype(BF16)

    c, sa, sb = c_ref[...], sa_ref[...], sb_ref[...]

    def tile_lanes(t, reps):
        return jnp.concatenate([t] * reps, axis=1)

    q = seg(OFF_Q, N_HEADS * HEAD_DIM) * (HEAD_DIM ** -0.5 * LOG2_E)
    reps = N_HEADS * HEAD_DIM // LANES
    q_rot = _rope(q, tile_lanes(c, reps), tile_lanes(sa, reps), tile_lanes(sb, reps))
    qraw_ref[0] = q.T.astype(BF16).reshape(N_HEADS, HEAD_DIM, tm)
    qrot_ref[0] = q_rot.T.astype(BF16).reshape(N_HEADS, HEAD_DIM, tm)

    kv = seg(OFF_KV, 6 * KV_WIDTH)
    k_c, v_c, k_s, v_s, k_w, v_w = [kv[:, i * KV_WIDTH:(i + 1) * KV_WIDTH] for i in range(6)]
    reps = KV_WIDTH // LANES
    ck, sak, sbk = tile_lanes(c, reps), tile_lanes(sa, reps), tile_lanes(sb, reps)
    k_s = _rope(k_s, ck, sak, sbk)
    k_w = _rope(k_w, ck, sak, sbk)
    pos = (pl.program_id(0) % (SEQ // tm)) * tm + lax.broadcasted_iota(jnp.int32, (tm, N_SEL), 0)
    blk = lax.broadcasted_iota(jnp.int32, (tm, N_SEL), 1)
    onehot = jnp.where(lax.shift_right_logical(pos, SEL_SHIFT) == blk, 1.0, 0.0).astype(F32)
    ones_t = jnp.ones((V_ROWS - HEAD_DIM, NSA_Q), F32)
    ones_col = jnp.where(lax.broadcasted_iota(jnp.int32, (tm, HEAD_DIM), 1) == 0, 1.0, 0.0)
    v_s_t, v_w_t = v_s.T, v_w.T
    for h in range(N_KV_HEADS):
        sl = slice(h * HEAD_DIM, (h + 1) * HEAD_DIM)
        kc_ref[0, h] = k_c[:, sl]
        vc_ref[0, h] = v_c[:, sl]
        ks_ref[0, h] = jnp.concatenate([k_s[:, sl], onehot], axis=1).astype(BF16)
        kw_ref[0, h] = jnp.concatenate([k_w[:, sl], ones_col], axis=1).astype(BF16)
        for t in range(tm // NSA_Q):
            keys = slice(t * NSA_Q, (t + 1) * NSA_Q)
            vs_ref[0, h, t] = jnp.concatenate([v_s_t[sl, keys], ones_t], axis=0).astype(BF16)
            vw_ref[0, h, t] = jnp.concatenate([v_w_t[sl, keys], ones_t], axis=0).astype(BF16)

    merge = jax.nn.sigmoid(seg(OFF_MERGE, 2 * D_MODEL))
    ga_ref[...] = merge[:, :D_MODEL].astype(BF16)
    gb_ref[...] = merge[:, D_MODEL:].astype(BF16)
    gates = jax.nn.sigmoid(seg(OFF_NSA_GATE, LANES))
    group_cols = Q_PER_KV * N_NSA_BRANCH
    for h in range(N_KV_HEADS):
        gate_ref[h] = gates if h == 0 else pltpu.roll(gates, LANES - h * group_cols, 1)


def _inproj(h, g, w_in_t, ln_g, ln_b, rope_c, rope_sa, rope_sb):
    tm = INPROJ_KEY_TILES * NSA_Q
    nq = SEQ // tm
    row = lambda i: (i, 0)
    head = lambda i: (i // nq, 0, i % nq, 0)
    tok_bf = jax.ShapeDtypeStruct((TOKENS, D_MODEL), BF16)

    def head_shape(nh, width, dtype):
        return jax.ShapeDtypeStruct((BATCH, nh, SEQ, width), dtype)

    def head_spec(nh, width):
        return pl.BlockSpec((1, nh, tm, width), head)

    qt_spec = pl.BlockSpec((1, N_HEADS, HEAD_DIM, tm), lambda i: (i // nq, 0, 0, i % nq))
    qt_shape = jax.ShapeDtypeStruct((BATCH, N_HEADS, HEAD_DIM, SEQ), BF16)
    vt_spec = pl.BlockSpec((1, N_KV_HEADS, INPROJ_KEY_TILES, V_ROWS, NSA_Q),
                           lambda i: (i // nq, 0, i % nq, 0, 0))
    vt_shape = jax.ShapeDtypeStruct((BATCH, N_KV_HEADS, SEQ // NSA_Q, V_ROWS, NSA_Q), BF16)

    return pl.pallas_call(
        functools.partial(_inproj_kernel, tm=tm),
        grid=(TOKENS // tm,),
        in_specs=[
            pl.BlockSpec((tm, D_MODEL), row),
            _const_spec((1, D_MODEL)),
            _const_spec(w_in_t.shape),
            _const_spec((1, GM_WIDTH)),
            _const_spec((1, GM_WIDTH)),
            pl.BlockSpec((tm, LANES), lambda i: (i % nq, 0)),
            pl.BlockSpec((tm, LANES), lambda i: (i % nq, 0)),
            pl.BlockSpec((tm, LANES), lambda i: (i % nq, 0)),
        ],
        out_specs=[
            pl.BlockSpec((tm, GM_WIDTH), row),
            pl.BlockSpec((tm, GM_WIDTH), row),
            qt_spec,
            qt_spec,
            head_spec(N_KV_HEADS, HEAD_DIM),
            head_spec(N_KV_HEADS, HEAD_DIM),
            head_spec(N_KV_HEADS, LANES),
            vt_spec,
            head_spec(N_KV_HEADS, LANES),
            vt_spec,
            pl.BlockSpec((N_KV_HEADS, tm, LANES), lambda i: (0, i, 0)),
            pl.BlockSpec((tm, D_MODEL), row),
            pl.BlockSpec((tm, D_MODEL), row),
        ],
        out_shape=[
            tok_bf, tok_bf,
            qt_shape, qt_shape,
            head_shape(N_KV_HEADS, HEAD_DIM, F32), head_shape(N_KV_HEADS, HEAD_DIM, F32),
            head_shape(N_KV_HEADS, LANES, BF16), vt_shape,
            head_shape(N_KV_HEADS, LANES, BF16), vt_shape,
            jax.ShapeDtypeStruct((N_KV_HEADS, TOKENS, LANES), F32),
            tok_bf, tok_bf,
        ],
        compiler_params=pltpu.CompilerParams(
            dimension_semantics=("parallel",), vmem_limit_bytes=VMEM_LIMIT),
        name="inproj",
    )(h, g, w_in_t, ln_g, ln_b, rope_c, rope_sa, rope_sb)


def _compress_kernel(x_ref, ptop_ref, pbot_ref, w1t_ref, w1b_ref, w2_ref, o_ref, *, feature_major):
    x = jnp.concatenate([x_ref[0, 0, pl.ds(j, N_CHUNKS, stride=CMP_STRIDE), :]
                         for j in range(CMP_STRIDE)], axis=1)
    top = jnp.dot((x + ptop_ref[...]).astype(BF16), w1t_ref[...], preferred_element_type=F32)
    bot = jnp.dot((x + pbot_ref[...]).astype(BF16), w1b_ref[...], preferred_element_type=F32)
    hidden = jax.nn.gelu(top + pltpu.roll(bot, N_CHUNKS - 1, 0)).astype(BF16)
    if feature_major:
        out = lax.dot_general(w2_ref[...], hidden, NT_DIMS, preferred_element_type=F32)
    else:
        out = jnp.dot(hidden, w2_ref[...], preferred_element_type=F32)
    o_ref[0, 0] = out.astype(BF16)


def _compress(x_heads, pos_top, pos_bot, w1_top, w1_bot, w2, *, feature_major):
    half = CMP_STRIDE * HEAD_DIM
    out_tile = (HEAD_DIM, N_CHUNKS) if feature_major else (N_CHUNKS, HEAD_DIM)
    return pl.pallas_call(
        functools.partial(_compress_kernel, feature_major=feature_major),
        grid=(BATCH, N_KV_HEADS),
        in_specs=[
            pl.BlockSpec((1, 1, SEQ, HEAD_DIM), lambda b, k: (b, k, 0, 0)),
            _const_spec((1, half)),
            _const_spec((1, half)),
            _const_spec((half, CMP_HIDDEN)),
            _const_spec((half, CMP_HIDDEN)),
            _const_spec(w2.shape),
        ],
        out_specs=pl.BlockSpec((1, 1) + out_tile, lambda b, k: (b, k, 0, 0)),
        out_shape=jax.ShapeDtypeStruct((BATCH, N_KV_HEADS) + out_tile, BF16),
        compiler_params=pltpu.CompilerParams(dimension_semantics=("parallel", "parallel")),
        name="compress",
    )(x_heads, pos_top, pos_bot, w1_top, w1_bot, w2)


RANK_ACCUMULATORS = 4
SELECT_Q = 1024


def _selection_bias(score, n_blocks):
    n_q = score.shape[1]
    sub_iota = lax.broadcasted_iota(jnp.int32, (SUBLANES, n_q), 0)
    groups = [score[g * SUBLANES:(g + 1) * SUBLANES] for g in range(n_blocks // SUBLANES)]
    counts = [[None] * RANK_ACCUMULATORS for _ in groups]
    for jp in range(n_blocks):
        other = score[jp:jp + 1, :]
        for g, grp in enumerate(groups):
            lo = g * SUBLANES
            if lo > jp:
                before = other >= grp
            elif lo + SUBLANES - 1 <= jp:
                before = other > grp
            else:
                before = (other > grp) | ((other == grp) & (sub_iota + lo > jp))
            inc = jnp.where(before, 1, 0)
            a = jp % RANK_ACCUMULATORS
            counts[g][a] = inc if counts[g][a] is None else counts[g][a] + inc
    rank = jnp.concatenate([(c[0] + c[1]) + (c[2] + c[3]) for c in counts], axis=0)
    bias = jnp.where(rank < SEL_TOP, 0.0, MASK_VALUE)
    if n_blocks < N_SEL:
        bias = jnp.concatenate([bias, jnp.zeros((N_SEL - n_blocks, n_q), F32)], axis=0)
    return bias


def _select_kernel(qraw_ref, kcmp_ref, vcmp_ref, bias_ref, ocmp_ref, psum_sc):
    qi = pl.program_id(2)
    s0 = qi * SELECT_Q
    key_i = lax.broadcasted_iota(jnp.int32, (N_CHUNKS, SELECT_Q), 0)
    qry_i = lax.broadcasted_iota(jnp.int32, (N_CHUNKS, SELECT_Q), 1)
    heads = range(Q_PER_KV)

    k_cmp = kcmp_ref[0, 0]
    v_cmp_t = vcmp_ref[0, 0]
    s_cmp = [jnp.dot(k_cmp, qraw_ref[0, h], preferred_element_type=F32) for h in heads]
    cmp_valid = key_i * CMP_STRIDE + (CMP_LEN - 1) <= s0 + qry_i
    any_valid = s0 + lax.broadcasted_iota(jnp.int32, (1, SELECT_Q), 1) >= CMP_LEN - 1
    p_cmp = []
    for h in heads:
        s_c = jnp.where(cmp_valid, s_cmp[h], MASK_VALUE)
        e_c = jnp.exp2(s_c - jnp.max(s_c, axis=0, keepdims=True))
        p_cmp.append(e_c * jnp.where(any_valid, 1.0 / jnp.sum(e_c, axis=0, keepdims=True), 0.0))
    for h in heads:
        ocmp_ref[0, 0, h * HEAD_DIM:(h + 1) * HEAD_DIM, :] = jnp.dot(
            v_cmp_t, p_cmp[h].astype(BF16), preferred_element_type=F32)

    p_sum = (p_cmp[0] + p_cmp[1]) + (p_cmp[2] + p_cmp[3])
    lane_tiles = range(SELECT_Q // LANES)
    for t in lane_tiles:
        psum_sc[t] = p_sum[:, t * LANES:(t + 1) * LANES]
    per_sel = SEL_LEN // CMP_STRIDE
    terms = [jnp.concatenate([psum_sc[t, pl.ds(r, N_SEL, stride=per_sel), :] for t in lane_tiles], axis=1)
             for r in range(per_sel)]
    blk = lax.broadcasted_iota(jnp.int32, (N_SEL, SELECT_Q), 0)
    before = jnp.where(blk == 0, 0.0, pltpu.roll(terms[per_sel - 1], 1, 0))
    imp_t = (before + terms[0]) + (terms[1] + terms[2]) + terms[3]
    cur = lax.shift_right_logical(s0 + lax.broadcasted_iota(jnp.int32, (N_SEL, SELECT_Q), 1), SEL_SHIFT)
    forced = (blk == 0) | (blk == cur) | (blk == cur - 1)
    score = jnp.where(forced, FORCE_SCORE, jnp.where(blk > cur, -FORCE_SCORE, imp_t))

    blocks_per_tile = NSA_Q // SEL_LEN
    for variant in range(SEQ // SELECT_Q):
        @pl.when(qi == variant)
        def _():
            for t in range(SELECT_Q // NSA_Q):
                lanes = slice(t * NSA_Q, (t + 1) * NSA_Q)
                reachable = (variant * (SELECT_Q // NSA_Q) + t + 1) * blocks_per_tile
                n_blocks = min(N_SEL, -(-reachable // SUBLANES) * SUBLANES)
                bias_ref[0, 0, :, lanes] = _selection_bias(score[:, lanes], n_blocks).astype(BF16)


def _select(q_raw_t, k_cmp, v_cmp_t):
    assert CMP_LEN == 2 * CMP_STRIDE
    nq = SEQ // SELECT_Q
    step = lambda rows: pl.BlockSpec((1, 1, rows, SELECT_Q), lambda b, k, i: (b, k, 0, i))
    whole = lambda *tile: pl.BlockSpec((1, 1) + tile, lambda b, k, i: (b, k) + (0,) * len(tile))
    return pl.pallas_call(
        _select_kernel,
        grid=(BATCH, N_KV_HEADS, nq),
        in_specs=[
            pl.BlockSpec((1, Q_PER_KV, HEAD_DIM, SELECT_Q), lambda b, k, i: (b, k, 0, i)),
            whole(N_CHUNKS, HEAD_DIM), whole(HEAD_DIM, N_CHUNKS),
        ],
        out_specs=[step(N_SEL), step(Q_PER_KV * HEAD_DIM)],
        out_shape=[jax.ShapeDtypeStruct((BATCH, N_KV_HEADS, N_SEL, SEQ), BF16),
                   jax.ShapeDtypeStruct((BATCH, N_KV_HEADS, Q_PER_KV * HEAD_DIM, SEQ), F32)],
        scratch_shapes=[pltpu.VMEM((SELECT_Q // LANES, N_CHUNKS, LANES), F32)],
        compiler_params=pltpu.CompilerParams(
            dimension_semantics=("parallel", "parallel", "parallel"), vmem_limit_bytes=VMEM_LIMIT),
        name="select",
    )(q_raw_t, k_cmp, v_cmp_t)


KV_PER_STEP = 2


def _attend_kernel(qrot_ref, bias_ref, ocmp_ref, ks_ref, vs_ref, kw_ref, vw_ref, gate_ref,
                   o_ref, m_sc, acc_sc, sa_sc, sb_sc, owin_sc):
    qi = pl.program_id(2)
    key_i = lax.broadcasted_iota(jnp.int32, (NSA_Q, NSA_Q), 0)
    qry_i = lax.broadcasted_iota(jnp.int32, (NSA_Q, NSA_Q), 1)
    causal = key_i <= qry_i
    kvs = range(KV_PER_STEP)
    heads = range(KV_PER_STEP * Q_PER_KV)

    def scores(k, q_t):
        return jnp.dot(k, q_t, preferred_element_type=F32)

    def softmax_step(h, s, v_t):
        m_prev = m_sc[h]
        m_new = jnp.maximum(m_prev, jnp.max(s, axis=0, keepdims=True))
        p = jnp.exp2(s - m_new).astype(BF16)
        acc_sc[h] = jnp.exp2(m_prev - m_new) * acc_sc[h] + jnp.dot(v_t, p, preferred_element_type=F32)
        m_sc[h] = m_new

    q_sel = [jnp.concatenate([qrot_ref[0, h], bias_ref[0, h // Q_PER_KV]], axis=0) for h in heads]
    m_sc[...] = jnp.full_like(m_sc, MASK_VALUE)
    acc_sc[...] = jnp.zeros_like(acc_sc)

    def sel_scores(kt, dst):
        rows = pl.ds(pl.multiple_of(kt * NSA_Q, NSA_Q), NSA_Q)
        k = [ks_ref[0, kv, rows, :] for kv in kvs]
        for h in heads:
            dst[h] = scores(k[h // Q_PER_KV], q_sel[h])

    def sel_softmax(kt, src, diagonal=False):
        v_t = [vs_ref[0, kv, kt] for kv in kvs]
        for h in heads:
            s = src[h]
            softmax_step(h, jnp.where(causal, s, MASK_VALUE) if diagonal else s, v_t[h // Q_PER_KV])

    def window_keys(kv, kt):
        return kw_ref[0, kv, pl.ds(pl.multiple_of(kt * NSA_Q, NSA_Q), NSA_Q), :]

    def penalty_rows(penalty):
        rows = jnp.where(lax.broadcasted_iota(jnp.int32, (HEAD_DIM, NSA_Q), 0) == 0, penalty, 0.0)
        return rows.astype(BF16)

    win_mid, win_far = jnp.maximum(qi - 1, 0), jnp.maximum(qi - 2, 0)
    k_diag = [window_keys(kv, qi)[:, 0:HEAD_DIM] for kv in kvs]
    k_mid = [window_keys(kv, win_mid) for kv in kvs]
    k_far = [window_keys(kv, win_far) for kv in kvs]
    pen_mid = penalty_rows(jnp.where(qi >= 1, 0.0, MASK_VALUE))
    pen_far = penalty_rows(jnp.where(qi >= 2, 0.0, MASK_VALUE))
    s_win_diag, s_win_mid, s_win_far = [], [], []
    for h in heads:
        q_t = qrot_ref[0, h]
        kv = h // Q_PER_KV
        s_win_diag.append(scores(k_diag[kv], q_t))
        s_win_mid.append(scores(k_mid[kv], jnp.concatenate([q_t, pen_mid], axis=0)))
        s_win_far.append(scores(k_far[kv], jnp.concatenate([q_t, pen_far], axis=0)))

    sel_scores(0, sa_sc)

    for h in heads:
        s_edge = jnp.where(causal, s_win_diag[h], s_win_far[h])
        s_mid = s_win_mid[h]
        m_w = jnp.max(jnp.maximum(s_edge, s_mid), axis=0, keepdims=True)
        p_edge = jnp.exp2(s_edge - m_w).astype(BF16)
        p_mid = jnp.exp2(s_mid - m_w).astype(BF16)
        zero = jnp.zeros_like(p_edge)
        kv = h // Q_PER_KV
        acc = (jnp.dot(vw_ref[0, kv, qi], jnp.where(causal, p_edge, zero), preferred_element_type=F32)
               + jnp.dot(vw_ref[0, kv, win_far], jnp.where(causal, zero, p_edge), preferred_element_type=F32)
               + jnp.dot(vw_ref[0, kv, win_mid], p_mid, preferred_element_type=F32))
        owin_sc[h] = acc[:HEAD_DIM] / acc[HEAD_DIM:HEAD_DIM + 1]

    def sel_pair(first):
        sel_scores(first + 1, sb_sc)
        sel_softmax(first, sa_sc)
        sel_scores(first + 2, sa_sc)
        sel_softmax(first + 1, sb_sc)

    def sel_quad(t, carry):
        sel_pair(4 * t)
        sel_pair(4 * t + 2)
        return carry

    lax.fori_loop(0, lax.shift_right_logical(qi, 2), sel_quad, 0)

    @pl.when((qi & 2) != 0)
    def _():
        sel_pair(qi & ~3)

    def finish():
        gates_t = [gate_ref[kv].T for kv in kvs]
        outs = []
        for h in heads:
            kv, hh = divmod(h, Q_PER_KV)
            g = gates_t[kv]
            c = hh * N_NSA_BRANCH
            acc = acc_sc[h]
            o_sel = acc[:HEAD_DIM] / acc[HEAD_DIM:HEAD_DIM + 1]
            o_cmp = ocmp_ref[0, kv, hh * HEAD_DIM:(hh + 1) * HEAD_DIM, :]
            outs.append(g[c:c + 1] * o_cmp + g[c + 1:c + 2] * o_sel + g[c + 2:c + 3] * owin_sc[h])
        o_ref[0] = jnp.concatenate(outs, axis=0).T.astype(BF16)

    @pl.when((qi & 1) == 1)
    def _():
        sel_scores(qi, sb_sc)
        sel_softmax(qi - 1, sa_sc)
        sel_softmax(qi, sb_sc, diagonal=True)
        finish()

    @pl.when((qi & 1) == 0)
    def _():
        sel_softmax(qi, sa_sc, diagonal=True)
        finish()


def _attend(q_rot_t, bias, o_cmp_t, k_sel, v_sel_t, k_win, v_win_t, gates):
    nq = SEQ // NSA_Q
    n_heads = KV_PER_STEP * Q_PER_KV
    step = lambda rows: pl.BlockSpec((1, KV_PER_STEP, rows, NSA_Q), lambda b, k, i: (b, k, 0, i))
    whole = lambda *tile: pl.BlockSpec((1, KV_PER_STEP) + tile, lambda b, k, i: (b, k) + (0,) * len(tile))
    return pl.pallas_call(
        _attend_kernel,
        grid=(BATCH, N_KV_HEADS // KV_PER_STEP, nq),
        in_specs=[
            pl.BlockSpec((1, n_heads, HEAD_DIM, NSA_Q), lambda b, k, i: (b, k, 0, i)),
            step(N_SEL), step(Q_PER_KV * HEAD_DIM),
            whole(SEQ, LANES), whole(nq, V_ROWS, NSA_Q),
            whole(SEQ, LANES), whole(nq, V_ROWS, NSA_Q),
            pl.BlockSpec((KV_PER_STEP, NSA_Q, LANES), lambda b, k, i: (k, b * nq + i, 0)),
        ],
        out_specs=pl.BlockSpec((1, NSA_Q, n_heads * HEAD_DIM), lambda b, k, i: (b, i, k)),
        out_shape=jax.ShapeDtypeStruct((BATCH, SEQ, N_HEADS * HEAD_DIM), BF16),
        scratch_shapes=[pltpu.VMEM((n_heads, 1, NSA_Q), F32),
                        pltpu.VMEM((n_heads, V_ROWS, NSA_Q), F32),
                        pltpu.VMEM((n_heads, NSA_Q, NSA_Q), F32),
                        pltpu.VMEM((n_heads, NSA_Q, NSA_Q), F32),
                        pltpu.VMEM((n_heads, HEAD_DIM, NSA_Q), F32)],
        compiler_params=pltpu.CompilerParams(
            dimension_semantics=("parallel", "parallel", "arbitrary"),
            vmem_limit_bytes=VMEM_LIMIT),
        name="attend",
    )(q_rot_t, bias, o_cmp_t, k_sel, v_sel_t, k_win, v_win_t, gates)


def _tail_kernel(h_ref, gu_ref, vn_ref, ob_ref, ga_ref, gb_ref, ws_ref, bs_ref,
                 wa_ref, wb_ref, wo_ref, gn_ref, wi_ref, wf_ref, p_ref, gp_ref, wg_ref, wp_ref,
                 gf_ref, o_ref, *, tm):
    r = lax.broadcasted_iota(jnp.int32, (GM_CHUNK, GM_CHUNK), 0)
    c = lax.broadcasted_iota(jnp.int32, (GM_CHUNK, GM_CHUNK), 1)
    w_tril = [jnp.where(c <= r, ws_ref[g], 0.0).astype(BF16) for g in range(GM_GROUPS)]
    bias = bs_ref[...]
    rows = []
    for ch in range(tm // GM_CHUNK):
        rs = slice(ch * GM_CHUNK, (ch + 1) * GM_CHUNK)
        mix = jnp.concatenate(
            [jnp.dot(w_tril[g], vn_ref[rs, g * LANES:(g + 1) * LANES], preferred_element_type=F32)
             for g in range(GM_GROUPS)], axis=1)
        rows.append(gu_ref[rs, :].astype(F32) * (mix + bias))
    z = jnp.concatenate(rows, axis=0).astype(BF16)
    y_a = jnp.dot(z, wa_ref[...], preferred_element_type=F32)
    y_b = jnp.dot(ob_ref[...], wb_ref[...], preferred_element_type=F32)
    merged = (ga_ref[...].astype(F32) * y_a + gb_ref[...].astype(F32) * y_b).astype(BF16)
    h = h_ref[...] + jnp.dot(merged, wo_ref[...], preferred_element_type=F32)

    h = _half_step_ffn(h, gn_ref[...], wi_ref, wf_ref)

    gate = jax.nn.sigmoid(jnp.dot(_rms(h, gp_ref[...]).astype(BF16), wg_ref[...],
                                  preferred_element_type=F32))
    proj = jnp.dot(p_ref[...].astype(BF16), wp_ref[...], preferred_element_type=F32)
    o_ref[...] = _rms(h + gate * proj, gf_ref[...])


def _tail(h, gu, vn, o_b, ga, gb, w_s, b_s_exp, w_a, w_b, w_o, ffn_norm, ffn_w_in, ffn_w_out,
          p, ple_norm, ple_w_gate, ple_w_proj, final_norm, *, tm=512):
    row = pl.BlockSpec((tm, D_MODEL), lambda i: (i, 0))
    square = _const_spec((D_MODEL, D_MODEL))
    gain = _const_spec((1, D_MODEL))
    return pl.pallas_call(
        functools.partial(_tail_kernel, tm=tm),
        grid=(TOKENS // tm,),
        in_specs=[row, row, row, row, row, row,
                  _const_spec((GM_GROUPS, GM_CHUNK, GM_CHUNK)),
                  _const_spec((GM_CHUNK, GM_WIDTH)),
                  square, square, square,
                  gain, _const_spec((D_MODEL, 2 * D_FF)), _const_spec((D_FF, D_MODEL)),
                  pl.BlockSpec((tm, PLE_DIM), lambda i: (i, 0)), gain, square,
                  _const_spec((PLE_DIM, D_MODEL)), gain],
        out_specs=row,
        out_shape=jax.ShapeDtypeStruct((TOKENS, D_MODEL), F32),
        compiler_params=pltpu.CompilerParams(
            dimension_semantics=("parallel",), vmem_limit_bytes=VMEM_LIMIT),
        name="tail",
    )(h, gu, vn, o_b, ga, gb, w_s, b_s_exp, w_a, w_b, w_o, ffn_norm, ffn_w_in, ffn_w_out,
      p, ple_norm, ple_w_gate, ple_w_proj, final_norm)


def _rope_tables():
    inv_freq = ROPE_THETA ** (-jnp.arange(0, ROPE_DIM, 2, dtype=jnp.float32) / ROPE_DIM)
    ang = jnp.arange(SEQ).astype(jnp.float32)[:, None] * inv_freq[None, :]
    cos, sin = jnp.cos(ang), jnp.sin(ang)
    zero = jnp.zeros_like(cos)
    rest = HEAD_DIM - ROPE_DIM
    c = jnp.concatenate([cos, cos, jnp.ones((SEQ, rest), F32)], axis=1)
    sa = jnp.concatenate([zero, sin, jnp.zeros((SEQ, rest), F32)], axis=1)
    sb = jnp.concatenate([-sin, zero, jnp.zeros((SEQ, rest), F32)], axis=1)
    return [jnp.tile(t, (1, LANES // HEAD_DIM)) for t in (c, sa, sb)]


def kernel(x, p, ffn1_norm, ffn1_w_in, ffn1_w_out, mix_norm, w_in, gm_ln_g, gm_ln_b, gm_w_s, gm_b_s,
           w_branch_a, cmp_pos_k, cmp_k_w1, cmp_k_w2, cmp_pos_v, cmp_v_w1, cmp_v_w2, w_branch_b, w_out,
           ffn2_norm, ffn2_w_in, ffn2_w_out, ple_norm, ple_w_gate, ple_w_proj, final_norm):
    assert x.shape == (BATCH, SEQ, D_MODEL) and p.shape == (1, BATCH, SEQ, PLE_DIM)
    row = lambda a: a.reshape(1, -1)
    h = x.reshape(TOKENS, D_MODEL)

    h = _ffn(h, row(ffn1_norm[0]), ffn1_w_in[0].astype(BF16), ffn1_w_out[0].astype(BF16))

    rope_c, rope_sa, rope_sb = _rope_tables()
    (gu, vn, q_raw, q_rot, k_c, v_c, k_sel, v_sel, k_win, v_win, gates, g_a, g_b) = _inproj(
        h, row(mix_norm[0]), w_in[0].T.astype(BF16), row(gm_ln_g[0]), row(gm_ln_b[0]),
        rope_c, rope_sa, rope_sb)

    half = CMP_STRIDE * HEAD_DIM

    def compress(x_heads, pos, w1, w2, feature_major):
        w2 = (w2.T if feature_major else w2).astype(BF16)
        return _compress(x_heads, pos[:CMP_STRIDE].reshape(1, half), pos[CMP_STRIDE:].reshape(1, half),
                         w1[:half].astype(BF16), w1[half:].astype(BF16), w2, feature_major=feature_major)

    k_cmp = compress(k_c, cmp_pos_k[0], cmp_k_w1[0], cmp_k_w2[0], False)
    v_cmp = compress(v_c, cmp_pos_v[0], cmp_v_w1[0], cmp_v_w2[0], True)

    sel_bias, o_cmp = _select(q_raw, k_cmp, v_cmp)
    o_b = _attend(q_rot, sel_bias, o_cmp, k_sel, v_sel, k_win, v_win, gates)

    b_s_exp = jnp.repeat(gm_b_s[0].T, GM_WIDTH // GM_GROUPS, axis=1)
    out = _tail(h, gu, vn, o_b.reshape(TOKENS, N_HEADS * HEAD_DIM), g_a, g_b, gm_w_s[0], b_s_exp,
                w_branch_a[0].astype(BF16), w_branch_b[0].astype(BF16), w_out[0].astype(BF16),
                row(ffn2_norm[0]), ffn2_w_in[0].astype(BF16), ffn2_w_out[0].astype(BF16),
                p[0].reshape(TOKENS, PLE_DIM), row(ple_norm[0]), ple_w_gate[0].astype(BF16),
                ple_w_proj[0].astype(BF16), row(final_norm))
    return out.reshape(BATCH, SEQ, D_MODEL)
```

```python
import functools

import jax
import jax.numpy as jnp
from jax import lax
from jax.experimental import pallas as pl
from jax.experimental.pallas import tpu as pltpu

D_MODEL = 1024
BATCH = 4
SEQ = 4096
PLE_DIM = 256
D_FF = 2816
NORM_EPS = 1e-6
GM_WIDTH = 1024
GM_GROUPS = 8
GM_CHUNK = 128
N_HEADS = 16
N_KV_HEADS = 4
HEAD_DIM = 64
Q_PER_KV = N_HEADS // N_KV_HEADS
KV_WIDTH = N_KV_HEADS * HEAD_DIM
ROPE_DIM = HEAD_DIM // 4
ROPE_HALF = ROPE_DIM // 2
ROPE_THETA = 500000.0
CMP_LEN = 32
CMP_STRIDE = 16
CMP_HIDDEN = 256
SEL_LEN = 64
SEL_TOP = 16
WINDOW = 512
N_NSA_BRANCH = 3
MASK_VALUE = -1e30
FORCE_SCORE = 1e9
LOG2_E = 1.4426950408889634

TOKENS = BATCH * SEQ
N_CHUNKS = SEQ // CMP_STRIDE
N_SEL = SEQ // SEL_LEN
LANES = 128
SUBLANES = 8
VMEM_LIMIT = 56 * 1024 * 1024

OFF_U = 0
OFF_V = OFF_U + GM_WIDTH
OFF_Q = OFF_V + GM_WIDTH
OFF_KV = OFF_Q + N_HEADS * HEAD_DIM
OFF_NSA_GATE = OFF_KV + 6 * KV_WIDTH
OFF_MERGE = OFF_NSA_GATE + N_HEADS * N_NSA_BRANCH
SEL_SHIFT = SEL_LEN.bit_length() - 1

F32 = jnp.float32
BF16 = jnp.bfloat16

NT_DIMS = (((1,), (1,)), ((), ()))


def _const_spec(shape):
    nd = len(shape)
    return pl.BlockSpec(shape, lambda *_: (0,) * nd, pipeline_mode=pl.Buffered(1))


def _rms(x, g):
    return x * lax.rsqrt(jnp.mean(x * x, axis=-1, keepdims=True) + NORM_EPS) * g


MXU_WIDTH = 256


def _half_step_ffn(x, g, wi_ref, wo_ref):
    xn = _rms(x, g).astype(BF16)
    acc = None
    for c in range(D_FF // MXU_WIDTH):
        lo = c * MXU_WIDTH
        gate = jnp.dot(xn, wi_ref[:, lo:lo + MXU_WIDTH], preferred_element_type=F32)
        up = jnp.dot(xn, wi_ref[:, D_FF + lo:D_FF + lo + MXU_WIDTH], preferred_element_type=F32)
        act = (gate * jax.nn.sigmoid(gate) * up).astype(BF16)
        part = jnp.dot(act, wo_ref[lo:lo + MXU_WIDTH, :], preferred_element_type=F32)
        acc = part if acc is None else acc + part
    return x + 0.5 * acc


def _ffn_kernel(x_ref, g_ref, wi_ref, wo_ref, o_ref):
    o_ref[...] = _half_step_ffn(x_ref[...], g_ref[...], wi_ref, wo_ref)


def _ffn(x, g, w_in, w_out, *, tm=512):
    row = pl.BlockSpec((tm, D_MODEL), lambda i: (i, 0))
    return pl.pallas_call(
        _ffn_kernel,
        grid=(TOKENS // tm,),
        in_specs=[row, _const_spec((1, D_MODEL)), _const_spec((D_MODEL, 2 * D_FF)),
                  _const_spec((D_FF, D_MODEL))],
        out_specs=row,
        out_shape=jax.ShapeDtypeStruct((TOKENS, D_MODEL), F32),
        compiler_params=pltpu.CompilerParams(
            dimension_semantics=("parallel",), vmem_limit_bytes=VMEM_LIMIT),
        name="ffn",
    )(x, g, w_in, w_out)


NSA_Q = 256
BF16_ROWS = 16
V_ROWS = HEAD_DIM + BF16_ROWS
INPROJ_KEY_TILES = 2


def _rope(x, c, sa, sb):
    w = x.shape[1]
    return x * c + pltpu.roll(x, ROPE_HALF, 1) * sa + pltpu.roll(x, w - ROPE_HALF, 1) * sb


def _inproj_kernel(h_ref, g_ref, wt_ref, lng_ref, lnb_ref, c_ref, sa_ref, sb_ref,
                   gu_ref, vn_ref, qraw_ref, qrot_ref, kc_ref, vc_ref, ks_ref, vs_ref,
                   kw_ref, vw_ref, gate_ref, ga_ref, gb_ref, *, tm):
    n = _rms(h_ref[...], g_ref[...]).astype(BF16)

    def seg(lo, width):
        return lax.dot_general(n, wt_ref[lo:lo + width, :], NT_DIMS, preferred_element_type=F32)

    gu_ref[...] = jax.nn.gelu(seg(OFF_U, GM_WIDTH)).astype(BF16)
    v = jax.nn.gelu(seg(OFF_V, GM_WIDTH))
    mu = jnp.mean(v, axis=-1, keepdims=True)
    vc = v - mu
    var = jnp.mean(vc * vc, axis=-1, keepdims=True)
    vn_ref[...] = (vc * lax.rsqrt(var + NORM_EPS) * lng_ref[...] + lnb_ref[...]).astype(BF16)

    c, sa, sb = c_ref[...], sa_ref[...], sb_ref[...]

    def tile_lanes(t, reps):
        return jnp.concatenate([t] * reps, axis=1)

    q = seg(OFF_Q, N_HEADS * HEAD_DIM) * (HEAD_DIM ** -0.5 * LOG2_E)
    reps = N_HEADS * HEAD_DIM // LANES
    q_rot = _rope(q, tile_lanes(c, reps), tile_lanes(sa, reps), tile_lanes(sb, reps))
    qraw_ref[0] = q.T.astype(BF16).reshape(N_HEADS, HEAD_DIM, tm)
    qrot_ref[0] = q_rot.T.astype(BF16).reshape(N_HEADS, HEAD_DIM, tm)

    kv = seg(OFF_KV, 6 * KV_WIDTH)
    k_c, v_c, k_s, v_s, k_w, v_w = [kv[:, i * KV_WIDTH:(i + 1) * KV_WIDTH] for i in range(6)]
    reps = KV_WIDTH // LANES
    ck, sak, sbk = tile_lanes(c, reps), tile_lanes(sa, reps), tile_lanes(sb, reps)
    k_s = _rope(k_s, ck, sak, sbk)
    k_w = _rope(k_w, ck, sak, sbk)
    pos = (pl.program_id(0) % (SEQ // tm)) * tm + lax.broadcasted_iota(jnp.int32, (tm, N_SEL), 0)
    blk = lax.broadcasted_iota(jnp.int32, (tm, N_SEL), 1)
    onehot = jnp.where(lax.shift_right_logical(pos, SEL_SHIFT) == blk, 1.0, 0.0).astype(F32)
    ones_t = jnp.ones((V_ROWS - HEAD_DIM, NSA_Q), F32)
    ones_col = jnp.where(lax.broadcasted_iota(jnp.int32, (tm, HEAD_DIM), 1) == 0, 1.0, 0.0)
    v_s_t, v_w_t = v_s.T, v_w.T
    for h in range(N_KV_HEADS):
        sl = slice(h * HEAD_DIM, (h + 1) * HEAD_DIM)
        kc_ref[0, h] = k_c[:, sl]
        vc_ref[0, h] = v_c[:, sl]
        ks_ref[0, h] = jnp.concatenate([k_s[:, sl], onehot], axis=1).astype(BF16)
        kw_ref[0, h] = jnp.concatenate([k_w[:, sl], ones_col], axis=1).astype(BF16)
        for t in range(tm // NSA_Q):
            keys = slice(t * NSA_Q, (t + 1) * NSA_Q)
            vs_ref[0, h, t] = jnp.concatenate([v_s_t[sl, keys], ones_t], axis=0).astype(BF16)
            vw_ref[0, h, t] = jnp.concatenate([v_w_t[sl, keys], ones_t], axis=0).astype(BF16)

    merge = jax.nn.sigmoid(seg(OFF_MERGE, 2 * D_MODEL))
    ga_ref[...] = merge[:, :D_MODEL].astype(BF16)
    gb_ref[...] = merge[:, D_MODEL:].astype(BF16)
    gates = jax.nn.sigmoid(seg(OFF_NSA_GATE, LANES))
    group_cols = Q_PER_KV * N_NSA_BRANCH
    for h in range(N_KV_HEADS):
        gate_ref[h] = gates if h == 0 else pltpu.roll(gates, LANES - h * group_cols, 1)


def _inproj(h, g, w_in_t, ln_g, ln_b, rope_c, rope_sa, rope_sb):
    tm = INPROJ_KEY_TILES * NSA_Q
    nq = SEQ // tm
    row = lambda i: (i, 0)
    head = lambda i: (i // nq, 0, i % nq, 0)
    tok_bf = jax.ShapeDtypeStruct((TOKENS, D_MODEL), BF16)

    def head_shape(nh, width, dtype):
        return jax.ShapeDtypeStruct((BATCH, nh, SEQ, width), dtype)

    def head_spec(nh, width):
        return pl.BlockSpec((1, nh, tm, width), head)

    qt_spec = pl.BlockSpec((1, N_HEADS, HEAD_DIM, tm), lambda i: (i // nq, 0, 0, i % nq))
    qt_shape = jax.ShapeDtypeStruct((BATCH, N_HEADS, HEAD_DIM, SEQ), BF16)
    vt_spec = pl.BlockSpec((1, N_KV_HEADS, INPROJ_KEY_TILES, V_ROWS, NSA_Q),
                           lambda i: (i // nq, 0, i % nq, 0, 0))
    vt_shape = jax.ShapeDtypeStruct((BATCH, N_KV_HEADS, SEQ // NSA_Q, V_ROWS, NSA_Q), BF16)

    return pl.pallas_call(
        functools.partial(_inproj_kernel, tm=tm),
        grid=(TOKENS // tm,),
        in_specs=[
            pl.BlockSpec((tm, D_MODEL), row),
            _const_spec((1, D_MODEL)),
            _const_spec(w_in_t.shape),
            _const_spec((1, GM_WIDTH)),
            _const_spec((1, GM_WIDTH)),
            pl.BlockSpec((tm, LANES), lambda i: (i % nq, 0)),
            pl.BlockSpec((tm, LANES), lambda i: (i % nq, 0)),
            pl.BlockSpec((tm, LANES), lambda i: (i % nq, 0)),
        ],
        out_specs=[
            pl.BlockSpec((tm, GM_WIDTH), row),
            pl.BlockSpec((tm, GM_WIDTH), row),
            qt_spec,
            qt_spec,
            head_spec(N_KV_HEADS, HEAD_DIM),
            head_spec(N_KV_HEADS, HEAD_DIM),
            head_spec(N_KV_HEADS, LANES),
            vt_spec,
            head_spec(N_KV_HEADS, LANES),
            vt_spec,
            pl.BlockSpec((N_KV_HEADS, tm, LANES), lambda i: (0, i, 0)),
            pl.BlockSpec((tm, D_MODEL), row),
            pl.BlockSpec((tm, D_MODEL), row),
        ],
        out_shape=[
            tok_bf, tok_bf,
            qt_shape, qt_shape,
            head_shape(N_KV_HEADS, HEAD_DIM, F32), head_shape(N_KV_HEADS, HEAD_DIM, F32),
            head_shape(N_KV_HEADS, LANES, BF16), vt_shape,
            head_shape(N_KV_HEADS, LANES, BF16), vt_shape,
            jax.ShapeDtypeStruct((N_KV_HEADS, TOKENS, LANES), F32),
            tok_bf, tok_bf,
        ],
        compiler_params=pltpu.CompilerParams(
            dimension_semantics=("parallel",), vmem_limit_bytes=VMEM_LIMIT),
        name="inproj",
    )(h, g, w_in_t, ln_g, ln_b, rope_c, rope_sa, rope_sb)


def _compress_kernel(x_ref, ptop_ref, pbot_ref, w1t_ref, w1b_ref, w2_ref, o_ref, *, feature_major):
    kv_heads = range(N_KV_HEADS)
    x = jnp.concatenate(
        [jnp.concatenate([x_ref[0, h, pl.ds(j, N_CHUNKS, stride=CMP_STRIDE), :]
                          for j in range(CMP_STRIDE)], axis=1) for h in kv_heads], axis=0)
    top = jnp.dot((x + ptop_ref[...]).astype(BF16), w1t_ref[...], preferred_element_type=F32)
    bot = jnp.dot((x + pbot_ref[...]).astype(BF16), w1b_ref[...], preferred_element_type=F32)
    rows = [slice(h * N_CHUNKS, (h + 1) * N_CHUNKS) for h in kv_heads]
    bot_next = jnp.concatenate([pltpu.roll(bot[r], N_CHUNKS - 1, 0) for r in rows], axis=0)
    hidden = jax.nn.gelu(top + bot_next).astype(BF16)
    if feature_major:
        out = lax.dot_general(w2_ref[...], hidden, NT_DIMS, preferred_element_type=F32)
        for h in kv_heads:
            o_ref[0, h] = out[:, rows[h]].astype(BF16)
    else:
        out = jnp.dot(hidden, w2_ref[...], preferred_element_type=F32)
        for h in kv_heads:
            o_ref[0, h] = out[rows[h]].astype(BF16)


def _compress(x_heads, pos_top, pos_bot, w1_top, w1_bot, w2, *, feature_major):
    half = CMP_STRIDE * HEAD_DIM
    out_tile = (HEAD_DIM, N_CHUNKS) if feature_major else (N_CHUNKS, HEAD_DIM)
    return pl.pallas_call(
        functools.partial(_compress_kernel, feature_major=feature_major),
        grid=(BATCH,),
        in_specs=[
            pl.BlockSpec((1, N_KV_HEADS, SEQ, HEAD_DIM), lambda b: (b, 0, 0, 0)),
            _const_spec((1, half)),
            _const_spec((1, half)),
            _const_spec((half, CMP_HIDDEN)),
            _const_spec((half, CMP_HIDDEN)),
            _const_spec(w2.shape),
        ],
        out_specs=pl.BlockSpec((1, N_KV_HEADS) + out_tile, lambda b: (b, 0, 0, 0)),
        out_shape=jax.ShapeDtypeStruct((BATCH, N_KV_HEADS) + out_tile, BF16),
        compiler_params=pltpu.CompilerParams(
            dimension_semantics=("parallel",), vmem_limit_bytes=VMEM_LIMIT),
        name="compress",
    )(x_heads, pos_top, pos_bot, w1_top, w1_bot, w2)


RANK_ACCUMULATORS = 4
SELECT_Q = 1024


def _selection_bias(score, n_blocks):
    n_q = score.shape[1]
    sub_iota = lax.broadcasted_iota(jnp.int32, (SUBLANES, n_q), 0)
    groups = [score[g * SUBLANES:(g + 1) * SUBLANES] for g in range(n_blocks // SUBLANES)]
    counts = [[None] * RANK_ACCUMULATORS for _ in groups]
    for jp in range(n_blocks):
        other = score[jp:jp + 1, :]
        for g, grp in enumerate(groups):
            lo = g * SUBLANES
            if lo > jp:
                before = other >= grp
            elif lo + SUBLANES - 1 <= jp:
                before = other > grp
            else:
                before = (other > grp) | ((other == grp) & (sub_iota + lo > jp))
            inc = jnp.where(before, 1, 0)
            a = jp % RANK_ACCUMULATORS
            counts[g][a] = inc if counts[g][a] is None else counts[g][a] + inc
    rank = jnp.concatenate([(c[0] + c[1]) + (c[2] + c[3]) for c in counts], axis=0)
    bias = jnp.where(rank < SEL_TOP, 0.0, MASK_VALUE)
    if n_blocks < N_SEL:
        bias = jnp.concatenate([bias, jnp.zeros((N_SEL - n_blocks, n_q), F32)], axis=0)
    return bias


def _select_kernel(qraw_ref, kcmp_ref, vcmp_ref, bias_ref, ocmp_ref, psum_sc):
    qi = pl.program_id(2)
    s0 = qi * SELECT_Q
    key_i = lax.broadcasted_iota(jnp.int32, (N_CHUNKS, SELECT_Q), 0)
    qry_i = lax.broadcasted_iota(jnp.int32, (N_CHUNKS, SELECT_Q), 1)
    heads = range(Q_PER_KV)

    k_cmp = kcmp_ref[0, 0]
    v_cmp_t = vcmp_ref[0, 0]
    s_cmp = [jnp.dot(k_cmp, qraw_ref[0, h], preferred_element_type=F32) for h in heads]
    cmp_valid = key_i * CMP_STRIDE + (CMP_LEN - 1) <= s0 + qry_i
    any_valid = s0 + lax.broadcasted_iota(jnp.int32, (1, SELECT_Q), 1) >= CMP_LEN - 1
    p_cmp = []
    for h in heads:
        s_c = jnp.where(cmp_valid, s_cmp[h], MASK_VALUE)
        e_c = jnp.exp2(s_c - jnp.max(s_c, axis=0, keepdims=True))
        p_cmp.append(e_c * jnp.where(any_valid, 1.0 / jnp.sum(e_c, axis=0, keepdims=True), 0.0))
    for h in heads:
        ocmp_ref[0, 0, h * HEAD_DIM:(h + 1) * HEAD_DIM, :] = jnp.dot(
            v_cmp_t, p_cmp[h].astype(BF16), preferred_element_type=F32)

    p_sum = (p_cmp[0] + p_cmp[1]) + (p_cmp[2] + p_cmp[3])
    lane_tiles = range(SELECT_Q // LANES)
    for t in lane_tiles:
        psum_sc[t] = p_sum[:, t * LANES:(t + 1) * LANES]
    per_sel = SEL_LEN // CMP_STRIDE
    terms = [jnp.concatenate([psum_sc[t, pl.ds(r, N_SEL, stride=per_sel), :] for t in lane_tiles], axis=1)
             for r in range(per_sel)]
    blk = lax.broadcasted_iota(jnp.int32, (N_SEL, SELECT_Q), 0)
    before = jnp.where(blk == 0, 0.0, pltpu.roll(terms[per_sel - 1], 1, 0))
    imp_t = (before + terms[0]) + (terms[1] + terms[2]) + terms[3]
    cur = lax.shift_right_logical(s0 + lax.broadcasted_iota(jnp.int32, (N_SEL, SELECT_Q), 1), SEL_SHIFT)
    forced = (blk == 0) | (blk == cur) | (blk == cur - 1)
    score = jnp.where(forced, FORCE_SCORE, jnp.where(blk > cur, -FORCE_SCORE, imp_t))

    blocks_per_tile = NSA_Q // SEL_LEN
    for variant in range(SEQ // SELECT_Q):
        @pl.when(qi == variant)
        def _():
            for t in range(SELECT_Q // NSA_Q):
                lanes = slice(t * NSA_Q, (t + 1) * NSA_Q)
                reachable = (variant * (SELECT_Q // NSA_Q) + t + 1) * blocks_per_tile
                n_blocks = min(N_SEL, -(-reachable // SUBLANES) * SUBLANES)
                bias_ref[0, 0, :, lanes] = _selection_bias(score[:, lanes], n_blocks).astype(BF16)


def _select(q_raw_t, k_cmp, v_cmp_t):
    assert CMP_LEN == 2 * CMP_STRIDE
    nq = SEQ // SELECT_Q
    step = lambda rows: pl.BlockSpec((1, 1, rows, SELECT_Q), lambda b, k, i: (b, k, 0, i))
    whole = lambda *tile: pl.BlockSpec((1, 1) + tile, lambda b, k, i: (b, k) + (0,) * len(tile))
    return pl.pallas_call(
        _select_kernel,
        grid=(BATCH, N_KV_HEADS, nq),
        in_specs=[
            pl.BlockSpec((1, Q_PER_KV, HEAD_DIM, SELECT_Q), lambda b, k, i: (b, k, 0, i)),
            whole(N_CHUNKS, HEAD_DIM), whole(HEAD_DIM, N_CHUNKS),
        ],
        out_specs=[step(N_SEL), step(Q_PER_KV * HEAD_DIM)],
        out_shape=[jax.ShapeDtypeStruct((BATCH, N_KV_HEADS, N_SEL, SEQ), BF16),
                   jax.ShapeDtypeStruct((BATCH, N_KV_HEADS, Q_PER_KV * HEAD_DIM, SEQ), F32)],
        scratch_shapes=[pltpu.VMEM((SELECT_Q // LANES, N_CHUNKS, LANES), F32)],
        compiler_params=pltpu.CompilerParams(
            dimension_semantics=("parallel", "parallel", "parallel"), vmem_limit_bytes=VMEM_LIMIT),
        name="select",
    )(q_raw_t, k_cmp, v_cmp_t)


KV_PER_STEP = 2


def _attend_kernel(qrot_ref, bias_ref, ocmp_ref, ks_ref, vs_ref, kw_ref, vw_ref, gate_ref,
                   o_ref, m_sc, acc_sc, sa_sc, sb_sc, owin_sc):
    qi = pl.program_id(2)
    key_i = lax.broadcasted_iota(jnp.int32, (NSA_Q, NSA_Q), 0)
    qry_i = lax.broadcasted_iota(jnp.int32, (NSA_Q, NSA_Q), 1)
    causal = key_i <= qry_i
    kvs = range(KV_PER_STEP)
    heads = range(KV_PER_STEP * Q_PER_KV)

    def scores(k, q_t):
        return jnp.dot(k, q_t, preferred_element_type=F32)

    def softmax_step(h, s, v_t):
        m_prev = m_sc[h]
        m_new = jnp.maximum(m_prev, jnp.max(s, axis=0, keepdims=True))
        p = jnp.exp2(s - m_new).astype(BF16)
        acc_sc[h] = jnp.exp2(m_prev - m_new) * acc_sc[h] + jnp.dot(v_t, p, preferred_element_type=F32)
        m_sc[h] = m_new

    q_sel = [jnp.concatenate([qrot_ref[0, h], bias_ref[0, h // Q_PER_KV]], axis=0) for h in heads]
    m_sc[...] = jnp.full_like(m_sc, MASK_VALUE)
    acc_sc[...] = jnp.zeros_like(acc_sc)

    def sel_scores(kt, dst):
        rows = pl.ds(pl.multiple_of(kt * NSA_Q, NSA_Q), NSA_Q)
        k = [ks_ref[0, kv, rows, :] for kv in kvs]
        for h in heads:
            dst[h] = scores(k[h // Q_PER_KV], q_sel[h])

    def sel_softmax(kt, src, diagonal=False):
        v_t = [vs_ref[0, kv, kt] for kv in kvs]
        for h in heads:
            s = src[h]
            softmax_step(h, jnp.where(causal, s, MASK_VALUE) if diagonal else s, v_t[h // Q_PER_KV])

    def window_keys(kv, kt):
        return kw_ref[0, kv, pl.ds(pl.multiple_of(kt * NSA_Q, NSA_Q), NSA_Q), :]

    def penalty_rows(penalty):
        rows = jnp.where(lax.broadcasted_iota(jnp.int32, (HEAD_DIM, NSA_Q), 0) == 0, penalty, 0.0)
        return rows.astype(BF16)

    win_mid, win_far = jnp.maximum(qi - 1, 0), jnp.maximum(qi - 2, 0)
    k_diag = [window_keys(kv, qi)[:, 0:HEAD_DIM] for kv in kvs]
    k_mid = [window_keys(kv, win_mid) for kv in kvs]
    k_far = [window_keys(kv, win_far) for kv in kvs]
    pen_mid = penalty_rows(jnp.where(qi >= 1, 0.0, MASK_VALUE))
    pen_far = penalty_rows(jnp.where(qi >= 2, 0.0, MASK_VALUE))
    s_win_diag, s_win_mid, s_win_far = [], [], []
    for h in heads:
        q_t = qrot_ref[0, h]
        kv = h // Q_PER_KV
        s_win_diag.append(scores(k_diag[kv], q_t))
        s_win_mid.append(scores(k_mid[kv], jnp.concatenate([q_t, pen_mid], axis=0)))
        s_win_far.append(scores(k_far[kv], jnp.concatenate([q_t, pen_far], axis=0)))

    sel_scores(0, sa_sc)

    for h in heads:
        s_edge = jnp.where(causal, s_win_diag[h], s_win_far[h])
        s_mid = s_win_mid[h]
        m_w = jnp.max(jnp.maximum(s_edge, s_mid), axis=0, keepdims=True)
        p_edge = jnp.exp2(s_edge - m_w).astype(BF16)
        p_mid = jnp.exp2(s_mid - m_w).astype(BF16)
        zero = jnp.zeros_like(p_edge)
        kv = h // Q_PER_KV
        acc = (jnp.dot(vw_ref[0, kv, qi], jnp.where(causal, p_edge, zero), preferred_element_type=F32)
               + jnp.dot(vw_ref[0, kv, win_far], jnp.where(causal, zero, p_edge), preferred_element_type=F32)
               + jnp.dot(vw_ref[0, kv, win_mid], p_mid, preferred_element_type=F32))
        owin_sc[h] = acc[:HEAD_DIM] / acc[HEAD_DIM:HEAD_DIM + 1]

    def sel_pair(first):
        sel_scores(first + 1, sb_sc)
        sel_softmax(first, sa_sc)
        sel_scores(first + 2, sa_sc)
        sel_softmax(first + 1, sb_sc)

    def sel_quad(t, carry):
        sel_pair(4 * t)
        sel_pair(4 * t + 2)
        return carry

    lax.fori_loop(0, lax.shift_right_logical(qi, 2), sel_quad, 0)

    @pl.when((qi & 2) != 0)
    def _():
        sel_pair(qi & ~3)

    def finish():
        gates_t = [gate_ref[kv].T for kv in kvs]
        outs = []
        for h in heads:
            kv, hh = divmod(h, Q_PER_KV)
            g = gates_t[kv]
            c = hh * N_NSA_BRANCH
            acc = acc_sc[h]
            o_sel = acc[:HEAD_DIM] / acc[HEAD_DIM:HEAD_DIM + 1]
            o_cmp = ocmp_ref[0, kv, hh * HEAD_DIM:(hh + 1) * HEAD_DIM, :]
            outs.append(g[c:c + 1] * o_cmp + g[c + 1:c + 2] * o_sel + g[c + 2:c + 3] * owin_sc[h])
        o_ref[0] = jnp.concatenate(outs, axis=0).T.astype(BF16)

    @pl.when((qi & 1) == 1)
    def _():
        sel_scores(qi, sb_sc)
        sel_softmax(qi - 1, sa_sc)
        sel_softmax(qi, sb_sc, diagonal=True)
        finish()

    @pl.when((qi & 1) == 0)
    def _():
        sel_softmax(qi, sa_sc, diagonal=True)
        finish()


def _attend(q_rot_t, bias, o_cmp_t, k_sel, v_sel_t, k_win, v_win_t, gates):
    nq = SEQ // NSA_Q
    n_heads = KV_PER_STEP * Q_PER_KV
    step = lambda rows: pl.BlockSpec((1, KV_PER_STEP, rows, NSA_Q), lambda b, k, i: (b, k, 0, i))
    whole = lambda *tile: pl.BlockSpec((1, KV_PER_STEP) + tile, lambda b, k, i: (b, k) + (0,) * len(tile))
    return pl.pallas_call(
        _attend_kernel,
        grid=(BATCH, N_KV_HEADS // KV_PER_STEP, nq),
        in_specs=[
            pl.BlockSpec((1, n_heads, HEAD_DIM, NSA_Q), lambda b, k, i: (b, k, 0, i)),
            step(N_SEL), step(Q_PER_KV * HEAD_DIM),
            whole(SEQ, LANES), whole(nq, V_ROWS, NSA_Q),
            whole(SEQ, LANES), whole(nq, V_ROWS, NSA_Q),
            pl.BlockSpec((KV_PER_STEP, NSA_Q, LANES), lambda b, k, i: (k, b * nq + i, 0)),
        ],
        out_specs=pl.BlockSpec((1, NSA_Q, n_heads * HEAD_DIM), lambda b, k, i: (b, i, k)),
        out_shape=jax.ShapeDtypeStruct((BATCH, SEQ, N_HEADS * HEAD_DIM), BF16),
        scratch_shapes=[pltpu.VMEM((n_heads, 1, NSA_Q), F32),
                        pltpu.VMEM((n_heads, V_ROWS, NSA_Q), F32),
                        pltpu.VMEM((n_heads, NSA_Q, NSA_Q), F32),
                        pltpu.VMEM((n_heads, NSA_Q, NSA_Q), F32),
                        pltpu.VMEM((n_heads, HEAD_DIM, NSA_Q), F32)],
        compiler_params=pltpu.CompilerParams(
            dimension_semantics=("parallel", "parallel", "arbitrary"),
            vmem_limit_bytes=VMEM_LIMIT),
        name="attend",
    )(q_rot_t, bias, o_cmp_t, k_sel, v_sel_t, k_win, v_win_t, gates)


def _tail_kernel(h_ref, gu_ref, vn_ref, ob_ref, ga_ref, gb_ref, ws_ref, bs_ref,
                 wa_ref, wb_ref, wo_ref, gn_ref, wi_ref, wf_ref, p_ref, gp_ref, wg_ref, wp_ref,
                 gf_ref, o_ref, *, tm):
    r = lax.broadcasted_iota(jnp.int32, (GM_CHUNK, GM_CHUNK), 0)
    c = lax.broadcasted_iota(jnp.int32, (GM_CHUNK, GM_CHUNK), 1)
    w_tril = [jnp.where(c <= r, ws_ref[g], 0.0).astype(BF16) for g in range(GM_GROUPS)]
    bias = bs_ref[...]
    rows = []
    for ch in range(tm // GM_CHUNK):
        rs = slice(ch * GM_CHUNK, (ch + 1) * GM_CHUNK)
        mix = jnp.concatenate(
            [jnp.dot(w_tril[g], vn_ref[rs, g * LANES:(g + 1) * LANES], preferred_element_type=F32)
             for g in range(GM_GROUPS)], axis=1)
        rows.append(gu_ref[rs, :].astype(F32) * (mix + bias))
    z = jnp.concatenate(rows, axis=0).astype(BF16)
    y_a = jnp.dot(z, wa_ref[...], preferred_element_type=F32)
    y_b = jnp.dot(ob_ref[...], wb_ref[...], preferred_element_type=F32)
    merged = (ga_ref[...].astype(F32) * y_a + gb_ref[...].astype(F32) * y_b).astype(BF16)
    h = h_ref[...] + jnp.dot(merged, wo_ref[...], preferred_element_type=F32)

    h = _half_step_ffn(h, gn_ref[...], wi_ref, wf_ref)

    gate = jax.nn.sigmoid(jnp.dot(_rms(h, gp_ref[...]).astype(BF16), wg_ref[...],
                                  preferred_element_type=F32))
    proj = jnp.dot(p_ref[...].astype(BF16), wp_ref[...], preferred_element_type=F32)
    o_ref[...] = _rms(h + gate * proj, gf_ref[...])


def _tail(h, gu, vn, o_b, ga, gb, w_s, b_s_exp, w_a, w_b, w_o, ffn_norm, ffn_w_in, ffn_w_out,
          p, ple_norm, ple_w_gate, ple_w_proj, final_norm, *, tm=512):
    row = pl.BlockSpec((tm, D_MODEL), lambda i: (i, 0))
    square = _const_spec((D_MODEL, D_MODEL))
    gain = _const_spec((1, D_MODEL))
    return pl.pallas_call(
        functools.partial(_tail_kernel, tm=tm),
        grid=(TOKENS // tm,),
        in_specs=[row, row, row, row, row, row,
                  _const_spec((GM_GROUPS, GM_CHUNK, GM_CHUNK)),
                  _const_spec((GM_CHUNK, GM_WIDTH)),
                  square, square, square,
                  gain, _const_spec((D_MODEL, 2 * D_FF)), _const_spec((D_FF, D_MODEL)),
                  pl.BlockSpec((tm, PLE_DIM), lambda i: (i, 0)), gain, square,
                  _const_spec((PLE_DIM, D_MODEL)), gain],
        out_specs=row,
        out_shape=jax.ShapeDtypeStruct((TOKENS, D_MODEL), F32),
        compiler_params=pltpu.CompilerParams(
            dimension_semantics=("parallel",), vmem_limit_bytes=VMEM_LIMIT),
        name="tail",
    )(h, gu, vn, o_b, ga, gb, w_s, b_s_exp, w_a, w_b, w_o, ffn_norm, ffn_w_in, ffn_w_out,
      p, ple_norm, ple_w_gate, ple_w_proj, final_norm)


def _rope_tables():
    inv_freq = ROPE_THETA ** (-jnp.arange(0, ROPE_DIM, 2, dtype=jnp.float32) / ROPE_DIM)
    ang = jnp.arange(SEQ).astype(jnp.float32)[:, None] * inv_freq[None, :]
    cos, sin = jnp.cos(ang), jnp.sin(ang)
    zero = jnp.zeros_like(cos)
    rest = HEAD_DIM - ROPE_DIM
    c = jnp.concatenate([cos, cos, jnp.ones((SEQ, rest), F32)], axis=1)
    sa = jnp.concatenate([zero, sin, jnp.zeros((SEQ, rest), F32)], axis=1)
    sb = jnp.concatenate([-sin, zero, jnp.zeros((SEQ, rest), F32)], axis=1)
    return [jnp.tile(t, (1, LANES // HEAD_DIM)) for t in (c, sa, sb)]


def kernel(x, p, ffn1_norm, ffn1_w_in, ffn1_w_out, mix_norm, w_in, gm_ln_g, gm_ln_b, gm_w_s, gm_b_s,
           w_branch_a, cmp_pos_k, cmp_k_w1, cmp_k_w2, cmp_pos_v, cmp_v_w1, cmp_v_w2, w_branch_b, w_out,
           ffn2_norm, ffn2_w_in, ffn2_w_out, ple_norm, ple_w_gate, ple_w_proj, final_norm):
    assert x.shape == (BATCH, SEQ, D_MODEL) and p.shape == (1, BATCH, SEQ, PLE_DIM)
    row = lambda a: a.reshape(1, -1)
    h = x.reshape(TOKENS, D_MODEL)

    h = _ffn(h, row(ffn1_norm[0]), ffn1_w_in[0].astype(BF16), ffn1_w_out[0].astype(BF16))

    rope_c, rope_sa, rope_sb = _rope_tables()
    (gu, vn, q_raw, q_rot, k_c, v_c, k_sel, v_sel, k_win, v_win, gates, g_a, g_b) = _inproj(
        h, row(mix_norm[0]), w_in[0].T.astype(BF16), row(gm_ln_g[0]), row(gm_ln_b[0]),
        rope_c, rope_sa, rope_sb)

    half = CMP_STRIDE * HEAD_DIM

    def compress(x_heads, pos, w1, w2, feature_major):
        w2 = (w2.T if feature_major else w2).astype(BF16)
        return _compress(x_heads, pos[:CMP_STRIDE].reshape(1, half), pos[CMP_STRIDE:].reshape(1, half),
                         w1[:half].astype(BF16), w1[half:].astype(BF16), w2, feature_major=feature_major)

    k_cmp = compress(k_c, cmp_pos_k[0], cmp_k_w1[0], cmp_k_w2[0], False)
    v_cmp = compress(v_c, cmp_pos_v[0], cmp_v_w1[0], cmp_v_w2[0], True)

    sel_bias, o_cmp = _select(q_raw, k_cmp, v_cmp)
    o_b = _attend(q_rot, sel_bias, o_cmp, k_sel, v_sel, k_win, v_win, gates)

    b_s_exp = jnp.repeat(gm_b_s[0].T, GM_WIDTH // GM_GROUPS, axis=1)
    out = _tail(h, gu, vn, o_b.reshape(TOKENS, N_HEADS * HEAD_DIM), g_a, g_b, gm_w_s[0], b_s_exp,
                w_branch_a[0].astype(BF16), w_branch_b[0].astype(BF16), w_out[0].astype(BF16),
                row(ffn2_norm[0]), ffn2_w_in[0].astype(BF16), ffn2_w_out[0].astype(BF16),
                p[0].reshape(TOKENS, PLE_DIM), row(ple_norm[0]), ple_w_gate[0].astype(BF16),
                ple_w_proj[0].astype(BF16), row(final_norm))
    return out.reshape(BATCH, SEQ, D_MODEL)
```

```python
import functools

import jax
import jax.numpy as jnp
from jax import lax
from jax.experimental import pallas as pl
from jax.experimental.pallas import tpu as pltpu

D_MODEL = 1024
BATCH = 4
SEQ = 4096
PLE_DIM = 256
D_FF = 2816
NORM_EPS = 1e-6
GM_WIDTH = 1024
GM_GROUPS = 8
GM_CHUNK = 128
N_HEADS = 16
N_KV_HEADS = 4
HEAD_DIM = 64
Q_PER_KV = N_HEADS // N_KV_HEADS
KV_WIDTH = N_KV_HEADS * HEAD_DIM
ROPE_DIM = HEAD_DIM // 4
ROPE_HALF = ROPE_DIM // 2
ROPE_THETA = 500000.0
CMP_LEN = 32
CMP_STRIDE = 16
CMP_HIDDEN = 256
SEL_LEN = 64
SEL_TOP = 16
WINDOW = 512
N_NSA_BRANCH = 3
MASK_VALUE = -1e30
FORCE_SCORE = 1e9
LOG2_E = 1.4426950408889634

TOKENS = BATCH * SEQ
N_CHUNKS = SEQ // CMP_STRIDE
N_SEL = SEQ // SEL_LEN
LANES = 128
SUBLANES = 8
VMEM_LIMIT = 56 * 1024 * 1024

OFF_U = 0
OFF_V = OFF_U + GM_WIDTH
OFF_Q = OFF_V + GM_WIDTH
OFF_KV = OFF_Q + N_HEADS * HEAD_DIM
OFF_NSA_GATE = OFF_KV + 6 * KV_WIDTH
OFF_MERGE = OFF_NSA_GATE + N_HEADS * N_NSA_BRANCH
SEL_SHIFT = SEL_LEN.bit_length() - 1

F32 = jnp.float32
BF16 = jnp.bfloat16

NT_DIMS = (((1,), (1,)), ((), ()))


def _const_spec(shape):
    nd = len(shape)
    return pl.BlockSpec(shape, lambda *_: (0,) * nd, pipeline_mode=pl.Buffered(1))


def _rms(x, g):
    return x * lax.rsqrt(jnp.mean(x * x, axis=-1, keepdims=True) + NORM_EPS) * g


MXU_WIDTH = 256


def _half_step_ffn(x, g, wi_ref, wo_ref):
    xn = _rms(x, g).astype(BF16)
    acc = None
    for c in range(D_FF // MXU_WIDTH):
        lo = c * MXU_WIDTH
        gate = jnp.dot(xn, wi_ref[:, lo:lo + MXU_WIDTH], preferred_element_type=F32)
        up = jnp.dot(xn, wi_ref[:, D_FF + lo:D_FF + lo + MXU_WIDTH], preferred_element_type=F32)
        act = (gate * jax.nn.sigmoid(gate) * up).astype(BF16)
        part = jnp.dot(act, wo_ref[lo:lo + MXU_WIDTH, :], preferred_element_type=F32)
        acc = part if acc is None else acc + part
    return x + 0.5 * acc


def _ffn_kernel(x_ref, g_ref, wi_ref, wo_ref, o_ref):
    o_ref[...] = _half_step_ffn(x_ref[...], g_ref[...], wi_ref, wo_ref)


def _ffn(x, g, w_in, w_out, *, tm=512):
    row = pl.BlockSpec((tm, D_MODEL), lambda i: (i, 0))
    return pl.pallas_call(
        _ffn_kernel,
        grid=(TOKENS // tm,),
        in_specs=[row, _const_spec((1, D_MODEL)), _const_spec((D_MODEL, 2 * D_FF)),
                  _const_spec((D_FF, D_MODEL))],
        out_specs=row,
        out_shape=jax.ShapeDtypeStruct((TOKENS, D_MODEL), F32),
        compiler_params=pltpu.CompilerParams(
            dimension_semantics=("parallel",), vmem_limit_bytes=VMEM_LIMIT),
        name="ffn",
    )(x, g, w_in, w_out)


NSA_Q = 256
BF16_ROWS = 16
V_ROWS = HEAD_DIM + BF16_ROWS
INPROJ_KEY_TILES = 2


def _rope(x, c, sa, sb):
    w = x.shape[1]
    return x * c + pltpu.roll(x, ROPE_HALF, 1) * sa + pltpu.roll(x, w - ROPE_HALF, 1) * sb


def _inproj_kernel(h_ref, g_ref, wt_ref, lng_ref, lnb_ref, c_ref, sa_ref, sb_ref,
                   gu_ref, vn_ref, qraw_ref, qrot_ref, kc_ref, vc_ref, ks_ref, vs_ref,
                   kw_ref, vw_ref, gate_ref, ga_ref, gb_ref, *, tm):
    n = _rms(h_ref[...], g_ref[...]).astype(BF16)

    def seg(lo, width):
        return lax.dot_general(n, wt_ref[lo:lo + width, :], NT_DIMS, preferred_element_type=F32)

    gu_ref[...] = jax.nn.gelu(seg(OFF_U, GM_WIDTH)).astype(BF16)
    v = jax.nn.gelu(seg(OFF_V, GM_WIDTH))
    mu = jnp.mean(v, axis=-1, keepdims=True)
    vc = v - mu
    var = jnp.mean(vc * vc, axis=-1, keepdims=True)
    vn_ref[...] = (vc * lax.rsqrt(var + NORM_EPS) * lng_ref[...] + lnb_ref[...]).astype(BF16)

    c, sa, sb = c_ref[...], sa_ref[...], sb_ref[...]

    def tile_lanes(t, reps):
        return jnp.concatenate([t] * reps, axis=1)

    q = seg(OFF_Q, N_HEADS * HEAD_DIM) * (HEAD_DIM ** -0.5 * LOG2_E)
    reps = N_HEADS * HEAD_DIM // LANES
    q_rot = _rope(q, tile_lanes(c, reps), tile_lanes(sa, reps), tile_lanes(sb, reps))
    qraw_ref[0] = q.T.astype(BF16).reshape(N_HEADS, HEAD_DIM, tm)
    qrot_ref[0] = q_rot.T.astype(BF16).reshape(N_HEADS, HEAD_DIM, tm)

    kv = seg(OFF_KV, 6 * KV_WIDTH)
    k_c, v_c, k_s, v_s, k_w, v_w = [kv[:, i * KV_WIDTH:(i + 1) * KV_WIDTH] for i in range(6)]
    reps = KV_WIDTH // LANES
    ck, sak, sbk = tile_lanes(c, reps), tile_lanes(sa, reps), tile_lanes(sb, reps)
    k_s = _rope(k_s, ck, sak, sbk)
    k_w = _rope(k_w, ck, sak, sbk)
    pos = (pl.program_id(0) % (SEQ // tm)) * tm + lax.broadcasted_iota(jnp.int32, (tm, N_SEL), 0)
    blk = lax.broadcasted_iota(jnp.int32, (tm, N_SEL), 1)
    onehot = jnp.where(lax.shift_right_logical(pos, SEL_SHIFT) == blk, 1.0, 0.0).astype(F32)
    ones_t = jnp.ones((V_ROWS - HEAD_DIM, NSA_Q), F32)
    ones_col = jnp.where(lax.broadcasted_iota(jnp.int32, (tm, HEAD_DIM), 1) == 0, 1.0, 0.0)
    v_s_t, v_w_t = v_s.T, v_w.T
    for h in range(N_KV_HEADS):
        sl = slice(h * HEAD_DIM, (h + 1) * HEAD_DIM)
        kc_ref[0, h] = k_c[:, sl]
        vc_ref[0, h] = v_c[:, sl]
        ks_ref[0, h] = jnp.concatenate([k_s[:, sl], onehot], axis=1).astype(BF16)
        kw_ref[0, h] = jnp.concatenate([k_w[:, sl], ones_col], axis=1).astype(BF16)
        for t in range(tm // NSA_Q):
            keys = slice(t * NSA_Q, (t + 1) * NSA_Q)
            vs_ref[0, h, t] = jnp.concatenate([v_s_t[sl, keys], ones_t], axis=0).astype(BF16)
            vw_ref[0, h, t] = jnp.concatenate([v_w_t[sl, keys], ones_t], axis=0).astype(BF16)

    merge = jax.nn.sigmoid(seg(OFF_MERGE, 2 * D_MODEL))
    ga_ref[...] = merge[:, :D_MODEL].astype(BF16)
    gb_ref[...] = merge[:, D_MODEL:].astype(BF16)
    gates = jax.nn.sigmoid(seg(OFF_NSA_GATE, LANES))
    group_cols = Q_PER_KV * N_NSA_BRANCH
    for h in range(N_KV_HEADS):
        gate_ref[h] = gates if h == 0 else pltpu.roll(gates, LANES - h * group_cols, 1)


def _inproj(h, g, w_in_t, ln_g, ln_b, rope_c, rope_sa, rope_sb):
    tm = INPROJ_KEY_TILES * NSA_Q
    nq = SEQ // tm
    row = lambda i: (i, 0)
    head = lambda i: (i // nq, 0, i % nq, 0)
    tok_bf = jax.ShapeDtypeStruct((TOKENS, D_MODEL), BF16)

    def head_shape(nh, width, dtype):
        return jax.ShapeDtypeStruct((BATCH, nh, SEQ, width), dtype)

    def head_spec(nh, width):
        return pl.BlockSpec((1, nh, tm, width), head)

    qt_spec = pl.BlockSpec((1, N_HEADS, HEAD_DIM, tm), lambda i: (i // nq, 0, 0, i % nq))
    qt_shape = jax.ShapeDtypeStruct((BATCH, N_HEADS, HEAD_DIM, SEQ), BF16)
    vt_spec = pl.BlockSpec((1, N_KV_HEADS, INPROJ_KEY_TILES, V_ROWS, NSA_Q),
                           lambda i: (i // nq, 0, i % nq, 0, 0))
    vt_shape = jax.ShapeDtypeStruct((BATCH, N_KV_HEADS, SEQ // NSA_Q, V_ROWS, NSA_Q), BF16)

    return pl.pallas_call(
        functools.partial(_inproj_kernel, tm=tm),
        grid=(TOKENS // tm,),
        in_specs=[
            pl.BlockSpec((tm, D_MODEL), row),
            _const_spec((1, D_MODEL)),
            _const_spec(w_in_t.shape),
            _const_spec((1, GM_WIDTH)),
            _const_spec((1, GM_WIDTH)),
            pl.BlockSpec((tm, LANES), lambda i: (i % nq, 0)),
            pl.BlockSpec((tm, LANES), lambda i: (i % nq, 0)),
            pl.BlockSpec((tm, LANES), lambda i: (i % nq, 0)),
        ],
        out_specs=[
            pl.BlockSpec((tm, GM_WIDTH), row),
            pl.BlockSpec((tm, GM_WIDTH), row),
            qt_spec,
            qt_spec,
            head_spec(N_KV_HEADS, HEAD_DIM),
            head_spec(N_KV_HEADS, HEAD_DIM),
            head_spec(N_KV_HEADS, LANES),
            vt_spec,
            head_spec(N_KV_HEADS, LANES),
            vt_spec,
            pl.BlockSpec((N_KV_HEADS, tm, LANES), lambda i: (0, i, 0)),
            pl.BlockSpec((tm, D_MODEL), row),
            pl.BlockSpec((tm, D_MODEL), row),
        ],
        out_shape=[
            tok_bf, tok_bf,
            qt_shape, qt_shape,
            head_shape(N_KV_HEADS, HEAD_DIM, F32), head_shape(N_KV_HEADS, HEAD_DIM, F32),
            head_shape(N_KV_HEADS, LANES, BF16), vt_shape,
            head_shape(N_KV_HEADS, LANES, BF16), vt_shape,
            jax.ShapeDtypeStruct((N_KV_HEADS, TOKENS, LANES), F32),
            tok_bf, tok_bf,
        ],
        compiler_params=pltpu.CompilerParams(
            dimension_semantics=("parallel",), vmem_limit_bytes=VMEM_LIMIT),
        name="inproj",
    )(h, g, w_in_t, ln_g, ln_b, rope_c, rope_sa, rope_sb)


def _compress_kernel(x_ref, ptop_ref, pbot_ref, w1t_ref, w1b_ref, w2_ref, o_ref, *, feature_major):
    kv_heads = range(N_KV_HEADS)
    x = jnp.concatenate(
        [jnp.concatenate([x_ref[0, h, pl.ds(j, N_CHUNKS, stride=CMP_STRIDE), :]
                          for j in range(CMP_STRIDE)], axis=1) for h in kv_heads], axis=0)
    top = jnp.dot((x + ptop_ref[...]).astype(BF16), w1t_ref[...], preferred_element_type=F32)
    bot = jnp.dot((x + pbot_ref[...]).astype(BF16), w1b_ref[...], preferred_element_type=F32)
    rows = [slice(h * N_CHUNKS, (h + 1) * N_CHUNKS) for h in kv_heads]
    bot_next = jnp.concatenate([pltpu.roll(bot[r], N_CHUNKS - 1, 0) for r in rows], axis=0)
    hidden = jax.nn.gelu(top + bot_next).astype(BF16)
    if feature_major:
        out = lax.dot_general(w2_ref[...], hidden, NT_DIMS, preferred_element_type=F32)
        for h in kv_heads:
            o_ref[0, h] = out[:, rows[h]].astype(BF16)
    else:
        out = jnp.dot(hidden, w2_ref[...], preferred_element_type=F32)
        for h in kv_heads:
            o_ref[0, h] = out[rows[h]].astype(BF16)


def _compress(x_heads, pos_top, pos_bot, w1_top, w1_bot, w2, *, feature_major):
    half = CMP_STRIDE * HEAD_DIM
    out_tile = (HEAD_DIM, N_CHUNKS) if feature_major else (N_CHUNKS, HEAD_DIM)
    return pl.pallas_call(
        functools.partial(_compress_kernel, feature_major=feature_major),
        grid=(BATCH,),
        in_specs=[
            pl.BlockSpec((1, N_KV_HEADS, SEQ, HEAD_DIM), lambda b: (b, 0, 0, 0)),
            _const_spec((1, half)),
            _const_spec((1, half)),
            _const_spec((half, CMP_HIDDEN)),
            _const_spec((half, CMP_HIDDEN)),
            _const_spec(w2.shape),
        ],
        out_specs=pl.BlockSpec((1, N_KV_HEADS) + out_tile, lambda b: (b, 0, 0, 0)),
        out_shape=jax.ShapeDtypeStruct((BATCH, N_KV_HEADS) + out_tile, BF16),
        compiler_params=pltpu.CompilerParams(
            dimension_semantics=("parallel",), vmem_limit_bytes=VMEM_LIMIT),
        name="compress",
    )(x_heads, pos_top, pos_bot, w1_top, w1_bot, w2)


RANK_ACCUMULATORS = 4
SELECT_Q = 1024


def _selection_bias(score, n_blocks):
    n_q = score.shape[1]
    sub_iota = lax.broadcasted_iota(jnp.int32, (SUBLANES, n_q), 0)
    groups = [score[g * SUBLANES:(g + 1) * SUBLANES] for g in range(n_blocks // SUBLANES)]
    counts = [[None] * RANK_ACCUMULATORS for _ in groups]
    for jp in range(n_blocks):
        other = score[jp:jp + 1, :]
        for g, grp in enumerate(groups):
            lo = g * SUBLANES
            if lo > jp:
                before = other >= grp
            elif lo + SUBLANES - 1 <= jp:
                before = other > grp
            else:
                before = (other > grp) | ((other == grp) & (sub_iota + lo > jp))
            inc = jnp.where(before, 1, 0)
            a = jp % RANK_ACCUMULATORS
            counts[g][a] = inc if counts[g][a] is None else counts[g][a] + inc
    rank = jnp.concatenate([(c[0] + c[1]) + (c[2] + c[3]) for c in counts], axis=0)
    bias = jnp.where(rank < SEL_TOP, 0.0, MASK_VALUE)
    if n_blocks < N_SEL:
        bias = jnp.concatenate([bias, jnp.zeros((N_SEL - n_blocks, n_q), F32)], axis=0)
    return bias


def _select_kernel(qraw_ref, kcmp_ref, vcmp_ref, bias_ref, ocmp_ref, psum_sc):
    qi = pl.program_id(2)
    s0 = qi * SELECT_Q
    key_i = lax.broadcasted_iota(jnp.int32, (N_CHUNKS, SELECT_Q), 0)
    qry_i = lax.broadcasted_iota(jnp.int32, (N_CHUNKS, SELECT_Q), 1)
    heads = range(Q_PER_KV)

    k_cmp = kcmp_ref[0, 0]
    v_cmp_t = vcmp_ref[0, 0]
    s_cmp = [jnp.dot(k_cmp, qraw_ref[0, h], preferred_element_type=F32) for h in heads]
    cmp_valid = key_i * CMP_STRIDE + (CMP_LEN - 1) <= s0 + qry_i
    any_valid = s0 + lax.broadcasted_iota(jnp.int32, (1, SELECT_Q), 1) >= CMP_LEN - 1
    p_cmp = []
    for h in heads:
        s_c = jnp.where(cmp_valid, s_cmp[h], MASK_VALUE)
        e_c = jnp.exp2(s_c - jnp.max(s_c, axis=0, keepdims=True))
        p_cmp.append(e_c * jnp.where(any_valid, 1.0 / jnp.sum(e_c, axis=0, keepdims=True), 0.0))
    for h in heads:
        ocmp_ref[0, 0, h * HEAD_DIM:(h + 1) * HEAD_DIM, :] = jnp.dot(
            v_cmp_t, p_cmp[h].astype(BF16), preferred_element_type=F32)

    p_sum = (p_cmp[0] + p_cmp[1]) + (p_cmp[2] + p_cmp[3])
    lane_tiles = range(SELECT_Q // LANES)
    for t in lane_tiles:
        psum_sc[t] = p_sum[:, t * LANES:(t + 1) * LANES]
    per_sel = SEL_LEN // CMP_STRIDE
    terms = [jnp.concatenate([psum_sc[t, pl.ds(r, N_SEL, stride=per_sel), :] for t in lane_tiles], axis=1)
             for r in range(per_sel)]
    blk = lax.broadcasted_iota(jnp.int32, (N_SEL, SELECT_Q), 0)
    before = jnp.where(blk == 0, 0.0, pltpu.roll(terms[per_sel - 1], 1, 0))
    imp_t = (before + terms[0]) + (terms[1] + terms[2]) + terms[3]
    cur = lax.shift_right_logical(s0 + lax.broadcasted_iota(jnp.int32, (N_SEL, SELECT_Q), 1), SEL_SHIFT)
    forced = (blk == 0) | (blk == cur) | (blk == cur - 1)
    score = jnp.where(forced, FORCE_SCORE, jnp.where(blk > cur, -FORCE_SCORE, imp_t))

    blocks_per_tile = NSA_Q // SEL_LEN
    for variant in range(SEQ // SELECT_Q):
        @pl.when(qi == variant)
        def _():
            for t in range(SELECT_Q // NSA_Q):
                lanes = slice(t * NSA_Q, (t + 1) * NSA_Q)
                reachable = (variant * (SELECT_Q // NSA_Q) + t + 1) * blocks_per_tile
                n_blocks = min(N_SEL, -(-reachable // SUBLANES) * SUBLANES)
                bias_ref[0, 0, :, lanes] = _selection_bias(score[:, lanes], n_blocks).astype(BF16)


def _select(q_raw_t, k_cmp, v_cmp_t):
    assert CMP_LEN == 2 * CMP_STRIDE
    nq = SEQ // SELECT_Q
    step = lambda rows: pl.BlockSpec((1, 1, rows, SELECT_Q), lambda b, k, i: (b, k, 0, i))
    whole = lambda *tile: pl.BlockSpec((1, 1) + tile, lambda b, k, i: (b, k) + (0,) * len(tile))
    return pl.pallas_call(
        _select_kernel,
        grid=(BATCH, N_KV_HEADS, nq),
        in_specs=[
            pl.BlockSpec((1, Q_PER_KV, HEAD_DIM, SELECT_Q), lambda b, k, i: (b, k, 0, i)),
            whole(N_CHUNKS, HEAD_DIM), whole(HEAD_DIM, N_CHUNKS),
        ],
        out_specs=[step(N_SEL), step(Q_PER_KV * HEAD_DIM)],
        out_shape=[jax.ShapeDtypeStruct((BATCH, N_KV_HEADS, N_SEL, SEQ), BF16),
                   jax.ShapeDtypeStruct((BATCH, N_KV_HEADS, Q_PER_KV * HEAD_DIM, SEQ), F32)],
        scratch_shapes=[pltpu.VMEM((SELECT_Q // LANES, N_CHUNKS, LANES), F32)],
        compiler_params=pltpu.CompilerParams(
            dimension_semantics=("parallel", "parallel", "parallel"), vmem_limit_bytes=VMEM_LIMIT),
        name="select",
    )(q_raw_t, k_cmp, v_cmp_t)


KV_PER_STEP = 2


def _attend_kernel(qrot_ref, bias_ref, ocmp_ref, ks_ref, vs_ref, kw_ref, vw_ref, gate_ref,
                   o_ref, m_sc, acc_sc, sa_sc, sb_sc, owin_sc):
    qi = pl.program_id(2)
    key_i = lax.broadcasted_iota(jnp.int32, (NSA_Q, NSA_Q), 0)
    qry_i = lax.broadcasted_iota(jnp.int32, (NSA_Q, NSA_Q), 1)
    causal = key_i <= qry_i
    kvs = range(KV_PER_STEP)
    heads = range(KV_PER_STEP * Q_PER_KV)

    def scores(k, q_t):
        return jnp.dot(k, q_t, preferred_element_type=F32)

    def softmax_step(h, s, v_t):
        m_prev = m_sc[h]
        m_new = jnp.maximum(m_prev, jnp.max(s, axis=0, keepdims=True))
        p = jnp.exp2(s - m_new).astype(BF16)
        acc_sc[h] = jnp.exp2(m_prev - m_new) * acc_sc[h] + jnp.dot(v_t, p, preferred_element_type=F32)
        m_sc[h] = m_new

    q_sel = [jnp.concatenate([qrot_ref[0, h], bias_ref[0, h // Q_PER_KV]], axis=0) for h in heads]
    m_sc[...] = jnp.full_like(m_sc, MASK_VALUE)
    acc_sc[...] = jnp.zeros_like(acc_sc)

    def sel_scores(kt, dst):
        rows = pl.ds(pl.multiple_of(kt * NSA_Q, NSA_Q), NSA_Q)
        k = [ks_ref[0, kv, rows, :] for kv in kvs]
        for h in heads:
            dst[h] = scores(k[h // Q_PER_KV], q_sel[h])

    def sel_softmax(kt, src, diagonal=False):
        v_t = [vs_ref[0, kv, kt] for kv in kvs]
        for h in heads:
            s = src[h]
            softmax_step(h, jnp.where(causal, s, MASK_VALUE) if diagonal else s, v_t[h // Q_PER_KV])

    def window_keys(kv, kt):
        return kw_ref[0, kv, pl.ds(pl.multiple_of(kt * NSA_Q, NSA_Q), NSA_Q), :]

    def penalty_rows(penalty):
        rows = jnp.where(lax.broadcasted_iota(jnp.int32, (HEAD_DIM, NSA_Q), 0) == 0, penalty, 0.0)
        return rows.astype(BF16)

    win_mid, win_far = jnp.maximum(qi - 1, 0), jnp.maximum(qi - 2, 0)
    k_diag = [window_keys(kv, qi)[:, 0:HEAD_DIM] for kv in kvs]
    k_mid = [window_keys(kv, win_mid) for kv in kvs]
    k_far = [window_keys(kv, win_far) for kv in kvs]
    pen_mid = penalty_rows(jnp.where(qi >= 1, 0.0, MASK_VALUE))
    pen_far = penalty_rows(jnp.where(qi >= 2, 0.0, MASK_VALUE))
    s_win_diag, s_win_mid, s_win_far = [], [], []
    for h in heads:
        q_t = qrot_ref[0, h]
        kv = h // Q_PER_KV
        s_win_diag.append(scores(k_diag[kv], q_t))
        s_win_mid.append(scores(k_mid[kv], jnp.concatenate([q_t, pen_mid], axis=0)))
        s_win_far.append(scores(k_far[kv], jnp.concatenate([q_t, pen_far], axis=0)))

    sel_scores(0, sa_sc)

    for h in heads:
        s_edge = jnp.where(causal, s_win_diag[h], s_win_far[h])
        s_mid = s_win_mid[h]
        m_w = jnp.max(jnp.maximum(s_edge, s_mid), axis=0, keepdims=True)
        p_edge = jnp.exp2(s_edge - m_w).astype(BF16)
        p_mid = jnp.exp2(s_mid - m_w).astype(BF16)
        zero = jnp.zeros_like(p_edge)
        kv = h // Q_PER_KV
        acc = (jnp.dot(vw_ref[0, kv, qi], jnp.where(causal, p_edge, zero), preferred_element_type=F32)
               + jnp.dot(vw_ref[0, kv, win_far], jnp.where(causal, zero, p_edge), preferred_element_type=F32)
               + jnp.dot(vw_ref[0, kv, win_mid], p_mid, preferred_element_type=F32))
        owin_sc[h] = acc[:HEAD_DIM] / acc[HEAD_DIM:HEAD_DIM + 1]

    def sel_pair(first):
        sel_scores(first + 1, sb_sc)
        sel_softmax(first, sa_sc)
        sel_scores(first + 2, sa_sc)
        sel_softmax(first + 1, sb_sc)

    def sel_quad(t, carry):
        sel_pair(4 * t)
        sel_pair(4 * t + 2)
        return carry

    lax.fori_loop(0, lax.shift_right_logical(qi, 2), sel_quad, 0)

    @pl.when((qi & 2) != 0)
    def _():
        sel_pair(qi & ~3)

    def finish():
        gates_t = [gate_ref[kv].T for kv in kvs]
        outs = []
        for h in heads:
            kv, hh = divmod(h, Q_PER_KV)
            g = gates_t[kv]
            c = hh * N_NSA_BRANCH
            acc = acc_sc[h]
            o_sel = acc[:HEAD_DIM] / acc[HEAD_DIM:HEAD_DIM + 1]
            o_cmp = ocmp_ref[0, kv, hh * HEAD_DIM:(hh + 1) * HEAD_DIM, :]
            outs.append(g[c:c + 1] * o_cmp + g[c + 1:c + 2] * o_sel + g[c + 2:c + 3] * owin_sc[h])
        o_ref[0] = jnp.concatenate(outs, axis=0).T.astype(BF16)

    @pl.when((qi & 1) == 1)
    def _():
        sel_scores(qi, sb_sc)
        sel_softmax(qi - 1, sa_sc)
        sel_softmax(qi, sb_sc, diagonal=True)
        finish()

    @pl.when((qi & 1) == 0)
    def _():
        sel_softmax(qi, sa_sc, diagonal=True)
        finish()


def _attend(q_rot_t, bias, o_cmp_t, k_sel, v_sel_t, k_win, v_win_t, gates):
    assert WINDOW == 2 * NSA_Q
    assert NSA_Q % SEL_LEN == 0 and HEAD_DIM + N_SEL == LANES
    nq = SEQ // NSA_Q
    n_heads = KV_PER_STEP * Q_PER_KV
    step = lambda rows: pl.BlockSpec((1, KV_PER_STEP, rows, NSA_Q), lambda b, k, i: (b, k, 0, i))
    whole = lambda *tile: pl.BlockSpec((1, KV_PER_STEP) + tile, lambda b, k, i: (b, k) + (0,) * len(tile))
    return pl.pallas_call(
        _attend_kernel,
        grid=(BATCH, N_KV_HEADS // KV_PER_STEP, nq),
        in_specs=[
            pl.BlockSpec((1, n_heads, HEAD_DIM, NSA_Q), lambda b, k, i: (b, k, 0, i)),
            step(N_SEL), step(Q_PER_KV * HEAD_DIM),
            whole(SEQ, LANES), whole(nq, V_ROWS, NSA_Q),
            whole(SEQ, LANES), whole(nq, V_ROWS, NSA_Q),
            pl.BlockSpec((KV_PER_STEP, NSA_Q, LANES), lambda b, k, i: (k, b * nq + i, 0)),
        ],
        out_specs=pl.BlockSpec((1, NSA_Q, n_heads * HEAD_DIM), lambda b, k, i: (b, i, k)),
        out_shape=jax.ShapeDtypeStruct((BATCH, SEQ, N_HEADS * HEAD_DIM), BF16),
        scratch_shapes=[pltpu.VMEM((n_heads, 1, NSA_Q), F32),
                        pltpu.VMEM((n_heads, V_ROWS, NSA_Q), F32),
                        pltpu.VMEM((n_heads, NSA_Q, NSA_Q), F32),
                        pltpu.VMEM((n_heads, NSA_Q, NSA_Q), F32),
                        pltpu.VMEM((n_heads, HEAD_DIM, NSA_Q), F32)],
        compiler_params=pltpu.CompilerParams(
            dimension_semantics=("parallel", "parallel", "arbitrary"),
            vmem_limit_bytes=VMEM_LIMIT),
        name="attend",
    )(q_rot_t, bias, o_cmp_t, k_sel, v_sel_t, k_win, v_win_t, gates)


def _tail_kernel(h_ref, gu_ref, vn_ref, ob_ref, ga_ref, gb_ref, ws_ref, bs_ref,
                 wa_ref, wb_ref, wo_ref, gn_ref, wi_ref, wf_ref, p_ref, gp_ref, wg_ref, wp_ref,
                 gf_ref, o_ref, *, tm):
    r = lax.broadcasted_iota(jnp.int32, (GM_CHUNK, GM_CHUNK), 0)
    c = lax.broadcasted_iota(jnp.int32, (GM_CHUNK, GM_CHUNK), 1)
    w_tril = [jnp.where(c <= r, ws_ref[g], 0.0).astype(BF16) for g in range(GM_GROUPS)]
    bias = bs_ref[...]
    rows = []
    for ch in range(tm // GM_CHUNK):
        rs = slice(ch * GM_CHUNK, (ch + 1) * GM_CHUNK)
        mix = jnp.concatenate(
            [jnp.dot(w_tril[g], vn_ref[rs, g * LANES:(g + 1) * LANES], preferred_element_type=F32)
             for g in range(GM_GROUPS)], axis=1)
        rows.append(gu_ref[rs, :].astype(F32) * (mix + bias))
    z = jnp.concatenate(rows, axis=0).astype(BF16)
    y_a = jnp.dot(z, wa_ref[...], preferred_element_type=F32)
    y_b = jnp.dot(ob_ref[...], wb_ref[...], preferred_element_type=F32)
    merged = (ga_ref[...].astype(F32) * y_a + gb_ref[...].astype(F32) * y_b).astype(BF16)
    h = h_ref[...] + jnp.dot(merged, wo_ref[...], preferred_element_type=F32)

    h = _half_step_ffn(h, gn_ref[...], wi_ref, wf_ref)

    gate = jax.nn.sigmoid(jnp.dot(_rms(h, gp_ref[...]).astype(BF16), wg_ref[...],
                                  preferred_element_type=F32))
    proj = jnp.dot(p_ref[...].astype(BF16), wp_ref[...], preferred_element_type=F32)
    o_ref[...] = _rms(h + gate * proj, gf_ref[...])


def _tail(h, gu, vn, o_b, ga, gb, w_s, b_s_exp, w_a, w_b, w_o, ffn_norm, ffn_w_in, ffn_w_out,
          p, ple_norm, ple_w_gate, ple_w_proj, final_norm, *, tm=512):
    row = pl.BlockSpec((tm, D_MODEL), lambda i: (i, 0))
    square = _const_spec((D_MODEL, D_MODEL))
    gain = _const_spec((1, D_MODEL))
    return pl.pallas_call(
        functools.partial(_tail_kernel, tm=tm),
        grid=(TOKENS // tm,),
        in_specs=[row, row, row, row, row, row,
                  _const_spec((GM_GROUPS, GM_CHUNK, GM_CHUNK)),
                  _const_spec((GM_CHUNK, GM_WIDTH)),
                  square, square, square,
                  gain, _const_spec((D_MODEL, 2 * D_FF)), _const_spec((D_FF, D_MODEL)),
                  pl.BlockSpec((tm, PLE_DIM), lambda i: (i, 0)), gain, square,
                  _const_spec((PLE_DIM, D_MODEL)), gain],
        out_specs=row,
        out_shape=jax.ShapeDtypeStruct((TOKENS, D_MODEL), F32),
        compiler_params=pltpu.CompilerParams(
            dimension_semantics=("parallel",), vmem_limit_bytes=VMEM_LIMIT),
        name="tail",
    )(h, gu, vn, o_b, ga, gb, w_s, b_s_exp, w_a, w_b, w_o, ffn_norm, ffn_w_in, ffn_w_out,
      p, ple_norm, ple_w_gate, ple_w_proj, final_norm)


def _rope_tables():
    inv_freq = ROPE_THETA ** (-jnp.arange(0, ROPE_DIM, 2, dtype=jnp.float32) / ROPE_DIM)
    ang = jnp.arange(SEQ).astype(jnp.float32)[:, None] * inv_freq[None, :]
    cos, sin = jnp.cos(ang), jnp.sin(ang)
    zero = jnp.zeros_like(cos)
    rest = HEAD_DIM - ROPE_DIM
    c = jnp.concatenate([cos, cos, jnp.ones((SEQ, rest), F32)], axis=1)
    sa = jnp.concatenate([zero, sin, jnp.zeros((SEQ, rest), F32)], axis=1)
    sb = jnp.concatenate([-sin, zero, jnp.zeros((SEQ, rest), F32)], axis=1)
    return [jnp.tile(t, (1, LANES // HEAD_DIM)) for t in (c, sa, sb)]


def kernel(x, p, ffn1_norm, ffn1_w_in, ffn1_w_out, mix_norm, w_in, gm_ln_g, gm_ln_b, gm_w_s, gm_b_s,
           w_branch_a, cmp_pos_k, cmp_k_w1, cmp_k_w2, cmp_pos_v, cmp_v_w1, cmp_v_w2, w_branch_b, w_out,
           ffn2_norm, ffn2_w_in, ffn2_w_out, ple_norm, ple_w_gate, ple_w_proj, final_norm):
    assert x.shape == (BATCH, SEQ, D_MODEL) and p.shape == (1, BATCH, SEQ, PLE_DIM)
    row = lambda a: a.reshape(1, -1)
    h = x.reshape(TOKENS, D_MODEL)

    h = _ffn(h, row(ffn1_norm[0]), ffn1_w_in[0].astype(BF16), ffn1_w_out[0].astype(BF16))

    rope_c, rope_sa, rope_sb = _rope_tables()
    (gu, vn, q_raw, q_rot, k_c, v_c, k_sel, v_sel, k_win, v_win, gates, g_a, g_b) = _inproj(
        h, row(mix_norm[0]), w_in[0].T.astype(BF16), row(gm_ln_g[0]), row(gm_ln_b[0]),
        rope_c, rope_sa, rope_sb)

    half = CMP_STRIDE * HEAD_DIM

    def compress(x_heads, pos, w1, w2, feature_major):
        w2 = (w2.T if feature_major else w2).astype(BF16)
        return _compress(x_heads, pos[:CMP_STRIDE].reshape(1, half), pos[CMP_STRIDE:].reshape(1, half),
                         w1[:half].astype(BF16), w1[half:].astype(BF16), w2, feature_major=feature_major)

    k_cmp = compress(k_c, cmp_pos_k[0], cmp_k_w1[0], cmp_k_w2[0], False)
    v_cmp = compress(v_c, cmp_pos_v[0], cmp_v_w1[0], cmp_v_w2[0], True)

    sel_bias, o_cmp = _select(q_raw, k_cmp, v_cmp)
    o_b = _attend(q_rot, sel_bias, o_cmp, k_sel, v_sel, k_win, v_win, gates)

    b_s_exp = jnp.repeat(gm_b_s[0].T, GM_WIDTH // GM_GROUPS, axis=1)
    out = _tail(h, gu, vn, o_b.reshape(TOKENS, N_HEADS * HEAD_DIM), g_a, g_b, gm_w_s[0], b_s_exp,
                w_branch_a[0].astype(BF16), w_branch_b[0].astype(BF16), w_out[0].astype(BF16),
                row(ffn2_norm[0]), ffn2_w_in[0].astype(BF16), ffn2_w_out[0].astype(BF16),
                p[0].reshape(TOKENS, PLE_DIM), row(ple_norm[0]), ple_w_gate[0].astype(BF16),
                ple_w_proj[0].astype(BF16), row(final_norm))
    return out.reshape(BATCH, SEQ, D_MODEL)
```

```python
import functools

import jax
import jax.numpy as jnp
from jax import lax
from jax.experimental import pallas as pl
from jax.experimental.pallas import tpu as pltpu

D_MODEL = 1024
BATCH = 4
SEQ = 4096
PLE_DIM = 256
D_FF = 2816
NORM_EPS = 1e-6
GM_WIDTH = 1024
GM_GROUPS = 8
GM_CHUNK = 128
N_HEADS = 16
N_KV_HEADS = 4
HEAD_DIM = 64
Q_PER_KV = N_HEADS // N_KV_HEADS
KV_WIDTH = N_KV_HEADS * HEAD_DIM
ROPE_DIM = HEAD_DIM // 4
ROPE_HALF = ROPE_DIM // 2
ROPE_THETA = 500000.0
CMP_LEN = 32
CMP_STRIDE = 16
CMP_HIDDEN = 256
SEL_LEN = 64
SEL_TOP = 16
WINDOW = 512
N_NSA_BRANCH = 3
MASK_VALUE = -1e30
FORCE_SCORE = 1e9
LOG2_E = 1.4426950408889634

TOKENS = BATCH * SEQ
N_CHUNKS = SEQ // CMP_STRIDE
N_SEL = SEQ // SEL_LEN
LANES = 128
SUBLANES = 8
VMEM_LIMIT = 56 * 1024 * 1024

OFF_U = 0
OFF_V = OFF_U + GM_WIDTH
OFF_Q = OFF_V + GM_WIDTH
OFF_KV = OFF_Q + N_HEADS * HEAD_DIM
OFF_NSA_GATE = OFF_KV + 6 * KV_WIDTH
OFF_MERGE = OFF_NSA_GATE + N_HEADS * N_NSA_BRANCH
SEL_SHIFT = SEL_LEN.bit_length() - 1

F32 = jnp.float32
BF16 = jnp.bfloat16

NT_DIMS = (((1,), (1,)), ((), ()))


def _const_spec(shape):
    nd = len(shape)
    return pl.BlockSpec(shape, lambda *_: (0,) * nd, pipeline_mode=pl.Buffered(1))


def _rms(x, g):
    return x * lax.rsqrt(jnp.mean(x * x, axis=-1, keepdims=True) + NORM_EPS) * g


MXU_WIDTH = 256


def _half_step_ffn(x, g, wi_ref, wo_ref):
    xn = _rms(x, g).astype(BF16)
    acc = None
    for c in range(D_FF // MXU_WIDTH):
        lo = c * MXU_WIDTH
        gate = jnp.dot(xn, wi_ref[:, lo:lo + MXU_WIDTH], preferred_element_type=F32)
        up = jnp.dot(xn, wi_ref[:, D_FF + lo:D_FF + lo + MXU_WIDTH], preferred_element_type=F32)
        act = (gate * jax.nn.sigmoid(gate) * up).astype(BF16)
        part = jnp.dot(act, wo_ref[lo:lo + MXU_WIDTH, :], preferred_element_type=F32)
        acc = part if acc is None else acc + part
    return x + 0.5 * acc


def _ffn_kernel(x_ref, g_ref, wi_ref, wo_ref, o_ref):
    o_ref[...] = _half_step_ffn(x_ref[...], g_ref[...], wi_ref, wo_ref)


def _ffn(x, g, w_in, w_out, *, tm=512):
    row = pl.BlockSpec((tm, D_MODEL), lambda i: (i, 0))
    return pl.pallas_call(
        _ffn_kernel,
        grid=(TOKENS // tm,),
        in_specs=[row, _const_spec((1, D_MODEL)), _const_spec((D_MODEL, 2 * D_FF)),
                  _const_spec((D_FF, D_MODEL))],
        out_specs=row,
        out_shape=jax.ShapeDtypeStruct((TOKENS, D_MODEL), F32),
        compiler_params=pltpu.CompilerParams(
            dimension_semantics=("parallel",), vmem_limit_bytes=VMEM_LIMIT),
        name="ffn",
    )(x, g, w_in, w_out)


NSA_Q = 256
BF16_ROWS = 16
V_ROWS = HEAD_DIM + BF16_ROWS
INPROJ_KEY_TILES = 2


def _rope(x, c, sa, sb):
    w = x.shape[1]
    return x * c + pltpu.roll(x, ROPE_HALF, 1) * sa + pltpu.roll(x, w - ROPE_HALF, 1) * sb


def _inproj_kernel(h_ref, g_ref, wt_ref, lng_ref, lnb_ref, c_ref, sa_ref, sb_ref,
                   gu_ref, vn_ref, qraw_ref, qrot_ref, kc_ref, vc_ref, ks_ref, vs_ref,
                   kw_ref, vw_ref, gate_ref, ga_ref, gb_ref, *, tm):
    n = _rms(h_ref[...], g_ref[...]).astype(BF16)

    def seg(lo, width):
        return lax.dot_general(n, wt_ref[lo:lo + width, :], NT_DIMS, preferred_element_type=F32)

    gu_ref[...] = jax.nn.gelu(seg(OFF_U, GM_WIDTH)).astype(BF16)
    v = jax.nn.gelu(seg(OFF_V, GM_WIDTH))
    mu = jnp.mean(v, axis=-1, keepdims=True)
    vc = v - mu
    var = jnp.mean(vc * vc, axis=-1, keepdims=True)
    vn_ref[...] = (vc * lax.rsqrt(var + NORM_EPS) * lng_ref[...] + lnb_ref[...]).astype(BF16)

    c, sa, sb = c_ref[...], sa_ref[...], sb_ref[...]

    def tile_lanes(t, reps):
        return jnp.concatenate([t] * reps, axis=1)

    q = seg(OFF_Q, N_HEADS * HEAD_DIM) * (HEAD_DIM ** -0.5 * LOG2_E)
    reps = N_HEADS * HEAD_DIM // LANES
    q_rot = _rope(q, tile_lanes(c, reps), tile_lanes(sa, reps), tile_lanes(sb, reps))
    qraw_ref[0] = q.T.astype(BF16).reshape(N_HEADS, HEAD_DIM, tm)
    qrot_ref[0] = q_rot.T.astype(BF16).reshape(N_HEADS, HEAD_DIM, tm)

    kv = seg(OFF_KV, 6 * KV_WIDTH)
    k_c, v_c, k_s, v_s, k_w, v_w = [kv[:, i * KV_WIDTH:(i + 1) * KV_WIDTH] for i in range(6)]
    reps = KV_WIDTH // LANES
    ck, sak, sbk = tile_lanes(c, reps), tile_lanes(sa, reps), tile_lanes(sb, reps)
    k_s = _rope(k_s, ck, sak, sbk)
    k_w = _rope(k_w, ck, sak, sbk)
    pos = (pl.program_id(0) % (SEQ // tm)) * tm + lax.broadcasted_iota(jnp.int32, (tm, N_SEL), 0)
    blk = lax.broadcasted_iota(jnp.int32, (tm, N_SEL), 1)
    onehot = jnp.where(lax.shift_right_logical(pos, SEL_SHIFT) == blk, 1.0, 0.0).astype(F32)
    ones_t = jnp.ones((V_ROWS - HEAD_DIM, NSA_Q), F32)
    ones_col = jnp.where(lax.broadcasted_iota(jnp.int32, (tm, HEAD_DIM), 1) == 0, 1.0, 0.0)
    v_s_t, v_w_t = v_s.T, v_w.T
    for h in range(N_KV_HEADS):
        sl = slice(h * HEAD_DIM, (h + 1) * HEAD_DIM)
        kc_ref[0, h] = k_c[:, sl]
        vc_ref[0, h] = v_c[:, sl]
        ks_ref[0, h] = jnp.concatenate([k_s[:, sl], onehot], axis=1).astype(BF16)
        kw_ref[0, h] = jnp.concatenate([k_w[:, sl], ones_col], axis=1).astype(BF16)
        for t in range(tm // NSA_Q):
            keys = slice(t * NSA_Q, (t + 1) * NSA_Q)
            vs_ref[0, h, t] = jnp.concatenate([v_s_t[sl, keys], ones_t], axis=0).astype(BF16)
            vw_ref[0, h, t] = jnp.concatenate([v_w_t[sl, keys], ones_t], axis=0).astype(BF16)

    merge = jax.nn.sigmoid(seg(OFF_MERGE, 2 * D_MODEL))
    ga_ref[...] = merge[:, :D_MODEL].astype(BF16)
    gb_ref[...] = merge[:, D_MODEL:].astype(BF16)
    gates = jax.nn.sigmoid(seg(OFF_NSA_GATE, LANES))
    group_cols = Q_PER_KV * N_NSA_BRANCH
    for h in range(N_KV_HEADS):
        gate_ref[h] = gates if h == 0 else pltpu.roll(gates, LANES - h * group_cols, 1)


def _inproj(h, g, w_in_t, ln_g, ln_b, rope_c, rope_sa, rope_sb):
    tm = INPROJ_KEY_TILES * NSA_Q
    nq = SEQ // tm
    row = lambda i: (i, 0)
    head = lambda i: (i // nq, 0, i % nq, 0)
    tok_bf = jax.ShapeDtypeStruct((TOKENS, D_MODEL), BF16)

    def head_shape(nh, width, dtype):
        return jax.ShapeDtypeStruct((BATCH, nh, SEQ, width), dtype)

    def head_spec(nh, width):
        return pl.BlockSpec((1, nh, tm, width), head)

    qt_spec = pl.BlockSpec((1, N_HEADS, HEAD_DIM, tm), lambda i: (i // nq, 0, 0, i % nq))
    qt_shape = jax.ShapeDtypeStruct((BATCH, N_HEADS, HEAD_DIM, SEQ), BF16)
    vt_spec = pl.BlockSpec((1, N_KV_HEADS, INPROJ_KEY_TILES, V_ROWS, NSA_Q),
                           lambda i: (i // nq, 0, i % nq, 0, 0))
    vt_shape = jax.ShapeDtypeStruct((BATCH, N_KV_HEADS, SEQ // NSA_Q, V_ROWS, NSA_Q), BF16)

    return pl.pallas_call(
        functools.partial(_inproj_kernel, tm=tm),
        grid=(TOKENS // tm,),
        in_specs=[
            pl.BlockSpec((tm, D_MODEL), row),
            _const_spec((1, D_MODEL)),
            _const_spec(w_in_t.shape),
            _const_spec((1, GM_WIDTH)),
            _const_spec((1, GM_WIDTH)),
            pl.BlockSpec((tm, LANES), lambda i: (i % nq, 0)),
            pl.BlockSpec((tm, LANES), lambda i: (i % nq, 0)),
            pl.BlockSpec((tm, LANES), lambda i: (i % nq, 0)),
        ],
        out_specs=[
            pl.BlockSpec((tm, GM_WIDTH), row),
            pl.BlockSpec((tm, GM_WIDTH), row),
            qt_spec,
            qt_spec,
            head_spec(N_KV_HEADS, HEAD_DIM),
            head_spec(N_KV_HEADS, HEAD_DIM),
            head_spec(N_KV_HEADS, LANES),
            vt_spec,
            head_spec(N_KV_HEADS, LANES),
            vt_spec,
            pl.BlockSpec((N_KV_HEADS, tm, LANES), lambda i: (0, i, 0)),
            pl.BlockSpec((tm, D_MODEL), row),
            pl.BlockSpec((tm, D_MODEL), row),
        ],
        out_shape=[
            tok_bf, tok_bf,
            qt_shape, qt_shape,
            head_shape(N_KV_HEADS, HEAD_DIM, F32), head_shape(N_KV_HEADS, HEAD_DIM, F32),
            head_shape(N_KV_HEADS, LANES, BF16), vt_shape,
            head_shape(N_KV_HEADS, LANES, BF16), vt_shape,
            jax.ShapeDtypeStruct((N_KV_HEADS, TOKENS, LANES), F32),
            tok_bf, tok_bf,
        ],
        compiler_params=pltpu.CompilerParams(
            dimension_semantics=("parallel",), vmem_limit_bytes=VMEM_LIMIT),
        name="inproj",
    )(h, g, w_in_t, ln_g, ln_b, rope_c, rope_sa, rope_sb)


def _compress_kernel(x_ref, ptop_ref, pbot_ref, w1t_ref, w1b_ref, w2_ref, o_ref, *, feature_major):
    kv_heads = range(N_KV_HEADS)
    x = jnp.concatenate(
        [jnp.concatenate([x_ref[0, h, pl.ds(j, N_CHUNKS, stride=CMP_STRIDE), :]
                          for j in range(CMP_STRIDE)], axis=1) for h in kv_heads], axis=0)
    top = jnp.dot((x + ptop_ref[...]).astype(BF16), w1t_ref[...], preferred_element_type=F32)
    bot = jnp.dot((x + pbot_ref[...]).astype(BF16), w1b_ref[...], preferred_element_type=F32)
    rows = [slice(h * N_CHUNKS, (h + 1) * N_CHUNKS) for h in kv_heads]
    bot_next = jnp.concatenate([pltpu.roll(bot[r], N_CHUNKS - 1, 0) for r in rows], axis=0)
    hidden = jax.nn.gelu(top + bot_next).astype(BF16)
    if feature_major:
        out = lax.dot_general(w2_ref[...], hidden, NT_DIMS, preferred_element_type=F32)
        for h in kv_heads:
            o_ref[0, h] = out[:, rows[h]].astype(BF16)
    else:
        out = jnp.dot(hidden, w2_ref[...], preferred_element_type=F32)
        for h in kv_heads:
            o_ref[0, h] = out[rows[h]].astype(BF16)


def _compress(x_heads, pos_top, pos_bot, w1_top, w1_bot, w2, *, feature_major):
    half = CMP_STRIDE * HEAD_DIM
    out_tile = (HEAD_DIM, N_CHUNKS) if feature_major else (N_CHUNKS, HEAD_DIM)
    return pl.pallas_call(
        functools.partial(_compress_kernel, feature_major=feature_major),
        grid=(BATCH,),
        in_specs=[
            pl.BlockSpec((1, N_KV_HEADS, SEQ, HEAD_DIM), lambda b: (b, 0, 0, 0)),
            _const_spec((1, half)),
            _const_spec((1, half)),
            _const_spec((half, CMP_HIDDEN)),
            _const_spec((half, CMP_HIDDEN)),
            _const_spec(w2.shape),
        ],
        out_specs=pl.BlockSpec((1, N_KV_HEADS) + out_tile, lambda b: (b, 0, 0, 0)),
        out_shape=jax.ShapeDtypeStruct((BATCH, N_KV_HEADS) + out_tile, BF16),
        compiler_params=pltpu.CompilerParams(
            dimension_semantics=("parallel",), vmem_limit_bytes=VMEM_LIMIT),
        name="compress",
    )(x_heads, pos_top, pos_bot, w1_top, w1_bot, w2)


RANK_ACCUMULATORS = 4
SELECT_Q = 1024


def _selection_bias(score, n_blocks):
    n_q = score.shape[1]
    sub_iota = lax.broadcasted_iota(jnp.int32, (SUBLANES, n_q), 0)
    groups = [score[g * SUBLANES:(g + 1) * SUBLANES] for g in range(n_blocks // SUBLANES)]
    counts = [[None] * RANK_ACCUMULATORS for _ in groups]
    for jp in range(n_blocks):
        other = score[jp:jp + 1, :]
        for g, grp in enumerate(groups):
            lo = g * SUBLANES
            if lo > jp:
                before = other >= grp
            elif lo + SUBLANES - 1 <= jp:
                before = other > grp
            else:
                before = (other > grp) | ((other == grp) & (sub_iota + lo > jp))
            inc = jnp.where(before, 1, 0)
            a = jp % RANK_ACCUMULATORS
            counts[g][a] = inc if counts[g][a] is None else counts[g][a] + inc
    rank = jnp.concatenate([(c[0] + c[1]) + (c[2] + c[3]) for c in counts], axis=0)
    bias = jnp.where(rank < SEL_TOP, 0.0, MASK_VALUE)
    if n_blocks < N_SEL:
        bias = jnp.concatenate([bias, jnp.zeros((N_SEL - n_blocks, n_q), F32)], axis=0)
    return bias


def _select_kernel(qraw_ref, kcmp_ref, vcmp_ref, bias_ref, ocmp_ref, psum_sc):
    qi = pl.program_id(2)
    s0 = qi * SELECT_Q
    key_i = lax.broadcasted_iota(jnp.int32, (N_CHUNKS, SELECT_Q), 0)
    qry_i = lax.broadcasted_iota(jnp.int32, (N_CHUNKS, SELECT_Q), 1)
    heads = range(Q_PER_KV)

    k_cmp = kcmp_ref[0, 0]
    v_cmp_t = vcmp_ref[0, 0]
    s_cmp = [jnp.dot(k_cmp, qraw_ref[0, h], preferred_element_type=F32) for h in heads]
    cmp_valid = key_i * CMP_STRIDE + (CMP_LEN - 1) <= s0 + qry_i
    any_valid = s0 + lax.broadcasted_iota(jnp.int32, (1, SELECT_Q), 1) >= CMP_LEN - 1
    p_cmp = []
    for h in heads:
        s_c = jnp.where(cmp_valid, s_cmp[h], MASK_VALUE)
        e_c = jnp.exp2(s_c - jnp.max(s_c, axis=0, keepdims=True))
        p_cmp.append(e_c * jnp.where(any_valid, 1.0 / jnp.sum(e_c, axis=0, keepdims=True), 0.0))
    for h in heads:
        ocmp_ref[0, 0, h * HEAD_DIM:(h + 1) * HEAD_DIM, :] = jnp.dot(
            v_cmp_t, p_cmp[h].astype(BF16), preferred_element_type=F32)

    p_sum = (p_cmp[0] + p_cmp[1]) + (p_cmp[2] + p_cmp[3])
    lane_tiles = range(SELECT_Q // LANES)
    for t in lane_tiles:
        psum_sc[t] = p_sum[:, t * LANES:(t + 1) * LANES]
    per_sel = SEL_LEN // CMP_STRIDE
    terms = [jnp.concatenate([psum_sc[t, pl.ds(r, N_SEL, stride=per_sel), :] for t in lane_tiles], axis=1)
             for r in range(per_sel)]
    blk = lax.broadcasted_iota(jnp.int32, (N_SEL, SELECT_Q), 0)
    before = jnp.where(blk == 0, 0.0, pltpu.roll(terms[per_sel - 1], 1, 0))
    imp_t = (before + terms[0]) + (terms[1] + terms[2]) + terms[3]
    cur = lax.shift_right_logical(s0 + lax.broadcasted_iota(jnp.int32, (N_SEL, SELECT_Q), 1), SEL_SHIFT)
    forced = (blk == 0) | (blk == cur) | (blk == cur - 1)
    score = jnp.where(forced, FORCE_SCORE, jnp.where(blk > cur, -FORCE_SCORE, imp_t))

    blocks_per_tile = NSA_Q // SEL_LEN
    for variant in range(SEQ // SELECT_Q):
        @pl.when(qi == variant)
        def _():
            for t in range(SELECT_Q // NSA_Q):
                lanes = slice(t * NSA_Q, (t + 1) * NSA_Q)
                reachable = (variant * (SELECT_Q // NSA_Q) + t + 1) * blocks_per_tile
                n_blocks = min(N_SEL, -(-reachable // SUBLANES) * SUBLANES)
                bias_ref[0, 0, :, lanes] = _selection_bias(score[:, lanes], n_blocks).astype(BF16)


def _select(q_raw_t, k_cmp, v_cmp_t):
    assert CMP_LEN == 2 * CMP_STRIDE
    nq = SEQ // SELECT_Q
    step = lambda rows: pl.BlockSpec((1, 1, rows, SELECT_Q), lambda b, k, i: (b, k, 0, i))
    whole = lambda *tile: pl.BlockSpec((1, 1) + tile, lambda b, k, i: (b, k) + (0,) * len(tile))
    return pl.pallas_call(
        _select_kernel,
        grid=(BATCH, N_KV_HEADS, nq),
        in_specs=[
            pl.BlockSpec((1, Q_PER_KV, HEAD_DIM, SELECT_Q), lambda b, k, i: (b, k, 0, i)),
            whole(N_CHUNKS, HEAD_DIM), whole(HEAD_DIM, N_CHUNKS),
        ],
        out_specs=[step(N_SEL), step(Q_PER_KV * HEAD_DIM)],
        out_shape=[jax.ShapeDtypeStruct((BATCH, N_KV_HEADS, N_SEL, SEQ), BF16),
                   jax.ShapeDtypeStruct((BATCH, N_KV_HEADS, Q_PER_KV * HEAD_DIM, SEQ), F32)],
        scratch_shapes=[pltpu.VMEM((SELECT_Q // LANES, N_CHUNKS, LANES), F32)],
        compiler_params=pltpu.CompilerParams(
            dimension_semantics=("parallel", "parallel", "parallel"), vmem_limit_bytes=VMEM_LIMIT),
        name="select",
    )(q_raw_t, k_cmp, v_cmp_t)


KV_PER_STEP = 2


def _attend_kernel(qrot_ref, bias_ref, ocmp_ref, ks_ref, vs_ref, kw_ref, vw_ref, gate_ref,
                   o_ref, m_sc, acc_sc, sa_sc, sb_sc, owin_sc):
    qi = pl.program_id(2)
    key_i = lax.broadcasted_iota(jnp.int32, (NSA_Q, NSA_Q), 0)
    qry_i = lax.broadcasted_iota(jnp.int32, (NSA_Q, NSA_Q), 1)
    causal = key_i <= qry_i
    kvs = range(KV_PER_STEP)
    heads = range(KV_PER_STEP * Q_PER_KV)

    def scores(k, q_t):
        return jnp.dot(k, q_t, preferred_element_type=F32)

    def softmax_step(h, s, v_t):
        m_prev = m_sc[h]
        m_new = jnp.maximum(m_prev, jnp.max(s, axis=0, keepdims=True))
        p = jnp.exp2(s - m_new).astype(BF16)
        acc_sc[h] = jnp.exp2(m_prev - m_new) * acc_sc[h] + jnp.dot(v_t, p, preferred_element_type=F32)
        m_sc[h] = m_new

    q_sel = [jnp.concatenate([qrot_ref[0, h], bias_ref[0, h // Q_PER_KV]], axis=0) for h in heads]
    m_sc[...] = jnp.full_like(m_sc, MASK_VALUE)
    acc_sc[...] = jnp.zeros_like(acc_sc)

    def sel_scores(kt, dst, group=None):
        rows = pl.ds(pl.multiple_of(kt * NSA_Q, NSA_Q), NSA_Q)
        for kv in kvs if group is None else (group,):
            k = ks_ref[0, kv, rows, :]
            for h in range(kv * Q_PER_KV, (kv + 1) * Q_PER_KV):
                dst[h] = scores(k, q_sel[h])

    def sel_softmax(kt, src, diagonal=False, group=None):
        for kv in kvs if group is None else (group,):
            v_t = vs_ref[0, kv, kt]
            for h in range(kv * Q_PER_KV, (kv + 1) * Q_PER_KV):
                s = src[h]
                softmax_step(h, jnp.where(causal, s, MASK_VALUE) if diagonal else s, v_t)

    def window_keys(kv, kt):
        return kw_ref[0, kv, pl.ds(pl.multiple_of(kt * NSA_Q, NSA_Q), NSA_Q), :]

    def penalty_rows(penalty):
        rows = jnp.where(lax.broadcasted_iota(jnp.int32, (HEAD_DIM, NSA_Q), 0) == 0, penalty, 0.0)
        return rows.astype(BF16)

    win_mid, win_far = jnp.maximum(qi - 1, 0), jnp.maximum(qi - 2, 0)
    k_diag = [window_keys(kv, qi)[:, 0:HEAD_DIM] for kv in kvs]
    k_mid = [window_keys(kv, win_mid) for kv in kvs]
    k_far = [window_keys(kv, win_far) for kv in kvs]
    pen_mid = penalty_rows(jnp.where(qi >= 1, 0.0, MASK_VALUE))
    pen_far = penalty_rows(jnp.where(qi >= 2, 0.0, MASK_VALUE))
    s_win_diag, s_win_mid, s_win_far = [], [], []
    for h in heads:
        q_t = qrot_ref[0, h]
        kv = h // Q_PER_KV
        s_win_diag.append(scores(k_diag[kv], q_t))
        s_win_mid.append(scores(k_mid[kv], jnp.concatenate([q_t, pen_mid], axis=0)))
        s_win_far.append(scores(k_far[kv], jnp.concatenate([q_t, pen_far], axis=0)))

    sel_scores(0, sa_sc)

    for h in heads:
        s_edge = jnp.where(causal, s_win_diag[h], s_win_far[h])
        s_mid = s_win_mid[h]
        m_w = jnp.max(jnp.maximum(s_edge, s_mid), axis=0, keepdims=True)
        p_edge = jnp.exp2(s_edge - m_w).astype(BF16)
        p_mid = jnp.exp2(s_mid - m_w).astype(BF16)
        zero = jnp.zeros_like(p_edge)
        kv = h // Q_PER_KV
        acc = (jnp.dot(vw_ref[0, kv, qi], jnp.where(causal, p_edge, zero), preferred_element_type=F32)
               + jnp.dot(vw_ref[0, kv, win_far], jnp.where(causal, zero, p_edge), preferred_element_type=F32)
               + jnp.dot(vw_ref[0, kv, win_mid], p_mid, preferred_element_type=F32))
        owin_sc[h] = acc[:HEAD_DIM] / acc[HEAD_DIM:HEAD_DIM + 1]

    def sel_pair(first):
        for kv in kvs:
            sel_scores(first + 1, sb_sc, group=kv)
            sel_softmax(first, sa_sc, group=kv)
        for kv in kvs:
            sel_scores(first + 2, sa_sc, group=kv)
            sel_softmax(first + 1, sb_sc, group=kv)

    def sel_quad(t, carry):
        sel_pair(4 * t)
        sel_pair(4 * t + 2)
        return carry

    lax.fori_loop(0, lax.shift_right_logical(qi, 2), sel_quad, 0)

    @pl.when((qi & 2) != 0)
    def _():
        sel_pair(qi & ~3)

    def finish():
        gates_t = [gate_ref[kv].T for kv in kvs]
        outs = []
        for h in heads:
            kv, hh = divmod(h, Q_PER_KV)
            g = gates_t[kv]
            c = hh * N_NSA_BRANCH
            acc = acc_sc[h]
            o_sel = acc[:HEAD_DIM] / acc[HEAD_DIM:HEAD_DIM + 1]
            o_cmp = ocmp_ref[0, kv, hh * HEAD_DIM:(hh + 1) * HEAD_DIM, :]
            outs.append(g[c:c + 1] * o_cmp + g[c + 1:c + 2] * o_sel + g[c + 2:c + 3] * owin_sc[h])
        o_ref[0] = jnp.concatenate(outs, axis=0).T.astype(BF16)

    @pl.when((qi & 1) == 1)
    def _():
        for kv in kvs:
            sel_scores(qi, sb_sc, group=kv)
            sel_softmax(qi - 1, sa_sc, group=kv)
        sel_softmax(qi, sb_sc, diagonal=True)
        finish()

    @pl.when((qi & 1) == 0)
    def _():
        sel_softmax(qi, sa_sc, diagonal=True)
        finish()


def _attend(q_rot_t, bias, o_cmp_t, k_sel, v_sel_t, k_win, v_win_t, gates):
    assert WINDOW == 2 * NSA_Q
    assert NSA_Q % SEL_LEN == 0 and HEAD_DIM + N_SEL == LANES
    nq = SEQ // NSA_Q
    n_heads = KV_PER_STEP * Q_PER_KV
    step = lambda rows: pl.BlockSpec((1, KV_PER_STEP, rows, NSA_Q), lambda b, k, i: (b, k, 0, i))
    whole = lambda *tile: pl.BlockSpec((1, KV_PER_STEP) + tile, lambda b, k, i: (b, k) + (0,) * len(tile))
    return pl.pallas_call(
        _attend_kernel,
        grid=(BATCH, N_KV_HEADS // KV_PER_STEP, nq),
        in_specs=[
            pl.BlockSpec((1, n_heads, HEAD_DIM, NSA_Q), lambda b, k, i: (b, k, 0, i)),
            step(N_SEL), step(Q_PER_KV * HEAD_DIM),
            whole(SEQ, LANES), whole(nq, V_ROWS, NSA_Q),
            whole(SEQ, LANES), whole(nq, V_ROWS, NSA_Q),
            pl.BlockSpec((KV_PER_STEP, NSA_Q, LANES), lambda b, k, i: (k, b * nq + i, 0)),
        ],
        out_specs=pl.BlockSpec((1, NSA_Q, n_heads * HEAD_DIM), lambda b, k, i: (b, i, k)),
        out_shape=jax.ShapeDtypeStruct((BATCH, SEQ, N_HEADS * HEAD_DIM), BF16),
        scratch_shapes=[pltpu.VMEM((n_heads, 1, NSA_Q), F32),
                        pltpu.VMEM((n_heads, V_ROWS, NSA_Q), F32),
                        pltpu.VMEM((n_heads, NSA_Q, NSA_Q), F32),
                        pltpu.VMEM((n_heads, NSA_Q, NSA_Q), F32),
                        pltpu.VMEM((n_heads, HEAD_DIM, NSA_Q), F32)],
        compiler_params=pltpu.CompilerParams(
            dimension_semantics=("parallel", "parallel", "arbitrary"),
            vmem_limit_bytes=VMEM_LIMIT),
        name="attend",
    )(q_rot_t, bias, o_cmp_t, k_sel, v_sel_t, k_win, v_win_t, gates)


def _tail_kernel(h_ref, gu_ref, vn_ref, ob_ref, ga_ref, gb_ref, ws_ref, bs_ref,
                 wa_ref, wb_ref, wo_ref, gn_ref, wi_ref, wf_ref, p_ref, gp_ref, wg_ref, wp_ref,
                 gf_ref, o_ref, *, tm):
    r = lax.broadcasted_iota(jnp.int32, (GM_CHUNK, GM_CHUNK), 0)
    c = lax.broadcasted_iota(jnp.int32, (GM_CHUNK, GM_CHUNK), 1)
    w_tril = [jnp.where(c <= r, ws_ref[g], 0.0).astype(BF16) for g in range(GM_GROUPS)]
    bias = bs_ref[...]
    rows = []
    for ch in range(tm // GM_CHUNK):
        rs = slice(ch * GM_CHUNK, (ch + 1) * GM_CHUNK)
        mix = jnp.concatenate(
            [jnp.dot(w_tril[g], vn_ref[rs, g * LANES:(g + 1) * LANES], preferred_element_type=F32)
             for g in range(GM_GROUPS)], axis=1)
        rows.append(gu_ref[rs, :].astype(F32) * (mix + bias))
    z = jnp.concatenate(rows, axis=0).astype(BF16)
    y_a = jnp.dot(z, wa_ref[...], preferred_element_type=F32)
    y_b = jnp.dot(ob_ref[...], wb_ref[...], preferred_element_type=F32)
    merged = (ga_ref[...].astype(F32) * y_a + gb_ref[...].astype(F32) * y_b).astype(BF16)
    h = h_ref[...] + jnp.dot(merged, wo_ref[...], preferred_element_type=F32)

    h = _half_step_ffn(h, gn_ref[...], wi_ref, wf_ref)

    gate = jax.nn.sigmoid(jnp.dot(_rms(h, gp_ref[...]).astype(BF16), wg_ref[...],
                                  preferred_element_type=F32))
    proj = jnp.dot(p_ref[...].astype(BF16), wp_ref[...], preferred_element_type=F32)
    o_ref[...] = _rms(h + gate * proj, gf_ref[...])


def _tail(h, gu, vn, o_b, ga, gb, w_s, b_s_exp, w_a, w_b, w_o, ffn_norm, ffn_w_in, ffn_w_out,
          p, ple_norm, ple_w_gate, ple_w_proj, final_norm, *, tm=512):
    row = pl.BlockSpec((tm, D_MODEL), lambda i: (i, 0))
    square = _const_spec((D_MODEL, D_MODEL))
    gain = _const_spec((1, D_MODEL))
    return pl.pallas_call(
        functools.partial(_tail_kernel, tm=tm),
        grid=(TOKENS // tm,),
        in_specs=[row, row, row, row, row, row,
                  _const_spec((GM_GROUPS, GM_CHUNK, GM_CHUNK)),
                  _const_spec((GM_CHUNK, GM_WIDTH)),
                  square, square, square,
                  gain, _const_spec((D_MODEL, 2 * D_FF)), _const_spec((D_FF, D_MODEL)),
                  pl.BlockSpec((tm, PLE_DIM), lambda i: (i, 0)), gain, square,
                  _const_spec((PLE_DIM, D_MODEL)), gain],
        out_specs=row,
        out_shape=jax.ShapeDtypeStruct((TOKENS, D_MODEL), F32),
        compiler_params=pltpu.CompilerParams(
            dimension_semantics=("parallel",), vmem_limit_bytes=VMEM_LIMIT),
        name="tail",
    )(h, gu, vn, o_b, ga, gb, w_s, b_s_exp, w_a, w_b, w_o, ffn_norm, ffn_w_in, ffn_w_out,
      p, ple_norm, ple_w_gate, ple_w_proj, final_norm)


def _rope_tables():
    inv_freq = ROPE_THETA ** (-jnp.arange(0, ROPE_DIM, 2, dtype=jnp.float32) / ROPE_DIM)
    ang = jnp.arange(SEQ).astype(jnp.float32)[:, None] * inv_freq[None, :]
    cos, sin = jnp.cos(ang), jnp.sin(ang)
    zero = jnp.zeros_like(cos)
    rest = HEAD_DIM - ROPE_DIM
    c = jnp.concatenate([cos, cos, jnp.ones((SEQ, rest), F32)], axis=1)
    sa = jnp.concatenate([zero, sin, jnp.zeros((SEQ, rest), F32)], axis=1)
    sb = jnp.concatenate([-sin, zero, jnp.zeros((SEQ, rest), F32)], axis=1)
    return [jnp.tile(t, (1, LANES // HEAD_DIM)) for t in (c, sa, sb)]


def kernel(x, p, ffn1_norm, ffn1_w_in, ffn1_w_out, mix_norm, w_in, gm_ln_g, gm_ln_b, gm_w_s, gm_b_s,
           w_branch_a, cmp_pos_k, cmp_k_w1, cmp_k_w2, cmp_pos_v, cmp_v_w1, cmp_v_w2, w_branch_b, w_out,
           ffn2_norm, ffn2_w_in, ffn2_w_out, ple_norm, ple_w_gate, ple_w_proj, final_norm):
    assert x.shape == (BATCH, SEQ, D_MODEL) and p.shape == (1, BATCH, SEQ, PLE_DIM)
    row = lambda a: a.reshape(1, -1)
    h = x.reshape(TOKENS, D_MODEL)

    h = _ffn(h, row(ffn1_norm[0]), ffn1_w_in[0].astype(BF16), ffn1_w_out[0].astype(BF16))

    rope_c, rope_sa, rope_sb = _rope_tables()
    (gu, vn, q_raw, q_rot, k_c, v_c, k_sel, v_sel, k_win, v_win, gates, g_a, g_b) = _inproj(
        h, row(mix_norm[0]), w_in[0].T.astype(BF16), row(gm_ln_g[0]), row(gm_ln_b[0]),
        rope_c, rope_sa, rope_sb)

    half = CMP_STRIDE * HEAD_DIM

    def compress(x_heads, pos, w1, w2, feature_major):
        w2 = (w2.T if feature_major else w2).astype(BF16)
        return _compress(x_heads, pos[:CMP_STRIDE].reshape(1, half), pos[CMP_STRIDE:].reshape(1, half),
                         w1[:half].astype(BF16), w1[half:].astype(BF16), w2, feature_major=feature_major)

    k_cmp = compress(k_c, cmp_pos_k[0], cmp_k_w1[0], cmp_k_w2[0], False)
    v_cmp = compress(v_c, cmp_pos_v[0], cmp_v_w1[0], cmp_v_w2[0], True)

    sel_bias, o_cmp = _select(q_raw, k_cmp, v_cmp)
    o_b = _attend(q_rot, sel_bias, o_cmp, k_sel, v_sel, k_win, v_win, gates)

    b_s_exp = jnp.repeat(gm_b_s[0].T, GM_WIDTH // GM_GROUPS, axis=1)
    out = _tail(h, gu, vn, o_b.reshape(TOKENS, N_HEADS * HEAD_DIM), g_a, g_b, gm_w_s[0], b_s_exp,
                w_branch_a[0].astype(BF16), w_branch_b[0].astype(BF16), w_out[0].astype(BF16),
                row(ffn2_norm[0]), ffn2_w_in[0].astype(BF16), ffn2_w_out[0].astype(BF16),
                p[0].reshape(TOKENS, PLE_DIM), row(ple_norm[0]), ple_w_gate[0].astype(BF16),
                ple_w_proj[0].astype(BF16), row(final_norm))
    return out.reshape(BATCH, SEQ, D_MODEL)
```

```python
import functools

import jax
import jax.numpy as jnp
from jax import lax
from jax.experimental import pallas as pl
from jax.experimental.pallas import tpu as pltpu

D_MODEL = 1024
BATCH = 4
SEQ = 4096
PLE_DIM = 256
D_FF = 2816
NORM_EPS = 1e-6
GM_WIDTH = 1024
GM_GROUPS = 8
GM_CHUNK = 128
N_HEADS = 16
N_KV_HEADS = 4
HEAD_DIM = 64
Q_PER_KV = N_HEADS // N_KV_HEADS
KV_WIDTH = N_KV_HEADS * HEAD_DIM
ROPE_DIM = HEAD_DIM // 4
ROPE_HALF = ROPE_DIM // 2
ROPE_THETA = 500000.0
CMP_LEN = 32
CMP_STRIDE = 16
CMP_HIDDEN = 256
SEL_LEN = 64
SEL_TOP = 16
WINDOW = 512
N_NSA_BRANCH = 3
MASK_VALUE = -1e30
FORCE_SCORE = 1e9
LOG2_E = 1.4426950408889634

TOKENS = BATCH * SEQ
N_CHUNKS = SEQ // CMP_STRIDE
N_SEL = SEQ // SEL_LEN
LANES = 128
SUBLANES = 8
VMEM_LIMIT = 56 * 1024 * 1024

OFF_U = 0
OFF_V = OFF_U + GM_WIDTH
OFF_Q = OFF_V + GM_WIDTH
OFF_KV = OFF_Q + N_HEADS * HEAD_DIM
OFF_NSA_GATE = OFF_KV + 6 * KV_WIDTH
OFF_MERGE = OFF_NSA_GATE + N_HEADS * N_NSA_BRANCH
SEL_SHIFT = SEL_LEN.bit_length() - 1

F32 = jnp.float32
BF16 = jnp.bfloat16

NT_DIMS = (((1,), (1,)), ((), ()))


def _const_spec(shape):
    nd = len(shape)
    return pl.BlockSpec(shape, lambda *_: (0,) * nd, pipeline_mode=pl.Buffered(1))


def _rms(x, g):
    return x * lax.rsqrt(jnp.mean(x * x, axis=-1, keepdims=True) + NORM_EPS) * g


MXU_WIDTH = 256


def _half_step_ffn(x, g, wi_ref, wo_ref):
    xn = _rms(x, g).astype(BF16)
    acc = None
    for c in range(D_FF // MXU_WIDTH):
        lo = c * MXU_WIDTH
        gate = jnp.dot(xn, wi_ref[:, lo:lo + MXU_WIDTH], preferred_element_type=F32)
        up = jnp.dot(xn, wi_ref[:, D_FF + lo:D_FF + lo + MXU_WIDTH], preferred_element_type=F32)
        act = (gate * jax.nn.sigmoid(gate) * up).astype(BF16)
        part = jnp.dot(act, wo_ref[lo:lo + MXU_WIDTH, :], preferred_element_type=F32)
        acc = part if acc is None else acc + part
    return x + 0.5 * acc


def _ffn_kernel(x_ref, g_ref, wi_ref, wo_ref, o_ref):
    o_ref[...] = _half_step_ffn(x_ref[...], g_ref[...], wi_ref, wo_ref)


def _ffn(x, g, w_in, w_out, *, tm=512):
    row = pl.BlockSpec((tm, D_MODEL), lambda i: (i, 0))
    return pl.pallas_call(
        _ffn_kernel,
        grid=(TOKENS // tm,),
        in_specs=[row, _const_spec((1, D_MODEL)), _const_spec((D_MODEL, 2 * D_FF)),
                  _const_spec((D_FF, D_MODEL))],
        out_specs=row,
        out_shape=jax.ShapeDtypeStruct((TOKENS, D_MODEL), F32),
        compiler_params=pltpu.CompilerParams(
            dimension_semantics=("parallel",), vmem_limit_bytes=VMEM_LIMIT),
        name="ffn",
    )(x, g, w_in, w_out)


NSA_Q = 256
BF16_ROWS = 16
V_ROWS = HEAD_DIM + BF16_ROWS
INPROJ_KEY_TILES = 2


def _rope(x, c, sa, sb):
    w = x.shape[1]
    return x * c + pltpu.roll(x, ROPE_HALF, 1) * sa + pltpu.roll(x, w - ROPE_HALF, 1) * sb


def _inproj_kernel(h_ref, g_ref, wt_ref, lng_ref, lnb_ref, c_ref, sa_ref, sb_ref,
                   gu_ref, vn_ref, qraw_ref, qrot_ref, kc_ref, vc_ref, ks_ref, vs_ref,
                   kw_ref, vw_ref, gate_ref, ga_ref, gb_ref, *, tm):
    n = _rms(h_ref[...], g_ref[...]).astype(BF16)

    def seg(lo, width):
        return lax.dot_general(n, wt_ref[lo:lo + width, :], NT_DIMS, preferred_element_type=F32)

    gu_ref[...] = jax.nn.gelu(seg(OFF_U, GM_WIDTH)).astype(BF16)
    v = jax.nn.gelu(seg(OFF_V, GM_WIDTH))
    mu = jnp.mean(v, axis=-1, keepdims=True)
    vc = v - mu
    var = jnp.mean(vc * vc, axis=-1, keepdims=True)
    vn_ref[...] = (vc * lax.rsqrt(var + NORM_EPS) * lng_ref[...] + lnb_ref[...]).astype(BF16)

    c, sa, sb = c_ref[...], sa_ref[...], sb_ref[...]

    def tile_lanes(t, reps):
        return jnp.concatenate([t] * reps, axis=1)

    q = seg(OFF_Q, N_HEADS * HEAD_DIM) * (HEAD_DIM ** -0.5 * LOG2_E)
    reps = N_HEADS * HEAD_DIM // LANES
    q_rot = _rope(q, tile_lanes(c, reps), tile_lanes(sa, reps), tile_lanes(sb, reps))
    qraw_ref[0] = q.T.astype(BF16).reshape(N_HEADS, HEAD_DIM, tm)
    qrot_ref[0] = q_rot.T.astype(BF16).reshape(N_HEADS, HEAD_DIM, tm)

    kv = seg(OFF_KV, 6 * KV_WIDTH)
    k_c, v_c, k_s, v_s, k_w, v_w = [kv[:, i * KV_WIDTH:(i + 1) * KV_WIDTH] for i in range(6)]
    reps = KV_WIDTH // LANES
    ck, sak, sbk = tile_lanes(c, reps), tile_lanes(sa, reps), tile_lanes(sb, reps)
    k_s = _rope(k_s, ck, sak, sbk)
    k_w = _rope(k_w, ck, sak, sbk)
    pos = (pl.program_id(0) % (SEQ // tm)) * tm + lax.broadcasted_iota(jnp.int32, (tm, N_SEL), 0)
    blk = lax.broadcasted_iota(jnp.int32, (tm, N_SEL), 1)
    onehot = jnp.where(lax.shift_right_logical(pos, SEL_SHIFT) == blk, 1.0, 0.0).astype(F32)
    ones_t = jnp.ones((V_ROWS - HEAD_DIM, NSA_Q), F32)
    ones_col = jnp.where(lax.broadcasted_iota(jnp.int32, (tm, HEAD_DIM), 1) == 0, 1.0, 0.0)
    v_s_t, v_w_t = v_s.T, v_w.T
    for h in range(N_KV_HEADS):
        sl = slice(h * HEAD_DIM, (h + 1) * HEAD_DIM)
        kc_ref[0, h] = k_c[:, sl]
        vc_ref[0, h] = v_c[:, sl]
        ks_ref[0, h] = jnp.concatenate([k_s[:, sl], onehot], axis=1).astype(BF16)
        kw_ref[0, h] = jnp.concatenate([k_w[:, sl], ones_col], axis=1).astype(BF16)
        for t in range(tm // NSA_Q):
            keys = slice(t * NSA_Q, (t + 1) * NSA_Q)
            vs_ref[0, h, t] = jnp.concatenate([v_s_t[sl, keys], ones_t], axis=0).astype(BF16)
            vw_ref[0, h, t] = jnp.concatenate([v_w_t[sl, keys], ones_t], axis=0).astype(BF16)

    merge = jax.nn.sigmoid(seg(OFF_MERGE, 2 * D_MODEL))
    ga_ref[...] = merge[:, :D_MODEL].astype(BF16)
    gb_ref[...] = merge[:, D_MODEL:].astype(BF16)
    gates = jax.nn.sigmoid(seg(OFF_NSA_GATE, LANES))
    group_cols = Q_PER_KV * N_NSA_BRANCH
    for h in range(N_KV_HEADS):
        gate_ref[h] = gates if h == 0 else pltpu.roll(gates, LANES - h * group_cols, 1)


def _inproj(h, g, w_in_t, ln_g, ln_b, rope_c, rope_sa, rope_sb):
    tm = INPROJ_KEY_TILES * NSA_Q
    nq = SEQ // tm
    row = lambda i: (i, 0)
    head = lambda i: (i // nq, 0, i % nq, 0)
    tok_bf = jax.ShapeDtypeStruct((TOKENS, D_MODEL), BF16)

    def head_shape(nh, width, dtype):
        return jax.ShapeDtypeStruct((BATCH, nh, SEQ, width), dtype)

    def head_spec(nh, width):
        return pl.BlockSpec((1, nh, tm, width), head)

    qt_spec = pl.BlockSpec((1, N_HEADS, HEAD_DIM, tm), lambda i: (i // nq, 0, 0, i % nq))
    qt_shape = jax.ShapeDtypeStruct((BATCH, N_HEADS, HEAD_DIM, SEQ), BF16)
    vt_spec = pl.BlockSpec((1, N_KV_HEADS, INPROJ_KEY_TILES, V_ROWS, NSA_Q),
                           lambda i: (i // nq, 0, i % nq, 0, 0))
    vt_shape = jax.ShapeDtypeStruct((BATCH, N_KV_HEADS, SEQ // NSA_Q, V_ROWS, NSA_Q), BF16)

    return pl.pallas_call(
        functools.partial(_inproj_kernel, tm=tm),
        grid=(TOKENS // tm,),
        in_specs=[
            pl.BlockSpec((tm, D_MODEL), row),
            _const_spec((1, D_MODEL)),
            _const_spec(w_in_t.shape),
            _const_spec((1, GM_WIDTH)),
            _const_spec((1, GM_WIDTH)),
            pl.BlockSpec((tm, LANES), lambda i: (i % nq, 0)),
            pl.BlockSpec((tm, LANES), lambda i: (i % nq, 0)),
            pl.BlockSpec((tm, LANES), lambda i: (i % nq, 0)),
        ],
        out_specs=[
            pl.BlockSpec((tm, GM_WIDTH), row),
            pl.BlockSpec((tm, GM_WIDTH), row),
            qt_spec,
            qt_spec,
            head_spec(N_KV_HEADS, HEAD_DIM),
            head_spec(N_KV_HEADS, HEAD_DIM),
            head_spec(N_KV_HEADS, LANES),
            vt_spec,
            head_spec(N_KV_HEADS, LANES),
            vt_spec,
            pl.BlockSpec((N_KV_HEADS, tm, LANES), lambda i: (0, i, 0)),
            pl.BlockSpec((tm, D_MODEL), row),
            pl.BlockSpec((tm, D_MODEL), row),
        ],
        out_shape=[
            tok_bf, tok_bf,
            qt_shape, qt_shape,
            head_shape(N_KV_HEADS, HEAD_DIM, F32), head_shape(N_KV_HEADS, HEAD_DIM, F32),
            head_shape(N_KV_HEADS, LANES, BF16), vt_shape,
            head_shape(N_KV_HEADS, LANES, BF16), vt_shape,
            jax.ShapeDtypeStruct((N_KV_HEADS, TOKENS, LANES), F32),
            tok_bf, tok_bf,
        ],
        compiler_params=pltpu.CompilerParams(
            dimension_semantics=("parallel",), vmem_limit_bytes=VMEM_LIMIT),
        name="inproj",
    )(h, g, w_in_t, ln_g, ln_b, rope_c, rope_sa, rope_sb)


def _compress_kernel(x_ref, ptop_ref, pbot_ref, w1t_ref, w1b_ref, w2_ref, o_ref, *, feature_major):
    kv_heads = range(N_KV_HEADS)
    x = jnp.concatenate(
        [jnp.concatenate([x_ref[0, h, pl.ds(j, N_CHUNKS, stride=CMP_STRIDE), :]
                          for j in range(CMP_STRIDE)], axis=1) for h in kv_heads], axis=0)
    top = jnp.dot((x + ptop_ref[...]).astype(BF16), w1t_ref[...], preferred_element_type=F32)
    bot = jnp.dot((x + pbot_ref[...]).astype(BF16), w1b_ref[...], preferred_element_type=F32)
    rows = [slice(h * N_CHUNKS, (h + 1) * N_CHUNKS) for h in kv_heads]
    bot_next = jnp.concatenate([pltpu.roll(bot[r], N_CHUNKS - 1, 0) for r in rows], axis=0)
    hidden = jax.nn.gelu(top + bot_next).astype(BF16)
    if feature_major:
        out = lax.dot_general(w2_ref[...], hidden, NT_DIMS, preferred_element_type=F32)
        for h in kv_heads:
            o_ref[0, h] = out[:, rows[h]].astype(BF16)
    else:
        out = jnp.dot(hidden, w2_ref[...], preferred_element_type=F32)
        for h in kv_heads:
            o_ref[0, h] = out[rows[h]].astype(BF16)


def _compress(x_heads, pos_top, pos_bot, w1_top, w1_bot, w2, *, feature_major):
    half = CMP_STRIDE * HEAD_DIM
    out_tile = (HEAD_DIM, N_CHUNKS) if feature_major else (N_CHUNKS, HEAD_DIM)
    return pl.pallas_call(
        functools.partial(_compress_kernel, feature_major=feature_major),
        grid=(BATCH,),
        in_specs=[
            pl.BlockSpec((1, N_KV_HEADS, SEQ, HEAD_DIM), lambda b: (b, 0, 0, 0)),
            _const_spec((1, half)),
            _const_spec((1, half)),
            _const_spec((half, CMP_HIDDEN)),
            _const_spec((half, CMP_HIDDEN)),
            _const_spec(w2.shape),
        ],
        out_specs=pl.BlockSpec((1, N_KV_HEADS) + out_tile, lambda b: (b, 0, 0, 0)),
        out_shape=jax.ShapeDtypeStruct((BATCH, N_KV_HEADS) + out_tile, BF16),
        compiler_params=pltpu.CompilerParams(
            dimension_semantics=("parallel",), vmem_limit_bytes=VMEM_LIMIT),
        name="compress",
    )(x_heads, pos_top, pos_bot, w1_top, w1_bot, w2)


RANK_ACCUMULATORS = 4
SELECT_Q = 1024


def _selection_bias(score, n_blocks):
    n_q = score.shape[1]
    sub_iota = lax.broadcasted_iota(jnp.int32, (SUBLANES, n_q), 0)
    groups = [score[g * SUBLANES:(g + 1) * SUBLANES] for g in range(n_blocks // SUBLANES)]
    counts = [[None] * RANK_ACCUMULATORS for _ in groups]
    for jp in range(n_blocks):
        other = score[jp:jp + 1, :]
        for g, grp in enumerate(groups):
            lo = g * SUBLANES
            if lo > jp:
                before = other >= grp
            elif lo + SUBLANES - 1 <= jp:
                before = other > grp
            else:
                before = (other > grp) | ((other == grp) & (sub_iota + lo > jp))
            inc = jnp.where(before, 1, 0)
            a = jp % RANK_ACCUMULATORS
            counts[g][a] = inc if counts[g][a] is None else counts[g][a] + inc
    rank = jnp.concatenate([(c[0] + c[1]) + (c[2] + c[3]) for c in counts], axis=0)
    bias = jnp.where(rank < SEL_TOP, 0.0, MASK_VALUE)
    if n_blocks < N_SEL:
        bias = jnp.concatenate([bias, jnp.zeros((N_SEL - n_blocks, n_q), F32)], axis=0)
    return bias


def _select_kernel(qraw_ref, kcmp_ref, vcmp_ref, bias_ref, ocmp_ref, psum_sc):
    qi = pl.program_id(2)
    s0 = qi * SELECT_Q
    key_i = lax.broadcasted_iota(jnp.int32, (N_CHUNKS, SELECT_Q), 0)
    qry_i = lax.broadcasted_iota(jnp.int32, (N_CHUNKS, SELECT_Q), 1)
    heads = range(Q_PER_KV)

    k_cmp = kcmp_ref[0, 0]
    v_cmp_t = vcmp_ref[0, 0]
    s_cmp = [jnp.dot(k_cmp, qraw_ref[0, h], preferred_element_type=F32) for h in heads]
    cmp_valid = key_i * CMP_STRIDE + (CMP_LEN - 1) <= s0 + qry_i
    any_valid = s0 + lax.broadcasted_iota(jnp.int32, (1, SELECT_Q), 1) >= CMP_LEN - 1
    p_cmp = []
    for h in heads:
        s_c = jnp.where(cmp_valid, s_cmp[h], MASK_VALUE)
        e_c = jnp.exp2(s_c - jnp.max(s_c, axis=0, keepdims=True))
        p_cmp.append(e_c * jnp.where(any_valid, 1.0 / jnp.sum(e_c, axis=0, keepdims=True), 0.0))
    for h in heads:
        ocmp_ref[0, 0, h * HEAD_DIM:(h + 1) * HEAD_DIM, :] = jnp.dot(
            v_cmp_t, p_cmp[h].astype(BF16), preferred_element_type=F32)

    p_sum = (p_cmp[0] + p_cmp[1]) + (p_cmp[2] + p_cmp[3])
    lane_tiles = range(SELECT_Q // LANES)
    for t in lane_tiles:
        psum_sc[t] = p_sum[:, t * LANES:(t + 1) * LANES]
    per_sel = SEL_LEN // CMP_STRIDE
    terms = [jnp.concatenate([psum_sc[t, pl.ds(r, N_SEL, stride=per_sel), :] for t in lane_tiles], axis=1)
             for r in range(per_sel)]
    blk = lax.broadcasted_iota(jnp.int32, (N_SEL, SELECT_Q), 0)
    before = jnp.where(blk == 0, 0.0, pltpu.roll(terms[per_sel - 1], 1, 0))
    imp_t = (before + terms[0]) + (terms[1] + terms[2]) + terms[3]
    cur = lax.shift_right_logical(s0 + lax.broadcasted_iota(jnp.int32, (N_SEL, SELECT_Q), 1), SEL_SHIFT)
    forced = (blk == 0) | (blk == cur) | (blk == cur - 1)
    score = jnp.where(forced, FORCE_SCORE, jnp.where(blk > cur, -FORCE_SCORE, imp_t))

    blocks_per_tile = NSA_Q // SEL_LEN
    for variant in range(SEQ // SELECT_Q):
        @pl.when(qi == variant)
        def _():
            for t in range(SELECT_Q // NSA_Q):
                lanes = slice(t * NSA_Q, (t + 1) * NSA_Q)
                reachable = (variant * (SELECT_Q // NSA_Q) + t + 1) * blocks_per_tile
                n_blocks = min(N_SEL, -(-reachable // SUBLANES) * SUBLANES)
                bias_ref[0, 0, :, lanes] = _selection_bias(score[:, lanes], n_blocks).astype(BF16)


def _select(q_raw_t, k_cmp, v_cmp_t):
    assert CMP_LEN == 2 * CMP_STRIDE
    nq = SEQ // SELECT_Q
    step = lambda rows: pl.BlockSpec((1, 1, rows, SELECT_Q), lambda b, k, i: (b, k, 0, i))
    whole = lambda *tile: pl.BlockSpec((1, 1) + tile, lambda b, k, i: (b, k) + (0,) * len(tile))
    return pl.pallas_call(
        _select_kernel,
        grid=(BATCH, N_KV_HEADS, nq),
        in_specs=[
            pl.BlockSpec((1, Q_PER_KV, HEAD_DIM, SELECT_Q), lambda b, k, i: (b, k, 0, i)),
            whole(N_CHUNKS, HEAD_DIM), whole(HEAD_DIM, N_CHUNKS),
        ],
        out_specs=[step(N_SEL), step(Q_PER_KV * HEAD_DIM)],
        out_shape=[jax.ShapeDtypeStruct((BATCH, N_KV_HEADS, N_SEL, SEQ), BF16),
                   jax.ShapeDtypeStruct((BATCH, N_KV_HEADS, Q_PER_KV * HEAD_DIM, SEQ), F32)],
        scratch_shapes=[pltpu.VMEM((SELECT_Q // LANES, N_CHUNKS, LANES), F32)],
        compiler_params=pltpu.CompilerParams(
            dimension_semantics=("parallel", "parallel", "parallel"), vmem_limit_bytes=VMEM_LIMIT),
        name="select",
    )(q_raw_t, k_cmp, v_cmp_t)


KV_PER_STEP = 2


def _attend_kernel(qrot_ref, bias_ref, ocmp_ref, ks_ref, vs_ref, kw_ref, vw_ref, gate_ref,
                   o_ref, m_sc, acc_sc, sa_sc, sb_sc, owin_sc):
    qi = pl.program_id(2)
    key_i = lax.broadcasted_iota(jnp.int32, (NSA_Q, NSA_Q), 0)
    qry_i = lax.broadcasted_iota(jnp.int32, (NSA_Q, NSA_Q), 1)
    causal = key_i <= qry_i
    kvs = range(KV_PER_STEP)
    heads = range(KV_PER_STEP * Q_PER_KV)

    def scores(k, q_t):
        return jnp.dot(k, q_t, preferred_element_type=F32)

    def softmax_step(h, s, v_t):
        m_prev = m_sc[h]
        m_new = jnp.maximum(m_prev, jnp.max(s, axis=0, keepdims=True))
        p = jnp.exp2(s - m_new).astype(BF16)
        acc_sc[h] = jnp.exp2(m_prev - m_new) * acc_sc[h] + jnp.dot(v_t, p, preferred_element_type=F32)
        m_sc[h] = m_new

    q_sel = [jnp.concatenate([qrot_ref[0, h], bias_ref[0, h // Q_PER_KV]], axis=0) for h in heads]
    m_sc[...] = jnp.full_like(m_sc, MASK_VALUE)
    acc_sc[...] = jnp.zeros_like(acc_sc)

    def sel_scores(kt, dst, group=None):
        rows = pl.ds(pl.multiple_of(kt * NSA_Q, NSA_Q), NSA_Q)
        for kv in kvs if group is None else (group,):
            k = ks_ref[0, kv, rows, :]
            for h in range(kv * Q_PER_KV, (kv + 1) * Q_PER_KV):
                dst[h] = scores(k, q_sel[h])

    def sel_softmax(kt, src, diagonal=False, group=None):
        for kv in kvs if group is None else (group,):
            v_t = vs_ref[0, kv, kt]
            for h in range(kv * Q_PER_KV, (kv + 1) * Q_PER_KV):
                s = src[h]
                softmax_step(h, jnp.where(causal, s, MASK_VALUE) if diagonal else s, v_t)

    def window_keys(kv, kt):
        return kw_ref[0, kv, pl.ds(pl.multiple_of(kt * NSA_Q, NSA_Q), NSA_Q), :]

    def penalty_rows(penalty):
        rows = jnp.where(lax.broadcasted_iota(jnp.int32, (HEAD_DIM, NSA_Q), 0) == 0, penalty, 0.0)
        return rows.astype(BF16)

    win_mid, win_far = jnp.maximum(qi - 1, 0), jnp.maximum(qi - 2, 0)
    k_diag = [window_keys(kv, qi)[:, 0:HEAD_DIM] for kv in kvs]
    k_mid = [window_keys(kv, win_mid) for kv in kvs]
    k_far = [window_keys(kv, win_far) for kv in kvs]
    pen_mid = penalty_rows(jnp.where(qi >= 1, 0.0, MASK_VALUE))
    pen_far = penalty_rows(jnp.where(qi >= 2, 0.0, MASK_VALUE))
    s_win_diag, s_win_mid, s_win_far = [], [], []
    for h in heads:
        q_t = qrot_ref[0, h]
        kv = h // Q_PER_KV
        s_win_diag.append(scores(k_diag[kv], q_t))
        s_win_mid.append(scores(k_mid[kv], jnp.concatenate([q_t, pen_mid], axis=0)))
        s_win_far.append(scores(k_far[kv], jnp.concatenate([q_t, pen_far], axis=0)))

    sel_scores(0, sa_sc)

    for h in heads:
        s_edge = jnp.where(causal, s_win_diag[h], s_win_far[h])
        s_mid = s_win_mid[h]
        m_w = jnp.max(jnp.maximum(s_edge, s_mid), axis=0, keepdims=True)
        p_edge = jnp.exp2(s_edge - m_w).astype(BF16)
        p_mid = jnp.exp2(s_mid - m_w).astype(BF16)
        zero = jnp.zeros_like(p_edge)
        kv = h // Q_PER_KV
        acc = (jnp.dot(vw_ref[0, kv, qi], jnp.where(causal, p_edge, zero), preferred_element_type=F32)
               + jnp.dot(vw_ref[0, kv, win_far], jnp.where(causal, zero, p_edge), preferred_element_type=F32)
               + jnp.dot(vw_ref[0, kv, win_mid], p_mid, preferred_element_type=F32))
        owin_sc[h] = acc[:HEAD_DIM] / acc[HEAD_DIM:HEAD_DIM + 1]

    def sel_pair(first):
        for kv in kvs:
            sel_scores(first + 1, sb_sc, group=kv)
            sel_softmax(first, sa_sc, group=kv)
        for kv in kvs:
            sel_scores(first + 2, sa_sc, group=kv)
            sel_softmax(first + 1, sb_sc, group=kv)

    def sel_quad(t, carry):
        sel_pair(4 * t)
        sel_pair(4 * t + 2)
        return carry

    lax.fori_loop(0, lax.shift_right_logical(qi, 2), sel_quad, 0)

    def finish():
        gates_t = [gate_ref[kv].T for kv in kvs]
        outs = []
        for h in heads:
            kv, hh = divmod(h, Q_PER_KV)
            g = gates_t[kv]
            c = hh * N_NSA_BRANCH
            acc = acc_sc[h]
            o_sel = acc[:HEAD_DIM] / acc[HEAD_DIM:HEAD_DIM + 1]
            o_cmp = ocmp_ref[0, kv, hh * HEAD_DIM:(hh + 1) * HEAD_DIM, :]
            outs.append(g[c:c + 1] * o_cmp + g[c + 1:c + 2] * o_sel + g[c + 2:c + 3] * owin_sc[h])
        o_ref[0] = jnp.concatenate(outs, axis=0).T.astype(BF16)

    first = qi & ~3
    for remaining in range(4):
        @pl.when((qi & 3) == remaining)
        def _():
            if remaining >= 2:
                sel_pair(first)
            if remaining % 2 == 1:
                for kv in kvs:
                    sel_scores(qi, sb_sc, group=kv)
                    sel_softmax(qi - 1, sa_sc, group=kv)
                sel_softmax(qi, sb_sc, diagonal=True)
            else:
                sel_softmax(qi, sa_sc, diagonal=True)
            finish()


def _attend(q_rot_t, bias, o_cmp_t, k_sel, v_sel_t, k_win, v_win_t, gates):
    assert WINDOW == 2 * NSA_Q
    assert NSA_Q % SEL_LEN == 0 and HEAD_DIM + N_SEL == LANES
    nq = SEQ // NSA_Q
    n_heads = KV_PER_STEP * Q_PER_KV
    step = lambda rows: pl.BlockSpec((1, KV_PER_STEP, rows, NSA_Q), lambda b, k, i: (b, k, 0, i))
    whole = lambda *tile: pl.BlockSpec((1, KV_PER_STEP) + tile, lambda b, k, i: (b, k) + (0,) * len(tile))
    return pl.pallas_call(
        _attend_kernel,
        grid=(BATCH, N_KV_HEADS // KV_PER_STEP, nq),
        in_specs=[
            pl.BlockSpec((1, n_heads, HEAD_DIM, NSA_Q), lambda b, k, i: (b, k, 0, i)),
            step(N_SEL), step(Q_PER_KV * HEAD_DIM),
            whole(SEQ, LANES), whole(nq, V_ROWS, NSA_Q),
            whole(SEQ, LANES), whole(nq, V_ROWS, NSA_Q),
            pl.BlockSpec((KV_PER_STEP, NSA_Q, LANES), lambda b, k, i: (k, b * nq + i, 0)),
        ],
        out_specs=pl.BlockSpec((1, NSA_Q, n_heads * HEAD_DIM), lambda b, k, i: (b, i, k)),
        out_shape=jax.ShapeDtypeStruct((BATCH, SEQ, N_HEADS * HEAD_DIM), BF16),
        scratch_shapes=[pltpu.VMEM((n_heads, 1, NSA_Q), F32),
                        pltpu.VMEM((n_heads, V_ROWS, NSA_Q), F32),
                        pltpu.VMEM((n_heads, NSA_Q, NSA_Q), F32),
                        pltpu.VMEM((n_heads, NSA_Q, NSA_Q), F32),
                        pltpu.VMEM((n_heads, HEAD_DIM, NSA_Q), F32)],
        compiler_params=pltpu.CompilerParams(
            dimension_semantics=("parallel", "parallel", "arbitrary"),
            vmem_limit_bytes=VMEM_LIMIT),
        name="attend",
    )(q_rot_t, bias, o_cmp_t, k_sel, v_sel_t, k_win, v_win_t, gates)


def _tail_kernel(h_ref, gu_ref, vn_ref, ob_ref, ga_ref, gb_ref, ws_ref, bs_ref,
                 wa_ref, wb_ref, wo_ref, gn_ref, wi_ref, wf_ref, p_ref, gp_ref, wg_ref, wp_ref,
                 gf_ref, o_ref, *, tm):
    r = lax.broadcasted_iota(jnp.int32, (GM_CHUNK, GM_CHUNK), 0)
    c = lax.broadcasted_iota(jnp.int32, (GM_CHUNK, GM_CHUNK), 1)
    w_tril = [jnp.where(c <= r, ws_ref[g], 0.0).astype(BF16) for g in range(GM_GROUPS)]
    bias = bs_ref[...]
    rows = []
    for ch in range(tm // GM_CHUNK):
        rs = slice(ch * GM_CHUNK, (ch + 1) * GM_CHUNK)
        mix = jnp.concatenate(
            [jnp.dot(w_tril[g], vn_ref[rs, g * LANES:(g + 1) * LANES], preferred_element_type=F32)
             for g in range(GM_GROUPS)], axis=1)
        rows.append(gu_ref[rs, :].astype(F32) * (mix + bias))
    z = jnp.concatenate(rows, axis=0).astype(BF16)
    y_a = jnp.dot(z, wa_ref[...], preferred_element_type=F32)
    y_b = jnp.dot(ob_ref[...], wb_ref[...], preferred_element_type=F32)
    merged = (ga_ref[...].astype(F32) * y_a + gb_ref[...].astype(F32) * y_b).astype(BF16)
    h = h_ref[...] + jnp.dot(merged, wo_ref[...], preferred_element_type=F32)

    h = _half_step_ffn(h, gn_ref[...], wi_ref, wf_ref)

    gate = jax.nn.sigmoid(jnp.dot(_rms(h, gp_ref[...]).astype(BF16), wg_ref[...],
                                  preferred_element_type=F32))
    proj = jnp.dot(p_ref[...].astype(BF16), wp_ref[...], preferred_element_type=F32)
    o_ref[...] = _rms(h + gate * proj, gf_ref[...])


def _tail(h, gu, vn, o_b, ga, gb, w_s, b_s_exp, w_a, w_b, w_o, ffn_norm, ffn_w_in, ffn_w_out,
          p, ple_norm, ple_w_gate, ple_w_proj, final_norm, *, tm=512):
    row = pl.BlockSpec((tm, D_MODEL), lambda i: (i, 0))
    square = _const_spec((D_MODEL, D_MODEL))
    gain = _const_spec((1, D_MODEL))
    return pl.pallas_call(
        functools.partial(_tail_kernel, tm=tm),
        grid=(TOKENS // tm,),
        in_specs=[row, row, row, row, row, row,
                  _const_spec((GM_GROUPS, GM_CHUNK, GM_CHUNK)),
                  _const_spec((GM_CHUNK, GM_WIDTH)),
                  square, square, square,
                  gain, _const_spec((D_MODEL, 2 * D_FF)), _const_spec((D_FF, D_MODEL)),
                  pl.BlockSpec((tm, PLE_DIM), lambda i: (i, 0)), gain, square,
                  _const_spec((PLE_DIM, D_MODEL)), gain],
        out_specs=row,
        out_shape=jax.ShapeDtypeStruct((TOKENS, D_MODEL), F32),
        compiler_params=pltpu.CompilerParams(
            dimension_semantics=("parallel",), vmem_limit_bytes=VMEM_LIMIT),
        name="tail",
    )(h, gu, vn, o_b, ga, gb, w_s, b_s_exp, w_a, w_b, w_o, ffn_norm, ffn_w_in, ffn_w_out,
      p, ple_norm, ple_w_gate, ple_w_proj, final_norm)


def _rope_tables():
    inv_freq = ROPE_THETA ** (-jnp.arange(0, ROPE_DIM, 2, dtype=jnp.float32) / ROPE_DIM)
    ang = jnp.arange(SEQ).astype(jnp.float32)[:, None] * inv_freq[None, :]
    cos, sin = jnp.cos(ang), jnp.sin(ang)
    zero = jnp.zeros_like(cos)
    rest = HEAD_DIM - ROPE_DIM
    c = jnp.concatenate([cos, cos, jnp.ones((SEQ, rest), F32)], axis=1)
    sa = jnp.concatenate([zero, sin, jnp.zeros((SEQ, rest), F32)], axis=1)
    sb = jnp.concatenate([-sin, zero, jnp.zeros((SEQ, rest), F32)], axis=1)
    return [jnp.tile(t, (1, LANES // HEAD_DIM)) for t in (c, sa, sb)]


def kernel(x, p, ffn1_norm, ffn1_w_in, ffn1_w_out, mix_norm, w_in, gm_ln_g, gm_ln_b, gm_w_s, gm_b_s,
           w_branch_a, cmp_pos_k, cmp_k_w1, cmp_k_w2, cmp_pos_v, cmp_v_w1, cmp_v_w2, w_branch_b, w_out,
           ffn2_norm, ffn2_w_in, ffn2_w_out, ple_norm, ple_w_gate, ple_w_proj, final_norm):
    assert x.shape == (BATCH, SEQ, D_MODEL) and p.shape == (1, BATCH, SEQ, PLE_DIM)
    row = lambda a: a.reshape(1, -1)
    h = x.reshape(TOKENS, D_MODEL)

    h = _ffn(h, row(ffn1_norm[0]), ffn1_w_in[0].astype(BF16), ffn1_w_out[0].astype(BF16))

    rope_c, rope_sa, rope_sb = _rope_tables()
    (gu, vn, q_raw, q_rot, k_c, v_c, k_sel, v_sel, k_win, v_win, gates, g_a, g_b) = _inproj(
        h, row(mix_norm[0]), w_in[0].T.astype(BF16), row(gm_ln_g[0]), row(gm_ln_b[0]),
        rope_c, rope_sa, rope_sb)

    half = CMP_STRIDE * HEAD_DIM

    def compress(x_heads, pos, w1, w2, feature_major):
        w2 = (w2.T if feature_major else w2).astype(BF16)
        return _compress(x_heads, pos[:CMP_STRIDE].reshape(1, half), pos[CMP_STRIDE:].reshape(1, half),
                         w1[:half].astype(BF16), w1[half:].astype(BF16), w2, feature_major=feature_major)

    k_cmp = compress(k_c, cmp_pos_k[0], cmp_k_w1[0], cmp_k_w2[0], False)
    v_cmp = compress(v_c, cmp_pos_v[0], cmp_v_w1[0], cmp_v_w2[0], True)

    sel_bias, o_cmp = _select(q_raw, k_cmp, v_cmp)
    o_b = _attend(q_rot, sel_bias, o_cmp, k_sel, v_sel, k_win, v_win, gates)

    b_s_exp = jnp.repeat(gm_b_s[0].T, GM_WIDTH // GM_GROUPS, axis=1)
    out = _tail(h, gu, vn, o_b.reshape(TOKENS, N_HEADS * HEAD_DIM), g_a, g_b, gm_w_s[0], b_s_exp,
                w_branch_a[0].astype(BF16), w_branch_b[0].astype(BF16), w_out[0].astype(BF16),
                row(ffn2_norm[0]), ffn2_w_in[0].astype(BF16), ffn2_w_out[0].astype(BF16),
                p[0].reshape(TOKENS, PLE_DIM), row(ple_norm[0]), ple_w_gate[0].astype(BF16),
                ple_w_proj[0].astype(BF16), row(final_norm))
    return out.reshape(BATCH, SEQ, D_MODEL)
```
